```python
import jax, jax.numpy as jnp
from jax import lax
import numpy as np

D_MODEL = 2048
BATCH = 2
SEQ = 4096
DEPTH = 4
DEC_BATCH = 32
DEC_SEQ = 8
PAST_LEN = 16384
PAGE_SIZE = 128

D_MIX = D_MODEL
D_LRU = D_MIX // 4
LRU_HEADS = 8
LRU_HEAD_DIM = D_LRU // LRU_HEADS
CONV_WIDTH = 4
LRU_C = 8.0
D_POOL = D_MIX // 4
POOL_WINDOWS = (2, 4, 8, 16)
POOL_GROUPS = len(POOL_WINDOWS)
POOL_GROUP_DIM = D_POOL // POOL_GROUPS
POOL_PAD = max(POOL_WINDOWS) - 1
D_ATTN = D_MIX - D_LRU - D_POOL
HEAD_DIM = 64
N_HEADS = D_ATTN // HEAD_DIM
N_KV_HEADS = 4
GQ = N_HEADS // N_KV_HEADS
WINDOW = 128
ROPE_THETA = 10000.0
ATTN_SCALE = HEAD_DIM ** -0.5
D_KV = N_KV_HEADS * HEAD_DIM
D_IN = 2 * D_LRU + D_POOL + D_ATTN + 2 * D_KV
SPLIT_POINTS = (D_LRU, 2 * D_LRU, 2 * D_LRU + D_POOL, 2 * D_LRU + D_POOL + D_ATTN,
                2 * D_LRU + D_POOL + D_ATTN + D_KV)
N_EXPERT_GROUPS = 4
EXPERTS_PER_GROUP = 4
N_EXPERTS = N_EXPERT_GROUPS * EXPERTS_PER_GROUP
TOP_K = 2
D_EXPERT = 256
RMS_EPS = 1e-6

kernel_name = 'hymba_rglru_pool_swa_hmoe_step'


def rmsnorm(x, g):
    xf = x.astype(jnp.float32)
    y = xf * lax.rsqrt(jnp.mean(xf * xf, axis=-1, keepdims=True) + RMS_EPS)
    return (y * g.astype(jnp.float32)).astype(x.dtype)


def rope(x, pos):
    half = HEAD_DIM // 2
    inv = ROPE_THETA ** (-jnp.arange(half, dtype=jnp.float32) / half)
    ang = pos.astype(jnp.float32)[:, None] * inv[None, :]
    cos = jnp.cos(ang)[None, :, None, :]
    sin = jnp.sin(ang)[None, :, None, :]
    xf = x.astype(jnp.float32)
    x1, x2 = xf[..., :half], xf[..., half:]
    return jnp.concatenate([x1 * cos - x2 * sin, x2 * cos + x1 * sin], axis=-1).astype(x.dtype)


def causal_conv(u, buf, w, b):
    t = u.shape[1]
    xp = jnp.concatenate([buf, u], axis=1)
    y = b + w[0] * xp[:, 0:t]
    for j in range(1, CONV_WIDTH):
        y = y + w[j] * xp[:, j:j + t]
    return y, xp[:, -(CONV_WIDTH - 1):]


def rg_lru(xc, h0, wa, ba, wx, bx, lam):
    bsz, t, _ = xc.shape
    xh = xc.reshape(bsz, t, LRU_HEADS, LRU_HEAD_DIM)
    r = jax.nn.sigmoid((jnp.einsum('bthi,hij->bthj', xh, wa).reshape(bsz, t, D_LRU) + ba).astype(jnp.float32))
    ig = jax.nn.sigmoid((jnp.einsum('bthi,hij->bthj', xh, wx).reshape(bsz, t, D_LRU) + bx).astype(jnp.float32))
    log_a = -LRU_C * r * jax.nn.softplus(-lam.astype(jnp.float32))
    a = jnp.exp(log_a)
    bterm = jnp.sqrt(-jnp.expm1(2.0 * log_a)) * ig * xc.astype(jnp.float32)
    bterm = bterm.at[:, 0].add(a[:, 0] * h0.astype(jnp.float32))

    def combine(left, right):
        a1, b1 = left
        a2, b2 = right
        return a1 * a2, a2 * b1 + b2

    _, h = lax.associative_scan(combine, (a, bterm), axis=1)
    return h, h[:, -1]


def pool_mixer(u, buf, pos0, pw, pscale):
    bsz, t, _ = u.shape
    xb = jnp.concatenate([buf, u], axis=1)
    xp = xb.astype(jnp.float32)
    cs = jnp.concatenate([jnp.zeros((bsz, 1, D_POOL), jnp.float32), jnp.cumsum(xp, axis=1)], axis=1)
    pos = pos0 + jnp.arange(t)
    end = cs[:, POOL_PAD + 1:POOL_PAD + 1 + t]
    cur = xp[:, POOL_PAD:]
    outs = []
    for gi, w in enumerate(POOL_WINDOWS):
        lo, hi = gi * POOL_GROUP_DIM, (gi + 1) * POOL_GROUP_DIM
        start = cs[:, POOL_PAD + 1 - w:POOL_PAD + 1 - w + t, lo:hi]
        cnt = jnp.minimum(pos + 1, w).astype(jnp.float32)[None, :, None]
        outs.append((end[..., lo:hi] - start) / cnt - cur[..., lo:hi])
    d = jnp.concatenate(outs, axis=-1).reshape(bsz, t, POOL_GROUPS, POOL_GROUP_DIM)
    y = jnp.einsum('btgi,gij->btgj', d, pw.astype(jnp.float32)).reshape(bsz, t, D_POOL)
    return y * pscale.astype(jnp.float32), xb[:, -POOL_PAD:]


def sink_softmax(s, sink, mask):
    s = jnp.where(mask, s, -jnp.inf)
    m = jnp.maximum(jnp.max(s, axis=-1, keepdims=True), sink)
    e = jnp.exp(s - m)
    return e / (jnp.sum(e, axis=-1, keepdims=True) + jnp.exp(sink - m))


def swa_prompt(q, k, v, sink):
    bsz, t = q.shape[:2]
    nb = t // WINDOW
    qb = q.reshape(bsz, nb, WINDOW, N_KV_HEADS, GQ, HEAD_DIM)
    kb = k.reshape(bsz, nb, WINDOW, N_KV_HEADS, HEAD_DIM)
    vb = v.reshape(bsz, nb, WINDOW, N_KV_HEADS, HEAD_DIM)
    kk = jnp.concatenate([jnp.concatenate([jnp.zeros_like(kb[:, :1]), kb[:, :-1]], axis=1), kb], axis=2)
    vv = jnp.concatenate([jnp.concatenate([jnp.zeros_like(vb[:, :1]), vb[:, :-1]], axis=1), vb], axis=2)
    s = jnp.einsum('bnqkgd,bnskd->bnkgqs', qb, kk).astype(jnp.float32) * ATTN_SCALE
    qi = jnp.arange(WINDOW)[:, None]
    kj = jnp.arange(2 * WINDOW)[None, :]
    diff = WINDOW + qi - kj
    band = (diff >= 0) & (diff < WINDOW)
    has_prev = (jnp.arange(nb)[:, None, None] >= 1) | (kj[None] >= WINDOW)
    mask = (band[None] & has_prev)[None, :, None, None]
    p = sink_softmax(s, sink, mask)
    o = jnp.einsum('bnkgqs,bnskd->bnqkgd', p.astype(v.dtype), vv)
    wp = min(WINDOW, t)
    return o.reshape(bsz, t, D_ATTN), k[:, -wp:], v[:, -wp:]


def swa_sample(q, k, v, cache_k, cache_v, sink, pos0):
    bsz, t = q.shape[:2]
    wc = cache_k.shape[1]
    kk = jnp.concatenate([cache_k, k], axis=1)
    vv = jnp.concatenate([cache_v, v], axis=1)
    qpos = pos0 + jnp.arange(t)
    kpos = pos0 - wc + jnp.arange(wc + t)
    diff = qpos[:, None] - kpos[None, :]
    mask = ((diff >= 0) & (diff < WINDOW))[None, None, None]
    qg = q.reshape(bsz, t, N_KV_HEADS, GQ, HEAD_DIM)
    s = jnp.einsum('bqkgd,bskd->bkgqs', qg, kk).astype(jnp.float32) * ATTN_SCALE
    p = sink_softmax(s, sink, mask)
    o = jnp.einsum('bkgqs,bskd->bqkgd', p.astype(v.dtype), vv)
    return o.reshape(bsz, t, D_ATTN), kk[:, -wc:], vv[:, -wc:]


def hier_moe(h, wg, bg, we, be, w1, w3, w2):
    bsz, t, d = h.shape
    n = bsz * t
    x = h.reshape(n, d)
    lg = (x @ wg).astype(jnp.float32) + bg.astype(jnp.float32)
    pg = jax.nn.softmax(lg, axis=-1)
    g_top = jnp.argmax(lg, axis=-1)
    le = ((x @ we).astype(jnp.float32) + be.astype(jnp.float32)).reshape(n, N_EXPERT_GROUPS, EXPERTS_PER_GROUP)
    le_g = jnp.take_along_axis(le, g_top[:, None, None], axis=1)[:, 0]
    pe = jax.nn.softmax(le_g, axis=-1)
    vals, idx = lax.top_k(pe, TOP_K)
    vals = vals / jnp.sum(vals, axis=-1, keepdims=True)
    wts = vals * jnp.take_along_axis(pg, g_top[:, None], axis=1)
    eid = g_top[:, None] * EXPERTS_PER_GROUP + idx
    comb = jnp.sum(jax.nn.one_hot(eid, N_EXPERTS, dtype=jnp.float32) * wts[..., None], axis=1)
    a = jnp.einsum('nd,edf->nef', x, w1)
    b = jnp.einsum('nd,edf->nef', x, w3)
    hid = jax.nn.silu(a) * b * comb[..., None].astype(x.dtype)
    y = jnp.einsum('nef,efd->nd', hid, w2)
    return y.reshape(bsz, t, d)


def trunk_layer(x, pos0, conv_buf, h0, pool_buf, cache_k, cache_v, p):
    bsz, t, _ = x.shape
    dt = x.dtype
    h = rmsnorm(x, p['norm1_g'])
    proj = h @ p['w_in']
    u_lru, g_lru, u_pool, q, k, v = jnp.split(proj, SPLIT_POINTS, axis=-1)
    pos = pos0 + jnp.arange(t)
    xc, conv_new = causal_conv(u_lru, conv_buf, p['conv_w'], p['conv_b'])
    hs, h_last = rg_lru(xc, h0, p['lru_wa'], p['lru_ba'], p['lru_wx'], p['lru_bx'], p['lru_lambda'])
    y_lru = hs * jax.nn.gelu(g_lru.astype(jnp.float32))
    y_pool, pool_new = pool_mixer(u_pool, pool_buf, pos0, p['pool_w'], p['pool_scale'])
    q = rope(q.reshape(bsz, t, N_HEADS, HEAD_DIM), pos)
    k = rope(k.reshape(bsz, t, N_KV_HEADS, HEAD_DIM), pos)
    v = v.reshape(bsz, t, N_KV_HEADS, HEAD_DIM)
    sink = p['attn_sinks'].astype(jnp.float32).reshape(N_KV_HEADS, GQ)[..., None, None]
    if cache_k is None:
        y_attn, k_new, v_new = swa_prompt(q, k, v, sink)
    else:
        y_attn, k_new, v_new = swa_sample(q, k, v, cache_k, cache_v, sink, pos0)
    mix = jnp.concatenate([y_lru.astype(dt), y_pool.astype(dt), y_attn.astype(dt)], axis=-1)
    x = x + mix @ p['w_out']
    h2 = rmsnorm(x, p['norm2_g'])
    x = x + hier_moe(h2, p['router_group_w'], p['router_group_b'], p['router_expert_w'], p['router_expert_b'],
                     p['expert_w1'], p['expert_w3'], p['expert_w2'])
    return x, k_new, v_new, h_last.astype(dt), conv_new, pool_new


def setup_inputs(seed: int = 0) -> dict:
    key = jax.random.key(seed)
    ks = jax.random.split(key, 32)
    f32 = jnp.float32

    def nrm(k, shape, s):
        return s * jax.random.normal(k, shape, f32)

    win = min(WINDOW, PAST_LEN)
    a0 = jax.random.uniform(ks[15], (DEPTH, D_LRU), f32, 0.9, 0.999)
    return {
        'x_prompt': nrm(ks[0], (BATCH, SEQ, D_MODEL), 1.0),
        'x_sample': nrm(ks[1], (DEC_BATCH, DEC_SEQ, D_MODEL), 1.0),
        'cache_k': nrm(ks[2], (DEPTH, DEC_BATCH, win, N_KV_HEADS, HEAD_DIM), 1.0),
        'cache_v': nrm(ks[3], (DEPTH, DEC_BATCH, win, N_KV_HEADS, HEAD_DIM), 1.0),
        'state_lru_h': nrm(ks[4], (DEPTH, DEC_BATCH, D_LRU), 0.5),
        'state_conv': nrm(ks[5], (DEPTH, DEC_BATCH, CONV_WIDTH - 1, D_LRU), 1.0),
        'state_pool': nrm(ks[6], (DEPTH, DEC_BATCH, POOL_PAD, D_POOL), 1.0),
        'norm1_g': 1.0 + nrm(ks[7], (DEPTH, D_MODEL), 0.1),
        'w_in': nrm(ks[8], (DEPTH, D_MODEL, D_IN), D_MODEL ** -0.5),
        'conv_w': nrm(ks[9], (DEPTH, CONV_WIDTH, D_LRU), 0.5),
        'conv_b': nrm(ks[10], (DEPTH, D_LRU), 0.01),
        'lru_wa': nrm(ks[11], (DEPTH, LRU_HEADS, LRU_HEAD_DIM, LRU_HEAD_DIM), LRU_HEAD_DIM ** -0.5),
        'lru_ba': nrm(ks[12], (DEPTH, D_LRU), 0.01),
        'lru_wx': nrm(ks[13], (DEPTH, LRU_HEADS, LRU_HEAD_DIM, LRU_HEAD_DIM), LRU_HEAD_DIM ** -0.5),
        'lru_bx': nrm(ks[14], (DEPTH, D_LRU), 0.01),
        'lru_lambda': jnp.log(a0) - jnp.log1p(-a0),
        'pool_w': nrm(ks[16], (DEPTH, POOL_GROUPS, POOL_GROUP_DIM, POOL_GROUP_DIM), POOL_GROUP_DIM ** -0.5),
        'pool_scale': 1.0 + nrm(ks[17], (DEPTH, D_POOL), 0.1),
        'attn_sinks': nrm(ks[18], (DEPTH, N_HEADS), 1.0),
        'w_out': nrm(ks[19], (DEPTH, D_MIX, D_MODEL), D_MIX ** -0.5),
        'norm2_g': 1.0 + nrm(ks[20], (DEPTH, D_MODEL), 0.1),
        'router_group_w': nrm(ks[21], (DEPTH, D_MODEL, N_EXPERT_GROUPS), D_MODEL ** -0.5),
        'router_group_b': nrm(ks[22], (DEPTH, N_EXPERT_GROUPS), 0.01),
        'router_expert_w': nrm(ks[23], (DEPTH, D_MODEL, N_EXPERTS), D_MODEL ** -0.5),
        'router_expert_b': nrm(ks[24], (DEPTH, N_EXPERTS), 0.01),
        'expert_w1': nrm(ks[25], (DEPTH, N_EXPERTS, D_MODEL, D_EXPERT), D_MODEL ** -0.5),
        'expert_w3': nrm(ks[26], (DEPTH, N_EXPERTS, D_MODEL, D_EXPERT), D_MODEL ** -0.5),
        'expert_w2': nrm(ks[27], (DEPTH, N_EXPERTS, D_EXPERT, D_MODEL), D_EXPERT ** -0.5),
        'final_norm_g': 1.0 + nrm(ks[28], (D_MODEL,), 0.1),
    }


def reference(x_prompt, x_sample, cache_k, cache_v, state_lru_h, state_conv, state_pool,
              norm1_g, w_in, conv_w, conv_b, lru_wa, lru_ba, lru_wx, lru_bx, lru_lambda,
              pool_w, pool_scale, attn_sinks, w_out, norm2_g,
              router_group_w, router_group_b, router_expert_w, router_expert_b,
              expert_w1, expert_w3, expert_w2, final_norm_g):
    xp = x_prompt
    xs = x_sample
    bp = x_prompt.shape[0]
    pk, pv, ph, pc, pp = [], [], [], [], []
    sk, sv, sh, sc, sp = [], [], [], [], []
    for l in range(DEPTH):
        p = {
            'norm1_g': norm1_g[l], 'w_in': w_in[l], 'conv_w': conv_w[l], 'conv_b': conv_b[l],
            'lru_wa': lru_wa[l], 'lru_ba': lru_ba[l], 'lru_wx': lru_wx[l], 'lru_bx': lru_bx[l],
            'lru_lambda': lru_lambda[l], 'pool_w': pool_w[l], 'pool_scale': pool_scale[l],
            'attn_sinks': attn_sinks[l], 'w_out': w_out[l], 'norm2_g': norm2_g[l],
            'router_group_w': router_group_w[l], 'router_group_b': router_group_b[l],
            'router_expert_w': router_expert_w[l], 'router_expert_b': router_expert_b[l],
            'expert_w1': expert_w1[l], 'expert_w3': expert_w3[l], 'expert_w2': expert_w2[l],
        }
        conv0 = jnp.zeros((bp, CONV_WIDTH - 1, D_LRU), xp.dtype)
        h00 = jnp.zeros((bp, D_LRU), xp.dtype)
        pool0 = jnp.zeros((bp, POOL_PAD, D_POOL), xp.dtype)
        xp, k1, v1, h1, c1, q1 = trunk_layer(xp, 0, conv0, h00, pool0, None, None, p)
        pk.append(k1); pv.append(v1); ph.append(h1); pc.append(c1); pp.append(q1)
        xs, k2, v2, h2, c2, q2 = trunk_layer(xs, PAST_LEN, state_conv[l], state_lru_h[l], state_pool[l],
                                             cache_k[l], cache_v[l], p)
        sk.append(k2); sv.append(v2); sh.append(h2); sc.append(c2); sp.append(q2)
    y_prompt = rmsnorm(xp, final_norm_g)
    y_sample = rmsnorm(xs, final_norm_g)
    return (y_prompt, y_sample,
            jnp.stack(pk), jnp.stack(pv), jnp.stack(ph), jnp.stack(pc), jnp.stack(pp),
            jnp.stack(sk), jnp.stack(sv), jnp.stack(sh), jnp.stack(sc), jnp.stack(sp))
```

```python
import functools

import jax
import jax.numpy as jnp
from jax import lax
from jax.experimental import pallas as pl
from jax.experimental.pallas import tpu as pltpu

F32 = jnp.float32
BF16 = jnp.bfloat16
I32 = jnp.int32

LRU_HEADS = 8
CONV_WIDTH = 4
LRU_C = 8.0
POOL_WINDOWS = (2, 4, 8, 16)
POOL_PAD = max(POOL_WINDOWS) - 1
HEAD_DIM = 64
N_KV_HEADS = 4
WINDOW = 128
ROPE_THETA = 10000.0
ATTN_SCALE = HEAD_DIM ** -0.5
N_EXPERT_GROUPS = 4
EXPERTS_PER_GROUP = 4
N_EXPERTS = N_EXPERT_GROUPS * EXPERTS_PER_GROUP
RMS_EPS = 1e-6
PAST_LEN = 16384

LANES = 128
HALO = 16
ROW_BLOCK = 256
MOE_CHUNK = 128
PAIR_SLOT_A = (0, 0, 0, 1, 1, 3)
PAIR_SLOT_B = (1, 2, 3, 3, 2, 2)
N_PAIRS = len(PAIR_SLOT_A)
N_CLASSES = N_EXPERT_GROUPS * N_PAIRS
VMEM_LIMIT = 52 * 1024 * 1024


def _rms(x, g):
    return (x * lax.rsqrt(jnp.mean(x * x, axis=-1, keepdims=True) + RMS_EPS)) * g


def _in_proj_kernel(x_ref, g_ref, w_ref, o_ref):
    h = _rms(x_ref[...], g_ref[...])
    o_ref[...] = jnp.dot(h.astype(BF16), w_ref[...], preferred_element_type=F32)


def _in_proj(x, g, w, n_rows):
    d = x.shape[1]
    d_in = w.shape[1]
    return pl.pallas_call(
        _in_proj_kernel,
        grid=(n_rows // ROW_BLOCK,),
        in_specs=[
            pl.BlockSpec((ROW_BLOCK, d), lambda i: (i, 0)),
            pl.BlockSpec((1, d), lambda i: (0, 0)),
            pl.BlockSpec((d, d_in), lambda i: (0, 0)),
        ],
        out_specs=pl.BlockSpec((ROW_BLOCK, d_in), lambda i: (i, 0)),
        out_shape=jax.ShapeDtypeStruct((n_rows, d_in), F32),
        compiler_params=pltpu.CompilerParams(
            dimension_semantics=("arbitrary",), vmem_limit_bytes=VMEM_LIMIT),
        name="in_proj",
    )(x, g, w)


def _scan_linear(a, b):
    t = a.shape[0]
    row = lax.broadcasted_iota(I32, a.shape, 0)
    d = 1
    while d < t:
        a_sh = pltpu.roll(a, d, axis=0)
        b_sh = pltpu.roll(b, d, axis=0)
        m = row >= d
        b = jnp.where(m, a * b_sh + b, b)
        a = jnp.where(m, a * a_sh, a)
        d *= 2
    return a, b


def _lru_chunk(xc, g, h_prev, wg, ba, bx, sp):
    pre = jnp.dot(xc.astype(BF16), wg, preferred_element_type=F32)
    r = jax.nn.sigmoid(pre[:, :LANES] + ba)
    ig = jax.nn.sigmoid(pre[:, LANES:] + bx)
    log_a = (-LRU_C * r) * sp
    a = jnp.exp(log_a)
    bterm = jnp.sqrt(-jnp.tanh(log_a) * (a * a + 1.0)) * ig * xc
    a_cum, h0 = _scan_linear(a, bterm)
    hs = a_cum * h_prev + h0
    return hs * jax.nn.gelu(g), hs


def _rope(x, cos, sin_signed):
    n = x.shape[1] // LANES
    lane = lax.broadcasted_iota(I32, (x.shape[0], LANES), 1)
    first = (lane % HEAD_DIM) < (HEAD_DIM // 2)
    outs = []
    for c in range(n):
        xc = x[:, c * LANES:(c + 1) * LANES]
        swapped = jnp.where(first, pltpu.roll(xc, LANES - HEAD_DIM // 2, axis=1),
                            pltpu.roll(xc, HEAD_DIM // 2, axis=1))
        outs.append(xc * cos + swapped * sin_signed)
    return outs


def _attend(q_chunks, k_prev, v_prev, k_own, v_own, sinks, has_prev):
    qb = q_chunks[0].shape[0]
    gq = (len(q_chunks) * 2) // N_KV_HEADS
    qi = lax.broadcasted_iota(I32, (qb, WINDOW), 0)
    kj = lax.broadcasted_iota(I32, (qb, WINDOW), 1)
    mask_prev = kj > qi + jnp.where(has_prev, 0, WINDOW)
    qi2 = lax.broadcasted_iota(I32, (qb, qb), 0)
    kj2 = lax.broadcasted_iota(I32, (qb, qb), 1)
    mask_own = kj2 <= qi2
    nt = (((1,), (1,)), ((), ()))
    outs = []
    for c, qc in enumerate(q_chunks):
        pair = []
        for half in range(2):
            h = 2 * c + half
            kv = h // gq
            qh = qc[:, half * HEAD_DIM:(half + 1) * HEAD_DIM]
            ks = slice(kv * HEAD_DIM, (kv + 1) * HEAD_DIM)
            s1 = lax.dot_general(qh, k_prev[:, ks], nt, preferred_element_type=F32)
            s2 = lax.dot_general(qh, k_own[:, ks], nt, preferred_element_type=F32)
            s1 = jnp.where(mask_prev, s1, -jnp.inf)
            s2 = jnp.where(mask_own, s2, -jnp.inf)
            sink = sinks[:, h:h + 1]
            m = jnp.maximum(jnp.maximum(jnp.max(s1, axis=-1, keepdims=True),
                                        jnp.max(s2, axis=-1, keepdims=True)), sink)
            e1 = jnp.exp(s1 - m)
            e2 = jnp.exp(s2 - m)
            den = (jnp.sum(e1, axis=-1, keepdims=True) + jnp.sum(e2, axis=-1, keepdims=True)
                   + jnp.exp(sink - m))
            o = (jnp.dot(e1.astype(BF16), v_prev[:, ks], preferred_element_type=F32)
                 + jnp.dot(e2.astype(BF16), v_own[:, ks], preferred_element_type=F32))
            pair.append(o / den)
        outs.append(jnp.concatenate(pair, axis=-1))
    return outs


def _mixer_kernel(*refs, tile, qblock, is_prompt, pos0, d_lru, d_pool, d_attn, d_kv):
    (proj_ref, cos_ref, sin_ref, convw_ref, convb_ref, wg_ref, ba_ref, bx_ref, lam_ref,
     poolw_ref, pscale_ref, sink_ref) = refs[:12]
    if is_prompt:
        (mix_ref, ko_ref, vo_ref, ho_ref, co_ref, po_ref,
         lru_ext, pool_ext, hcar, kcar, vcar) = refs[12:]
    else:
        (ck_ref, cv_ref, sh_ref, sc_ref, sp_ref,
         mix_ref, ko_ref, vo_ref, ho_ref, co_ref, po_ref,
         lru_ext, pool_ext) = refs[12:]

    s = pl.program_id(1) if is_prompt else 0
    o_glru = d_lru
    o_pool = 2 * d_lru
    o_q = o_pool + d_pool
    o_k = o_q + d_attn
    o_v = o_k + d_kv

    if is_prompt:
        @pl.when(s == 0)
        def _():
            lru_ext[0:HALO, :] = jnp.zeros((HALO, d_lru), F32)
            pool_ext[0:HALO, :] = jnp.zeros((HALO, d_pool), F32)
            hcar[...] = jnp.zeros_like(hcar)
            kcar[...] = jnp.zeros_like(kcar)
            vcar[...] = jnp.zeros_like(vcar)
        h_prev = hcar[0:1, :]
    else:
        lru_ext[0:HALO, :] = jnp.zeros((HALO, d_lru), F32)
        pool_ext[0:HALO, :] = jnp.zeros((HALO, d_pool), F32)
        lru_ext[HALO - (CONV_WIDTH - 1):HALO, :] = sc_ref[0]
        pool_ext[HALO - POOL_PAD:HALO, :] = sp_ref[0]
        h_prev = sh_ref[0]

    lru_ext[HALO:HALO + tile, :] = proj_ref[:, 0:d_lru]
    pool_ext[HALO:HALO + tile, :] = proj_ref[:, o_pool:o_pool + d_pool]

    xc = convb_ref[...] + convw_ref[0:1, :] * lru_ext[HALO - 3:HALO - 3 + tile, :]
    for j in range(1, CONV_WIDTH):
        xc = xc + convw_ref[j:j + 1, :] * lru_ext[HALO - 3 + j:HALO - 3 + j + tile, :]
    lam = lam_ref[...]
    neg = -lam
    sp_all = jnp.maximum(neg, 0.0) + jnp.log1p(jnp.exp(-jnp.abs(neg)))
    h_last = []
    for c in range(d_lru // LANES):
        cs = slice(c * LANES, (c + 1) * LANES)
        y, hs = _lru_chunk(xc[:, cs], proj_ref[:, o_glru + c * LANES:o_glru + (c + 1) * LANES],
                           h_prev[:, cs], wg_ref[c], ba_ref[:, cs], bx_ref[:, cs], sp_all[:, cs])
        mix_ref[:, cs] = y.astype(mix_ref.dtype)
        h_last.append(hs[tile - 1:tile, :])
    h_last = jnp.concatenate(h_last, axis=-1)

    row = lax.broadcasted_iota(I32, (tile, LANES), 0)
    pos = pos0 + s * tile + row
    for gi, w in enumerate(POOL_WINDOWS):
        cs = slice(gi * LANES, (gi + 1) * LANES)
        e = pool_ext[:, cs]
        acc = e
        step = 1
        while step < w:
            acc = acc + pltpu.roll(acc, step, axis=0)
            step *= 2
        cnt = jnp.minimum(pos + 1, w).astype(F32)
        dlt = acc[HALO:HALO + tile, :] / cnt - e[HALO:HALO + tile, :]
        y = jnp.dot(dlt.astype(BF16), poolw_ref[gi], preferred_element_type=F32) * pscale_ref[:, cs]
        mix_ref[:, d_lru + gi * LANES:d_lru + (gi + 1) * LANES] = y.astype(mix_ref.dtype)

    sinks = sink_ref[...]
    o_attn = d_lru + d_pool
    if is_prompt:
        k_prev = kcar[...]
        v_prev = vcar[...]
    else:
        k_prev = ck_ref[0].astype(BF16)
        v_prev = cv_ref[0].astype(BF16)
    k_rot = None
    for blk in range(tile // qblock):
        rs = slice(blk * qblock, (blk + 1) * qblock)
        cos = cos_ref[rs, :]
        sin = sin_ref[rs, :]
        q_chunks = [(qc * ATTN_SCALE).astype(BF16)
                    for qc in _rope(proj_ref[rs, o_q:o_q + d_attn], cos, sin)]
        k_rot = jnp.concatenate(_rope(proj_ref[rs, o_k:o_k + d_kv], cos, sin), axis=-1)
        v_new = proj_ref[rs, o_v:o_v + d_kv]
        k_own = k_rot.astype(BF16)
        v_own = v_new.astype(BF16)
        has_prev = (s * tile + blk * qblock) > 0 if is_prompt else True
        outs = _attend(q_chunks, k_prev, v_prev, k_own, v_own, sinks, has_prev)
        for c, o in enumerate(outs):
            mix_ref[rs, o_attn + c * LANES:o_attn + (c + 1) * LANES] = o.astype(mix_ref.dtype)
        k_prev, v_prev = k_own, v_own

    if is_prompt:
        lru_ext[0:HALO, :] = lru_ext[tile:tile + HALO, :]
        pool_ext[0:HALO, :] = pool_ext[tile:tile + HALO, :]
        hcar[0:1, :] = h_last
        kcar[...] = k_prev
        vcar[...] = v_prev

        @pl.when(s == pl.num_programs(1) - 1)
        def _():
            ko_ref[0] = k_rot
            vo_ref[0] = proj_ref[tile - qblock:tile, o_v:o_v + d_kv]
            ho_ref[0] = h_last
            co_ref[0] = lru_ext[HALO + tile - (CONV_WIDTH - 1):HALO + tile, :]
            po_ref[0] = pool_ext[HALO + tile - POOL_PAD:HALO + tile, :]
    else:
        ko_ref[0, 0:WINDOW - tile, :] = ck_ref[0, tile:WINDOW, :]
        ko_ref[0, WINDOW - tile:WINDOW, :] = k_rot
        vo_ref[0, 0:WINDOW - tile, :] = cv_ref[0, tile:WINDOW, :]
        vo_ref[0, WINDOW - tile:WINDOW, :] = proj_ref[:, o_v:o_v + d_kv]
        ho_ref[0] = h_last
        co_ref[0] = lru_ext[HALO + tile - (CONV_WIDTH - 1):HALO + tile, :]
        po_ref[0] = pool_ext[HALO + tile - POOL_PAD:HALO + tile, :]


def _layer_consts(p, dims):
    d_lru, d_pool = dims["d_lru"], dims["d_pool"]
    hd = d_lru // LRU_HEADS
    per = LANES // hd
    nchunk = d_lru // LANES

    def blockdiag(w):
        w = w.reshape(nchunk, per, hd, hd)
        eye = jnp.eye(per, dtype=w.dtype)
        return jnp.einsum("cpij,pq->cpiqj", w, eye).reshape(nchunk, LANES, LANES)

    wg = jnp.concatenate([blockdiag(p["lru_wa"]), blockdiag(p["lru_wx"])], axis=-1).astype(BF16)
    return dict(
        convw=p["conv_w"], convb=p["conv_b"].reshape(1, d_lru), wg=wg,
        ba=p["lru_ba"].reshape(1, d_lru), bx=p["lru_bx"].reshape(1, d_lru),
        lam=p["lru_lambda"].reshape(1, d_lru), poolw=p["pool_w"].astype(BF16),
        pscale=p["pool_scale"].reshape(1, d_pool), sinks=p["attn_sinks"].reshape(1, -1))


def _rope_tables(pos):
    half = HEAD_DIM // 2
    inv = ROPE_THETA ** (-jnp.arange(half, dtype=F32) / half)
    ang = pos.astype(F32)[:, None] * inv[None, :]
    cos = jnp.cos(ang)
    sin = jnp.sin(ang)
    cos2 = jnp.concatenate([cos, cos], axis=-1)
    sin2 = jnp.concatenate([-sin, sin], axis=-1)
    reps = LANES // HEAD_DIM
    return jnp.tile(cos2, (1, reps)), jnp.tile(sin2, (1, reps))


CONST_NAMES = ("convw", "convb", "wg", "ba", "bx", "lam", "poolw", "pscale", "sinks")


def _const_specs(consts):
    return [pl.BlockSpec(consts[n].shape, functools.partial(lambda nd, *_: (0,) * nd, consts[n].ndim))
            for n in CONST_NAMES]


def _mixer_prompt(proj, consts, cos, sin, bsz, seq, dims, tile):
    d_lru, d_pool, d_attn, d_kv = dims["d_lru"], dims["d_pool"], dims["d_attn"], dims["d_kv"]
    d_in = proj.shape[1]
    d_mix = d_lru + d_pool + d_attn
    ns = seq // tile
    n_rows = proj.shape[0]
    kern = functools.partial(_mixer_kernel, tile=tile, qblock=WINDOW, is_prompt=True, pos0=0,
                             d_lru=d_lru, d_pool=d_pool, d_attn=d_attn, d_kv=d_kv)
    return pl.pallas_call(
        kern,
        grid=(bsz, ns),
        in_specs=[
            pl.BlockSpec((tile, d_in), lambda b, s: (b * ns + s, 0)),
            pl.BlockSpec((tile, LANES), lambda b, s: (s, 0)),
            pl.BlockSpec((tile, LANES), lambda b, s: (s, 0)),
        ] + _const_specs(consts),
        out_specs=[
            pl.BlockSpec((tile, d_mix), lambda b, s: (b * ns + s, 0)),
            pl.BlockSpec((1, WINDOW, d_kv), lambda b, s: (b, 0, 0)),
            pl.BlockSpec((1, WINDOW, d_kv), lambda b, s: (b, 0, 0)),
            pl.BlockSpec((1, 1, d_lru), lambda b, s: (b, 0, 0)),
            pl.BlockSpec((1, CONV_WIDTH - 1, d_lru), lambda b, s: (b, 0, 0)),
            pl.BlockSpec((1, POOL_PAD, d_pool), lambda b, s: (b, 0, 0)),
        ],
        out_shape=[
            jax.ShapeDtypeStruct((bsz * seq, d_mix), BF16),
            jax.ShapeDtypeStruct((bsz, WINDOW, d_kv), F32),
            jax.ShapeDtypeStruct((bsz, WINDOW, d_kv), F32),
            jax.ShapeDtypeStruct((bsz, 1, d_lru), F32),
            jax.ShapeDtypeStruct((bsz, CONV_WIDTH - 1, d_lru), F32),
            jax.ShapeDtypeStruct((bsz, POOL_PAD, d_pool), F32),
        ],
        scratch_shapes=[
            pltpu.VMEM((tile + HALO, d_lru), F32),
            pltpu.VMEM((tile + HALO, d_pool), F32),
            pltpu.VMEM((8, d_lru), F32),
            pltpu.VMEM((WINDOW, d_kv), BF16),
            pltpu.VMEM((WINDOW, d_kv), BF16),
        ],
        compiler_params=pltpu.CompilerParams(
            dimension_semantics=("arbitrary", "arbitrary"), vmem_limit_bytes=VMEM_LIMIT),
        name="mixer_prompt",
    )(proj, cos, sin, *[consts[n] for n in CONST_NAMES])


def _mixer_sample(proj, row0, consts, cos, sin, cache_k, cache_v, st_h, st_conv, st_pool, dims):
    d_lru, d_pool, d_attn, d_kv = dims["d_lru"], dims["d_pool"], dims["d_attn"], dims["d_kv"]
    d_in = proj.shape[1]
    d_mix = d_lru + d_pool + d_attn
    db, win = cache_k.shape[0], cache_k.shape[1]
    t = cos.shape[0]
    blk0 = row0 // t
    kern = functools.partial(_mixer_kernel, tile=t, qblock=t, is_prompt=False, pos0=PAST_LEN,
                             d_lru=d_lru, d_pool=d_pool, d_attn=d_attn, d_kv=d_kv)
    return pl.pallas_call(
        kern,
        grid=(db,),
        in_specs=[
            pl.BlockSpec((t, d_in), lambda b: (blk0 + b, 0)),
            pl.BlockSpec((t, LANES), lambda b: (0, 0)),
            pl.BlockSpec((t, LANES), lambda b: (0, 0)),
        ] + _const_specs(consts) + [
            pl.BlockSpec((1, win, d_kv), lambda b: (b, 0, 0)),
            pl.BlockSpec((1, win, d_kv), lambda b: (b, 0, 0)),
            pl.BlockSpec((1, 1, d_lru), lambda b: (b, 0, 0)),
            pl.BlockSpec((1, CONV_WIDTH - 1, d_lru), lambda b: (b, 0, 0)),
            pl.BlockSpec((1, POOL_PAD, d_pool), lambda b: (b, 0, 0)),
        ],
        out_specs=[
            pl.BlockSpec((t, d_mix), lambda b: (b, 0)),
            pl.BlockSpec((1, win, d_kv), lambda b: (b, 0, 0)),
            pl.BlockSpec((1, win, d_kv), lambda b: (b, 0, 0)),
            pl.BlockSpec((1, 1, d_lru), lambda b: (b, 0, 0)),
            pl.BlockSpec((1, CONV_WIDTH - 1, d_lru), lambda b: (b, 0, 0)),
            pl.BlockSpec((1, POOL_PAD, d_pool), lambda b: (b, 0, 0)),
        ],
        out_shape=[
            jax.ShapeDtypeStruct((db * t, d_mix), F32),
            jax.ShapeDtypeStruct((db, win, d_kv), F32),
            jax.ShapeDtypeStruct((db, win, d_kv), F32),
            jax.ShapeDtypeStruct((db, 1, d_lru), F32),
            jax.ShapeDtypeStruct((db, CONV_WIDTH - 1, d_lru), F32),
            jax.ShapeDtypeStruct((db, POOL_PAD, d_pool), F32),
        ],
        scratch_shapes=[
            pltpu.VMEM((t + HALO, d_lru), F32),
            pltpu.VMEM((t + HALO, d_pool), F32),
        ],
        compiler_params=pltpu.CompilerParams(
            dimension_semantics=("arbitrary",), vmem_limit_bytes=VMEM_LIMIT),
        name="mixer_sample",
    )(proj, cos, sin, *[consts[n] for n in CONST_NAMES], cache_k, cache_v, st_h, st_conv, st_pool)


def _route(logits):
    t = logits.shape[0]
    lane = lax.broadcasted_iota(I32, (t, LANES), 1)
    lane_f = lane.astype(F32)
    ninf = -jnp.inf
    big = float(LANES)
    is_g = lane < N_EXPERT_GROUPS
    lg = jnp.where(is_g, logits, ninf)
    mg = jnp.max(lg, axis=-1, keepdims=True)
    g_top = jnp.min(jnp.where(lg == mg, lane_f, big), axis=-1, keepdims=True).astype(I32)
    pg_top = 1.0 / jnp.sum(jnp.exp(lg - mg), axis=-1, keepdims=True)
    base = N_EXPERT_GROUPS + EXPERTS_PER_GROUP * g_top
    in_grp = (lane >= base) & (lane < base + EXPERTS_PER_GROUP)
    le = jnp.where(in_grp, logits, ninf)
    m1 = jnp.max(le, axis=-1, keepdims=True)
    i1 = jnp.min(jnp.where(le == m1, lane_f, big), axis=-1, keepdims=True).astype(I32)
    le2 = jnp.where(lane == i1, ninf, le)
    m2 = jnp.max(le2, axis=-1, keepdims=True)
    i2 = jnp.min(jnp.where((le2 == m2) & in_grp & (lane != i1), lane_f, big),
                 axis=-1, keepdims=True).astype(I32)
    se = jnp.sum(jnp.exp(le - m1), axis=-1, keepdims=True)
    p1 = 1.0 / se
    p2 = jnp.exp(m2 - m1) / se
    tot = p1 + p2
    w1 = (p1 / tot) * pg_top
    w2 = (p2 / tot) * pg_top
    a = i1 - base
    b = i2 - base
    lo = jnp.minimum(a, b)
    hi = jnp.maximum(a, b)
    w_lo = jnp.where(a < b, w1, w2)
    w_hi = jnp.where(a < b, w2, w1)
    pid = jnp.where(lo == 0, hi - 1, jnp.where(lo == 1, jnp.where(hi == 3, 3, 4), 5))
    swap = pid == 5
    w_a = jnp.where(swap, w_hi, w_lo)
    w_b = jnp.where(swap, w_lo, w_hi)
    cls = (g_top * N_PAIRS + pid).astype(F32)
    return jnp.where(lane == 0, w_a, jnp.where(lane == 1, w_b, jnp.where(lane == 2, cls, 0.0)))


def _out_proj_kernel(mixp_ref, mixs_ref, x_ref, w_ref, g_ref, wr_ref, br_ref, o_ref, *, n_prompt_blocks, d):
    i = pl.program_id(0)

    def body(mix):
        xres = x_ref[...] + jnp.dot(mix, w_ref[...], preferred_element_type=F32)
        h2 = _rms(xres, g_ref[...])
        logits = jnp.dot(h2.astype(BF16), wr_ref[...], preferred_element_type=F32) + br_ref[...]
        o_ref[:, 0:d] = xres
        o_ref[:, d:d + LANES] = _route(logits)

    @pl.when(i < n_prompt_blocks)
    def _():
        body(mixp_ref[...])

    @pl.when(i >= n_prompt_blocks)
    def _():
        body(mixs_ref[...].astype(BF16))


def _out_proj(mix_p, mix_s, x, w, g, wr, br, n_rows):
    d = x.shape[1]
    d_mix = w.shape[0]
    npb = mix_p.shape[0] // ROW_BLOCK
    kern = functools.partial(_out_proj_kernel, n_prompt_blocks=npb, d=d)
    return pl.pallas_call(
        kern,
        grid=(n_rows // ROW_BLOCK,),
        in_specs=[
            pl.BlockSpec((ROW_BLOCK, d_mix), lambda i: (jnp.minimum(i, npb - 1), 0)),
            pl.BlockSpec((ROW_BLOCK, d_mix), lambda i: (jnp.maximum(i - npb, 0), 0)),
            pl.BlockSpec((ROW_BLOCK, d), lambda i: (i, 0)),
            pl.BlockSpec((d_mix, d), lambda i: (0, 0)),
            pl.BlockSpec((1, d), lambda i: (0, 0)),
            pl.BlockSpec((d, LANES), lambda i: (0, 0)),
            pl.BlockSpec((1, LANES), lambda i: (0, 0)),
        ],
        out_specs=pl.BlockSpec((ROW_BLOCK, d + LANES), lambda i: (i, 0)),
        out_shape=jax.ShapeDtypeStruct((n_rows, d + LANES), F32),
        compiler_params=pltpu.CompilerParams(
            dimension_semantics=("arbitrary",), vmem_limit_bytes=VMEM_LIMIT),
        name="out_proj",
    )(mix_p, mix_s, x, w, g, wr, br)


def _moe_kernel(inv_ref, dst_ref, ea_ref, eb_ref, nch_ref,
                xr_hbm, g_ref, w1a_ref, w3a_ref, w2a_ref, w1b_ref, w3b_ref, w2b_ref,
                xo_hbm, xbuf, obuf, gsem, ssem, *, d):
    del ea_ref, eb_ref
    m = MOE_CHUNK
    c = pl.program_id(0)
    n = nch_ref[0]
    slot = c % 2

    def gather_rows(chunk, sl):
        def body(j, carry):
            tok = inv_ref[chunk * m + j]
            pltpu.make_async_copy(xr_hbm.at[pl.ds(tok, 1)], xbuf.at[sl, pl.ds(j, 1)], gsem.at[sl]).start()
            return carry
        lax.fori_loop(0, m, body, 0)

    def wait_gather(sl):
        pltpu.make_async_copy(xr_hbm.at[pl.ds(0, m)], xbuf.at[sl], gsem.at[sl]).wait()

    def wait_scatter(sl):
        pltpu.make_async_copy(obuf.at[sl], xo_hbm.at[pl.ds(0, m)], ssem.at[sl]).wait()

    @pl.when(c == 0)
    def _():
        obuf[1] = jnp.zeros((m, d), F32)
        for k in range(2):
            pad = pltpu.make_async_copy(obuf.at[1], xo_hbm.at[pl.ds(xo_hbm.shape[0] - (k + 1) * m, m)],
                                        ssem.at[1])
            pad.start()
            pad.wait()

    @pl.when((c == 0) & (n > 0))
    def _():
        gather_rows(0, 0)

    @pl.when(c + 1 < n)
    def _():
        gather_rows(c + 1, 1 - slot)

    @pl.when(c < n)
    def _():
        wait_gather(slot)

        @pl.when(c >= 2)
        def _():
            wait_scatter(slot)

        xin = xbuf[slot]
        x = xin[:, 0:d]
        w_a = xin[:, d:d + 1]
        w_b = xin[:, d + 1:d + 2]
        h = _rms(x, g_ref[...]).astype(BF16)
        ha = jnp.dot(h, w1a_ref[0], preferred_element_type=F32)
        hb = jnp.dot(h, w3a_ref[0], preferred_element_type=F32)
        hid_a = (jax.nn.silu(ha) * hb * w_a).astype(BF16)
        ha2 = jnp.dot(h, w1b_ref[0], preferred_element_type=F32)
        hb2 = jnp.dot(h, w3b_ref[0], preferred_element_type=F32)
        hid_b = (jax.nn.silu(ha2) * hb2 * w_b).astype(BF16)
        y = (jnp.dot(hid_a, w2a_ref[0], preferred_element_type=F32)
             + jnp.dot(hid_b, w2b_ref[0], preferred_element_type=F32))
        obuf[slot] = x + y

        def body(j, carry):
            r = dst_ref[c * m + j]
            pltpu.make_async_copy(obuf.at[slot, pl.ds(j, 1)], xo_hbm.at[pl.ds(r, 1)], ssem.at[slot]).start()
            return carry
        lax.fori_loop(0, m, body, 0)

        @pl.when(c == n - 1)
        def _():
            wait_scatter(slot)

            @pl.when(c >= 1)
            def _():
                wait_scatter(1 - slot)


def _moe(xr, g, w1, w3, w2, tables, n_rows, c_max):
    d = g.shape[1]
    f = w1.shape[2]
    m = MOE_CHUNK
    inv, dst, ea, eb, nch = tables
    kern = functools.partial(_moe_kernel, d=d)
    grid_spec = pltpu.PrefetchScalarGridSpec(
        num_scalar_prefetch=5,
        grid=(c_max,),
        in_specs=[
            pl.BlockSpec(memory_space=pl.ANY),
            pl.BlockSpec((1, d), lambda c, inv, dst, ea, eb, nch: (0, 0)),
            pl.BlockSpec((1, d, f), lambda c, inv, dst, ea, eb, nch: (ea[c], 0, 0)),
            pl.BlockSpec((1, d, f), lambda c, inv, dst, ea, eb, nch: (ea[c], 0, 0)),
            pl.BlockSpec((1, f, d), lambda c, inv, dst, ea, eb, nch: (ea[c], 0, 0)),
            pl.BlockSpec((1, d, f), lambda c, inv, dst, ea, eb, nch: (eb[c], 0, 0)),
            pl.BlockSpec((1, d, f), lambda c, inv, dst, ea, eb, nch: (eb[c], 0, 0)),
            pl.BlockSpec((1, f, d), lambda c, inv, dst, ea, eb, nch: (eb[c], 0, 0)),
        ],
        out_specs=pl.BlockSpec(memory_space=pl.ANY),
        scratch_shapes=[
            pltpu.VMEM((2, m, d + LANES), F32),
            pltpu.VMEM((2, m, d), F32),
            pltpu.SemaphoreType.DMA((2,)),
            pltpu.SemaphoreType.DMA((2,)),
        ],
    )
    return pl.pallas_call(
        kern,
        grid_spec=grid_spec,
        out_shape=jax.ShapeDtypeStruct((n_rows + 2 * m, d), F32),
        compiler_params=pltpu.CompilerParams(
            dimension_semantics=("arbitrary",), vmem_limit_bytes=VMEM_LIMIT),
        name="moe",
    )(inv, dst, ea, eb, nch, xr, g, w1, w3, w2, w1, w3, w2)


def _route_tables(cls, n_rows, c_max):
    m = MOE_CHUNK
    onehot = (cls[:, None] == jnp.arange(N_CLASSES, dtype=I32)[None, :]).astype(I32)
    csum = jnp.cumsum(onehot, axis=0)
    rank = jnp.sum(onehot * csum, axis=1) - 1
    counts = csum[-1]
    nch_c = (counts + m - 1) // m
    ch_end = jnp.cumsum(nch_c)
    ch_off = ch_end - nch_c
    pos = ch_off[cls] * m + rank
    tok = jnp.arange(n_rows, dtype=I32)
    p = jnp.arange(c_max * m, dtype=I32)
    inv = jnp.zeros((c_max * m,), I32).at[pos].set(tok)
    trash = n_rows + ((p // m) % 2) * m + (p % m)
    dst = trash.at[pos].set(tok)
    n_chunks = ch_end[-1]
    chunk = jnp.minimum(jnp.arange(c_max, dtype=I32), jnp.maximum(n_chunks - 1, 0))
    ccls = jnp.sum((chunk[:, None] >= ch_end[None, :]).astype(I32), axis=1)
    ccls = jnp.minimum(ccls, N_CLASSES - 1)
    grp = ccls // N_PAIRS
    pid = ccls % N_PAIRS
    ea = grp * EXPERTS_PER_GROUP + jnp.asarray(PAIR_SLOT_A, I32)[pid]
    eb = grp * EXPERTS_PER_GROUP + jnp.asarray(PAIR_SLOT_B, I32)[pid]
    return inv, dst, ea, eb, n_chunks.reshape(1)


def _final_norm_kernel(x_ref, g_ref, op_ref, os_ref, *, n_prompt_blocks):
    i = pl.program_id(0)
    y = _rms(x_ref[...], g_ref[...])

    @pl.when(i < n_prompt_blocks)
    def _():
        op_ref[...] = y

    @pl.when(i >= n_prompt_blocks)
    def _():
        os_ref[...] = y


def _final_norm(x, g, n_prompt, n_sample):
    d = x.shape[1]
    npb = n_prompt // ROW_BLOCK
    nsb = n_sample // ROW_BLOCK
    kern = functools.partial(_final_norm_kernel, n_prompt_blocks=npb)
    return pl.pallas_call(
        kern,
        grid=(npb + nsb,),
        in_specs=[pl.BlockSpec((ROW_BLOCK, d), lambda i: (i, 0)),
                  pl.BlockSpec((1, d), lambda i: (0, 0))],
        out_specs=[pl.BlockSpec((ROW_BLOCK, d), lambda i: (jnp.minimum(i, npb - 1), 0)),
                   pl.BlockSpec((ROW_BLOCK, d), lambda i: (jnp.maximum(i - npb, 0), 0))],
        out_shape=[jax.ShapeDtypeStruct((n_prompt, d), F32),
                   jax.ShapeDtypeStruct((n_sample, d), F32)],
        compiler_params=pltpu.CompilerParams(dimension_semantics=("arbitrary",)),
        name="final_norm",
    )(x, g)


def kernel(x_prompt, x_sample, cache_k, cache_v, state_lru_h, state_conv, state_pool, norm1_g, w_in, conv_w, conv_b, lru_wa, lru_ba, lru_wx, lru_bx, lru_lambda, pool_w, pool_scale, attn_sinks, w_out, norm2_g, router_group_w, router_group_b, router_expert_w, router_expert_b, expert_w1, expert_w3, expert_w2, final_norm_g):
    bsz, seq, d = x_prompt.shape
    db, ds, _ = x_sample.shape
    depth = w_in.shape[0]
    win = cache_k.shape[2]
    d_lru = lru_lambda.shape[1]
    d_pool = pool_scale.shape[1]
    d_kv = cache_k.shape[3] * cache_k.shape[4]
    d_attn = attn_sinks.shape[1] * HEAD_DIM
    dims = dict(d_lru=d_lru, d_pool=d_pool, d_attn=d_attn, d_kv=d_kv)
    n_prompt = bsz * seq
    n_sample = db * ds
    n_rows = n_prompt + n_sample
    assert n_prompt % ROW_BLOCK == 0 and n_sample % ROW_BLOCK == 0
    assert win == WINDOW and ds <= 8 and WINDOW % ds == 0
    tile = 256 if seq % 256 == 0 else WINDOW
    c_max = -(-n_rows // MOE_CHUNK) + N_CLASSES

    cos_p, sin_p = _rope_tables(jnp.arange(seq))
    cos_s, sin_s = _rope_tables(PAST_LEN + jnp.arange(ds))

    x = jnp.concatenate([x_prompt.reshape(n_prompt, d), x_sample.reshape(n_sample, d)], axis=0)
    ck = cache_k.reshape(depth, db, win, d_kv)
    cv = cache_v.reshape(depth, db, win, d_kv)
    outs = [[] for _ in range(10)]
    for l in range(depth):
        p = dict(conv_w=conv_w[l], conv_b=conv_b[l], lru_wa=lru_wa[l], lru_ba=lru_ba[l],
                 lru_wx=lru_wx[l], lru_bx=lru_bx[l], lru_lambda=lru_lambda[l], pool_w=pool_w[l],
                 pool_scale=pool_scale[l], attn_sinks=attn_sinks[l])
        consts = _layer_consts(p, dims)
        proj = _in_proj(x, norm1_g[l].reshape(1, d), w_in[l].astype(BF16), n_rows)
        mix_p, pk, pv, ph, pc, pp = _mixer_prompt(proj, consts, cos_p, sin_p, bsz, seq, dims, tile)
        mix_s, sk, sv, sh, sc, sp = _mixer_sample(
            proj, n_prompt, consts, cos_s, sin_s, ck[l], cv[l],
            state_lru_h[l].reshape(db, 1, d_lru), state_conv[l], state_pool[l], dims)
        wr = jnp.concatenate([router_group_w[l], router_expert_w[l]], axis=1)
        wr = jnp.pad(wr, ((0, 0), (0, LANES - wr.shape[1]))).astype(BF16)
        br = jnp.concatenate([router_group_b[l], router_expert_b[l]])
        br = jnp.pad(br, (0, LANES - br.shape[0])).reshape(1, LANES)
        xr = _out_proj(mix_p, mix_s, x, w_out[l].astype(BF16), norm2_g[l].reshape(1, d), wr, br, n_rows)
        cls = xr[:, d + 2].astype(I32)
        tables = _route_tables(cls, n_rows, c_max)
        x = _moe(xr, norm2_g[l].reshape(1, d), expert_w1[l].astype(BF16), expert_w3[l].astype(BF16),
                 expert_w2[l].astype(BF16), tables, n_rows, c_max)
        nkv = cache_k.shape[3]
        for lst, val in zip(outs, (pk.reshape(bsz, WINDOW, nkv, HEAD_DIM), pv.reshape(bsz, WINDOW, nkv, HEAD_DIM),
                                   ph.reshape(bsz, d_lru), pc, pp,
                                   sk.reshape(db, win, nkv, HEAD_DIM), sv.reshape(db, win, nkv, HEAD_DIM),
                                   sh.reshape(db, d_lru), sc, sp)):
            lst.append(val)
    y_p, y_s = _final_norm(x, final_norm_g.reshape(1, d), n_prompt, n_sample)
    return (y_p.reshape(bsz, seq, d), y_s.reshape(db, ds, d)) + tuple(jnp.stack(o) for o in outs)
```

```python
import functools

import jax
import jax.numpy as jnp
from jax import lax
from jax.experimental import pallas as pl
from jax.experimental.pallas import tpu as pltpu

F32 = jnp.float32
BF16 = jnp.bfloat16
I32 = jnp.int32

LRU_HEADS = 8
CONV_WIDTH = 4
LRU_C = 8.0
POOL_WINDOWS = (2, 4, 8, 16)
POOL_PAD = max(POOL_WINDOWS) - 1
HEAD_DIM = 64
N_KV_HEADS = 4
WINDOW = 128
ROPE_THETA = 10000.0
ATTN_SCALE = HEAD_DIM ** -0.5
N_EXPERT_GROUPS = 4
EXPERTS_PER_GROUP = 4
N_EXPERTS = N_EXPERT_GROUPS * EXPERTS_PER_GROUP
RMS_EPS = 1e-6
PAST_LEN = 16384

LANES = 128
HALF = LANES // 2
HALO = 16
ROW_BLOCK = 256
MOE_CHUNK = 128
PAIR_SLOT_A = (0, 0, 0, 1, 1, 3)
PAIR_SLOT_B = (1, 2, 3, 3, 2, 2)
N_PAIRS = len(PAIR_SLOT_A)
N_CLASSES = N_EXPERT_GROUPS * N_PAIRS
VMEM_LIMIT = 52 * 1024 * 1024


def _rms(x, g):
    return (x * lax.rsqrt(jnp.mean(x * x, axis=-1, keepdims=True) + RMS_EPS)) * g


def _load_weight_bf16(w_hbm, w_bf, stage, sem):
    rows = stage.shape[1]
    n = w_hbm.shape[0] // rows

    def copy(i, sl):
        return pltpu.make_async_copy(w_hbm.at[pl.ds(i * rows, rows)], stage.at[sl], sem.at[sl])

    copy(0, 0).start()
    for i in range(n):
        sl = i % 2
        if i + 1 < n:
            copy(i + 1, 1 - sl).start()
        copy(i, sl).wait()
        w_bf[i * rows:(i + 1) * rows, :] = stage[sl].astype(BF16)


def _pick_rows(i, n_main_blocks, main_ref, tail_ref):
    return jnp.where(i < n_main_blocks, main_ref[...], tail_ref[...])


def _in_proj_kernel(xm_ref, xt_ref, g_ref, w_hbm, o_ref, w_bf, stage, sem, *, n_main_blocks):
    i = pl.program_id(0)

    @pl.when(i == 0)
    def _():
        _load_weight_bf16(w_hbm, w_bf, stage, sem)

    h = _rms(_pick_rows(i, n_main_blocks, xm_ref, xt_ref), g_ref[...])
    o_ref[...] = jnp.dot(h.astype(BF16), w_bf[...], preferred_element_type=F32)


def _row_specs(d, n_main_blocks, tail_block0):
    return [pl.BlockSpec((ROW_BLOCK, d), lambda i: (jnp.minimum(i, n_main_blocks - 1), 0)),
            pl.BlockSpec((ROW_BLOCK, d), lambda i: (jnp.maximum(i - n_main_blocks, 0) + tail_block0, 0))]


def _in_proj(x_main, x_tail, tail_block0, g, w, n_rows, n_main):
    d = g.shape[1]
    d_in = w.shape[1]
    nmb = n_main // ROW_BLOCK
    kern = functools.partial(_in_proj_kernel, n_main_blocks=nmb)
    return pl.pallas_call(
        kern,
        grid=(n_rows // ROW_BLOCK,),
        in_specs=_row_specs(d, nmb, tail_block0) + [
            pl.BlockSpec((1, d), lambda i: (0, 0)),
            pl.BlockSpec(memory_space=pl.ANY),
        ],
        out_specs=pl.BlockSpec((ROW_BLOCK, d_in), lambda i: (i, 0)),
        out_shape=jax.ShapeDtypeStruct((n_rows, d_in), F32),
        scratch_shapes=[
            pltpu.VMEM((d, d_in), BF16),
            pltpu.VMEM((2, ROW_BLOCK, d_in), F32),
            pltpu.SemaphoreType.DMA((2,)),
        ],
        compiler_params=pltpu.CompilerParams(
            dimension_semantics=("arbitrary",), vmem_limit_bytes=VMEM_LIMIT),
        name="in_proj",
    )(x_main, x_tail, g, w)


def _scan_linear(a, b):
    t = a.shape[0]
    row = lax.broadcasted_iota(I32, a.shape, 0)
    d = 1
    while d < t:
        a_sh = pltpu.roll(a, d, axis=0)
        b_sh = pltpu.roll(b, d, axis=0)
        m = row >= d
        b = jnp.where(m, a * b_sh + b, b)
        a = jnp.where(m, a * a_sh, a)
        d *= 2
    return a, b


def _lru_chunk(xc, g, h_prev, wg, ba, bx, sp):
    pre = jnp.dot(xc.astype(BF16), wg, preferred_element_type=F32)
    r = jax.nn.sigmoid(pre[:, :LANES] + ba)
    ig = jax.nn.sigmoid(pre[:, LANES:] + bx)
    log_a = (-LRU_C * r) * sp
    a = jnp.exp(log_a)
    bterm = jnp.sqrt(-jnp.tanh(log_a) * (a * a + 1.0)) * ig * xc
    a_cum, h0 = _scan_linear(a, bterm)
    hs = a_cum * h_prev + h0
    return hs * jax.nn.gelu(g), hs


def _rope(x, cos, sin_signed):
    n = x.shape[1] // LANES
    lane = lax.broadcasted_iota(I32, (x.shape[0], LANES), 1)
    first = (lane % HEAD_DIM) < (HEAD_DIM // 2)
    outs = []
    for c in range(n):
        xc = x[:, c * LANES:(c + 1) * LANES]
        swapped = jnp.where(first, pltpu.roll(xc, LANES - HEAD_DIM // 2, axis=1),
                            pltpu.roll(xc, HEAD_DIM // 2, axis=1))
        outs.append(xc * cos + swapped * sin_signed)
    return outs


def _store_head_variants(var_ref, row0, chunks):
    t = chunks[0].shape[0]
    lane = lax.broadcasted_iota(I32, (t, LANES), 1)
    for kc, x in enumerate(chunks):
        swapped = pltpu.roll(x, HALF, axis=1)
        for hh in range(2):
            for p in range(2):
                src = x if p == hh else swapped
                keep = (lane < HALF) if p == 0 else (lane >= HALF)
                var_ref[2 * kc + hh, p, row0:row0 + t, :] = jnp.where(keep, src, 0.0).astype(BF16)


def _attend_block(q_chunks, kvar, vvar, key0, sinks, lim, out_ref, out_rows, out_col0):
    qb = q_chunks[0].shape[0]
    nk = 2 * WINDOW
    gq = (2 * len(q_chunks)) // N_KV_HEADS
    rows = 2 * qb
    qi = lax.broadcasted_iota(I32, (rows, nk), 0) % qb
    kj = lax.broadcasted_iota(I32, (rows, nk), 1)
    valid = ((kj < WINDOW) & (kj > qi + lim)) | ((kj >= WINDOW) & (kj - WINDOW <= qi))
    top = lax.broadcasted_iota(I32, (rows, 1), 0) < qb
    lane = lax.broadcasted_iota(I32, (nk, LANES), 1)
    ones_lo = jnp.where(lane < HALF, 1.0, 0.0).astype(BF16)
    ones_hi = jnp.where(lane >= HALF, 1.0, 0.0).astype(BF16)
    lane_o = lax.broadcasted_iota(I32, (rows, LANES), 1)
    nt = (((1,), (1,)), ((), ()))
    for c in range(N_KV_HEADS):
        c0 = c * gq // 2
        qs = jnp.concatenate([q_chunks[c0], q_chunks[c0 + 1]], axis=0)
        es, sink_terms = [], []
        for p in range(2):
            kc = kvar[c, p, key0:key0 + nk, :]
            s = lax.dot_general(qs, kc, nt, preferred_element_type=F32)
            s = jnp.where(valid, s, -jnp.inf)
            h0 = c * gq + p
            sink = jnp.where(top, sinks[:, h0:h0 + 1], sinks[:, h0 + 2:h0 + 3])
            m = jnp.maximum(jnp.max(s, axis=-1, keepdims=True), sink)
            es.append(jnp.exp(s - m).astype(BF16))
            sink_terms.append(jnp.exp(sink - m))
        r0 = jnp.concatenate([vvar[c, 0, key0:key0 + nk, :], ones_lo], axis=1)
        r1 = jnp.concatenate([vvar[c, 1, key0:key0 + nk, :], ones_hi], axis=1)
        od = (jnp.dot(es[0], r0, preferred_element_type=F32)
              + jnp.dot(es[1], r1, preferred_element_type=F32))
        den = od[:, LANES:] + jnp.where(lane_o < HALF, sink_terms[0], sink_terms[1])
        o = (od[:, :LANES] / den).astype(out_ref.dtype)
        out_ref[out_rows, out_col0 + c0 * LANES:out_col0 + (c0 + 1) * LANES] = o[0:qb]
        out_ref[out_rows, out_col0 + (c0 + 1) * LANES:out_col0 + (c0 + 2) * LANES] = o[qb:rows]


def _mixer_kernel(*refs, tile, qblock, is_prompt, pos0, d_lru, d_pool, d_attn, d_kv):
    (proj_ref, cos_ref, sin_ref, convw_ref, convb_ref, wg_ref, ba_ref, bx_ref, lam_ref,
     poolw_ref, pscale_ref, sink_ref) = refs[:12]
    if is_prompt:
        (mix_ref, ko_ref, vo_ref, ho_ref, co_ref, po_ref,
         lru_ext, pool_ext, kvar, vvar, hcar) = refs[12:]
    else:
        (ck_ref, cv_ref, sh_ref, sc_ref, sp_ref,
         mix_ref, ko_ref, vo_ref, ho_ref, co_ref, po_ref,
         lru_ext, pool_ext, kvar, vvar) = refs[12:]

    s = pl.program_id(1) if is_prompt else 0
    o_glru = d_lru
    o_pool = 2 * d_lru
    o_q = o_pool + d_pool
    o_k = o_q + d_attn
    o_v = o_k + d_kv
    n_kc = d_kv // LANES

    if is_prompt:
        @pl.when(s == 0)
        def _():
            lru_ext[0:HALO, :] = jnp.zeros((HALO, d_lru), F32)
            pool_ext[0:HALO, :] = jnp.zeros((HALO, d_pool), F32)
            hcar[...] = jnp.zeros_like(hcar)
            kvar[:, :, 0:WINDOW, :] = jnp.zeros((N_KV_HEADS, 2, WINDOW, LANES), BF16)
            vvar[:, :, 0:WINDOW, :] = jnp.zeros((N_KV_HEADS, 2, WINDOW, LANES), BF16)
        h_prev = hcar[0:1, :]
    else:
        lru_ext[0:HALO, :] = jnp.zeros((HALO, d_lru), F32)
        pool_ext[0:HALO, :] = jnp.zeros((HALO, d_pool), F32)
        lru_ext[HALO - (CONV_WIDTH - 1):HALO, :] = sc_ref[0]
        pool_ext[HALO - POOL_PAD:HALO, :] = sp_ref[0]
        h_prev = sh_ref[0]
        kvar[:, :, WINDOW:2 * WINDOW, :] = jnp.zeros((N_KV_HEADS, 2, WINDOW, LANES), BF16)
        vvar[:, :, WINDOW:2 * WINDOW, :] = jnp.zeros((N_KV_HEADS, 2, WINDOW, LANES), BF16)
        _store_head_variants(kvar, 0, [ck_ref[0, :, c * LANES:(c + 1) * LANES] for c in range(n_kc)])
        _store_head_variants(vvar, 0, [cv_ref[0, :, c * LANES:(c + 1) * LANES] for c in range(n_kc)])

    lru_ext[HALO:HALO + tile, :] = proj_ref[:, 0:d_lru]
    pool_ext[HALO:HALO + tile, :] = proj_ref[:, o_pool:o_pool + d_pool]

    xc = convb_ref[...] + convw_ref[0:1, :] * lru_ext[HALO - 3:HALO - 3 + tile, :]
    for j in range(1, CONV_WIDTH):
        xc = xc + convw_ref[j:j + 1, :] * lru_ext[HALO - 3 + j:HALO - 3 + j + tile, :]
    neg = -lam_ref[...]
    sp_all = jnp.maximum(neg, 0.0) + jnp.log1p(jnp.exp(-jnp.abs(neg)))
    h_last = []
    for c in range(d_lru // LANES):
        cs = slice(c * LANES, (c + 1) * LANES)
        y, hs = _lru_chunk(xc[:, cs], proj_ref[:, o_glru + c * LANES:o_glru + (c + 1) * LANES],
                           h_prev[:, cs], wg_ref[c], ba_ref[:, cs], bx_ref[:, cs], sp_all[:, cs])
        mix_ref[:, cs] = y.astype(mix_ref.dtype)
        h_last.append(hs[tile - 1:tile, :])
    h_last = jnp.concatenate(h_last, axis=-1)

    row = lax.broadcasted_iota(I32, (tile, LANES), 0)
    pos = pos0 + s * tile + row
    for gi, w in enumerate(POOL_WINDOWS):
        cs = slice(gi * LANES, (gi + 1) * LANES)
        e = pool_ext[:, cs]
        acc = e
        step = 1
        while step < w:
            acc = acc + pltpu.roll(acc, step, axis=0)
            step *= 2
        cnt = jnp.minimum(pos + 1, w).astype(F32)
        dlt = acc[HALO:HALO + tile, :] / cnt - e[HALO:HALO + tile, :]
        y = jnp.dot(dlt.astype(BF16), poolw_ref[gi], preferred_element_type=F32) * pscale_ref[:, cs]
        mix_ref[:, d_lru + gi * LANES:d_lru + (gi + 1) * LANES] = y.astype(mix_ref.dtype)

    sinks = sink_ref[...]
    o_attn = d_lru + d_pool
    cos = cos_ref[...]
    sin = sin_ref[...]
    q_chunks = [(qc * ATTN_SCALE).astype(BF16) for qc in _rope(proj_ref[:, o_q:o_q + d_attn], cos, sin)]
    k_rot = _rope(proj_ref[:, o_k:o_k + d_kv], cos, sin)
    _store_head_variants(kvar, WINDOW, k_rot)
    _store_head_variants(vvar, WINDOW, [proj_ref[:, o_v + c * LANES:o_v + (c + 1) * LANES]
                                        for c in range(n_kc)])
    for blk in range(tile // qblock):
        rs = slice(blk * qblock, (blk + 1) * qblock)
        if is_prompt and blk == 0:
            lim = jnp.where(s == 0, WINDOW, 0)
        else:
            lim = 0
        _attend_block([qc[rs] for qc in q_chunks], kvar, vvar, blk * qblock, sinks, lim,
                      mix_ref, rs, o_attn)

    k_last = jnp.concatenate([kc[tile - qblock:tile] for kc in k_rot], axis=-1)
    if is_prompt:
        lru_ext[0:HALO, :] = lru_ext[tile:tile + HALO, :]
        pool_ext[0:HALO, :] = pool_ext[tile:tile + HALO, :]
        hcar[0:1, :] = h_last
        kvar[:, :, 0:WINDOW, :] = kvar[:, :, tile:tile + WINDOW, :]
        vvar[:, :, 0:WINDOW, :] = vvar[:, :, tile:tile + WINDOW, :]

        @pl.when(s == pl.num_programs(1) - 1)
        def _():
            ko_ref[0] = k_last
            vo_ref[0] = proj_ref[tile - qblock:tile, o_v:o_v + d_kv]
            ho_ref[0] = h_last
            co_ref[0] = lru_ext[HALO + tile - (CONV_WIDTH - 1):HALO + tile, :]
            po_ref[0] = pool_ext[HALO + tile - POOL_PAD:HALO + tile, :]
    else:
        ko_ref[0, 0:WINDOW - tile, :] = ck_ref[0, tile:WINDOW, :]
        ko_ref[0, WINDOW - tile:WINDOW, :] = k_last
        vo_ref[0, 0:WINDOW - tile, :] = cv_ref[0, tile:WINDOW, :]
        vo_ref[0, WINDOW - tile:WINDOW, :] = proj_ref[:, o_v:o_v + d_kv]
        ho_ref[0] = h_last
        co_ref[0] = lru_ext[HALO + tile - (CONV_WIDTH - 1):HALO + tile, :]
        po_ref[0] = pool_ext[HALO + tile - POOL_PAD:HALO + tile, :]


def _layer_consts(p, dims):
    d_lru, d_pool = dims["d_lru"], dims["d_pool"]
    hd = d_lru // LRU_HEADS
    per = LANES // hd
    nchunk = d_lru // LANES

    def blockdiag(w):
        w = w.reshape(nchunk, per, hd, hd)
        eye = jnp.eye(per, dtype=w.dtype)
        return jnp.einsum("cpij,pq->cpiqj", w, eye).reshape(nchunk, LANES, LANES)

    wg = jnp.concatenate([blockdiag(p["lru_wa"]), blockdiag(p["lru_wx"])], axis=-1).astype(BF16)
    return dict(
        convw=p["conv_w"], convb=p["conv_b"].reshape(1, d_lru), wg=wg,
        ba=p["lru_ba"].reshape(1, d_lru), bx=p["lru_bx"].reshape(1, d_lru),
        lam=p["lru_lambda"].reshape(1, d_lru), poolw=p["pool_w"].astype(BF16),
        pscale=p["pool_scale"].reshape(1, d_pool), sinks=p["attn_sinks"].reshape(1, -1))


def _rope_tables(pos):
    half = HEAD_DIM // 2
    inv = ROPE_THETA ** (-jnp.arange(half, dtype=F32) / half)
    ang = pos.astype(F32)[:, None] * inv[None, :]
    cos = jnp.cos(ang)
    sin = jnp.sin(ang)
    cos2 = jnp.concatenate([cos, cos], axis=-1)
    sin2 = jnp.concatenate([-sin, sin], axis=-1)
    reps = LANES // HEAD_DIM
    return jnp.tile(cos2, (1, reps)), jnp.tile(sin2, (1, reps))


CONST_NAMES = ("convw", "convb", "wg", "ba", "bx", "lam", "poolw", "pscale", "sinks")


def _const_specs(consts):
    return [pl.BlockSpec(consts[n].shape, functools.partial(lambda nd, *_: (0,) * nd, consts[n].ndim))
            for n in CONST_NAMES]


def _mixer_prompt(proj, consts, cos, sin, bsz, seq, dims, tile):
    d_lru, d_pool, d_attn, d_kv = dims["d_lru"], dims["d_pool"], dims["d_attn"], dims["d_kv"]
    d_in = proj.shape[1]
    d_mix = d_lru + d_pool + d_attn
    ns = seq // tile
    kern = functools.partial(_mixer_kernel, tile=tile, qblock=WINDOW, is_prompt=True, pos0=0,
                             d_lru=d_lru, d_pool=d_pool, d_attn=d_attn, d_kv=d_kv)
    return pl.pallas_call(
        kern,
        grid=(bsz, ns),
        in_specs=[
            pl.BlockSpec((tile, d_in), lambda b, s: (b * ns + s, 0)),
            pl.BlockSpec((tile, LANES), lambda b, s: (s, 0)),
            pl.BlockSpec((tile, LANES), lambda b, s: (s, 0)),
        ] + _const_specs(consts),
        out_specs=[
            pl.BlockSpec((tile, d_mix), lambda b, s: (b * ns + s, 0)),
            pl.BlockSpec((1, WINDOW, d_kv), lambda b, s: (b, 0, 0)),
            pl.BlockSpec((1, WINDOW, d_kv), lambda b, s: (b, 0, 0)),
            pl.BlockSpec((1, 1, d_lru), lambda b, s: (b, 0, 0)),
            pl.BlockSpec((1, CONV_WIDTH - 1, d_lru), lambda b, s: (b, 0, 0)),
            pl.BlockSpec((1, POOL_PAD, d_pool), lambda b, s: (b, 0, 0)),
        ],
        out_shape=[
            jax.ShapeDtypeStruct((bsz * seq, d_mix), BF16),
            jax.ShapeDtypeStruct((bsz, WINDOW, d_kv), F32),
            jax.ShapeDtypeStruct((bsz, WINDOW, d_kv), F32),
            jax.ShapeDtypeStruct((bsz, 1, d_lru), F32),
            jax.ShapeDtypeStruct((bsz, CONV_WIDTH - 1, d_lru), F32),
            jax.ShapeDtypeStruct((bsz, POOL_PAD, d_pool), F32),
        ],
        scratch_shapes=[
            pltpu.VMEM((tile + HALO, d_lru), F32),
            pltpu.VMEM((tile + HALO, d_pool), F32),
            pltpu.VMEM((N_KV_HEADS, 2, WINDOW + tile, LANES), BF16),
            pltpu.VMEM((N_KV_HEADS, 2, WINDOW + tile, LANES), BF16),
            pltpu.VMEM((8, d_lru), F32),
        ],
        compiler_params=pltpu.CompilerParams(
            dimension_semantics=("arbitrary", "arbitrary"), vmem_limit_bytes=VMEM_LIMIT),
        name="mixer_prompt",
    )(proj, cos, sin, *[consts[n] for n in CONST_NAMES])


def _mixer_sample(proj, row0, consts, cos, sin, cache_k, cache_v, st_h, st_conv, st_pool, dims):
    d_lru, d_pool, d_attn, d_kv = dims["d_lru"], dims["d_pool"], dims["d_attn"], dims["d_kv"]
    d_in = proj.shape[1]
    d_mix = d_lru + d_pool + d_attn
    db, win = cache_k.shape[0], cache_k.shape[1]
    t = cos.shape[0]
    blk0 = row0 // t
    kern = functools.partial(_mixer_kernel, tile=t, qblock=t, is_prompt=False, pos0=PAST_LEN,
                             d_lru=d_lru, d_pool=d_pool, d_attn=d_attn, d_kv=d_kv)
    return pl.pallas_call(
        kern,
        grid=(db,),
        in_specs=[
            pl.BlockSpec((t, d_in), lambda b: (blk0 + b, 0)),
            pl.BlockSpec((t, LANES), lambda b: (0, 0)),
            pl.BlockSpec((t, LANES), lambda b: (0, 0)),
        ] + _const_specs(consts) + [
            pl.BlockSpec((1, win, d_kv), lambda b: (b, 0, 0)),
            pl.BlockSpec((1, win, d_kv), lambda b: (b, 0, 0)),
            pl.BlockSpec((1, 1, d_lru), lambda b: (b, 0, 0)),
            pl.BlockSpec((1, CONV_WIDTH - 1, d_lru), lambda b: (b, 0, 0)),
            pl.BlockSpec((1, POOL_PAD, d_pool), lambda b: (b, 0, 0)),
        ],
        out_specs=[
            pl.BlockSpec((t, d_mix), lambda b: (b, 0)),
            pl.BlockSpec((1, win, d_kv), lambda b: (b, 0, 0)),
            pl.BlockSpec((1, win, d_kv), lambda b: (b, 0, 0)),
            pl.BlockSpec((1, 1, d_lru), lambda b: (b, 0, 0)),
            pl.BlockSpec((1, CONV_WIDTH - 1, d_lru), lambda b: (b, 0, 0)),
            pl.BlockSpec((1, POOL_PAD, d_pool), lambda b: (b, 0, 0)),
        ],
        out_shape=[
            jax.ShapeDtypeStruct((db * t, d_mix), F32),
            jax.ShapeDtypeStruct((db, win, d_kv), F32),
            jax.ShapeDtypeStruct((db, win, d_kv), F32),
            jax.ShapeDtypeStruct((db, 1, d_lru), F32),
            jax.ShapeDtypeStruct((db, CONV_WIDTH - 1, d_lru), F32),
            jax.ShapeDtypeStruct((db, POOL_PAD, d_pool), F32),
        ],
        scratch_shapes=[
            pltpu.VMEM((t + HALO, d_lru), F32),
            pltpu.VMEM((t + HALO, d_pool), F32),
            pltpu.VMEM((N_KV_HEADS, 2, 2 * WINDOW, LANES), BF16),
            pltpu.VMEM((N_KV_HEADS, 2, 2 * WINDOW, LANES), BF16),
        ],
        compiler_params=pltpu.CompilerParams(
            dimension_semantics=("arbitrary",), vmem_limit_bytes=VMEM_LIMIT),
        name="mixer_sample",
    )(proj, cos, sin, *[consts[n] for n in CONST_NAMES], cache_k, cache_v, st_h, st_conv, st_pool)


def _route(logits, run_cnt):
    t = logits.shape[0]
    lane = lax.broadcasted_iota(I32, (t, LANES), 1)
    lane_f = lane.astype(F32)
    ninf = -jnp.inf
    big = float(LANES)
    is_g = lane < N_EXPERT_GROUPS
    lg = jnp.where(is_g, logits, ninf)
    mg = jnp.max(lg, axis=-1, keepdims=True)
    g_top = jnp.min(jnp.where(lg == mg, lane_f, big), axis=-1, keepdims=True).astype(I32)
    pg_top = 1.0 / jnp.sum(jnp.exp(lg - mg), axis=-1, keepdims=True)
    base = N_EXPERT_GROUPS + EXPERTS_PER_GROUP * g_top
    in_grp = (lane >= base) & (lane < base + EXPERTS_PER_GROUP)
    le = jnp.where(in_grp, logits, ninf)
    m1 = jnp.max(le, axis=-1, keepdims=True)
    i1 = jnp.min(jnp.where(le == m1, lane_f, big), axis=-1, keepdims=True).astype(I32)
    le2 = jnp.where(lane == i1, ninf, le)
    m2 = jnp.max(le2, axis=-1, keepdims=True)
    i2 = jnp.min(jnp.where((le2 == m2) & in_grp & (lane != i1), lane_f, big),
                 axis=-1, keepdims=True).astype(I32)
    se = jnp.sum(jnp.exp(le - m1), axis=-1, keepdims=True)
    p1 = 1.0 / se
    p2 = jnp.exp(m2 - m1) / se
    tot = p1 + p2
    w1 = (p1 / tot) * pg_top
    w2 = (p2 / tot) * pg_top
    a = i1 - base
    b = i2 - base
    lo = jnp.minimum(a, b)
    hi = jnp.maximum(a, b)
    w_lo = jnp.where(a < b, w1, w2)
    w_hi = jnp.where(a < b, w2, w1)
    pid = jnp.where(lo == 0, hi - 1, jnp.where(lo == 1, jnp.where(hi == 3, 3, 4), 5))
    swap = pid == 5
    w_a = jnp.where(swap, w_hi, w_lo)
    w_b = jnp.where(swap, w_lo, w_hi)
    cls = g_top * N_PAIRS + pid
    onehot = lane == cls
    ti = lax.broadcasted_iota(I32, (t, t), 0)
    tj = lax.broadcasted_iota(I32, (t, t), 1)
    lower = jnp.where(tj <= ti, 1.0, 0.0).astype(BF16)
    prefix = jnp.dot(lower, jnp.where(onehot, 1.0, 0.0).astype(BF16), preferred_element_type=F32)
    rank = jnp.sum(jnp.where(onehot, prefix - 1.0 + run_cnt, 0.0), axis=-1, keepdims=True)
    info = jnp.where(lane == 0, w_a, jnp.where(lane == 1, w_b, jnp.where(
        lane == 2, cls.astype(F32), jnp.where(lane == 3, rank, 0.0))))
    return info, run_cnt + prefix[t - 1:t, :]


def _out_proj_kernel(mixp_ref, mixs_ref, xm_ref, xt_ref, w_hbm, g_ref, wr_ref, br_ref,
                     o_ref, cnt_ref, w_bf, stage, sem, run_cnt, *, n_main_blocks, d):
    i = pl.program_id(0)

    @pl.when(i == 0)
    def _():
        _load_weight_bf16(w_hbm, w_bf, stage, sem)
        run_cnt[...] = jnp.zeros_like(run_cnt)

    mix = jnp.where(i < n_main_blocks, mixp_ref[...], mixs_ref[...].astype(BF16))
    xres = _pick_rows(i, n_main_blocks, xm_ref, xt_ref) + jnp.dot(mix, w_bf[...], preferred_element_type=F32)
    h2 = _rms(xres, g_ref[...])
    logits = jnp.dot(h2.astype(BF16), wr_ref[...], preferred_element_type=F32) + br_ref[...]
    info, cnt = _route(logits, run_cnt[0:1, :])
    run_cnt[0:1, :] = cnt
    o_ref[:, 0:d] = xres
    o_ref[:, d:d + LANES] = info
    cnt_ref[...] = jnp.broadcast_to(cnt, cnt_ref.shape)


def _out_proj(mix_p, mix_s, x_main, x_tail, tail_block0, w, g, wr, br, n_rows):
    d = g.shape[1]
    d_mix = w.shape[0]
    npb = mix_p.shape[0] // ROW_BLOCK
    kern = functools.partial(_out_proj_kernel, n_main_blocks=npb, d=d)
    return pl.pallas_call(
        kern,
        grid=(n_rows // ROW_BLOCK,),
        in_specs=[
            pl.BlockSpec((ROW_BLOCK, d_mix), lambda i: (jnp.minimum(i, npb - 1), 0)),
            pl.BlockSpec((ROW_BLOCK, d_mix), lambda i: (jnp.maximum(i - npb, 0), 0)),
        ] + _row_specs(d, npb, tail_block0) + [
            pl.BlockSpec(memory_space=pl.ANY),
            pl.BlockSpec((1, d), lambda i: (0, 0)),
            pl.BlockSpec((d, LANES), lambda i: (0, 0)),
            pl.BlockSpec((1, LANES), lambda i: (0, 0)),
        ],
        out_specs=[pl.BlockSpec((ROW_BLOCK, d + LANES), lambda i: (i, 0)),
                   pl.BlockSpec((8, LANES), lambda i: (0, 0))],
        out_shape=[jax.ShapeDtypeStruct((n_rows, d + LANES), F32),
                   jax.ShapeDtypeStruct((8, LANES), F32)],
        scratch_shapes=[
            pltpu.VMEM((d_mix, d), BF16),
            pltpu.VMEM((2, ROW_BLOCK, d), F32),
            pltpu.SemaphoreType.DMA((2,)),
            pltpu.VMEM((8, LANES), F32),
        ],
        compiler_params=pltpu.CompilerParams(
            dimension_semantics=("arbitrary",), vmem_limit_bytes=VMEM_LIMIT),
        name="out_proj",
    )(mix_p, mix_s, x_main, x_tail, w, g, wr, br)


def _moe_kernel(cls_ref, rank_ref, off_ref, cnt_ref, ea_ref, eb_ref, chga_ref, chgb_ref, nch_ref,
                xr_hbm, g_ref, w1a_ref, w3a_ref, w2a_ref, w1b_ref, w3b_ref, w2b_ref,
                xo_hbm,
                perm, xbuf, obuf, wa1, wa3, wa2, wb1, wb3, wb2, gsem, ssem, *, d, n_rows):
    del ea_ref, eb_ref
    m = MOE_CHUNK
    c = pl.program_id(0)
    n = nch_ref[0]
    slot = c % 2
    other = 1 - slot

    def start_gather(chunk, sl, j):
        tok = jnp.maximum(perm[(chunk + 1) * m + j], 0)
        pltpu.make_async_copy(xr_hbm.at[pl.ds(tok, 1)], xbuf.at[sl, pl.ds(j, 1)], gsem.at[sl]).start()

    def start_scatter(chunk, sl, j):
        tok = perm[(chunk + 1) * m + j]
        r = jnp.where(tok < 0, n_rows + sl * m + j, tok)
        pltpu.make_async_copy(obuf.at[sl, pl.ds(j, 1)], xo_hbm.at[pl.ds(r, 1)], ssem.at[sl]).start()

    def wait_gather(sl):
        pltpu.make_async_copy(xr_hbm.at[pl.ds(0, m)], xbuf.at[sl], gsem.at[sl]).wait()

    def wait_scatter(sl):
        pltpu.make_async_copy(obuf.at[sl], xo_hbm.at[pl.ds(0, m)], ssem.at[sl]).wait()

    @pl.when(c == 0)
    def _():
        obuf[...] = jnp.zeros_like(obuf)
        pad = pltpu.make_async_copy(obuf.at[0], xo_hbm.at[pl.ds(n_rows, m)], ssem.at[0])
        pad.start()
        pad.wait()

        def fill(lo, hi):
            def body(i, z):
                perm[i] = -1
                return z
            lax.fori_loop(lo, hi, body, 0)

        def first_gather(j, z):
            start_gather(0, 0, j)
            return z

        fill(0, m)
        fill((n + 1) * m, (n + 3) * m)
        for k in range(N_CLASSES):
            base = (off_ref[k] + 1) * m
            cnt = cnt_ref[k]
            fill(base + cnt, base + ((cnt + m - 1) // m) * m)

        def place(t, z):
            perm[(off_ref[cls_ref[t]] + 1) * m + rank_ref[t]] = t
            return z
        lax.fori_loop(0, n_rows, place, 0, unroll=8)
        lax.fori_loop(0, m, first_gather, 0)

    @pl.when(c <= n)
    def _():
        wait_gather(slot)

        @pl.when(c >= 1)
        def _():
            wait_scatter(slot)

        @pl.when(chga_ref[c] == 1)
        def _():
            wa1[...] = w1a_ref[0].astype(BF16)
            wa3[...] = w3a_ref[0].astype(BF16)
            wa2[...] = w2a_ref[0].astype(BF16)

        @pl.when(chgb_ref[c] == 1)
        def _():
            wb1[...] = w1b_ref[0].astype(BF16)
            wb3[...] = w3b_ref[0].astype(BF16)
            wb2[...] = w2b_ref[0].astype(BF16)

        for j in range(m):
            start_gather(c + 1, other, j)
        for j in range(m):
            start_scatter(c - 1, other, j)
        xin = xbuf[slot]
        x = xin[:, 0:d]
        w_a = xin[:, d:d + 1]
        w_b = xin[:, d + 1:d + 2]
        h = _rms(x, g_ref[...]).astype(BF16)
        hid_a = (jax.nn.silu(jnp.dot(h, wa1[...], preferred_element_type=F32))
                 * jnp.dot(h, wa3[...], preferred_element_type=F32) * w_a).astype(BF16)
        hid_b = (jax.nn.silu(jnp.dot(h, wb1[...], preferred_element_type=F32))
                 * jnp.dot(h, wb3[...], preferred_element_type=F32) * w_b).astype(BF16)
        y = (jnp.dot(hid_a, wa2[...], preferred_element_type=F32)
             + jnp.dot(hid_b, wb2[...], preferred_element_type=F32))
        obuf[slot] = x + y

        @pl.when(c == n)
        def _():
            wait_gather(other)
            wait_scatter(other)


def _moe(xr, g, w1, w3, w2, tables, n_rows, c_max):
    d = g.shape[1]
    f = w1.shape[2]
    m = MOE_CHUNK
    kern = functools.partial(_moe_kernel, d=d, n_rows=n_rows)

    def wspec(shape, which):
        return pl.BlockSpec(shape, lambda c, *pref: (pref[which][c], 0, 0))

    grid_spec = pltpu.PrefetchScalarGridSpec(
        num_scalar_prefetch=9,
        grid=(c_max + 1,),
        in_specs=[
            pl.BlockSpec(memory_space=pl.ANY),
            pl.BlockSpec((1, d), lambda c, *pref: (0, 0)),
            wspec((1, d, f), 4), wspec((1, d, f), 4), wspec((1, f, d), 4),
            wspec((1, d, f), 5), wspec((1, d, f), 5), wspec((1, f, d), 5),
        ],
        out_specs=pl.BlockSpec(memory_space=pl.ANY),
        scratch_shapes=[
            pltpu.SMEM(((c_max + 3) * m,), I32),
            pltpu.VMEM((2, m, d + LANES), F32),
            pltpu.VMEM((2, m, d), F32),
            pltpu.VMEM((d, f), BF16), pltpu.VMEM((d, f), BF16), pltpu.VMEM((f, d), BF16),
            pltpu.VMEM((d, f), BF16), pltpu.VMEM((d, f), BF16), pltpu.VMEM((f, d), BF16),
            pltpu.SemaphoreType.DMA((2,)),
            pltpu.SemaphoreType.DMA((2,)),
        ],
    )
    return pl.pallas_call(
        kern,
        grid_spec=grid_spec,
        out_shape=jax.ShapeDtypeStruct((n_rows + 2 * m, d), F32),
        compiler_params=pltpu.CompilerParams(
            dimension_semantics=("arbitrary",), vmem_limit_bytes=VMEM_LIMIT),
        name="moe",
    )(*tables, xr, g, w1, w3, w2, w1, w3, w2)


def _chunk_tables(counts, c_max):
    m = MOE_CHUNK
    nch_c = (counts + m - 1) // m
    ch_end = jnp.cumsum(nch_c)
    ch_off = ch_end - nch_c
    n_chunks = ch_end[-1]
    chunk = jnp.minimum(jnp.arange(c_max + 1, dtype=I32), jnp.maximum(n_chunks - 1, 0))
    ccls = jnp.sum((chunk[:, None] >= ch_end[None, :]).astype(I32), axis=1)
    ccls = jnp.minimum(ccls, N_CLASSES - 1)
    grp = ccls // N_PAIRS
    pid = ccls % N_PAIRS
    ea = grp * EXPERTS_PER_GROUP + jnp.asarray(PAIR_SLOT_A, I32)[pid]
    eb = grp * EXPERTS_PER_GROUP + jnp.asarray(PAIR_SLOT_B, I32)[pid]
    first = jnp.ones((1,), I32)
    chga = jnp.concatenate([first, (ea[1:] != ea[:-1]).astype(I32)])
    chgb = jnp.concatenate([first, (eb[1:] != eb[:-1]).astype(I32)])
    return ch_off.astype(I32), counts, ea, eb, chga, chgb, n_chunks.reshape(1).astype(I32)


def _final_norm_kernel(x_ref, g_ref, op_ref, os_ref, *, n_prompt_blocks):
    i = pl.program_id(0)
    y = _rms(x_ref[...], g_ref[...])

    @pl.when(i < n_prompt_blocks)
    def _():
        op_ref[...] = y

    @pl.when(i >= n_prompt_blocks)
    def _():
        os_ref[...] = y


def _final_norm(x, g, n_prompt, n_sample):
    d = x.shape[1]
    npb = n_prompt // ROW_BLOCK
    nsb = n_sample // ROW_BLOCK
    kern = functools.partial(_final_norm_kernel, n_prompt_blocks=npb)
    return pl.pallas_call(
        kern,
        grid=(npb + nsb,),
        in_specs=[pl.BlockSpec((ROW_BLOCK, d), lambda i: (i, 0)),
                  pl.BlockSpec((1, d), lambda i: (0, 0))],
        out_specs=[pl.BlockSpec((ROW_BLOCK, d), lambda i: (jnp.minimum(i, npb - 1), 0)),
                   pl.BlockSpec((ROW_BLOCK, d), lambda i: (jnp.maximum(i - npb, 0), 0))],
        out_shape=[jax.ShapeDtypeStruct((n_prompt, d), F32),
                   jax.ShapeDtypeStruct((n_sample, d), F32)],
        compiler_params=pltpu.CompilerParams(dimension_semantics=("arbitrary",)),
        name="final_norm",
    )(x, g)


def kernel(x_prompt, x_sample, cache_k, cache_v, state_lru_h, state_conv, state_pool, norm1_g, w_in, conv_w, conv_b, lru_wa, lru_ba, lru_wx, lru_bx, lru_lambda, pool_w, pool_scale, attn_sinks, w_out, norm2_g, router_group_w, router_group_b, router_expert_w, router_expert_b, expert_w1, expert_w3, expert_w2, final_norm_g):
    bsz, seq, d = x_prompt.shape
    db, ds, _ = x_sample.shape
    depth = w_in.shape[0]
    win = cache_k.shape[2]
    nkv = cache_k.shape[3]
    d_lru = lru_lambda.shape[1]
    d_pool = pool_scale.shape[1]
    d_kv = nkv * cache_k.shape[4]
    d_attn = attn_sinks.shape[1] * HEAD_DIM
    dims = dict(d_lru=d_lru, d_pool=d_pool, d_attn=d_attn, d_kv=d_kv)
    n_prompt = bsz * seq
    n_sample = db * ds
    n_rows = n_prompt + n_sample
    assert n_prompt % ROW_BLOCK == 0 and n_sample % ROW_BLOCK == 0
    assert win == WINDOW and nkv == N_KV_HEADS and ds <= 8 and WINDOW % ds == 0
    tile = 256 if seq % 256 == 0 else WINDOW
    c_max = -(-n_rows // MOE_CHUNK) + N_CLASSES
    npb = n_prompt // ROW_BLOCK

    cos_p, sin_p = _rope_tables(jnp.arange(seq))
    cos_s, sin_s = _rope_tables(PAST_LEN + jnp.arange(ds))

    x_main, x_tail, tail0 = x_prompt.reshape(n_prompt, d), x_sample.reshape(n_sample, d), 0
    ck = cache_k.reshape(depth, db, win, d_kv)
    cv = cache_v.reshape(depth, db, win, d_kv)
    outs = [[] for _ in range(10)]
    for l in range(depth):
        p = dict(conv_w=conv_w[l], conv_b=conv_b[l], lru_wa=lru_wa[l], lru_ba=lru_ba[l],
                 lru_wx=lru_wx[l], lru_bx=lru_bx[l], lru_lambda=lru_lambda[l], pool_w=pool_w[l],
                 pool_scale=pool_scale[l], attn_sinks=attn_sinks[l])
        consts = _layer_consts(p, dims)
        proj = _in_proj(x_main, x_tail, tail0, norm1_g[l].reshape(1, d), w_in[l], n_rows, n_prompt)
        mix_p, pk, pv, ph, pc, pp = _mixer_prompt(proj, consts, cos_p, sin_p, bsz, seq, dims, tile)
        mix_s, sk, sv, sh, sc, sp = _mixer_sample(
            proj, n_prompt, consts, cos_s, sin_s, ck[l], cv[l],
            state_lru_h[l].reshape(db, 1, d_lru), state_conv[l], state_pool[l], dims)
        wr = jnp.concatenate([router_group_w[l], router_expert_w[l]], axis=1)
        wr = jnp.pad(wr, ((0, 0), (0, LANES - wr.shape[1]))).astype(BF16)
        br = jnp.concatenate([router_group_b[l], router_expert_b[l]])
        br = jnp.pad(br, (0, LANES - br.shape[0])).reshape(1, LANES)
        xr, cnt = _out_proj(mix_p, mix_s, x_main, x_tail, tail0, w_out[l], norm2_g[l].reshape(1, d),
                            wr, br, n_rows)
        route = xr[:, d + 2:d + 4].astype(I32)
        tables = (route[:, 0], route[:, 1]) + _chunk_tables(cnt[0, :N_CLASSES].astype(I32), c_max)
        x = _moe(xr, norm2_g[l].reshape(1, d), expert_w1[l], expert_w3[l], expert_w2[l],
                 tables, n_rows, c_max)
        x_main, x_tail, tail0 = x, x, npb
        for lst, val in zip(outs, (pk.reshape(bsz, WINDOW, nkv, HEAD_DIM), pv.reshape(bsz, WINDOW, nkv, HEAD_DIM),
                                   ph.reshape(bsz, d_lru), pc, pp,
                                   sk.reshape(db, win, nkv, HEAD_DIM), sv.reshape(db, win, nkv, HEAD_DIM),
                                   sh.reshape(db, d_lru), sc, sp)):
            lst.append(val)
    y_p, y_s = _final_norm(x_main, final_norm_g.reshape(1, d), n_prompt, n_sample)
    return (y_p.reshape(bsz, seq, d), y_s.reshape(db, ds, d)) + tuple(jnp.stack(o) for o in outs)
```

```python
import functools

import jax
import jax.numpy as jnp
from jax import lax
from jax.experimental import pallas as pl
from jax.experimental.pallas import tpu as pltpu

F32 = jnp.float32
BF16 = jnp.bfloat16
I32 = jnp.int32

LRU_HEADS = 8
CONV_WIDTH = 4
LRU_C = 8.0
POOL_WINDOWS = (2, 4, 8, 16)
POOL_PAD = max(POOL_WINDOWS) - 1
HEAD_DIM = 64
N_KV_HEADS = 4
WINDOW = 128
ROPE_THETA = 10000.0
ATTN_SCALE = HEAD_DIM ** -0.5
N_EXPERT_GROUPS = 4
EXPERTS_PER_GROUP = 4
N_EXPERTS = N_EXPERT_GROUPS * EXPERTS_PER_GROUP
RMS_EPS = 1e-6
PAST_LEN = 16384

LANES = 128
HALF = LANES // 2
HALO = 16
ROW_BLOCK = 256
MOE_CHUNK = 128
PAIR_SLOT_A = (0, 0, 0, 1, 1, 3)
PAIR_SLOT_B = (1, 2, 3, 3, 2, 2)
N_PAIRS = len(PAIR_SLOT_A)
N_CLASSES = N_EXPERT_GROUPS * N_PAIRS
VMEM_LIMIT = 52 * 1024 * 1024


def _rms(x, g):
    return (x * lax.rsqrt(jnp.mean(x * x, axis=-1, keepdims=True) + RMS_EPS)) * g


def _load_weight_bf16(w_hbm, w_bf, stage, sem):
    rows = stage.shape[1]
    n = w_hbm.shape[0] // rows

    def copy(i, sl):
        return pltpu.make_async_copy(w_hbm.at[pl.ds(i * rows, rows)], stage.at[sl], sem.at[sl])

    copy(0, 0).start()
    for i in range(n):
        sl = i % 2
        if i + 1 < n:
            copy(i + 1, 1 - sl).start()
        copy(i, sl).wait()
        w_bf[i * rows:(i + 1) * rows, :] = stage[sl].astype(BF16)


def _slab_pitch(d):
    return d // LANES + 1


def _read_slabs(ref, lead, n_tok, n_slabs, pitch):
    return jnp.concatenate(
        [ref[lead + (pl.ds(k, n_tok, stride=pitch), slice(None))] for k in range(n_slabs)], axis=1)


def _write_slabs(ref, lead, val, pitch):
    n_tok = val.shape[0]
    for k in range(val.shape[1] // LANES):
        ref[lead + (pl.ds(k, n_tok, stride=pitch), slice(None))] = val[:, k * LANES:(k + 1) * LANES]


def _pick_rows(i, n_main_blocks, main_ref, tail_ref, d, slab):
    if slab:
        main = _read_slabs(main_ref, (), ROW_BLOCK, d // LANES, _slab_pitch(d))
        tail = _read_slabs(tail_ref, (), ROW_BLOCK, d // LANES, _slab_pitch(d))
    else:
        main, tail = main_ref[...], tail_ref[...]
    return jnp.where(i < n_main_blocks, main, tail)


def _in_proj_kernel(xm_ref, xt_ref, g_ref, w_hbm, o_ref, w_bf, stage, sem, *, n_main_blocks, layer, slab):
    i = pl.program_id(0)

    @pl.when(i == 0)
    def _():
        _load_weight_bf16(w_hbm.at[layer], w_bf, stage, sem)

    h = _rms(_pick_rows(i, n_main_blocks, xm_ref, xt_ref, g_ref.shape[1], slab), g_ref[...])
    o_ref[...] = jnp.dot(h.astype(BF16), w_bf[...], preferred_element_type=F32)


def _row_specs(d, n_main_blocks, tail_block0, slab):
    shape = (ROW_BLOCK * _slab_pitch(d), LANES) if slab else (ROW_BLOCK, d)
    return [pl.BlockSpec(shape, lambda i: (jnp.minimum(i, n_main_blocks - 1), 0)),
            pl.BlockSpec(shape, lambda i: (jnp.maximum(i - n_main_blocks, 0) + tail_block0, 0))]


def _in_proj(x_main, x_tail, tail_block0, slab, g, w, layer, n_rows, n_main):
    d = g.shape[1]
    d_in = w.shape[2]
    nmb = n_main // ROW_BLOCK
    kern = functools.partial(_in_proj_kernel, n_main_blocks=nmb, layer=layer, slab=slab)
    return pl.pallas_call(
        kern,
        grid=(n_rows // ROW_BLOCK,),
        in_specs=_row_specs(d, nmb, tail_block0, slab) + [
            pl.BlockSpec((1, d), lambda i: (0, 0)),
            pl.BlockSpec(memory_space=pl.ANY),
        ],
        out_specs=pl.BlockSpec((ROW_BLOCK, d_in), lambda i: (i, 0)),
        out_shape=jax.ShapeDtypeStruct((n_rows, d_in), F32),
        scratch_shapes=[
            pltpu.VMEM((d, d_in), BF16),
            pltpu.VMEM((2, ROW_BLOCK, d_in), F32),
            pltpu.SemaphoreType.DMA((2,)),
        ],
        compiler_params=pltpu.CompilerParams(
            dimension_semantics=("arbitrary",), vmem_limit_bytes=VMEM_LIMIT),
        name="in_proj",
    )(x_main, x_tail, g, w)


def _scan_linear(a, b):
    t = a.shape[0]
    row = lax.broadcasted_iota(I32, a.shape, 0)
    d = 1
    while d < min(t, 8):
        a_sh = pltpu.roll(a, d, axis=0)
        b_sh = pltpu.roll(b, d, axis=0)
        m = row >= d
        b = jnp.where(m, a * b_sh + b, b)
        a = jnp.where(m, a * a_sh, a)
        d *= 2
    while d < t:
        b = jnp.concatenate([b[:d], a[d:] * b[:t - d] + b[d:]], axis=0)
        a = jnp.concatenate([a[:d], a[d:] * a[:t - d]], axis=0)
        d *= 2
    return a, b


def _lru_chunk(xc, g, h_prev, wg, ba, bx, sp):
    pre = jnp.dot(xc.astype(BF16), wg, preferred_element_type=F32)
    r = jax.nn.sigmoid(pre[:, :LANES] + ba)
    ig = jax.nn.sigmoid(pre[:, LANES:] + bx)
    log_a = (-LRU_C * r) * sp
    a = jnp.exp(log_a)
    bterm = jnp.sqrt(-jnp.tanh(log_a) * (a * a + 1.0)) * ig * xc
    a_cum, h0 = _scan_linear(a, bterm)
    hs = a_cum * h_prev + h0
    return hs * jax.nn.gelu(g), hs


def _rope(x, cos, sin_signed):
    n = x.shape[1] // LANES
    lane = lax.broadcasted_iota(I32, (x.shape[0], LANES), 1)
    first = (lane % HEAD_DIM) < (HEAD_DIM // 2)
    outs = []
    for c in range(n):
        xc = x[:, c * LANES:(c + 1) * LANES]
        swapped = jnp.where(first, pltpu.roll(xc, LANES - HEAD_DIM // 2, axis=1),
                            pltpu.roll(xc, HEAD_DIM // 2, axis=1))
        outs.append(xc * cos + swapped * sin_signed)
    return outs


def _store_head_variants(var_ref, row0, chunks):
    t = chunks[0].shape[0]
    lane = lax.broadcasted_iota(I32, (t, LANES), 1)
    for kc, x in enumerate(chunks):
        swapped = pltpu.roll(x, HALF, axis=1)
        for hh in range(2):
            for p in range(2):
                src = x if p == hh else swapped
                keep = (lane < HALF) if p == 0 else (lane >= HALF)
                var_ref[2 * kc + hh, p, row0:row0 + t, :] = jnp.where(keep, src, 0.0).astype(BF16)


def _attend_block(q_chunks, kvar, vvar, key0, sinks, lim, out_ref, out_rows, out_col0):
    qb = q_chunks[0].shape[0]
    nk = 2 * WINDOW
    gq = (2 * len(q_chunks)) // N_KV_HEADS
    rows = 2 * qb
    qi = lax.broadcasted_iota(I32, (rows, nk), 0) % qb
    kj = lax.broadcasted_iota(I32, (rows, nk), 1)
    valid = ((kj < WINDOW) & (kj > qi + lim)) | ((kj >= WINDOW) & (kj - WINDOW <= qi))
    top = lax.broadcasted_iota(I32, (rows, 1), 0) < qb
    lane = lax.broadcasted_iota(I32, (nk, LANES), 1)
    ones_lo = jnp.where(lane < HALF, 1.0, 0.0).astype(BF16)
    ones_hi = jnp.where(lane >= HALF, 1.0, 0.0).astype(BF16)
    lane_o = lax.broadcasted_iota(I32, (rows, LANES), 1)
    nt = (((1,), (1,)), ((), ()))
    for c in range(N_KV_HEADS):
        c0 = c * gq // 2
        qs = jnp.concatenate([q_chunks[c0], q_chunks[c0 + 1]], axis=0)
        es, sink_terms = [], []
        for p in range(2):
            kc = kvar[c, p, key0:key0 + nk, :]
            s = lax.dot_general(qs, kc, nt, preferred_element_type=F32)
            s = jnp.where(valid, s, -jnp.inf)
            h0 = c * gq + p
            sink = jnp.where(top, sinks[:, h0:h0 + 1], sinks[:, h0 + 2:h0 + 3])
            m = jnp.maximum(jnp.max(s, axis=-1, keepdims=True), sink)
            es.append(jnp.exp(s - m).astype(BF16))
            sink_terms.append(jnp.exp(sink - m))
        r0 = jnp.concatenate([vvar[c, 0, key0:key0 + nk, :], ones_lo], axis=1)
        r1 = jnp.concatenate([vvar[c, 1, key0:key0 + nk, :], ones_hi], axis=1)
        od = (jnp.dot(es[0], r0, preferred_element_type=F32)
              + jnp.dot(es[1], r1, preferred_element_type=F32))
        den = od[:, LANES:] + jnp.where(lane_o < HALF, sink_terms[0], sink_terms[1])
        o = (od[:, :LANES] / den).astype(out_ref.dtype)
        out_ref[out_rows, out_col0 + c0 * LANES:out_col0 + (c0 + 1) * LANES] = o[0:qb]
        out_ref[out_rows, out_col0 + (c0 + 1) * LANES:out_col0 + (c0 + 2) * LANES] = o[qb:rows]


def _mixer_kernel(*refs, tile, qblock, is_prompt, pos0, d_lru, d_pool, d_attn, d_kv):
    (proj_ref, cos_ref, sin_ref, convw_ref, convb_ref, wg_ref, ba_ref, bx_ref, lam_ref,
     poolw_ref, pscale_ref, sink_ref) = refs[:12]
    if is_prompt:
        (mix_ref, ko_ref, vo_ref, ho_ref, co_ref, po_ref,
         lru_ext, pool_ext, kvar, vvar, hcar) = refs[12:]
    else:
        (ck_ref, cv_ref, sh_ref, sc_ref, sp_ref,
         mix_ref, ko_ref, vo_ref, ho_ref, co_ref, po_ref,
         lru_ext, pool_ext, kvar, vvar) = refs[12:]

    s = pl.program_id(1) if is_prompt else 0
    o_glru = d_lru
    o_pool = 2 * d_lru
    o_q = o_pool + d_pool
    o_k = o_q + d_attn
    o_v = o_k + d_kv
    n_kc = d_kv // LANES

    if is_prompt:
        @pl.when(s == 0)
        def _():
            lru_ext[0:HALO, :] = jnp.zeros((HALO, d_lru), F32)
            pool_ext[0:HALO, :] = jnp.zeros((HALO, d_pool), F32)
            hcar[...] = jnp.zeros_like(hcar)
            kvar[:, :, 0:WINDOW, :] = jnp.zeros((N_KV_HEADS, 2, WINDOW, LANES), BF16)
            vvar[:, :, 0:WINDOW, :] = jnp.zeros((N_KV_HEADS, 2, WINDOW, LANES), BF16)
        h_prev = hcar[0:1, :]
    else:
        lru_ext[0:HALO, :] = jnp.zeros((HALO, d_lru), F32)
        pool_ext[0:HALO, :] = jnp.zeros((HALO, d_pool), F32)
        lru_ext[HALO - (CONV_WIDTH - 1):HALO, :] = sc_ref[0]
        pool_ext[HALO - POOL_PAD:HALO, :] = sp_ref[0]
        h_prev = sh_ref[0]
        kvar[:, :, WINDOW:2 * WINDOW, :] = jnp.zeros((N_KV_HEADS, 2, WINDOW, LANES), BF16)
        vvar[:, :, WINDOW:2 * WINDOW, :] = jnp.zeros((N_KV_HEADS, 2, WINDOW, LANES), BF16)
        _store_head_variants(kvar, 0, [ck_ref[0, :, c * LANES:(c + 1) * LANES] for c in range(n_kc)])
        _store_head_variants(vvar, 0, [cv_ref[0, :, c * LANES:(c + 1) * LANES] for c in range(n_kc)])

    lru_ext[HALO:HALO + tile, :] = proj_ref[:, 0:d_lru]
    pool_ext[HALO:HALO + tile, :] = proj_ref[:, o_pool:o_pool + d_pool]

    xc = convb_ref[...] + convw_ref[0:1, :] * lru_ext[HALO - 3:HALO - 3 + tile, :]
    for j in range(1, CONV_WIDTH):
        xc = xc + convw_ref[j:j + 1, :] * lru_ext[HALO - 3 + j:HALO - 3 + j + tile, :]
    neg = -lam_ref[...]
    sp_all = jnp.maximum(neg, 0.0) + jnp.log1p(jnp.exp(-jnp.abs(neg)))
    h_last = []
    for c in range(d_lru // LANES):
        cs = slice(c * LANES, (c + 1) * LANES)
        y, hs = _lru_chunk(xc[:, cs], proj_ref[:, o_glru + c * LANES:o_glru + (c + 1) * LANES],
                           h_prev[:, cs], wg_ref[c], ba_ref[:, cs], bx_ref[:, cs], sp_all[:, cs])
        mix_ref[:, cs] = y.astype(mix_ref.dtype)
        h_last.append(hs[tile - 1:tile, :])
    h_last = jnp.concatenate(h_last, axis=-1)

    row = lax.broadcasted_iota(I32, (tile, LANES), 0)
    pos = pos0 + s * tile + row
    for gi, w in enumerate(POOL_WINDOWS):
        cs = slice(gi * LANES, (gi + 1) * LANES)
        e = pool_ext[:, cs]
        acc = e
        step = 1
        while step < w:
            acc = acc + pltpu.roll(acc, step, axis=0)
            step *= 2
        cnt = jnp.minimum(pos + 1, w).astype(F32)
        dlt = acc[HALO:HALO + tile, :] / cnt - e[HALO:HALO + tile, :]
        y = jnp.dot(dlt.astype(BF16), poolw_ref[gi], preferred_element_type=F32) * pscale_ref[:, cs]
        mix_ref[:, d_lru + gi * LANES:d_lru + (gi + 1) * LANES] = y.astype(mix_ref.dtype)

    sinks = sink_ref[...]
    o_attn = d_lru + d_pool
    cos = cos_ref[...]
    sin = sin_ref[...]
    q_chunks = [(qc * ATTN_SCALE).astype(BF16) for qc in _rope(proj_ref[:, o_q:o_q + d_attn], cos, sin)]
    k_rot = _rope(proj_ref[:, o_k:o_k + d_kv], cos, sin)
    _store_head_variants(kvar, WINDOW, k_rot)
    _store_head_variants(vvar, WINDOW, [proj_ref[:, o_v + c * LANES:o_v + (c + 1) * LANES]
                                        for c in range(n_kc)])
    for blk in range(tile // qblock):
        rs = slice(blk * qblock, (blk + 1) * qblock)
        if is_prompt and blk == 0:
            lim = jnp.where(s == 0, WINDOW, 0)
        else:
            lim = 0
        _attend_block([qc[rs] for qc in q_chunks], kvar, vvar, blk * qblock, sinks, lim,
                      mix_ref, rs, o_attn)

    k_last = jnp.concatenate([kc[tile - qblock:tile] for kc in k_rot], axis=-1)
    if is_prompt:
        lru_ext[0:HALO, :] = lru_ext[tile:tile + HALO, :]
        pool_ext[0:HALO, :] = pool_ext[tile:tile + HALO, :]
        hcar[0:1, :] = h_last
        kvar[:, :, 0:WINDOW, :] = kvar[:, :, tile:tile + WINDOW, :]
        vvar[:, :, 0:WINDOW, :] = vvar[:, :, tile:tile + WINDOW, :]

        @pl.when(s == pl.num_programs(1) - 1)
        def _():
            ko_ref[0] = k_last
            vo_ref[0] = proj_ref[tile - qblock:tile, o_v:o_v + d_kv]
            ho_ref[0] = h_last
            co_ref[0] = lru_ext[HALO + tile - (CONV_WIDTH - 1):HALO + tile, :]
            po_ref[0] = pool_ext[HALO + tile - POOL_PAD:HALO + tile, :]
    else:
        ko_ref[0, 0:WINDOW - tile, :] = ck_ref[0, tile:WINDOW, :]
        ko_ref[0, WINDOW - tile:WINDOW, :] = k_last
        vo_ref[0, 0:WINDOW - tile, :] = cv_ref[0, tile:WINDOW, :]
        vo_ref[0, WINDOW - tile:WINDOW, :] = proj_ref[:, o_v:o_v + d_kv]
        ho_ref[0] = h_last
        co_ref[0] = lru_ext[HALO + tile - (CONV_WIDTH - 1):HALO + tile, :]
        po_ref[0] = pool_ext[HALO + tile - POOL_PAD:HALO + tile, :]


def _layer_consts(p, dims):
    d_lru, d_pool = dims["d_lru"], dims["d_pool"]
    hd = d_lru // LRU_HEADS
    per = LANES // hd
    nchunk = d_lru // LANES

    def blockdiag(w):
        w = w.reshape(nchunk, per, hd, hd)
        eye = jnp.eye(per, dtype=w.dtype)
        return jnp.einsum("cpij,pq->cpiqj", w, eye).reshape(nchunk, LANES, LANES)

    wg = jnp.concatenate([blockdiag(p["lru_wa"]), blockdiag(p["lru_wx"])], axis=-1).astype(BF16)
    return dict(
        convw=p["conv_w"], convb=p["conv_b"].reshape(1, d_lru), wg=wg,
        ba=p["lru_ba"].reshape(1, d_lru), bx=p["lru_bx"].reshape(1, d_lru),
        lam=p["lru_lambda"].reshape(1, d_lru), poolw=p["pool_w"].astype(BF16),
        pscale=p["pool_scale"].reshape(1, d_pool), sinks=p["attn_sinks"].reshape(1, -1))


def _rope_tables(pos):
    half = HEAD_DIM // 2
    inv = ROPE_THETA ** (-jnp.arange(half, dtype=F32) / half)
    ang = pos.astype(F32)[:, None] * inv[None, :]
    cos = jnp.cos(ang)
    sin = jnp.sin(ang)
    cos2 = jnp.concatenate([cos, cos], axis=-1)
    sin2 = jnp.concatenate([-sin, sin], axis=-1)
    reps = LANES // HEAD_DIM
    return jnp.tile(cos2, (1, reps)), jnp.tile(sin2, (1, reps))


CONST_NAMES = ("convw", "convb", "wg", "ba", "bx", "lam", "poolw", "pscale", "sinks")


def _const_specs(consts):
    return [pl.BlockSpec(consts[n].shape, functools.partial(lambda nd, *_: (0,) * nd, consts[n].ndim))
            for n in CONST_NAMES]


def _mixer_prompt(proj, consts, cos, sin, bsz, seq, dims, tile):
    d_lru, d_pool, d_attn, d_kv = dims["d_lru"], dims["d_pool"], dims["d_attn"], dims["d_kv"]
    d_in = proj.shape[1]
    d_mix = d_lru + d_pool + d_attn
    ns = seq // tile
    kern = functools.partial(_mixer_kernel, tile=tile, qblock=WINDOW, is_prompt=True, pos0=0,
                             d_lru=d_lru, d_pool=d_pool, d_attn=d_attn, d_kv=d_kv)
    return pl.pallas_call(
        kern,
        grid=(bsz, ns),
        in_specs=[
            pl.BlockSpec((tile, d_in), lambda b, s: (b * ns + s, 0)),
            pl.BlockSpec((tile, LANES), lambda b, s: (s, 0)),
            pl.BlockSpec((tile, LANES), lambda b, s: (s, 0)),
        ] + _const_specs(consts),
        out_specs=[
            pl.BlockSpec((tile, d_mix), lambda b, s: (b * ns + s, 0)),
            pl.BlockSpec((1, WINDOW, d_kv), lambda b, s: (b, 0, 0)),
            pl.BlockSpec((1, WINDOW, d_kv), lambda b, s: (b, 0, 0)),
            pl.BlockSpec((1, 1, d_lru), lambda b, s: (b, 0, 0)),
            pl.BlockSpec((1, CONV_WIDTH - 1, d_lru), lambda b, s: (b, 0, 0)),
            pl.BlockSpec((1, POOL_PAD, d_pool), lambda b, s: (b, 0, 0)),
        ],
        out_shape=[
            jax.ShapeDtypeStruct((bsz * seq, d_mix), BF16),
            jax.ShapeDtypeStruct((bsz, WINDOW, d_kv), F32),
            jax.ShapeDtypeStruct((bsz, WINDOW, d_kv), F32),
            jax.ShapeDtypeStruct((bsz, 1, d_lru), F32),
            jax.ShapeDtypeStruct((bsz, CONV_WIDTH - 1, d_lru), F32),
            jax.ShapeDtypeStruct((bsz, POOL_PAD, d_pool), F32),
        ],
        scratch_shapes=[
            pltpu.VMEM((tile + HALO, d_lru), F32),
            pltpu.VMEM((tile + HALO, d_pool), F32),
            pltpu.VMEM((N_KV_HEADS, 2, WINDOW + tile, LANES), BF16),
            pltpu.VMEM((N_KV_HEADS, 2, WINDOW + tile, LANES), BF16),
            pltpu.VMEM((8, d_lru), F32),
        ],
        compiler_params=pltpu.CompilerParams(
            dimension_semantics=("arbitrary", "arbitrary"), vmem_limit_bytes=VMEM_LIMIT),
        name="mixer_prompt",
    )(proj, cos, sin, *[consts[n] for n in CONST_NAMES])


def _mixer_sample(proj, row0, consts, cos, sin, cache_k, cache_v, st_h, st_conv, st_pool, layer, dims):
    d_lru, d_pool, d_attn, d_kv = dims["d_lru"], dims["d_pool"], dims["d_attn"], dims["d_kv"]
    d_in = proj.shape[1]
    d_mix = d_lru + d_pool + d_attn
    db, win = cache_k.shape[1], cache_k.shape[2]
    t = cos.shape[0]
    blk0 = row0 // t
    kern = functools.partial(_mixer_kernel, tile=t, qblock=t, is_prompt=False, pos0=PAST_LEN,
                             d_lru=d_lru, d_pool=d_pool, d_attn=d_attn, d_kv=d_kv)
    return pl.pallas_call(
        kern,
        grid=(db,),
        in_specs=[
            pl.BlockSpec((t, d_in), lambda b: (blk0 + b, 0)),
            pl.BlockSpec((t, LANES), lambda b: (0, 0)),
            pl.BlockSpec((t, LANES), lambda b: (0, 0)),
        ] + _const_specs(consts) + [
            pl.BlockSpec((None, 1, win, d_kv), lambda b: (layer, b, 0, 0)),
            pl.BlockSpec((None, 1, win, d_kv), lambda b: (layer, b, 0, 0)),
            pl.BlockSpec((None, 1, 1, d_lru), lambda b: (layer, b, 0, 0)),
            pl.BlockSpec((None, 1, CONV_WIDTH - 1, d_lru), lambda b: (layer, b, 0, 0)),
            pl.BlockSpec((None, 1, POOL_PAD, d_pool), lambda b: (layer, b, 0, 0)),
        ],
        out_specs=[
            pl.BlockSpec((t, d_mix), lambda b: (b, 0)),
            pl.BlockSpec((1, win, d_kv), lambda b: (b, 0, 0)),
            pl.BlockSpec((1, win, d_kv), lambda b: (b, 0, 0)),
            pl.BlockSpec((1, 1, d_lru), lambda b: (b, 0, 0)),
            pl.BlockSpec((1, CONV_WIDTH - 1, d_lru), lambda b: (b, 0, 0)),
            pl.BlockSpec((1, POOL_PAD, d_pool), lambda b: (b, 0, 0)),
        ],
        out_shape=[
            jax.ShapeDtypeStruct((db * t, d_mix), F32),
            jax.ShapeDtypeStruct((db, win, d_kv), F32),
            jax.ShapeDtypeStruct((db, win, d_kv), F32),
            jax.ShapeDtypeStruct((db, 1, d_lru), F32),
            jax.ShapeDtypeStruct((db, CONV_WIDTH - 1, d_lru), F32),
            jax.ShapeDtypeStruct((db, POOL_PAD, d_pool), F32),
        ],
        scratch_shapes=[
            pltpu.VMEM((t + HALO, d_lru), F32),
            pltpu.VMEM((t + HALO, d_pool), F32),
            pltpu.VMEM((N_KV_HEADS, 2, 2 * WINDOW, LANES), BF16),
            pltpu.VMEM((N_KV_HEADS, 2, 2 * WINDOW, LANES), BF16),
        ],
        compiler_params=pltpu.CompilerParams(
            dimension_semantics=("arbitrary",), vmem_limit_bytes=VMEM_LIMIT),
        name="mixer_sample",
    )(proj, cos, sin, *[consts[n] for n in CONST_NAMES], cache_k, cache_v, st_h, st_conv, st_pool)


def _route(logits, run_cnt):
    t = logits.shape[0]
    lane = lax.broadcasted_iota(I32, (t, LANES), 1)
    lane_f = lane.astype(F32)
    ninf = -jnp.inf
    big = float(LANES)
    is_g = lane < N_EXPERT_GROUPS
    lg = jnp.where(is_g, logits, ninf)
    mg = jnp.max(lg, axis=-1, keepdims=True)
    g_top = jnp.min(jnp.where(lg == mg, lane_f, big), axis=-1, keepdims=True).astype(I32)
    pg_top = 1.0 / jnp.sum(jnp.exp(lg - mg), axis=-1, keepdims=True)
    base = N_EXPERT_GROUPS + EXPERTS_PER_GROUP * g_top
    in_grp = (lane >= base) & (lane < base + EXPERTS_PER_GROUP)
    le = jnp.where(in_grp, logits, ninf)
    m1 = jnp.max(le, axis=-1, keepdims=True)
    i1 = jnp.min(jnp.where(le == m1, lane_f, big), axis=-1, keepdims=True).astype(I32)
    le2 = jnp.where(lane == i1, ninf, le)
    m2 = jnp.max(le2, axis=-1, keepdims=True)
    i2 = jnp.min(jnp.where((le2 == m2) & in_grp & (lane != i1), lane_f, big),
                 axis=-1, keepdims=True).astype(I32)
    se = jnp.sum(jnp.exp(le - m1), axis=-1, keepdims=True)
    p1 = 1.0 / se
    p2 = jnp.exp(m2 - m1) / se
    tot = p1 + p2
    w1 = (p1 / tot) * pg_top
    w2 = (p2 / tot) * pg_top
    a = i1 - base
    b = i2 - base
    lo = jnp.minimum(a, b)
    hi = jnp.maximum(a, b)
    w_lo = jnp.where(a < b, w1, w2)
    w_hi = jnp.where(a < b, w2, w1)
    pid = jnp.where(lo == 0, hi - 1, jnp.where(lo == 1, jnp.where(hi == 3, 3, 4), 5))
    swap = pid == 5
    w_a = jnp.where(swap, w_hi, w_lo)
    w_b = jnp.where(swap, w_lo, w_hi)
    cls = g_top * N_PAIRS + pid
    onehot = lane == cls
    ti = lax.broadcasted_iota(I32, (t, t), 0)
    tj = lax.broadcasted_iota(I32, (t, t), 1)
    lower = jnp.where(tj <= ti, 1.0, 0.0).astype(BF16)
    prefix = jnp.dot(lower, jnp.where(onehot, 1.0, 0.0).astype(BF16), preferred_element_type=F32)
    rank = jnp.sum(jnp.where(onehot, prefix - 1.0 + run_cnt, 0.0), axis=-1, keepdims=True)
    info = jnp.where(lane == 0, w_a, jnp.where(lane == 1, w_b, jnp.where(
        lane == 2, cls.astype(F32), jnp.where(lane == 3, rank, 0.0))))
    return info, run_cnt + prefix[t - 1:t, :]


def _out_proj_kernel(mixp_ref, mixs_ref, xm_ref, xt_ref, w_hbm, g_ref, wr_ref, br_ref,
                     o_ref, info_ref, cnt_ref, w_bf, stage, sem, run_cnt, *, n_main_blocks, d, layer, slab):
    i = pl.program_id(0)

    @pl.when(i == 0)
    def _():
        _load_weight_bf16(w_hbm.at[layer], w_bf, stage, sem)
        run_cnt[...] = jnp.zeros_like(run_cnt)

    mix = jnp.where(i < n_main_blocks, mixp_ref[...], mixs_ref[...].astype(BF16))
    xres = (_pick_rows(i, n_main_blocks, xm_ref, xt_ref, d, slab)
            + jnp.dot(mix, w_bf[...], preferred_element_type=F32))
    h2 = _rms(xres, g_ref[...])
    logits = jnp.dot(h2.astype(BF16), wr_ref[...], preferred_element_type=F32) + br_ref[...]
    info, cnt = _route(logits, run_cnt[0:1, :])
    run_cnt[0:1, :] = cnt
    pitch = _slab_pitch(d)
    _write_slabs(o_ref, (), xres, pitch)
    o_ref[pl.ds(d // LANES, ROW_BLOCK, stride=pitch), :] = info
    info_ref[...] = info
    cnt_ref[...] = jnp.broadcast_to(cnt, cnt_ref.shape)


def _out_proj(mix_p, mix_s, x_main, x_tail, tail_block0, slab, w, layer, g, wr, br, n_rows):
    d = g.shape[1]
    d_mix = w.shape[1]
    npb = mix_p.shape[0] // ROW_BLOCK
    pitch = _slab_pitch(d)
    kern = functools.partial(_out_proj_kernel, n_main_blocks=npb, d=d, layer=layer, slab=slab)
    return pl.pallas_call(
        kern,
        grid=(n_rows // ROW_BLOCK,),
        in_specs=[
            pl.BlockSpec((ROW_BLOCK, d_mix), lambda i: (jnp.minimum(i, npb - 1), 0)),
            pl.BlockSpec((ROW_BLOCK, d_mix), lambda i: (jnp.maximum(i - npb, 0), 0)),
        ] + _row_specs(d, npb, tail_block0, slab) + [
            pl.BlockSpec(memory_space=pl.ANY),
            pl.BlockSpec((1, d), lambda i: (0, 0)),
            pl.BlockSpec((d, LANES), lambda i: (0, 0)),
            pl.BlockSpec((1, LANES), lambda i: (0, 0)),
        ],
        out_specs=[pl.BlockSpec((ROW_BLOCK * pitch, LANES), lambda i: (i, 0)),
                   pl.BlockSpec((ROW_BLOCK, LANES), lambda i: (i, 0)),
                   pl.BlockSpec((8, LANES), lambda i: (0, 0))],
        out_shape=[jax.ShapeDtypeStruct((n_rows * pitch, LANES), F32),
                   jax.ShapeDtypeStruct((n_rows, LANES), F32),
                   jax.ShapeDtypeStruct((8, LANES), F32)],
        scratch_shapes=[
            pltpu.VMEM((d_mix, d), BF16),
            pltpu.VMEM((2, ROW_BLOCK, d), F32),
            pltpu.SemaphoreType.DMA((2,)),
            pltpu.VMEM((8, LANES), F32),
        ],
        compiler_params=pltpu.CompilerParams(
            dimension_semantics=("arbitrary",), vmem_limit_bytes=VMEM_LIMIT),
        name="out_proj",
    )(mix_p, mix_s, x_main, x_tail, w, g, wr, br)


def _moe_kernel(cls_ref, rank_ref, off_ref, cnt_ref, ea_ref, eb_ref, chga_ref, chgb_ref, nch_ref,
                xr_hbm, g_ref, w1a_ref, w3a_ref, w2a_ref, w1b_ref, w3b_ref, w2b_ref,
                xo_hbm,
                perm, xbuf, obuf, wa1, wa3, wa2, wb1, wb3, wb2, gsem, ssem, *, d, n_rows):
    del ea_ref, eb_ref
    m = MOE_CHUNK
    pitch = _slab_pitch(d)
    n_slabs = d // LANES
    c = pl.program_id(0)
    n = nch_ref[0]
    slot = c % 2
    other = 1 - slot

    def start_gather(chunk, sl, j):
        tok = jnp.maximum(perm[(chunk + 1) * m + j], 0)
        pltpu.make_async_copy(xr_hbm.at[pl.ds(tok * pitch, pitch)],
                              xbuf.at[sl, pl.ds(j * pitch, pitch)], gsem.at[sl]).start()

    def start_scatter(chunk, sl, j):
        tok = perm[(chunk + 1) * m + j]
        r = jnp.where(tok < 0, n_rows + sl * m + j, tok)
        pltpu.make_async_copy(obuf.at[sl, pl.ds(j * pitch, pitch)],
                              xo_hbm.at[pl.ds(r * pitch, pitch)], ssem.at[sl]).start()

    def wait_gather(sl):
        pltpu.make_async_copy(xr_hbm.at[pl.ds(0, m * pitch)], xbuf.at[sl], gsem.at[sl]).wait()

    def wait_scatter(sl):
        pltpu.make_async_copy(obuf.at[sl], xo_hbm.at[pl.ds(0, m * pitch)], ssem.at[sl]).wait()

    @pl.when(c == 0)
    def _():
        obuf[...] = jnp.zeros_like(obuf)
        pad = pltpu.make_async_copy(obuf.at[0], xo_hbm.at[pl.ds(n_rows * pitch, m * pitch)], ssem.at[0])
        pad.start()
        pad.wait()

        def fill(lo, hi):
            def body(i, z):
                perm[i] = -1
                return z
            lax.fori_loop(lo, hi, body, 0)

        def first_gather(j, z):
            start_gather(0, 0, j)
            return z

        fill(0, m)
        fill((n + 1) * m, (n + 3) * m)
        for k in range(N_CLASSES):
            base = (off_ref[k] + 1) * m
            cnt = cnt_ref[k]
            fill(base + cnt, base + ((cnt + m - 1) // m) * m)

        def place(t, z):
            perm[(off_ref[cls_ref[t]] + 1) * m + rank_ref[t]] = t
            return z
        lax.fori_loop(0, n_rows, place, 0, unroll=8)
        lax.fori_loop(0, m, first_gather, 0)

    @pl.when(c <= n)
    def _():
        wait_gather(slot)

        @pl.when(c >= 1)
        def _():
            wait_scatter(slot)

        @pl.when(chga_ref[c] == 1)
        def _():
            wa1[...] = w1a_ref[0].astype(BF16)
            wa3[...] = w3a_ref[0].astype(BF16)
            wa2[...] = w2a_ref[0].astype(BF16)

        @pl.when(chgb_ref[c] == 1)
        def _():
            wb1[...] = w1b_ref[0].astype(BF16)
            wb3[...] = w3b_ref[0].astype(BF16)
            wb2[...] = w2b_ref[0].astype(BF16)

        for j in range(m):
            start_gather(c + 1, other, j)
        for j in range(m):
            start_scatter(c - 1, other, j)
        x = _read_slabs(xbuf, (slot,), m, n_slabs, pitch)
        info = xbuf[slot, pl.ds(n_slabs, m, stride=pitch), :]
        w_a = info[:, 0:1]
        w_b = info[:, 1:2]
        h = _rms(x, g_ref[...]).astype(BF16)
        hid_a = (jax.nn.silu(jnp.dot(h, wa1[...], preferred_element_type=F32))
                 * jnp.dot(h, wa3[...], preferred_element_type=F32) * w_a).astype(BF16)
        hid_b = (jax.nn.silu(jnp.dot(h, wb1[...], preferred_element_type=F32))
                 * jnp.dot(h, wb3[...], preferred_element_type=F32) * w_b).astype(BF16)
        y = (jnp.dot(hid_a, wa2[...], preferred_element_type=F32)
             + jnp.dot(hid_b, wb2[...], preferred_element_type=F32))
        _write_slabs(obuf, (slot,), x + y, pitch)

        @pl.when(c == n)
        def _():
            wait_gather(other)
            wait_scatter(other)


def _moe(xr, g, w1, w3, w2, tables, n_rows, c_max):
    d = g.shape[1]
    f = w1.shape[2]
    m = MOE_CHUNK
    pitch = _slab_pitch(d)
    kern = functools.partial(_moe_kernel, d=d, n_rows=n_rows)

    def wspec(shape, which):
        return pl.BlockSpec(shape, lambda c, *pref: (pref[which][c], 0, 0))

    grid_spec = pltpu.PrefetchScalarGridSpec(
        num_scalar_prefetch=9,
        grid=(c_max + 1,),
        in_specs=[
            pl.BlockSpec(memory_space=pl.ANY),
            pl.BlockSpec((1, d), lambda c, *pref: (0, 0)),
            wspec((1, d, f), 4), wspec((1, d, f), 4), wspec((1, f, d), 4),
            wspec((1, d, f), 5), wspec((1, d, f), 5), wspec((1, f, d), 5),
        ],
        out_specs=pl.BlockSpec(memory_space=pl.ANY),
        scratch_shapes=[
            pltpu.SMEM(((c_max + 3) * m,), I32),
            pltpu.VMEM((2, m * pitch, LANES), F32),
            pltpu.VMEM((2, m * pitch, LANES), F32),
            pltpu.VMEM((d, f), BF16), pltpu.VMEM((d, f), BF16), pltpu.VMEM((f, d), BF16),
            pltpu.VMEM((d, f), BF16), pltpu.VMEM((d, f), BF16), pltpu.VMEM((f, d), BF16),
            pltpu.SemaphoreType.DMA((2,)),
            pltpu.SemaphoreType.DMA((2,)),
        ],
    )
    return pl.pallas_call(
        kern,
        grid_spec=grid_spec,
        out_shape=jax.ShapeDtypeStruct(((n_rows + 2 * m) * pitch, LANES), F32),
        compiler_params=pltpu.CompilerParams(
            dimension_semantics=("arbitrary",), vmem_limit_bytes=VMEM_LIMIT),
        name="moe",
    )(*tables, xr, g, w1, w3, w2, w1, w3, w2)


def _chunk_tables(counts, c_max, expert0):
    m = MOE_CHUNK
    nch_c = (counts + m - 1) // m
    ch_end = jnp.cumsum(nch_c)
    ch_off = ch_end - nch_c
    n_chunks = ch_end[-1]
    chunk = jnp.minimum(jnp.arange(c_max + 1, dtype=I32), jnp.maximum(n_chunks - 1, 0))
    ccls = jnp.sum((chunk[:, None] >= ch_end[None, :]).astype(I32), axis=1)
    ccls = jnp.minimum(ccls, N_CLASSES - 1)
    grp = ccls // N_PAIRS
    pid = ccls % N_PAIRS
    ea = expert0 + grp * EXPERTS_PER_GROUP + jnp.asarray(PAIR_SLOT_A, I32)[pid]
    eb = expert0 + grp * EXPERTS_PER_GROUP + jnp.asarray(PAIR_SLOT_B, I32)[pid]
    first = jnp.ones((1,), I32)
    chga = jnp.concatenate([first, (ea[1:] != ea[:-1]).astype(I32)])
    chgb = jnp.concatenate([first, (eb[1:] != eb[:-1]).astype(I32)])
    return ch_off.astype(I32), counts, ea, eb, chga, chgb, n_chunks.reshape(1).astype(I32)


def _final_norm_kernel(x_ref, g_ref, op_ref, os_ref, *, n_prompt_blocks):
    i = pl.program_id(0)
    d = g_ref.shape[1]
    y = _rms(_read_slabs(x_ref, (), ROW_BLOCK, d // LANES, _slab_pitch(d)), g_ref[...])

    @pl.when(i < n_prompt_blocks)
    def _():
        op_ref[...] = y

    @pl.when(i >= n_prompt_blocks)
    def _():
        os_ref[...] = y


def _final_norm(x, g, n_prompt, n_sample):
    d = g.shape[1]
    npb = n_prompt // ROW_BLOCK
    nsb = n_sample // ROW_BLOCK
    kern = functools.partial(_final_norm_kernel, n_prompt_blocks=npb)
    return pl.pallas_call(
        kern,
        grid=(npb + nsb,),
        in_specs=[pl.BlockSpec((ROW_BLOCK * _slab_pitch(d), LANES), lambda i: (i, 0)),
                  pl.BlockSpec((1, d), lambda i: (0, 0))],
        out_specs=[pl.BlockSpec((ROW_BLOCK, d), lambda i: (jnp.minimum(i, npb - 1), 0)),
                   pl.BlockSpec((ROW_BLOCK, d), lambda i: (jnp.maximum(i - npb, 0), 0))],
        out_shape=[jax.ShapeDtypeStruct((n_prompt, d), F32),
                   jax.ShapeDtypeStruct((n_sample, d), F32)],
        compiler_params=pltpu.CompilerParams(dimension_semantics=("arbitrary",)),
        name="final_norm",
    )(x, g)


def kernel(x_prompt, x_sample, cache_k, cache_v, state_lru_h, state_conv, state_pool, norm1_g, w_in, conv_w, conv_b, lru_wa, lru_ba, lru_wx, lru_bx, lru_lambda, pool_w, pool_scale, attn_sinks, w_out, norm2_g, router_group_w, router_group_b, router_expert_w, router_expert_b, expert_w1, expert_w3, expert_w2, final_norm_g):
    bsz, seq, d = x_prompt.shape
    db, ds, _ = x_sample.shape
    depth = w_in.shape[0]
    win = cache_k.shape[2]
    nkv = cache_k.shape[3]
    d_lru = lru_lambda.shape[1]
    d_pool = pool_scale.shape[1]
    d_kv = nkv * cache_k.shape[4]
    d_attn = attn_sinks.shape[1] * HEAD_DIM
    dims = dict(d_lru=d_lru, d_pool=d_pool, d_attn=d_attn, d_kv=d_kv)
    n_prompt = bsz * seq
    n_sample = db * ds
    n_rows = n_prompt + n_sample
    assert n_prompt % ROW_BLOCK == 0 and n_sample % ROW_BLOCK == 0
    assert win == WINDOW and nkv == N_KV_HEADS and ds <= 8 and WINDOW % ds == 0
    tile = 256 if seq % 256 == 0 else WINDOW
    c_max = -(-n_rows // MOE_CHUNK) + N_CLASSES
    npb = n_prompt // ROW_BLOCK

    cos_p, sin_p = _rope_tables(jnp.arange(seq))
    cos_s, sin_s = _rope_tables(PAST_LEN + jnp.arange(ds))

    x_main, x_tail, tail0, slab = x_prompt.reshape(n_prompt, d), x_sample.reshape(n_sample, d), 0, False
    ck = cache_k.reshape(depth, db, win, d_kv)
    cv = cache_v.reshape(depth, db, win, d_kv)
    st_h = state_lru_h.reshape(depth, db, 1, d_lru)
    n_exp, _, d_exp = expert_w1.shape[1:]
    w1 = expert_w1.reshape(depth * n_exp, d, d_exp)
    w3 = expert_w3.reshape(depth * n_exp, d, d_exp)
    w2 = expert_w2.reshape(depth * n_exp, d_exp, d)
    outs = [[] for _ in range(10)]
    for l in range(depth):
        p = dict(conv_w=conv_w[l], conv_b=conv_b[l], lru_wa=lru_wa[l], lru_ba=lru_ba[l],
                 lru_wx=lru_wx[l], lru_bx=lru_bx[l], lru_lambda=lru_lambda[l], pool_w=pool_w[l],
                 pool_scale=pool_scale[l], attn_sinks=attn_sinks[l])
        consts = _layer_consts(p, dims)
        proj = _in_proj(x_main, x_tail, tail0, slab, norm1_g[l].reshape(1, d), w_in, l, n_rows, n_prompt)
        mix_p, pk, pv, ph, pc, pp = _mixer_prompt(proj, consts, cos_p, sin_p, bsz, seq, dims, tile)
        mix_s, sk, sv, sh, sc, sp = _mixer_sample(
            proj, n_prompt, consts, cos_s, sin_s, ck, cv, st_h, state_conv, state_pool, l, dims)
        wr = jnp.concatenate([router_group_w[l], router_expert_w[l]], axis=1)
        wr = jnp.pad(wr, ((0, 0), (0, LANES - wr.shape[1]))).astype(BF16)
        br = jnp.concatenate([router_group_b[l], router_expert_b[l]])
        br = jnp.pad(br, (0, LANES - br.shape[0])).reshape(1, LANES)
        xr, info, cnt = _out_proj(mix_p, mix_s, x_main, x_tail, tail0, slab, w_out, l,
                                  norm2_g[l].reshape(1, d), wr, br, n_rows)
        route = info[:, 2:4].astype(I32)
        tables = (route[:, 0], route[:, 1]) + _chunk_tables(cnt[0, :N_CLASSES].astype(I32), c_max, l * n_exp)
        x = _moe(xr, norm2_g[l].reshape(1, d), w1, w3, w2, tables, n_rows, c_max)
        x_main, x_tail, tail0, slab = x, x, npb, True
        for lst, val in zip(outs, (pk.reshape(bsz, WINDOW, nkv, HEAD_DIM), pv.reshape(bsz, WINDOW, nkv, HEAD_DIM),
                                   ph.reshape(bsz, d_lru), pc, pp,
                                   sk.reshape(db, win, nkv, HEAD_DIM), sv.reshape(db, win, nkv, HEAD_DIM),
                                   sh.reshape(db, d_lru), sc, sp)):
            lst.append(val)
    y_p, y_s = _final_norm(x_main, final_norm_g.reshape(1, d), n_prompt, n_sample)
    return (y_p.reshape(bsz, seq, d), y_s.reshape(db, ds, d)) + tuple(jnp.stack(o) for o in outs)
```

```python
import functools

import jax
import jax.numpy as jnp
from jax import lax
from jax.experimental import pallas as pl
from jax.experimental.pallas import tpu as pltpu

F32 = jnp.float32
BF16 = jnp.bfloat16
I32 = jnp.int32

LRU_HEADS = 8
CONV_WIDTH = 4
LRU_C = 8.0
POOL_WINDOWS = (2, 4, 8, 16)
POOL_PAD = max(POOL_WINDOWS) - 1
HEAD_DIM = 64
N_KV_HEADS = 4
WINDOW = 128
ROPE_THETA = 10000.0
ATTN_SCALE = HEAD_DIM ** -0.5
N_EXPERT_GROUPS = 4
EXPERTS_PER_GROUP = 4
N_EXPERTS = N_EXPERT_GROUPS * EXPERTS_PER_GROUP
RMS_EPS = 1e-6
PAST_LEN = 16384

LANES = 128
HALF = LANES // 2
HALO = 16
ROW_BLOCK = 512
W_STAGE_ROWS = 128
MOE_CHUNK = 128
PAIR_SLOT_A = (0, 0, 0, 1, 1, 3)
PAIR_SLOT_B = (1, 2, 3, 3, 2, 2)
N_PAIRS = len(PAIR_SLOT_A)
N_CLASSES = N_EXPERT_GROUPS * N_PAIRS
VMEM_LIMIT = 52 * 1024 * 1024


def _rms(x, g):
    return (x * lax.rsqrt(jnp.mean(x * x, axis=-1, keepdims=True) + RMS_EPS)) * g


def _load_weight_bf16(w_hbm, w_bf, stage, sem):
    rows = stage.shape[1]
    n = w_hbm.shape[0] // rows

    def copy(i, sl):
        return pltpu.make_async_copy(w_hbm.at[pl.ds(i * rows, rows)], stage.at[sl], sem.at[sl])

    copy(0, 0).start()
    for i in range(n):
        sl = i % 2
        if i + 1 < n:
            copy(i + 1, 1 - sl).start()
        copy(i, sl).wait()
        w_bf[i * rows:(i + 1) * rows, :] = stage[sl].astype(BF16)


def _slab_pitch(d):
    return d // LANES + 1


def _read_slabs(ref, lead, n_tok, n_slabs, pitch):
    return jnp.concatenate(
        [ref[lead + (pl.ds(k, n_tok, stride=pitch), slice(None))] for k in range(n_slabs)], axis=1)


def _write_slabs(ref, lead, val, pitch):
    n_tok = val.shape[0]
    for k in range(val.shape[1] // LANES):
        ref[lead + (pl.ds(k, n_tok, stride=pitch), slice(None))] = val[:, k * LANES:(k + 1) * LANES]


def _pad_rows(x, n):
    if x.shape[0] == n:
        return x
    return jnp.concatenate([x, jnp.zeros((n - x.shape[0], x.shape[1]), x.dtype)], axis=0)


def _pick_rows(i, n_main_blocks, main_ref, tail_ref, d, slab):
    if slab:
        main = _read_slabs(main_ref, (), ROW_BLOCK, d // LANES, _slab_pitch(d))
        tail = _read_slabs(tail_ref, (), ROW_BLOCK, d // LANES, _slab_pitch(d))
    else:
        main, tail = main_ref[...], _pad_rows(tail_ref[...], ROW_BLOCK)
    return jnp.where(i < n_main_blocks, main, tail)


def _in_proj_kernel(xm_ref, xt_ref, g_ref, w_hbm, o_ref, w_bf, stage, sem, *, n_main_blocks, layer, slab):
    i = pl.program_id(0)

    @pl.when(i == 0)
    def _():
        _load_weight_bf16(w_hbm.at[layer], w_bf, stage, sem)

    h = _rms(_pick_rows(i, n_main_blocks, xm_ref, xt_ref, g_ref.shape[1], slab), g_ref[...])
    o_ref[...] = jnp.dot(h.astype(BF16), w_bf[...], preferred_element_type=F32)


def _row_specs(d, n_main_blocks, slab, n_tail):
    if slab:
        shape = (ROW_BLOCK * _slab_pitch(d), LANES)
        tail = pl.BlockSpec(shape, lambda i: (n_main_blocks, 0))
    else:
        shape = (ROW_BLOCK, d)
        tail = pl.BlockSpec((n_tail, d), lambda i: (0, 0))
    return [pl.BlockSpec(shape, lambda i: (jnp.minimum(i, n_main_blocks - 1), 0)), tail]


def _in_proj(x_main, x_tail, slab, g, w, layer, n_main, n_tail):
    d = g.shape[1]
    d_in = w.shape[2]
    nmb = n_main // ROW_BLOCK
    kern = functools.partial(_in_proj_kernel, n_main_blocks=nmb, layer=layer, slab=slab)
    return pl.pallas_call(
        kern,
        grid=(nmb + 1,),
        in_specs=_row_specs(d, nmb, slab, n_tail) + [
            pl.BlockSpec((1, d), lambda i: (0, 0)),
            pl.BlockSpec(memory_space=pl.ANY),
        ],
        out_specs=pl.BlockSpec((ROW_BLOCK, d_in), lambda i: (i, 0)),
        out_shape=jax.ShapeDtypeStruct((n_main + ROW_BLOCK, d_in), F32),
        scratch_shapes=[
            pltpu.VMEM((d, d_in), BF16),
            pltpu.VMEM((2, W_STAGE_ROWS, d_in), F32),
            pltpu.SemaphoreType.DMA((2,)),
        ],
        compiler_params=pltpu.CompilerParams(
            dimension_semantics=("arbitrary",), vmem_limit_bytes=VMEM_LIMIT),
        name="in_proj",
    )(x_main, x_tail, g, w)


def _scan_linear(a, b):
    t = a.shape[0]
    row = lax.broadcasted_iota(I32, a.shape, 0)
    d = 1
    while d < min(t, 8):
        a_sh = pltpu.roll(a, d, axis=0)
        b_sh = pltpu.roll(b, d, axis=0)
        m = row >= d
        b = jnp.where(m, a * b_sh + b, b)
        a = jnp.where(m, a * a_sh, a)
        d *= 2
    while d < t:
        b = jnp.concatenate([b[:d], a[d:] * b[:t - d] + b[d:]], axis=0)
        a = jnp.concatenate([a[:d], a[d:] * a[:t - d]], axis=0)
        d *= 2
    return a, b


def _lru_chunk(xc, g, h_prev, wg, ba, bx, sp):
    pre = jnp.dot(xc.astype(BF16), wg, preferred_element_type=F32)
    r = jax.nn.sigmoid(pre[:, :LANES] + ba)
    ig = jax.nn.sigmoid(pre[:, LANES:] + bx)
    log_a = (-LRU_C * r) * sp
    a = jnp.exp(log_a)
    bterm = jnp.sqrt(-jnp.tanh(log_a) * (a * a + 1.0)) * ig * xc
    a_cum, h0 = _scan_linear(a, bterm)
    hs = a_cum * h_prev + h0
    return hs * jax.nn.gelu(g), hs


def _rope(x, cos, sin_signed):
    n = x.shape[1] // LANES
    lane = lax.broadcasted_iota(I32, (x.shape[0], LANES), 1)
    first = (lane % HEAD_DIM) < (HEAD_DIM // 2)
    outs = []
    for c in range(n):
        xc = x[:, c * LANES:(c + 1) * LANES]
        swapped = jnp.where(first, pltpu.roll(xc, LANES - HEAD_DIM // 2, axis=1),
                            pltpu.roll(xc, HEAD_DIM // 2, axis=1))
        outs.append(xc * cos + swapped * sin_signed)
    return outs


def _store_head_variants(var_ref, row0, chunks):
    t = chunks[0].shape[0]
    lane = lax.broadcasted_iota(I32, (t, LANES), 1)
    for kc, x in enumerate(chunks):
        swapped = pltpu.roll(x, HALF, axis=1)
        for hh in range(2):
            for p in range(2):
                src = x if p == hh else swapped
                keep = (lane < HALF) if p == 0 else (lane >= HALF)
                var_ref[2 * kc + hh, p, row0:row0 + t, :] = jnp.where(keep, src, 0.0).astype(BF16)


def _attend_block(q_chunks, kvar, vvar, key0, sinks, lim, out_ref, out_rows, out_col0):
    qb = q_chunks[0].shape[0]
    nk = 2 * WINDOW
    gq = (2 * len(q_chunks)) // N_KV_HEADS
    rows = 2 * qb
    qi = lax.broadcasted_iota(I32, (rows, nk), 0) % qb
    kj = lax.broadcasted_iota(I32, (rows, nk), 1)
    valid = ((kj < WINDOW) & (kj > qi + lim)) | ((kj >= WINDOW) & (kj - WINDOW <= qi))
    top = lax.broadcasted_iota(I32, (rows, 1), 0) < qb
    lane = lax.broadcasted_iota(I32, (nk, LANES), 1)
    ones_lo = jnp.where(lane < HALF, 1.0, 0.0).astype(BF16)
    ones_hi = jnp.where(lane >= HALF, 1.0, 0.0).astype(BF16)
    lane_o = lax.broadcasted_iota(I32, (rows, LANES), 1)
    nt = (((1,), (1,)), ((), ()))
    for c in range(N_KV_HEADS):
        c0 = c * gq // 2
        qs = jnp.concatenate([q_chunks[c0], q_chunks[c0 + 1]], axis=0)
        es, sink_terms = [], []
        for p in range(2):
            kc = kvar[c, p, key0:key0 + nk, :]
            s = lax.dot_general(qs, kc, nt, preferred_element_type=F32)
            s = jnp.where(valid, s, -jnp.inf)
            h0 = c * gq + p
            sink = jnp.where(top, sinks[:, h0:h0 + 1], sinks[:, h0 + 2:h0 + 3])
            m = jnp.maximum(jnp.max(s, axis=-1, keepdims=True), sink)
            es.append(jnp.exp(s - m).astype(BF16))
            sink_terms.append(jnp.exp(sink - m))
        r0 = jnp.concatenate([vvar[c, 0, key0:key0 + nk, :], ones_lo], axis=1)
        r1 = jnp.concatenate([vvar[c, 1, key0:key0 + nk, :], ones_hi], axis=1)
        od = (jnp.dot(es[0], r0, preferred_element_type=F32)
              + jnp.dot(es[1], r1, preferred_element_type=F32))
        den = od[:, LANES:] + jnp.where(lane_o < HALF, sink_terms[0], sink_terms[1])
        o = (od[:, :LANES] / den).astype(out_ref.dtype)
        out_ref[out_rows, out_col0 + c0 * LANES:out_col0 + (c0 + 1) * LANES] = o[0:qb]
        out_ref[out_rows, out_col0 + (c0 + 1) * LANES:out_col0 + (c0 + 2) * LANES] = o[qb:rows]


def _mixer_kernel(*refs, tile, qblock, is_prompt, pos0, d_lru, d_pool, d_attn, d_kv):
    (proj_ref, cos_ref, sin_ref, convw_ref, convb_ref, wg_ref, ba_ref, bx_ref, lam_ref,
     poolw_ref, pscale_ref, sink_ref) = refs[:12]
    if is_prompt:
        (mix_ref, ko_ref, vo_ref, ho_ref, co_ref, po_ref,
         lru_ext, pool_ext, kvar, vvar, hcar) = refs[12:]
    else:
        (ck_ref, cv_ref, sh_ref, sc_ref, sp_ref,
         mix_ref, ko_ref, vo_ref, ho_ref, co_ref, po_ref,
         lru_ext, pool_ext, kvar, vvar) = refs[12:]

    s = pl.program_id(1) if is_prompt else 0
    o_glru = d_lru
    o_pool = 2 * d_lru
    o_q = o_pool + d_pool
    o_k = o_q + d_attn
    o_v = o_k + d_kv
    n_kc = d_kv // LANES

    if is_prompt:
        @pl.when(s == 0)
        def _():
            lru_ext[0:HALO, :] = jnp.zeros((HALO, d_lru), F32)
            pool_ext[0:HALO, :] = jnp.zeros((HALO, d_pool), F32)
            hcar[...] = jnp.zeros_like(hcar)
            kvar[:, :, 0:WINDOW, :] = jnp.zeros((N_KV_HEADS, 2, WINDOW, LANES), BF16)
            vvar[:, :, 0:WINDOW, :] = jnp.zeros((N_KV_HEADS, 2, WINDOW, LANES), BF16)
        h_prev = hcar[0:1, :]
    else:
        lru_ext[0:HALO, :] = jnp.zeros((HALO, d_lru), F32)
        pool_ext[0:HALO, :] = jnp.zeros((HALO, d_pool), F32)
        lru_ext[HALO - (CONV_WIDTH - 1):HALO, :] = sc_ref[0]
        pool_ext[HALO - POOL_PAD:HALO, :] = sp_ref[0]
        h_prev = sh_ref[0]
        kvar[:, :, WINDOW:2 * WINDOW, :] = jnp.zeros((N_KV_HEADS, 2, WINDOW, LANES), BF16)
        vvar[:, :, WINDOW:2 * WINDOW, :] = jnp.zeros((N_KV_HEADS, 2, WINDOW, LANES), BF16)
        _store_head_variants(kvar, 0, [ck_ref[0, :, c * LANES:(c + 1) * LANES] for c in range(n_kc)])
        _store_head_variants(vvar, 0, [cv_ref[0, :, c * LANES:(c + 1) * LANES] for c in range(n_kc)])

    lru_ext[HALO:HALO + tile, :] = proj_ref[:, 0:d_lru]
    pool_ext[HALO:HALO + tile, :] = proj_ref[:, o_pool:o_pool + d_pool]

    xc = convb_ref[...] + convw_ref[0:1, :] * lru_ext[HALO - 3:HALO - 3 + tile, :]
    for j in range(1, CONV_WIDTH):
        xc = xc + convw_ref[j:j + 1, :] * lru_ext[HALO - 3 + j:HALO - 3 + j + tile, :]
    neg = -lam_ref[...]
    sp_all = jnp.maximum(neg, 0.0) + jnp.log1p(jnp.exp(-jnp.abs(neg)))
    h_last = []
    for c in range(d_lru // LANES):
        cs = slice(c * LANES, (c + 1) * LANES)
        y, hs = _lru_chunk(xc[:, cs], proj_ref[:, o_glru + c * LANES:o_glru + (c + 1) * LANES],
                           h_prev[:, cs], wg_ref[c], ba_ref[:, cs], bx_ref[:, cs], sp_all[:, cs])
        mix_ref[:, cs] = y.astype(mix_ref.dtype)
        h_last.append(hs[tile - 1:tile, :])
    h_last = jnp.concatenate(h_last, axis=-1)

    row = lax.broadcasted_iota(I32, (tile, LANES), 0)
    pos = pos0 + s * tile + row
    for gi, w in enumerate(POOL_WINDOWS):
        cs = slice(gi * LANES, (gi + 1) * LANES)
        e = pool_ext[:, cs]
        acc = e
        step = 1
        while step < w:
            acc = acc + pltpu.roll(acc, step, axis=0)
            step *= 2
        cnt = jnp.minimum(pos + 1, w).astype(F32)
        dlt = acc[HALO:HALO + tile, :] / cnt - e[HALO:HALO + tile, :]
        y = jnp.dot(dlt.astype(BF16), poolw_ref[gi], preferred_element_type=F32) * pscale_ref[:, cs]
        mix_ref[:, d_lru + gi * LANES:d_lru + (gi + 1) * LANES] = y.astype(mix_ref.dtype)

    sinks = sink_ref[...]
    o_attn = d_lru + d_pool
    cos = cos_ref[...]
    sin = sin_ref[...]
    q_chunks = [(qc * ATTN_SCALE).astype(BF16) for qc in _rope(proj_ref[:, o_q:o_q + d_attn], cos, sin)]
    k_rot = _rope(proj_ref[:, o_k:o_k + d_kv], cos, sin)
    _store_head_variants(kvar, WINDOW, k_rot)
    _store_head_variants(vvar, WINDOW, [proj_ref[:, o_v + c * LANES:o_v + (c + 1) * LANES]
                                        for c in range(n_kc)])
    for blk in range(tile // qblock):
        rs = slice(blk * qblock, (blk + 1) * qblock)
        if is_prompt and blk == 0:
            lim = jnp.where(s == 0, WINDOW, 0)
        else:
            lim = 0
        _attend_block([qc[rs] for qc in q_chunks], kvar, vvar, blk * qblock, sinks, lim,
                      mix_ref, rs, o_attn)

    k_last = jnp.concatenate([kc[tile - qblock:tile] for kc in k_rot], axis=-1)
    if is_prompt:
        lru_ext[0:HALO, :] = lru_ext[tile:tile + HALO, :]
        pool_ext[0:HALO, :] = pool_ext[tile:tile + HALO, :]
        hcar[0:1, :] = h_last
        kvar[:, :, 0:WINDOW, :] = kvar[:, :, tile:tile + WINDOW, :]
        vvar[:, :, 0:WINDOW, :] = vvar[:, :, tile:tile + WINDOW, :]

        @pl.when(s == pl.num_programs(1) - 1)
        def _():
            ko_ref[0] = k_last
            vo_ref[0] = proj_ref[tile - qblock:tile, o_v:o_v + d_kv]
            ho_ref[0] = h_last
            co_ref[0] = lru_ext[HALO + tile - (CONV_WIDTH - 1):HALO + tile, :]
            po_ref[0] = pool_ext[HALO + tile - POOL_PAD:HALO + tile, :]
    else:
        ko_ref[0, 0:WINDOW - tile, :] = ck_ref[0, tile:WINDOW, :]
        ko_ref[0, WINDOW - tile:WINDOW, :] = k_last
        vo_ref[0, 0:WINDOW - tile, :] = cv_ref[0, tile:WINDOW, :]
        vo_ref[0, WINDOW - tile:WINDOW, :] = proj_ref[:, o_v:o_v + d_kv]
        ho_ref[0] = h_last
        co_ref[0] = lru_ext[HALO + tile - (CONV_WIDTH - 1):HALO + tile, :]
        po_ref[0] = pool_ext[HALO + tile - POOL_PAD:HALO + tile, :]


def _layer_consts(p, dims):
    d_lru, d_pool = dims["d_lru"], dims["d_pool"]
    hd = d_lru // LRU_HEADS
    per = LANES // hd
    nchunk = d_lru // LANES

    def blockdiag(w):
        w = w.reshape(nchunk, per, hd, hd)
        eye = jnp.eye(per, dtype=w.dtype)
        return jnp.einsum("cpij,pq->cpiqj", w, eye).reshape(nchunk, LANES, LANES)

    wg = jnp.concatenate([blockdiag(p["lru_wa"]), blockdiag(p["lru_wx"])], axis=-1).astype(BF16)
    return dict(
        convw=p["conv_w"], convb=p["conv_b"].reshape(1, d_lru), wg=wg,
        ba=p["lru_ba"].reshape(1, d_lru), bx=p["lru_bx"].reshape(1, d_lru),
        lam=p["lru_lambda"].reshape(1, d_lru), poolw=p["pool_w"].astype(BF16),
        pscale=p["pool_scale"].reshape(1, d_pool), sinks=p["attn_sinks"].reshape(1, -1))


def _rope_tables(pos):
    half = HEAD_DIM // 2
    inv = ROPE_THETA ** (-jnp.arange(half, dtype=F32) / half)
    ang = pos.astype(F32)[:, None] * inv[None, :]
    cos = jnp.cos(ang)
    sin = jnp.sin(ang)
    cos2 = jnp.concatenate([cos, cos], axis=-1)
    sin2 = jnp.concatenate([-sin, sin], axis=-1)
    reps = LANES // HEAD_DIM
    return jnp.tile(cos2, (1, reps)), jnp.tile(sin2, (1, reps))


CONST_NAMES = ("convw", "convb", "wg", "ba", "bx", "lam", "poolw", "pscale", "sinks")


def _const_specs(consts):
    return [pl.BlockSpec(consts[n].shape, functools.partial(lambda nd, *_: (0,) * nd, consts[n].ndim))
            for n in CONST_NAMES]


def _mixer_prompt(proj, consts, cos, sin, bsz, seq, dims, tile):
    d_lru, d_pool, d_attn, d_kv = dims["d_lru"], dims["d_pool"], dims["d_attn"], dims["d_kv"]
    d_in = proj.shape[1]
    d_mix = d_lru + d_pool + d_attn
    ns = seq // tile
    kern = functools.partial(_mixer_kernel, tile=tile, qblock=WINDOW, is_prompt=True, pos0=0,
                             d_lru=d_lru, d_pool=d_pool, d_attn=d_attn, d_kv=d_kv)
    return pl.pallas_call(
        kern,
        grid=(bsz, ns),
        in_specs=[
            pl.BlockSpec((tile, d_in), lambda b, s: (b * ns + s, 0)),
            pl.BlockSpec((tile, LANES), lambda b, s: (s, 0)),
            pl.BlockSpec((tile, LANES), lambda b, s: (s, 0)),
        ] + _const_specs(consts),
        out_specs=[
            pl.BlockSpec((tile, d_mix), lambda b, s: (b * ns + s, 0)),
            pl.BlockSpec((1, WINDOW, d_kv), lambda b, s: (b, 0, 0)),
            pl.BlockSpec((1, WINDOW, d_kv), lambda b, s: (b, 0, 0)),
            pl.BlockSpec((1, 1, d_lru), lambda b, s: (b, 0, 0)),
            pl.BlockSpec((1, CONV_WIDTH - 1, d_lru), lambda b, s: (b, 0, 0)),
            pl.BlockSpec((1, POOL_PAD, d_pool), lambda b, s: (b, 0, 0)),
        ],
        out_shape=[
            jax.ShapeDtypeStruct((bsz * seq, d_mix), BF16),
            jax.ShapeDtypeStruct((bsz, WINDOW, d_kv), F32),
            jax.ShapeDtypeStruct((bsz, WINDOW, d_kv), F32),
            jax.ShapeDtypeStruct((bsz, 1, d_lru), F32),
            jax.ShapeDtypeStruct((bsz, CONV_WIDTH - 1, d_lru), F32),
            jax.ShapeDtypeStruct((bsz, POOL_PAD, d_pool), F32),
        ],
        scratch_shapes=[
            pltpu.VMEM((tile + HALO, d_lru), F32),
            pltpu.VMEM((tile + HALO, d_pool), F32),
            pltpu.VMEM((N_KV_HEADS, 2, WINDOW + tile, LANES), BF16),
            pltpu.VMEM((N_KV_HEADS, 2, WINDOW + tile, LANES), BF16),
            pltpu.VMEM((8, d_lru), F32),
        ],
        compiler_params=pltpu.CompilerParams(
            dimension_semantics=("arbitrary", "arbitrary"), vmem_limit_bytes=VMEM_LIMIT),
        name="mixer_prompt",
    )(proj, cos, sin, *[consts[n] for n in CONST_NAMES])


def _mixer_sample(proj, row0, consts, cos, sin, cache_k, cache_v, st_h, st_conv, st_pool, layer, dims):
    d_lru, d_pool, d_attn, d_kv = dims["d_lru"], dims["d_pool"], dims["d_attn"], dims["d_kv"]
    d_in = proj.shape[1]
    d_mix = d_lru + d_pool + d_attn
    db, win = cache_k.shape[1], cache_k.shape[2]
    t = cos.shape[0]
    blk0 = row0 // t
    kern = functools.partial(_mixer_kernel, tile=t, qblock=t, is_prompt=False, pos0=PAST_LEN,
                             d_lru=d_lru, d_pool=d_pool, d_attn=d_attn, d_kv=d_kv)
    return pl.pallas_call(
        kern,
        grid=(db,),
        in_specs=[
            pl.BlockSpec((t, d_in), lambda b: (blk0 + b, 0)),
            pl.BlockSpec((t, LANES), lambda b: (0, 0)),
            pl.BlockSpec((t, LANES), lambda b: (0, 0)),
        ] + _const_specs(consts) + [
            pl.BlockSpec((None, 1, win, d_kv), lambda b: (layer, b, 0, 0)),
            pl.BlockSpec((None, 1, win, d_kv), lambda b: (layer, b, 0, 0)),
            pl.BlockSpec((None, 1, 1, d_lru), lambda b: (layer, b, 0, 0)),
            pl.BlockSpec((None, 1, CONV_WIDTH - 1, d_lru), lambda b: (layer, b, 0, 0)),
            pl.BlockSpec((None, 1, POOL_PAD, d_pool), lambda b: (layer, b, 0, 0)),
        ],
        out_specs=[
            pl.BlockSpec((t, d_mix), lambda b: (b, 0)),
            pl.BlockSpec((1, win, d_kv), lambda b: (b, 0, 0)),
            pl.BlockSpec((1, win, d_kv), lambda b: (b, 0, 0)),
            pl.BlockSpec((1, 1, d_lru), lambda b: (b, 0, 0)),
            pl.BlockSpec((1, CONV_WIDTH - 1, d_lru), lambda b: (b, 0, 0)),
            pl.BlockSpec((1, POOL_PAD, d_pool), lambda b: (b, 0, 0)),
        ],
        out_shape=[
            jax.ShapeDtypeStruct((db * t, d_mix), F32),
            jax.ShapeDtypeStruct((db, win, d_kv), F32),
            jax.ShapeDtypeStruct((db, win, d_kv), F32),
            jax.ShapeDtypeStruct((db, 1, d_lru), F32),
            jax.ShapeDtypeStruct((db, CONV_WIDTH - 1, d_lru), F32),
            jax.ShapeDtypeStruct((db, POOL_PAD, d_pool), F32),
        ],
        scratch_shapes=[
            pltpu.VMEM((t + HALO, d_lru), F32),
            pltpu.VMEM((t + HALO, d_pool), F32),
            pltpu.VMEM((N_KV_HEADS, 2, 2 * WINDOW, LANES), BF16),
            pltpu.VMEM((N_KV_HEADS, 2, 2 * WINDOW, LANES), BF16),
        ],
        compiler_params=pltpu.CompilerParams(
            dimension_semantics=("arbitrary",), vmem_limit_bytes=VMEM_LIMIT),
        name="mixer_sample",
    )(proj, cos, sin, *[consts[n] for n in CONST_NAMES], cache_k, cache_v, st_h, st_conv, st_pool)


def _route(logits, run_cnt, n_valid):
    t = logits.shape[0]
    lane = lax.broadcasted_iota(I32, (t, LANES), 1)
    lane_f = lane.astype(F32)
    ninf = -jnp.inf
    big = float(LANES)
    is_g = lane < N_EXPERT_GROUPS
    lg = jnp.where(is_g, logits, ninf)
    mg = jnp.max(lg, axis=-1, keepdims=True)
    g_top = jnp.min(jnp.where(lg == mg, lane_f, big), axis=-1, keepdims=True).astype(I32)
    pg_top = 1.0 / jnp.sum(jnp.exp(lg - mg), axis=-1, keepdims=True)
    base = N_EXPERT_GROUPS + EXPERTS_PER_GROUP * g_top
    in_grp = (lane >= base) & (lane < base + EXPERTS_PER_GROUP)
    le = jnp.where(in_grp, logits, ninf)
    m1 = jnp.max(le, axis=-1, keepdims=True)
    i1 = jnp.min(jnp.where(le == m1, lane_f, big), axis=-1, keepdims=True).astype(I32)
    le2 = jnp.where(lane == i1, ninf, le)
    m2 = jnp.max(le2, axis=-1, keepdims=True)
    i2 = jnp.min(jnp.where((le2 == m2) & in_grp & (lane != i1), lane_f, big),
                 axis=-1, keepdims=True).astype(I32)
    se = jnp.sum(jnp.exp(le - m1), axis=-1, keepdims=True)
    p1 = 1.0 / se
    p2 = jnp.exp(m2 - m1) / se
    tot = p1 + p2
    w1 = (p1 / tot) * pg_top
    w2 = (p2 / tot) * pg_top
    a = i1 - base
    b = i2 - base
    lo = jnp.minimum(a, b)
    hi = jnp.maximum(a, b)
    w_lo = jnp.where(a < b, w1, w2)
    w_hi = jnp.where(a < b, w2, w1)
    pid = jnp.where(lo == 0, hi - 1, jnp.where(lo == 1, jnp.where(hi == 3, 3, 4), 5))
    swap = pid == 5
    w_a = jnp.where(swap, w_hi, w_lo)
    w_b = jnp.where(swap, w_lo, w_hi)
    cls = g_top * N_PAIRS + pid
    onehot = (lane == cls) & (lax.broadcasted_iota(I32, (t, LANES), 0) < n_valid)
    ti = lax.broadcasted_iota(I32, (t, t), 0)
    tj = lax.broadcasted_iota(I32, (t, t), 1)
    lower = jnp.where(tj <= ti, 1.0, 0.0).astype(BF16)
    prefix = jnp.dot(lower, jnp.where(onehot, 1.0, 0.0).astype(BF16), preferred_element_type=F32)
    rank = jnp.sum(jnp.where(onehot, prefix - 1.0 + run_cnt, 0.0), axis=-1, keepdims=True)
    info = jnp.where(lane == 0, w_a, jnp.where(lane == 1, w_b, jnp.where(
        lane == 2, cls.astype(F32), jnp.where(lane == 3, rank, 0.0))))
    return info, run_cnt + prefix[t - 1:t, :]


def _out_proj_kernel(mixp_ref, mixs_ref, xm_ref, xt_ref, w_hbm, g_ref, wr_ref, br_ref,
                     o_ref, info_ref, cnt_ref, w_bf, stage, sem, run_cnt, *, n_main_blocks, d, layer, slab):
    i = pl.program_id(0)

    @pl.when(i == 0)
    def _():
        _load_weight_bf16(w_hbm.at[layer], w_bf, stage, sem)
        run_cnt[...] = jnp.zeros_like(run_cnt)

    n_tail = mixs_ref.shape[0]
    mix = jnp.where(i < n_main_blocks, mixp_ref[...], _pad_rows(mixs_ref[...].astype(BF16), ROW_BLOCK))
    xres = (_pick_rows(i, n_main_blocks, xm_ref, xt_ref, d, slab)
            + jnp.dot(mix, w_bf[...], preferred_element_type=F32))
    h2 = _rms(xres, g_ref[...])
    logits = jnp.dot(h2.astype(BF16), wr_ref[...], preferred_element_type=F32) + br_ref[...]
    info, cnt = _route(logits, run_cnt[0:1, :], jnp.where(i < n_main_blocks, ROW_BLOCK, n_tail))
    run_cnt[0:1, :] = cnt
    pitch = _slab_pitch(d)
    _write_slabs(o_ref, (), xres, pitch)
    o_ref[pl.ds(d // LANES, ROW_BLOCK, stride=pitch), :] = info
    info_ref[...] = info
    cnt_ref[...] = jnp.broadcast_to(cnt, cnt_ref.shape)


def _out_proj(mix_p, mix_s, x_main, x_tail, slab, w, layer, g, wr, br):
    d = g.shape[1]
    d_mix = w.shape[1]
    n_main, n_tail = mix_p.shape[0], mix_s.shape[0]
    npb = n_main // ROW_BLOCK
    pitch = _slab_pitch(d)
    kern = functools.partial(_out_proj_kernel, n_main_blocks=npb, d=d, layer=layer, slab=slab)
    return pl.pallas_call(
        kern,
        grid=(npb + 1,),
        in_specs=[
            pl.BlockSpec((ROW_BLOCK, d_mix), lambda i: (jnp.minimum(i, npb - 1), 0)),
            pl.BlockSpec((n_tail, d_mix), lambda i: (0, 0)),
        ] + _row_specs(d, npb, slab, n_tail) + [
            pl.BlockSpec(memory_space=pl.ANY),
            pl.BlockSpec((1, d), lambda i: (0, 0)),
            pl.BlockSpec((d, LANES), lambda i: (0, 0)),
            pl.BlockSpec((1, LANES), lambda i: (0, 0)),
        ],
        out_specs=[pl.BlockSpec((ROW_BLOCK * pitch, LANES), lambda i: (i, 0)),
                   pl.BlockSpec((ROW_BLOCK, LANES), lambda i: (i, 0)),
                   pl.BlockSpec((8, LANES), lambda i: (0, 0))],
        out_shape=[jax.ShapeDtypeStruct(((n_main + ROW_BLOCK) * pitch, LANES), F32),
                   jax.ShapeDtypeStruct((n_main + ROW_BLOCK, LANES), F32),
                   jax.ShapeDtypeStruct((8, LANES), F32)],
        scratch_shapes=[
            pltpu.VMEM((d_mix, d), BF16),
            pltpu.VMEM((2, W_STAGE_ROWS, d), F32),
            pltpu.SemaphoreType.DMA((2,)),
            pltpu.VMEM((8, LANES), F32),
        ],
        compiler_params=pltpu.CompilerParams(
            dimension_semantics=("arbitrary",), vmem_limit_bytes=VMEM_LIMIT),
        name="out_proj",
    )(mix_p, mix_s, x_main, x_tail, w, g, wr, br)


def _moe_kernel(cls_ref, rank_ref, off_ref, cnt_ref, ea_ref, eb_ref, chga_ref, chgb_ref, nch_ref,
                xr_hbm, g_ref, w1a_ref, w3a_ref, w2a_ref, w1b_ref, w3b_ref, w2b_ref,
                xo_hbm,
                perm, xbuf, obuf, wa1, wa3, wa2, wb1, wb3, wb2, gsem, ssem, *, d, n_rows):
    del ea_ref, eb_ref
    m = MOE_CHUNK
    pitch = _slab_pitch(d)
    n_slabs = d // LANES
    c = pl.program_id(0)
    n = nch_ref[0]
    slot = c % 2
    other = 1 - slot

    def prio(j):
        return j % 2 if isinstance(j, int) else 0

    def start_gather(chunk, sl, j):
        tok = jnp.maximum(perm[(chunk + 1) * m + j], 0)
        pltpu.make_async_copy(xr_hbm.at[pl.ds(tok * pitch, pitch)],
                              xbuf.at[sl, pl.ds(j * pitch, pitch)], gsem.at[sl]).start(priority=prio(j))

    def start_scatter(chunk, sl, j):
        tok = perm[(chunk + 1) * m + j]
        r = jnp.where(tok < 0, n_rows + sl * m + j, tok)
        pltpu.make_async_copy(obuf.at[sl, pl.ds(j * pitch, pitch)],
                              xo_hbm.at[pl.ds(r * pitch, pitch)], ssem.at[sl]).start(priority=prio(j))

    def wait_gather(sl):
        pltpu.make_async_copy(xr_hbm.at[pl.ds(0, m * pitch)], xbuf.at[sl], gsem.at[sl]).wait()

    def wait_scatter(sl):
        pltpu.make_async_copy(obuf.at[sl], xo_hbm.at[pl.ds(0, m * pitch)], ssem.at[sl]).wait()

    @pl.when(c == 0)
    def _():
        obuf[...] = jnp.zeros_like(obuf)
        pad = pltpu.make_async_copy(obuf.at[0], xo_hbm.at[pl.ds(n_rows * pitch, m * pitch)], ssem.at[0])
        pad.start()
        pad.wait()

        def fill(lo, hi):
            def body(i, z):
                perm[i] = -1
                return z
            lax.fori_loop(lo, hi, body, 0)

        def first_gather(j, z):
            start_gather(0, 0, j)
            return z

        fill(0, m)
        fill((n + 1) * m, (n + 3) * m)
        for k in range(N_CLASSES):
            base = (off_ref[k] + 1) * m
            cnt = cnt_ref[k]
            fill(base + cnt, base + ((cnt + m - 1) // m) * m)

        def place(t, z):
            perm[(off_ref[cls_ref[t]] + 1) * m + rank_ref[t]] = t
            return z
        lax.fori_loop(0, n_rows, place, 0, unroll=8)
        lax.fori_loop(0, m, first_gather, 0)

    @pl.when(c <= n)
    def _():
        wait_gather(slot)

        @pl.when(c >= 1)
        def _():
            wait_scatter(slot)

        @pl.when(chga_ref[c] == 1)
        def _():
            wa1[...] = w1a_ref[0].astype(BF16)
            wa3[...] = w3a_ref[0].astype(BF16)
            wa2[...] = w2a_ref[0].astype(BF16)

        @pl.when(chgb_ref[c] == 1)
        def _():
            wb1[...] = w1b_ref[0].astype(BF16)
            wb3[...] = w3b_ref[0].astype(BF16)
            wb2[...] = w2b_ref[0].astype(BF16)

        for j in range(m):
            start_gather(c + 1, other, j)
        for j in range(m):
            start_scatter(c - 1, other, j)
        x = _read_slabs(xbuf, (slot,), m, n_slabs, pitch)
        info = xbuf[slot, pl.ds(n_slabs, m, stride=pitch), :]
        w_a = info[:, 0:1]
        w_b = info[:, 1:2]
        h = _rms(x, g_ref[...]).astype(BF16)
        hid_a = (jax.nn.silu(jnp.dot(h, wa1[...], preferred_element_type=F32))
                 * jnp.dot(h, wa3[...], preferred_element_type=F32) * w_a).astype(BF16)
        hid_b = (jax.nn.silu(jnp.dot(h, wb1[...], preferred_element_type=F32))
                 * jnp.dot(h, wb3[...], preferred_element_type=F32) * w_b).astype(BF16)
        y = (jnp.dot(hid_a, wa2[...], preferred_element_type=F32)
             + jnp.dot(hid_b, wb2[...], preferred_element_type=F32))
        _write_slabs(obuf, (slot,), x + y, pitch)

        @pl.when(c == n)
        def _():
            wait_gather(other)
            wait_scatter(other)


def _moe(xr, g, w1, w3, w2, tables, n_rows, c_max):
    d = g.shape[1]
    f = w1.shape[2]
    m = MOE_CHUNK
    pitch = _slab_pitch(d)
    kern = functools.partial(_moe_kernel, d=d, n_rows=n_rows)

    def wspec(shape, which):
        return pl.BlockSpec(shape, lambda c, *pref: (pref[which][c], 0, 0))

    grid_spec = pltpu.PrefetchScalarGridSpec(
        num_scalar_prefetch=9,
        grid=(c_max + 1,),
        in_specs=[
            pl.BlockSpec(memory_space=pl.ANY),
            pl.BlockSpec((1, d), lambda c, *pref: (0, 0)),
            wspec((1, d, f), 4), wspec((1, d, f), 4), wspec((1, f, d), 4),
            wspec((1, d, f), 5), wspec((1, d, f), 5), wspec((1, f, d), 5),
        ],
        out_specs=pl.BlockSpec(memory_space=pl.ANY),
        scratch_shapes=[
            pltpu.SMEM(((c_max + 3) * m,), I32),
            pltpu.VMEM((2, m * pitch, LANES), F32),
            pltpu.VMEM((2, m * pitch, LANES), F32),
            pltpu.VMEM((d, f), BF16), pltpu.VMEM((d, f), BF16), pltpu.VMEM((f, d), BF16),
            pltpu.VMEM((d, f), BF16), pltpu.VMEM((d, f), BF16), pltpu.VMEM((f, d), BF16),
            pltpu.SemaphoreType.DMA((2,)),
            pltpu.SemaphoreType.DMA((2,)),
        ],
    )
    return pl.pallas_call(
        kern,
        grid_spec=grid_spec,
        out_shape=jax.ShapeDtypeStruct(((n_rows + 2 * m) * pitch, LANES), F32),
        compiler_params=pltpu.CompilerParams(
            dimension_semantics=("arbitrary",), vmem_limit_bytes=VMEM_LIMIT),
        name="moe",
    )(*tables, xr, g, w1, w3, w2, w1, w3, w2)


def _chunk_tables(counts, c_max, expert0):
    m = MOE_CHUNK
    nch_c = (counts + m - 1) // m
    ch_end = jnp.cumsum(nch_c)
    ch_off = ch_end - nch_c
    n_chunks = ch_end[-1]
    chunk = jnp.minimum(jnp.arange(c_max + 1, dtype=I32), jnp.maximum(n_chunks - 1, 0))
    ccls = jnp.sum((chunk[:, None] >= ch_end[None, :]).astype(I32), axis=1)
    ccls = jnp.minimum(ccls, N_CLASSES - 1)
    grp = ccls // N_PAIRS
    pid = ccls % N_PAIRS
    ea = expert0 + grp * EXPERTS_PER_GROUP + jnp.asarray(PAIR_SLOT_A, I32)[pid]
    eb = expert0 + grp * EXPERTS_PER_GROUP + jnp.asarray(PAIR_SLOT_B, I32)[pid]
    first = jnp.ones((1,), I32)
    chga = jnp.concatenate([first, (ea[1:] != ea[:-1]).astype(I32)])
    chgb = jnp.concatenate([first, (eb[1:] != eb[:-1]).astype(I32)])
    return ch_off.astype(I32), counts, ea, eb, chga, chgb, n_chunks.reshape(1).astype(I32)


def _final_norm_kernel(x_ref, g_ref, op_ref, os_ref, *, n_prompt_blocks):
    i = pl.program_id(0)
    d = g_ref.shape[1]
    y = _rms(_read_slabs(x_ref, (), ROW_BLOCK, d // LANES, _slab_pitch(d)), g_ref[...])

    @pl.when(i < n_prompt_blocks)
    def _():
        op_ref[...] = y

    @pl.when(i >= n_prompt_blocks)
    def _():
        os_ref[...] = y[0:os_ref.shape[0]]


def _final_norm(x, g, n_prompt, n_sample):
    d = g.shape[1]
    npb = n_prompt // ROW_BLOCK
    kern = functools.partial(_final_norm_kernel, n_prompt_blocks=npb)
    return pl.pallas_call(
        kern,
        grid=(npb + 1,),
        in_specs=[pl.BlockSpec((ROW_BLOCK * _slab_pitch(d), LANES), lambda i: (i, 0)),
                  pl.BlockSpec((1, d), lambda i: (0, 0))],
        out_specs=[pl.BlockSpec((ROW_BLOCK, d), lambda i: (jnp.minimum(i, npb - 1), 0)),
                   pl.BlockSpec((n_sample, d), lambda i: (0, 0))],
        out_shape=[jax.ShapeDtypeStruct((n_prompt, d), F32),
                   jax.ShapeDtypeStruct((n_sample, d), F32)],
        compiler_params=pltpu.CompilerParams(dimension_semantics=("arbitrary",)),
        name="final_norm",
    )(x, g)


def kernel(x_prompt, x_sample, cache_k, cache_v, state_lru_h, state_conv, state_pool, norm1_g, w_in, conv_w, conv_b, lru_wa, lru_ba, lru_wx, lru_bx, lru_lambda, pool_w, pool_scale, attn_sinks, w_out, norm2_g, router_group_w, router_group_b, router_expert_w, router_expert_b, expert_w1, expert_w3, expert_w2, final_norm_g):
    bsz, seq, d = x_prompt.shape
    db, ds, _ = x_sample.shape
    depth = w_in.shape[0]
    win = cache_k.shape[2]
    nkv = cache_k.shape[3]
    d_lru = lru_lambda.shape[1]
    d_pool = pool_scale.shape[1]
    d_kv = nkv * cache_k.shape[4]
    d_attn = attn_sinks.shape[1] * HEAD_DIM
    dims = dict(d_lru=d_lru, d_pool=d_pool, d_attn=d_attn, d_kv=d_kv)
    n_prompt = bsz * seq
    n_sample = db * ds
    n_rows = n_prompt + n_sample
    assert n_prompt % ROW_BLOCK == 0 and ROW_BLOCK - n_sample == 2 * MOE_CHUNK and n_sample % 8 == 0
    assert win == WINDOW and nkv == N_KV_HEADS and ds <= 8 and WINDOW % ds == 0
    tile = 256 if seq % 256 == 0 else WINDOW
    c_max = -(-n_rows // MOE_CHUNK) + N_CLASSES
    npb = n_prompt // ROW_BLOCK

    cos_p, sin_p = _rope_tables(jnp.arange(seq))
    cos_s, sin_s = _rope_tables(PAST_LEN + jnp.arange(ds))

    x_main, x_tail, slab = x_prompt.reshape(n_prompt, d), x_sample.reshape(n_sample, d), False
    ck = cache_k.reshape(depth, db, win, d_kv)
    cv = cache_v.reshape(depth, db, win, d_kv)
    st_h = state_lru_h.reshape(depth, db, 1, d_lru)
    n_exp, _, d_exp = expert_w1.shape[1:]
    w1 = expert_w1.reshape(depth * n_exp, d, d_exp)
    w3 = expert_w3.reshape(depth * n_exp, d, d_exp)
    w2 = expert_w2.reshape(depth * n_exp, d_exp, d)
    outs = [[] for _ in range(10)]
    for l in range(depth):
        p = dict(conv_w=conv_w[l], conv_b=conv_b[l], lru_wa=lru_wa[l], lru_ba=lru_ba[l],
                 lru_wx=lru_wx[l], lru_bx=lru_bx[l], lru_lambda=lru_lambda[l], pool_w=pool_w[l],
                 pool_scale=pool_scale[l], attn_sinks=attn_sinks[l])
        consts = _layer_consts(p, dims)
        proj = _in_proj(x_main, x_tail, slab, norm1_g[l].reshape(1, d), w_in, l, n_prompt, n_sample)
        mix_p, pk, pv, ph, pc, pp = _mixer_prompt(proj, consts, cos_p, sin_p, bsz, seq, dims, tile)
        mix_s, sk, sv, sh, sc, sp = _mixer_sample(
            proj, n_prompt, consts, cos_s, sin_s, ck, cv, st_h, state_conv, state_pool, l, dims)
        wr = jnp.concatenate([router_group_w[l], router_expert_w[l]], axis=1)
        wr = jnp.pad(wr, ((0, 0), (0, LANES - wr.shape[1]))).astype(BF16)
        br = jnp.concatenate([router_group_b[l], router_expert_b[l]])
        br = jnp.pad(br, (0, LANES - br.shape[0])).reshape(1, LANES)
        xr, info, cnt = _out_proj(mix_p, mix_s, x_main, x_tail, slab, w_out, l,
                                  norm2_g[l].reshape(1, d), wr, br)
        route = info[:n_rows, 2:4].astype(I32)
        tables = (route[:, 0], route[:, 1]) + _chunk_tables(cnt[0, :N_CLASSES].astype(I32), c_max, l * n_exp)
        x = _moe(xr, norm2_g[l].reshape(1, d), w1, w3, w2, tables, n_rows, c_max)
        x_main, x_tail, slab = x, x, True
        for lst, val in zip(outs, (pk.reshape(bsz, WINDOW, nkv, HEAD_DIM), pv.reshape(bsz, WINDOW, nkv, HEAD_DIM),
                                   ph.reshape(bsz, d_lru), pc, pp,
                                   sk.reshape(db, win, nkv, HEAD_DIM), sv.reshape(db, win, nkv, HEAD_DIM),
                                   sh.reshape(db, d_lru), sc, sp)):
            lst.append(val)
    y_p, y_s = _final_norm(x_main, final_norm_g.reshape(1, d), n_prompt, n_sample)
    return (y_p.reshape(bsz, seq, d), y_s.reshape(db, ds, d)) + tuple(jnp.stack(o) for o in outs)
```

```python
import functools

import jax
import jax.numpy as jnp
from jax import lax
from jax.experimental import pallas as pl
from jax.experimental.pallas import tpu as pltpu

F32 = jnp.float32
BF16 = jnp.bfloat16
I32 = jnp.int32

LRU_HEADS = 8
CONV_WIDTH = 4
LRU_C = 8.0
POOL_WINDOWS = (2, 4, 8, 16)
POOL_PAD = max(POOL_WINDOWS) - 1
HEAD_DIM = 64
N_KV_HEADS = 4
WINDOW = 128
ROPE_THETA = 10000.0
ATTN_SCALE = HEAD_DIM ** -0.5
N_EXPERT_GROUPS = 4
EXPERTS_PER_GROUP = 4
N_EXPERTS = N_EXPERT_GROUPS * EXPERTS_PER_GROUP
RMS_EPS = 1e-6
PAST_LEN = 16384

LANES = 128
HALF = LANES // 2
HALO = 16
ROW_BLOCK = 256
W_STAGE_ROWS = 128
MOE_CHUNK = 128
PAIR_SLOT_A = (0, 0, 0, 1, 1, 3)
PAIR_SLOT_B = (1, 2, 3, 3, 2, 2)
N_PAIRS = len(PAIR_SLOT_A)
N_CLASSES = N_EXPERT_GROUPS * N_PAIRS
VMEM_LIMIT = 52 * 1024 * 1024


def _rms(x, g):
    return (x * lax.rsqrt(jnp.mean(x * x, axis=-1, keepdims=True) + RMS_EPS)) * g


def _load_weight_bf16(w_hbm, w_bf, stage, sem):
    rows = stage.shape[1]
    n = w_hbm.shape[0] // rows

    def copy(i, sl):
        return pltpu.make_async_copy(w_hbm.at[pl.ds(i * rows, rows)], stage.at[sl], sem.at[sl])

    copy(0, 0).start()
    for i in range(n):
        sl = i % 2
        if i + 1 < n:
            copy(i + 1, 1 - sl).start()
        copy(i, sl).wait()
        w_bf[i * rows:(i + 1) * rows, :] = stage[sl].astype(BF16)


def _slab_pitch(d):
    return d // LANES + 1


def _read_slabs(ref, lead, n_tok, n_slabs, pitch):
    return jnp.concatenate(
        [ref[lead + (pl.ds(k, n_tok, stride=pitch), slice(None))] for k in range(n_slabs)], axis=1)


def _write_slabs(ref, lead, val, pitch):
    n_tok = val.shape[0]
    for k in range(val.shape[1] // LANES):
        ref[lead + (pl.ds(k, n_tok, stride=pitch), slice(None))] = val[:, k * LANES:(k + 1) * LANES]


def _pad_rows(x, n):
    if x.shape[0] == n:
        return x
    return jnp.concatenate([x, jnp.zeros((n - x.shape[0], x.shape[1]), x.dtype)], axis=0)


def _pick_rows(i, n_main_blocks, main_ref, tail_ref):
    return jnp.where(i < n_main_blocks, main_ref[...], _pad_rows(tail_ref[...], ROW_BLOCK))


def _row_specs(d, n_main_blocks, tail_rows, tail_block):
    return [pl.BlockSpec((ROW_BLOCK, d), lambda i: (jnp.minimum(i, n_main_blocks - 1), 0)),
            pl.BlockSpec((tail_rows, d), lambda i: (tail_block, 0))]


class _RowGather:
    def __init__(self, pos_ref, src_hbm, buf, sem, d):
        self.pos_ref, self.src, self.buf, self.sem = pos_ref, src_hbm, buf, sem
        self.pitch = _slab_pitch(d)
        self.n_slabs = d // LANES

    def _start(self, block, sl, j):
        p = self.pos_ref[block * ROW_BLOCK + j]
        pltpu.make_async_copy(self.src.at[pl.ds(p * self.pitch, self.pitch)],
                              self.buf.at[sl, pl.ds(j * self.pitch, self.pitch)], self.sem.at[sl]).start()

    def start_first(self):
        def body(j, z):
            self._start(0, 0, j)
            return z
        lax.fori_loop(0, ROW_BLOCK, body, 0)

    def start_next(self, block, sl):
        for j in range(ROW_BLOCK):
            self._start(block, sl, j)

    def wait(self, sl):
        pltpu.make_async_copy(self.src.at[pl.ds(0, ROW_BLOCK * self.pitch)], self.buf.at[sl],
                              self.sem.at[sl]).wait()

    def rows(self, sl):
        return _read_slabs(self.buf, (sl,), ROW_BLOCK, self.n_slabs, self.pitch)


def _in_proj_kernel(xm_ref, xt_ref, g_ref, w_hbm, o_ref, w_bf, stage, sem, *, n_main_blocks, layer):
    i = pl.program_id(0)

    @pl.when(i == 0)
    def _():
        _load_weight_bf16(w_hbm.at[layer], w_bf, stage, sem)

    h = _rms(_pick_rows(i, n_main_blocks, xm_ref, xt_ref), g_ref[...])
    o_ref[...] = jnp.dot(h.astype(BF16), w_bf[...], preferred_element_type=F32)


def _in_proj(x_main, x_tail, g, w, layer):
    d = g.shape[1]
    d_in = w.shape[2]
    n_main = x_main.shape[0]
    nmb = n_main // ROW_BLOCK
    kern = functools.partial(_in_proj_kernel, n_main_blocks=nmb, layer=layer)
    return pl.pallas_call(
        kern,
        grid=(nmb + 1,),
        in_specs=_row_specs(d, nmb, x_tail.shape[0], 0) + [
            pl.BlockSpec((1, d), lambda i: (0, 0)),
            pl.BlockSpec(memory_space=pl.ANY),
        ],
        out_specs=pl.BlockSpec((ROW_BLOCK, d_in), lambda i: (i, 0)),
        out_shape=jax.ShapeDtypeStruct((n_main + ROW_BLOCK, d_in), F32),
        scratch_shapes=[
            pltpu.VMEM((d, d_in), BF16),
            pltpu.VMEM((2, W_STAGE_ROWS, d_in), F32),
            pltpu.SemaphoreType.DMA((2,)),
        ],
        compiler_params=pltpu.CompilerParams(
            dimension_semantics=("arbitrary",), vmem_limit_bytes=VMEM_LIMIT),
        name="in_proj",
    )(x_main, x_tail, g, w)


def _in_proj_gather_kernel(pos_ref, xs_hbm, g_ref, w_hbm, o_ref, xnat_ref, w_bf, stage, wsem, xbuf, gsem,
                           *, layer):
    i = pl.program_id(0)
    slot = i % 2
    rows = _RowGather(pos_ref, xs_hbm, xbuf, gsem, g_ref.shape[1])

    @pl.when(i == 0)
    def _():
        rows.start_first()
        _load_weight_bf16(w_hbm.at[layer], w_bf, stage, wsem)

    rows.wait(slot)
    rows.start_next(i + 1, 1 - slot)
    x = rows.rows(slot)
    xnat_ref[...] = x
    o_ref[...] = jnp.dot(_rms(x, g_ref[...]).astype(BF16), w_bf[...], preferred_element_type=F32)

    @pl.when(i == pl.num_programs(0) - 1)
    def _():
        rows.wait(1 - slot)


def _in_proj_gather(pos, xs, g, w, layer, n_alloc):
    d = g.shape[1]
    d_in = w.shape[2]
    pitch = _slab_pitch(d)
    grid_spec = pltpu.PrefetchScalarGridSpec(
        num_scalar_prefetch=1,
        grid=(n_alloc // ROW_BLOCK,),
        in_specs=[
            pl.BlockSpec(memory_space=pl.ANY),
            pl.BlockSpec((1, d), lambda i, pos: (0, 0)),
            pl.BlockSpec(memory_space=pl.ANY),
        ],
        out_specs=[pl.BlockSpec((ROW_BLOCK, d_in), lambda i, pos: (i, 0)),
                   pl.BlockSpec((ROW_BLOCK, d), lambda i, pos: (i, 0))],
        scratch_shapes=[
            pltpu.VMEM((d, d_in), BF16),
            pltpu.VMEM((2, W_STAGE_ROWS, d_in), F32),
            pltpu.SemaphoreType.DMA((2,)),
            pltpu.VMEM((2, ROW_BLOCK * pitch, LANES), F32),
            pltpu.SemaphoreType.DMA((2,)),
        ],
    )
    return pl.pallas_call(
        functools.partial(_in_proj_gather_kernel, layer=layer),
        grid_spec=grid_spec,
        out_shape=[jax.ShapeDtypeStruct((n_alloc, d_in), F32),
                   jax.ShapeDtypeStruct((n_alloc, d), F32)],
        compiler_params=pltpu.CompilerParams(
            dimension_semantics=("arbitrary",), vmem_limit_bytes=VMEM_LIMIT),
        name="in_proj_gather",
    )(pos, xs, g, w)


def _scan_linear(a, b):
    t = a.shape[0]
    row = lax.broadcasted_iota(I32, a.shape, 0)
    d = 1
    while d < min(t, 8):
        a_sh = pltpu.roll(a, d, axis=0)
        b_sh = pltpu.roll(b, d, axis=0)
        m = row >= d
        b = jnp.where(m, a * b_sh + b, b)
        a = jnp.where(m, a * a_sh, a)
        d *= 2
    while d < t:
        b = jnp.concatenate([b[:d], a[d:] * b[:t - d] + b[d:]], axis=0)
        a = jnp.concatenate([a[:d], a[d:] * a[:t - d]], axis=0)
        d *= 2
    return a, b


def _lru_chunk(xc, g, h_prev, wg, ba, bx, sp):
    pre = jnp.dot(xc.astype(BF16), wg, preferred_element_type=F32)
    r = jax.nn.sigmoid(pre[:, :LANES] + ba)
    ig = jax.nn.sigmoid(pre[:, LANES:] + bx)
    log_a = (-LRU_C * r) * sp
    a = jnp.exp(log_a)
    bterm = jnp.sqrt(-jnp.tanh(log_a) * (a * a + 1.0)) * ig * xc
    a_cum, h0 = _scan_linear(a, bterm)
    hs = a_cum * h_prev + h0
    return hs * jax.nn.gelu(g), hs


def _rope(x, cos, sin_signed):
    n = x.shape[1] // LANES
    lane = lax.broadcasted_iota(I32, (x.shape[0], LANES), 1)
    first = (lane % HEAD_DIM) < (HEAD_DIM // 2)
    outs = []
    for c in range(n):
        xc = x[:, c * LANES:(c + 1) * LANES]
        swapped = jnp.where(first, pltpu.roll(xc, LANES - HEAD_DIM // 2, axis=1),
                            pltpu.roll(xc, HEAD_DIM // 2, axis=1))
        outs.append(xc * cos + swapped * sin_signed)
    return outs


def _store_head_variants(var_ref, row0, chunks):
    t = chunks[0].shape[0]
    lane = lax.broadcasted_iota(I32, (t, LANES), 1)
    for kc, x in enumerate(chunks):
        swapped = pltpu.roll(x, HALF, axis=1)
        for hh in range(2):
            for p in range(2):
                src = x if p == hh else swapped
                keep = (lane < HALF) if p == 0 else (lane >= HALF)
                var_ref[2 * kc + hh, p, row0:row0 + t, :] = jnp.where(keep, src, 0.0).astype(BF16)


def _attend_block(q_chunks, kvar, vvar, key0, sinks, lim, out_ref, out_rows, out_col0):
    qb = q_chunks[0].shape[0]
    nk = 2 * WINDOW
    gq = (2 * len(q_chunks)) // N_KV_HEADS
    rows = 2 * qb
    qi = lax.broadcasted_iota(I32, (rows, nk), 0) % qb
    kj = lax.broadcasted_iota(I32, (rows, nk), 1)
    valid = ((kj < WINDOW) & (kj > qi + lim)) | ((kj >= WINDOW) & (kj - WINDOW <= qi))
    top = lax.broadcasted_iota(I32, (rows, 1), 0) < qb
    lane = lax.broadcasted_iota(I32, (nk, LANES), 1)
    ones_lo = jnp.where(lane < HALF, 1.0, 0.0).astype(BF16)
    ones_hi = jnp.where(lane >= HALF, 1.0, 0.0).astype(BF16)
    lane_o = lax.broadcasted_iota(I32, (rows, LANES), 1)
    nt = (((1,), (1,)), ((), ()))
    for c in range(N_KV_HEADS):
        c0 = c * gq // 2
        qs = jnp.concatenate([q_chunks[c0], q_chunks[c0 + 1]], axis=0)
        es, sink_terms = [], []
        for p in range(2):
            kc = kvar[c, p, key0:key0 + nk, :]
            s = lax.dot_general(qs, kc, nt, preferred_element_type=F32)
            s = jnp.where(valid, s, -jnp.inf)
            h0 = c * gq + p
            sink = jnp.where(top, sinks[:, h0:h0 + 1], sinks[:, h0 + 2:h0 + 3])
            m = jnp.maximum(jnp.max(s, axis=-1, keepdims=True), sink)
            es.append(jnp.exp(s - m).astype(BF16))
            sink_terms.append(jnp.exp(sink - m))
        r0 = jnp.concatenate([vvar[c, 0, key0:key0 + nk, :], ones_lo], axis=1)
        r1 = jnp.concatenate([vvar[c, 1, key0:key0 + nk, :], ones_hi], axis=1)
        od = (jnp.dot(es[0], r0, preferred_element_type=F32)
              + jnp.dot(es[1], r1, preferred_element_type=F32))
        den = od[:, LANES:] + jnp.where(lane_o < HALF, sink_terms[0], sink_terms[1])
        o = (od[:, :LANES] / den).astype(out_ref.dtype)
        out_ref[out_rows, out_col0 + c0 * LANES:out_col0 + (c0 + 1) * LANES] = o[0:qb]
        out_ref[out_rows, out_col0 + (c0 + 1) * LANES:out_col0 + (c0 + 2) * LANES] = o[qb:rows]


def _mixer_kernel(*refs, tile, qblock, is_prompt, pos0, d_lru, d_pool, d_attn, d_kv):
    (proj_ref, cos_ref, sin_ref, convw_ref, convb_ref, wg_ref, ba_ref, bx_ref, lam_ref,
     poolw_ref, pscale_ref, sink_ref) = refs[:12]
    if is_prompt:
        (mix_ref, ko_ref, vo_ref, ho_ref, co_ref, po_ref,
         lru_ext, pool_ext, kvar, vvar, hcar) = refs[12:]
    else:
        (ck_ref, cv_ref, sh_ref, sc_ref, sp_ref,
         mix_ref, ko_ref, vo_ref, ho_ref, co_ref, po_ref,
         lru_ext, pool_ext, kvar, vvar) = refs[12:]

    s = pl.program_id(1) if is_prompt else 0
    o_glru = d_lru
    o_pool = 2 * d_lru
    o_q = o_pool + d_pool
    o_k = o_q + d_attn
    o_v = o_k + d_kv
    n_kc = d_kv // LANES

    if is_prompt:
        @pl.when(s == 0)
        def _():
            lru_ext[0:HALO, :] = jnp.zeros((HALO, d_lru), F32)
            pool_ext[0:HALO, :] = jnp.zeros((HALO, d_pool), F32)
            hcar[...] = jnp.zeros_like(hcar)
            kvar[:, :, 0:WINDOW, :] = jnp.zeros((N_KV_HEADS, 2, WINDOW, LANES), BF16)
            vvar[:, :, 0:WINDOW, :] = jnp.zeros((N_KV_HEADS, 2, WINDOW, LANES), BF16)
        h_prev = hcar[0:1, :]
    else:
        lru_ext[0:HALO, :] = jnp.zeros((HALO, d_lru), F32)
        pool_ext[0:HALO, :] = jnp.zeros((HALO, d_pool), F32)
        lru_ext[HALO - (CONV_WIDTH - 1):HALO, :] = sc_ref[0]
        pool_ext[HALO - POOL_PAD:HALO, :] = sp_ref[0]
        h_prev = sh_ref[0]
        kvar[:, :, WINDOW:2 * WINDOW, :] = jnp.zeros((N_KV_HEADS, 2, WINDOW, LANES), BF16)
        vvar[:, :, WINDOW:2 * WINDOW, :] = jnp.zeros((N_KV_HEADS, 2, WINDOW, LANES), BF16)
        _store_head_variants(kvar, 0, [ck_ref[0, :, c * LANES:(c + 1) * LANES] for c in range(n_kc)])
        _store_head_variants(vvar, 0, [cv_ref[0, :, c * LANES:(c + 1) * LANES] for c in range(n_kc)])

    lru_ext[HALO:HALO + tile, :] = proj_ref[:, 0:d_lru]
    pool_ext[HALO:HALO + tile, :] = proj_ref[:, o_pool:o_pool + d_pool]

    xc = convb_ref[...] + convw_ref[0:1, :] * lru_ext[HALO - 3:HALO - 3 + tile, :]
    for j in range(1, CONV_WIDTH):
        xc = xc + convw_ref[j:j + 1, :] * lru_ext[HALO - 3 + j:HALO - 3 + j + tile, :]
    neg = -lam_ref[...]
    sp_all = jnp.maximum(neg, 0.0) + jnp.log1p(jnp.exp(-jnp.abs(neg)))
    h_last = []
    for c in range(d_lru // LANES):
        cs = slice(c * LANES, (c + 1) * LANES)
        y, hs = _lru_chunk(xc[:, cs], proj_ref[:, o_glru + c * LANES:o_glru + (c + 1) * LANES],
                           h_prev[:, cs], wg_ref[c], ba_ref[:, cs], bx_ref[:, cs], sp_all[:, cs])
        mix_ref[:, cs] = y.astype(mix_ref.dtype)
        h_last.append(hs[tile - 1:tile, :])
    h_last = jnp.concatenate(h_last, axis=-1)

    row = lax.broadcasted_iota(I32, (tile, LANES), 0)
    pos = pos0 + s * tile + row
    for gi, w in enumerate(POOL_WINDOWS):
        cs = slice(gi * LANES, (gi + 1) * LANES)
        e = pool_ext[:, cs]
        acc = e
        step = 1
        while step < w:
            acc = acc + pltpu.roll(acc, step, axis=0)
            step *= 2
        cnt = jnp.minimum(pos + 1, w).astype(F32)
        dlt = acc[HALO:HALO + tile, :] / cnt - e[HALO:HALO + tile, :]
        y = jnp.dot(dlt.astype(BF16), poolw_ref[gi], preferred_element_type=F32) * pscale_ref[:, cs]
        mix_ref[:, d_lru + gi * LANES:d_lru + (gi + 1) * LANES] = y.astype(mix_ref.dtype)

    sinks = sink_ref[...]
    o_attn = d_lru + d_pool
    cos = cos_ref[...]
    sin = sin_ref[...]
    q_chunks = [(qc * ATTN_SCALE).astype(BF16) for qc in _rope(proj_ref[:, o_q:o_q + d_attn], cos, sin)]
    k_rot = _rope(proj_ref[:, o_k:o_k + d_kv], cos, sin)
    _store_head_variants(kvar, WINDOW, k_rot)
    _store_head_variants(vvar, WINDOW, [proj_ref[:, o_v + c * LANES:o_v + (c + 1) * LANES]
                                        for c in range(n_kc)])
    for blk in range(tile // qblock):
        rs = slice(blk * qblock, (blk + 1) * qblock)
        if is_prompt and blk == 0:
            lim = jnp.where(s == 0, WINDOW, 0)
        else:
            lim = 0
        _attend_block([qc[rs] for qc in q_chunks], kvar, vvar, blk * qblock, sinks, lim,
                      mix_ref, rs, o_attn)

    k_last = jnp.concatenate([kc[tile - qblock:tile] for kc in k_rot], axis=-1)
    if is_prompt:
        lru_ext[0:HALO, :] = lru_ext[tile:tile + HALO, :]
        pool_ext[0:HALO, :] = pool_ext[tile:tile + HALO, :]
        hcar[0:1, :] = h_last
        kvar[:, :, 0:WINDOW, :] = kvar[:, :, tile:tile + WINDOW, :]
        vvar[:, :, 0:WINDOW, :] = vvar[:, :, tile:tile + WINDOW, :]

        @pl.when(s == pl.num_programs(1) - 1)
        def _():
            ko_ref[0] = k_last
            vo_ref[0] = proj_ref[tile - qblock:tile, o_v:o_v + d_kv]
            ho_ref[0] = h_last
            co_ref[0] = lru_ext[HALO + tile - (CONV_WIDTH - 1):HALO + tile, :]
            po_ref[0] = pool_ext[HALO + tile - POOL_PAD:HALO + tile, :]
    else:
        ko_ref[0, 0:WINDOW - tile, :] = ck_ref[0, tile:WINDOW, :]
        ko_ref[0, WINDOW - tile:WINDOW, :] = k_last
        vo_ref[0, 0:WINDOW - tile, :] = cv_ref[0, tile:WINDOW, :]
        vo_ref[0, WINDOW - tile:WINDOW, :] = proj_ref[:, o_v:o_v + d_kv]
        ho_ref[0] = h_last
        co_ref[0] = lru_ext[HALO + tile - (CONV_WIDTH - 1):HALO + tile, :]
        po_ref[0] = pool_ext[HALO + tile - POOL_PAD:HALO + tile, :]


def _layer_consts(p, dims):
    d_lru, d_pool = dims["d_lru"], dims["d_pool"]
    hd = d_lru // LRU_HEADS
    per = LANES // hd
    nchunk = d_lru // LANES

    def blockdiag(w):
        w = w.reshape(nchunk, per, hd, hd)
        eye = jnp.eye(per, dtype=w.dtype)
        return jnp.einsum("cpij,pq->cpiqj", w, eye).reshape(nchunk, LANES, LANES)

    wg = jnp.concatenate([blockdiag(p["lru_wa"]), blockdiag(p["lru_wx"])], axis=-1).astype(BF16)
    return dict(
        convw=p["conv_w"], convb=p["conv_b"].reshape(1, d_lru), wg=wg,
        ba=p["lru_ba"].reshape(1, d_lru), bx=p["lru_bx"].reshape(1, d_lru),
        lam=p["lru_lambda"].reshape(1, d_lru), poolw=p["pool_w"].astype(BF16),
        pscale=p["pool_scale"].reshape(1, d_pool), sinks=p["attn_sinks"].reshape(1, -1))


def _rope_tables(pos):
    half = HEAD_DIM // 2
    inv = ROPE_THETA ** (-jnp.arange(half, dtype=F32) / half)
    ang = pos.astype(F32)[:, None] * inv[None, :]
    cos = jnp.cos(ang)
    sin = jnp.sin(ang)
    cos2 = jnp.concatenate([cos, cos], axis=-1)
    sin2 = jnp.concatenate([-sin, sin], axis=-1)
    reps = LANES // HEAD_DIM
    return jnp.tile(cos2, (1, reps)), jnp.tile(sin2, (1, reps))


CONST_NAMES = ("convw", "convb", "wg", "ba", "bx", "lam", "poolw", "pscale", "sinks")


def _const_specs(consts):
    return [pl.BlockSpec(consts[n].shape, functools.partial(lambda nd, *_: (0,) * nd, consts[n].ndim))
            for n in CONST_NAMES]


def _mixer_prompt(proj, consts, cos, sin, bsz, seq, dims, tile):
    d_lru, d_pool, d_attn, d_kv = dims["d_lru"], dims["d_pool"], dims["d_attn"], dims["d_kv"]
    d_in = proj.shape[1]
    d_mix = d_lru + d_pool + d_attn
    ns = seq // tile
    kern = functools.partial(_mixer_kernel, tile=tile, qblock=WINDOW, is_prompt=True, pos0=0,
                             d_lru=d_lru, d_pool=d_pool, d_attn=d_attn, d_kv=d_kv)
    return pl.pallas_call(
        kern,
        grid=(bsz, ns),
        in_specs=[
            pl.BlockSpec((tile, d_in), lambda b, s: (b * ns + s, 0)),
            pl.BlockSpec((tile, LANES), lambda b, s: (s, 0)),
            pl.BlockSpec((tile, LANES), lambda b, s: (s, 0)),
        ] + _const_specs(consts),
        out_specs=[
            pl.BlockSpec((tile, d_mix), lambda b, s: (b * ns + s, 0)),
            pl.BlockSpec((1, WINDOW, d_kv), lambda b, s: (b, 0, 0)),
            pl.BlockSpec((1, WINDOW, d_kv), lambda b, s: (b, 0, 0)),
            pl.BlockSpec((1, 1, d_lru), lambda b, s: (b, 0, 0)),
            pl.BlockSpec((1, CONV_WIDTH - 1, d_lru), lambda b, s: (b, 0, 0)),
            pl.BlockSpec((1, POOL_PAD, d_pool), lambda b, s: (b, 0, 0)),
        ],
        out_shape=[
            jax.ShapeDtypeStruct((bsz * seq, d_mix), BF16),
            jax.ShapeDtypeStruct((bsz, WINDOW, d_kv), F32),
            jax.ShapeDtypeStruct((bsz, WINDOW, d_kv), F32),
            jax.ShapeDtypeStruct((bsz, 1, d_lru), F32),
            jax.ShapeDtypeStruct((bsz, CONV_WIDTH - 1, d_lru), F32),
            jax.ShapeDtypeStruct((bsz, POOL_PAD, d_pool), F32),
        ],
        scratch_shapes=[
            pltpu.VMEM((tile + HALO, d_lru), F32),
            pltpu.VMEM((tile + HALO, d_pool), F32),
            pltpu.VMEM((N_KV_HEADS, 2, WINDOW + tile, LANES), BF16),
            pltpu.VMEM((N_KV_HEADS, 2, WINDOW + tile, LANES), BF16),
            pltpu.VMEM((8, d_lru), F32),
        ],
        compiler_params=pltpu.CompilerParams(
            dimension_semantics=("arbitrary", "arbitrary"), vmem_limit_bytes=VMEM_LIMIT),
        name="mixer_prompt",
    )(proj, cos, sin, *[consts[n] for n in CONST_NAMES])


def _mixer_sample(proj, row0, consts, cos, sin, cache_k, cache_v, st_h, st_conv, st_pool, layer, dims):
    d_lru, d_pool, d_attn, d_kv = dims["d_lru"], dims["d_pool"], dims["d_attn"], dims["d_kv"]
    d_in = proj.shape[1]
    d_mix = d_lru + d_pool + d_attn
    db, win = cache_k.shape[1], cache_k.shape[2]
    t = cos.shape[0]
    blk0 = row0 // t
    kern = functools.partial(_mixer_kernel, tile=t, qblock=t, is_prompt=False, pos0=PAST_LEN,
                             d_lru=d_lru, d_pool=d_pool, d_attn=d_attn, d_kv=d_kv)
    return pl.pallas_call(
        kern,
        grid=(db,),
        in_specs=[
            pl.BlockSpec((t, d_in), lambda b: (blk0 + b, 0)),
            pl.BlockSpec((t, LANES), lambda b: (0, 0)),
            pl.BlockSpec((t, LANES), lambda b: (0, 0)),
        ] + _const_specs(consts) + [
            pl.BlockSpec((None, 1, win, d_kv), lambda b: (layer, b, 0, 0)),
            pl.BlockSpec((None, 1, win, d_kv), lambda b: (layer, b, 0, 0)),
            pl.BlockSpec((None, 1, 1, d_lru), lambda b: (layer, b, 0, 0)),
            pl.BlockSpec((None, 1, CONV_WIDTH - 1, d_lru), lambda b: (layer, b, 0, 0)),
            pl.BlockSpec((None, 1, POOL_PAD, d_pool), lambda b: (layer, b, 0, 0)),
        ],
        out_specs=[
            pl.BlockSpec((t, d_mix), lambda b: (b, 0)),
            pl.BlockSpec((1, win, d_kv), lambda b: (b, 0, 0)),
            pl.BlockSpec((1, win, d_kv), lambda b: (b, 0, 0)),
            pl.BlockSpec((1, 1, d_lru), lambda b: (b, 0, 0)),
            pl.BlockSpec((1, CONV_WIDTH - 1, d_lru), lambda b: (b, 0, 0)),
            pl.BlockSpec((1, POOL_PAD, d_pool), lambda b: (b, 0, 0)),
        ],
        out_shape=[
            jax.ShapeDtypeStruct((db * t, d_mix), F32),
            jax.ShapeDtypeStruct((db, win, d_kv), F32),
            jax.ShapeDtypeStruct((db, win, d_kv), F32),
            jax.ShapeDtypeStruct((db, 1, d_lru), F32),
            jax.ShapeDtypeStruct((db, CONV_WIDTH - 1, d_lru), F32),
            jax.ShapeDtypeStruct((db, POOL_PAD, d_pool), F32),
        ],
        scratch_shapes=[
            pltpu.VMEM((t + HALO, d_lru), F32),
            pltpu.VMEM((t + HALO, d_pool), F32),
            pltpu.VMEM((N_KV_HEADS, 2, 2 * WINDOW, LANES), BF16),
            pltpu.VMEM((N_KV_HEADS, 2, 2 * WINDOW, LANES), BF16),
        ],
        compiler_params=pltpu.CompilerParams(
            dimension_semantics=("arbitrary",), vmem_limit_bytes=VMEM_LIMIT),
        name="mixer_sample",
    )(proj, cos, sin, *[consts[n] for n in CONST_NAMES], cache_k, cache_v, st_h, st_conv, st_pool)


def _route(logits, run_cnt, n_valid):
    t = logits.shape[0]
    lane = lax.broadcasted_iota(I32, (t, LANES), 1)
    lane_f = lane.astype(F32)
    ninf = -jnp.inf
    big = float(LANES)
    is_g = lane < N_EXPERT_GROUPS
    lg = jnp.where(is_g, logits, ninf)
    mg = jnp.max(lg, axis=-1, keepdims=True)
    g_top = jnp.min(jnp.where(lg == mg, lane_f, big), axis=-1, keepdims=True).astype(I32)
    pg_top = 1.0 / jnp.sum(jnp.exp(lg - mg), axis=-1, keepdims=True)
    base = N_EXPERT_GROUPS + EXPERTS_PER_GROUP * g_top
    in_grp = (lane >= base) & (lane < base + EXPERTS_PER_GROUP)
    le = jnp.where(in_grp, logits, ninf)
    m1 = jnp.max(le, axis=-1, keepdims=True)
    i1 = jnp.min(jnp.where(le == m1, lane_f, big), axis=-1, keepdims=True).astype(I32)
    le2 = jnp.where(lane == i1, ninf, le)
    m2 = jnp.max(le2, axis=-1, keepdims=True)
    i2 = jnp.min(jnp.where((le2 == m2) & in_grp & (lane != i1), lane_f, big),
                 axis=-1, keepdims=True).astype(I32)
    se = jnp.sum(jnp.exp(le - m1), axis=-1, keepdims=True)
    p1 = 1.0 / se
    p2 = jnp.exp(m2 - m1) / se
    tot = p1 + p2
    w1 = (p1 / tot) * pg_top
    w2 = (p2 / tot) * pg_top
    a = i1 - base
    b = i2 - base
    lo = jnp.minimum(a, b)
    hi = jnp.maximum(a, b)
    w_lo = jnp.where(a < b, w1, w2)
    w_hi = jnp.where(a < b, w2, w1)
    pid = jnp.where(lo == 0, hi - 1, jnp.where(lo == 1, jnp.where(hi == 3, 3, 4), 5))
    swap = pid == 5
    w_a = jnp.where(swap, w_hi, w_lo)
    w_b = jnp.where(swap, w_lo, w_hi)
    cls = g_top * N_PAIRS + pid
    onehot = (lane == cls) & (lax.broadcasted_iota(I32, (t, LANES), 0) < n_valid)
    ti = lax.broadcasted_iota(I32, (t, t), 0)
    tj = lax.broadcasted_iota(I32, (t, t), 1)
    lower = jnp.where(tj <= ti, 1.0, 0.0).astype(BF16)
    prefix = jnp.dot(lower, jnp.where(onehot, 1.0, 0.0).astype(BF16), preferred_element_type=F32)
    rank = jnp.sum(jnp.where(onehot, prefix - 1.0 + run_cnt, 0.0), axis=-1, keepdims=True)
    info = jnp.where(lane == 0, w_a, jnp.where(lane == 1, w_b, jnp.where(
        lane == 2, cls.astype(F32), jnp.where(lane == 3, rank, 0.0))))
    return info, run_cnt + prefix[t - 1:t, :]


def _out_proj_kernel(mixp_ref, mixs_ref, xm_ref, xt_ref, w_hbm, g_ref, wr_ref, br_ref,
                     o_ref, info_ref, cnt_ref, w_bf, stage, sem, run_cnt, *, n_main_blocks, d, layer):
    i = pl.program_id(0)

    @pl.when(i == 0)
    def _():
        _load_weight_bf16(w_hbm.at[layer], w_bf, stage, sem)
        run_cnt[...] = jnp.zeros_like(run_cnt)

    n_tail = mixs_ref.shape[0]
    mix = jnp.where(i < n_main_blocks, mixp_ref[...], _pad_rows(mixs_ref[...].astype(BF16), ROW_BLOCK))
    xres = _pick_rows(i, n_main_blocks, xm_ref, xt_ref) + jnp.dot(mix, w_bf[...], preferred_element_type=F32)
    h2 = _rms(xres, g_ref[...])
    logits = jnp.dot(h2.astype(BF16), wr_ref[...], preferred_element_type=F32) + br_ref[...]
    info, cnt = _route(logits, run_cnt[0:1, :], jnp.where(i < n_main_blocks, ROW_BLOCK, n_tail))
    run_cnt[0:1, :] = cnt
    pitch = _slab_pitch(d)
    _write_slabs(o_ref, (), xres, pitch)
    o_ref[pl.ds(d // LANES, ROW_BLOCK, stride=pitch), :] = info
    info_ref[...] = info
    cnt_ref[...] = jnp.broadcast_to(cnt, cnt_ref.shape)


def _out_proj(mix_p, mix_s, x_main, x_tail, tail_rows, tail_block, w, layer, g, wr, br):
    d = g.shape[1]
    d_mix = w.shape[1]
    n_main, n_tail = mix_p.shape[0], mix_s.shape[0]
    npb = n_main // ROW_BLOCK
    pitch = _slab_pitch(d)
    kern = functools.partial(_out_proj_kernel, n_main_blocks=npb, d=d, layer=layer)
    return pl.pallas_call(
        kern,
        grid=(npb + 1,),
        in_specs=[
            pl.BlockSpec((ROW_BLOCK, d_mix), lambda i: (jnp.minimum(i, npb - 1), 0)),
            pl.BlockSpec((n_tail, d_mix), lambda i: (0, 0)),
        ] + _row_specs(d, npb, tail_rows, tail_block) + [
            pl.BlockSpec(memory_space=pl.ANY),
            pl.BlockSpec((1, d), lambda i: (0, 0)),
            pl.BlockSpec((d, LANES), lambda i: (0, 0)),
            pl.BlockSpec((1, LANES), lambda i: (0, 0)),
        ],
        out_specs=[pl.BlockSpec((ROW_BLOCK * pitch, LANES), lambda i: (i, 0)),
                   pl.BlockSpec((ROW_BLOCK, LANES), lambda i: (i, 0)),
                   pl.BlockSpec((8, LANES), lambda i: (0, 0))],
        out_shape=[jax.ShapeDtypeStruct(((n_main + ROW_BLOCK) * pitch, LANES), F32),
                   jax.ShapeDtypeStruct((n_main + ROW_BLOCK, LANES), F32),
                   jax.ShapeDtypeStruct((8, LANES), F32)],
        scratch_shapes=[
            pltpu.VMEM((d_mix, d), BF16),
            pltpu.VMEM((2, W_STAGE_ROWS, d), F32),
            pltpu.SemaphoreType.DMA((2,)),
            pltpu.VMEM((8, LANES), F32),
        ],
        compiler_params=pltpu.CompilerParams(
            dimension_semantics=("arbitrary",), vmem_limit_bytes=VMEM_LIMIT),
        name="out_proj",
    )(mix_p, mix_s, x_main, x_tail, w, g, wr, br)


def _moe_kernel(cls_ref, rank_ref, off_ref, cnt_ref, ea_ref, eb_ref, chga_ref, chgb_ref, nch_ref,
                xr_hbm, g_ref, w1a_ref, w3a_ref, w2a_ref, w1b_ref, w3b_ref, w2b_ref,
                o_ref,
                perm, xbuf, wa1, wa3, wa2, wb1, wb3, wb2, gsem, *, d, n_rows):
    del ea_ref, eb_ref
    m = MOE_CHUNK
    pitch = _slab_pitch(d)
    n_slabs = d // LANES
    c = pl.program_id(0)
    n = nch_ref[0]
    slot = c % 2
    other = 1 - slot

    def start_gather(chunk, sl, j):
        tok = jnp.maximum(perm[chunk * m + j], 0)
        pltpu.make_async_copy(xr_hbm.at[pl.ds(tok * pitch, pitch)],
                              xbuf.at[sl, pl.ds(j * pitch, pitch)], gsem.at[sl]).start()

    def wait_gather(sl):
        pltpu.make_async_copy(xr_hbm.at[pl.ds(0, m * pitch)], xbuf.at[sl], gsem.at[sl]).wait()

    @pl.when(c == 0)
    def _():
        def fill(lo, hi):
            def body(i, z):
                perm[i] = -1
                return z
            lax.fori_loop(lo, hi, body, 0)

        def first_gather(j, z):
            start_gather(0, 0, j)
            return z

        fill(n * m, (n + 1) * m)
        for k in range(N_CLASSES):
            base = off_ref[k] * m
            cnt = cnt_ref[k]
            fill(base + cnt, base + ((cnt + m - 1) // m) * m)

        def place(t, z):
            perm[off_ref[cls_ref[t]] * m + rank_ref[t]] = t
            return z
        lax.fori_loop(0, n_rows, place, 0, unroll=8)
        lax.fori_loop(0, m, first_gather, 0)

    @pl.when(c >= n)
    def _():
        o_ref[...] = jnp.zeros_like(o_ref)

    @pl.when(c < n)
    def _():
        wait_gather(slot)

        @pl.when(chga_ref[c] == 1)
        def _():
            wa1[...] = w1a_ref[0].astype(BF16)
            wa3[...] = w3a_ref[0].astype(BF16)
            wa2[...] = w2a_ref[0].astype(BF16)

        @pl.when(chgb_ref[c] == 1)
        def _():
            wb1[...] = w1b_ref[0].astype(BF16)
            wb3[...] = w3b_ref[0].astype(BF16)
            wb2[...] = w2b_ref[0].astype(BF16)

        for j in range(m):
            start_gather(c + 1, other, j)
        x = _read_slabs(xbuf, (slot,), m, n_slabs, pitch)
        info = xbuf[slot, pl.ds(n_slabs, m, stride=pitch), :]
        w_a = info[:, 0:1]
        w_b = info[:, 1:2]
        h = _rms(x, g_ref[...]).astype(BF16)
        hid_a = (jax.nn.silu(jnp.dot(h, wa1[...], preferred_element_type=F32))
                 * jnp.dot(h, wa3[...], preferred_element_type=F32) * w_a).astype(BF16)
        hid_b = (jax.nn.silu(jnp.dot(h, wb1[...], preferred_element_type=F32))
                 * jnp.dot(h, wb3[...], preferred_element_type=F32) * w_b).astype(BF16)
        y = (jnp.dot(hid_a, wa2[...], preferred_element_type=F32)
             + jnp.dot(hid_b, wb2[...], preferred_element_type=F32))
        _write_slabs(o_ref, (), x + y, pitch)
        o_ref[pl.ds(n_slabs, m, stride=pitch), :] = jnp.zeros((m, LANES), F32)

        @pl.when(c == n - 1)
        def _():
            wait_gather(other)


def _moe(xr, g, w1, w3, w2, tables, n_rows, c_max):
    d = g.shape[1]
    f = w1.shape[2]
    m = MOE_CHUNK
    pitch = _slab_pitch(d)
    kern = functools.partial(_moe_kernel, d=d, n_rows=n_rows)

    def wspec(shape, which):
        return pl.BlockSpec(shape, lambda c, *pref: (pref[which][c], 0, 0))

    grid_spec = pltpu.PrefetchScalarGridSpec(
        num_scalar_prefetch=9,
        grid=(c_max,),
        in_specs=[
            pl.BlockSpec(memory_space=pl.ANY),
            pl.BlockSpec((1, d), lambda c, *pref: (0, 0)),
            wspec((1, d, f), 4), wspec((1, d, f), 4), wspec((1, f, d), 4),
            wspec((1, d, f), 5), wspec((1, d, f), 5), wspec((1, f, d), 5),
        ],
        out_specs=pl.BlockSpec((m * pitch, LANES), lambda c, *pref: (c, 0)),
        scratch_shapes=[
            pltpu.SMEM(((c_max + 1) * m,), I32),
            pltpu.VMEM((2, m * pitch, LANES), F32),
            pltpu.VMEM((d, f), BF16), pltpu.VMEM((d, f), BF16), pltpu.VMEM((f, d), BF16),
            pltpu.VMEM((d, f), BF16), pltpu.VMEM((d, f), BF16), pltpu.VMEM((f, d), BF16),
            pltpu.SemaphoreType.DMA((2,)),
        ],
    )
    return pl.pallas_call(
        kern,
        grid_spec=grid_spec,
        out_shape=jax.ShapeDtypeStruct((c_max * m * pitch, LANES), F32),
        compiler_params=pltpu.CompilerParams(
            dimension_semantics=("arbitrary",), vmem_limit_bytes=VMEM_LIMIT),
        name="moe",
    )(*tables, xr, g, w1, w3, w2, w1, w3, w2)


def _chunk_tables(counts, c_max, expert0):
    m = MOE_CHUNK
    nch_c = (counts + m - 1) // m
    ch_end = jnp.cumsum(nch_c)
    ch_off = ch_end - nch_c
    n_chunks = ch_end[-1]
    chunk = jnp.minimum(jnp.arange(c_max + 1, dtype=I32), jnp.maximum(n_chunks - 1, 0))
    ccls = jnp.sum((chunk[:, None] >= ch_end[None, :]).astype(I32), axis=1)
    ccls = jnp.minimum(ccls, N_CLASSES - 1)
    grp = ccls // N_PAIRS
    pid = ccls % N_PAIRS
    ea = expert0 + grp * EXPERTS_PER_GROUP + jnp.asarray(PAIR_SLOT_A, I32)[pid]
    eb = expert0 + grp * EXPERTS_PER_GROUP + jnp.asarray(PAIR_SLOT_B, I32)[pid]
    first = jnp.ones((1,), I32)
    chga = jnp.concatenate([first, (ea[1:] != ea[:-1]).astype(I32)])
    chgb = jnp.concatenate([first, (eb[1:] != eb[:-1]).astype(I32)])
    return ch_off.astype(I32), counts, ea, eb, chga, chgb, n_chunks.reshape(1).astype(I32)


def _final_norm_kernel(pos_ref, xs_hbm, g_ref, op_ref, os_ref, xbuf, gsem, *, n_prompt_blocks):
    i = pl.program_id(0)
    slot = i % 2
    rows = _RowGather(pos_ref, xs_hbm, xbuf, gsem, g_ref.shape[1])

    @pl.when(i == 0)
    def _():
        rows.start_first()

    rows.wait(slot)
    rows.start_next(i + 1, 1 - slot)
    y = _rms(rows.rows(slot), g_ref[...])

    @pl.when(i < n_prompt_blocks)
    def _():
        op_ref[...] = y

    @pl.when(i >= n_prompt_blocks)
    def _():
        os_ref[...] = y[0:os_ref.shape[0]]
        rows.wait(1 - slot)


def _final_norm(pos, xs, g, n_prompt, n_sample):
    d = g.shape[1]
    npb = n_prompt // ROW_BLOCK
    kern = functools.partial(_final_norm_kernel, n_prompt_blocks=npb)
    grid_spec = pltpu.PrefetchScalarGridSpec(
        num_scalar_prefetch=1,
        grid=(npb + 1,),
        in_specs=[pl.BlockSpec(memory_space=pl.ANY),
                  pl.BlockSpec((1, d), lambda i, pos: (0, 0))],
        out_specs=[pl.BlockSpec((ROW_BLOCK, d), lambda i, pos: (jnp.minimum(i, npb - 1), 0)),
                   pl.BlockSpec((n_sample, d), lambda i, pos: (0, 0))],
        scratch_shapes=[pltpu.VMEM((2, ROW_BLOCK * _slab_pitch(d), LANES), F32),
                        pltpu.SemaphoreType.DMA((2,))],
    )
    return pl.pallas_call(
        kern,
        grid_spec=grid_spec,
        out_shape=[jax.ShapeDtypeStruct((n_prompt, d), F32),
                   jax.ShapeDtypeStruct((n_sample, d), F32)],
        compiler_params=pltpu.CompilerParams(
            dimension_semantics=("arbitrary",), vmem_limit_bytes=VMEM_LIMIT),
        name="final_norm",
    )(pos, xs, g)


def kernel(x_prompt, x_sample, cache_k, cache_v, state_lru_h, state_conv, state_pool, norm1_g, w_in, conv_w, conv_b, lru_wa, lru_ba, lru_wx, lru_bx, lru_lambda, pool_w, pool_scale, attn_sinks, w_out, norm2_g, router_group_w, router_group_b, router_expert_w, router_expert_b, expert_w1, expert_w3, expert_w2, final_norm_g):
    bsz, seq, d = x_prompt.shape
    db, ds, _ = x_sample.shape
    depth = w_in.shape[0]
    win = cache_k.shape[2]
    nkv = cache_k.shape[3]
    d_lru = lru_lambda.shape[1]
    d_pool = pool_scale.shape[1]
    d_kv = nkv * cache_k.shape[4]
    d_attn = attn_sinks.shape[1] * HEAD_DIM
    dims = dict(d_lru=d_lru, d_pool=d_pool, d_attn=d_attn, d_kv=d_kv)
    n_prompt = bsz * seq
    n_sample = db * ds
    n_rows = n_prompt + n_sample
    assert n_prompt % ROW_BLOCK == 0 and n_sample <= ROW_BLOCK and n_sample % 8 == 0
    assert win == WINDOW and nkv == N_KV_HEADS and ds <= 8 and WINDOW % ds == 0
    tile = 256 if seq % 256 == 0 else WINDOW
    c_max = -(-n_rows // MOE_CHUNK) + N_CLASSES
    npb = n_prompt // ROW_BLOCK
    n_alloc = n_prompt + ROW_BLOCK

    cos_p, sin_p = _rope_tables(jnp.arange(seq))
    cos_s, sin_s = _rope_tables(PAST_LEN + jnp.arange(ds))

    x_main, x_tail = x_prompt.reshape(n_prompt, d), x_sample.reshape(n_sample, d)
    tail_rows, tail_block = n_sample, 0
    x_sorted = pos = None
    ck = cache_k.reshape(depth, db, win, d_kv)
    cv = cache_v.reshape(depth, db, win, d_kv)
    st_h = state_lru_h.reshape(depth, db, 1, d_lru)
    n_exp, _, d_exp = expert_w1.shape[1:]
    w1 = expert_w1.reshape(depth * n_exp, d, d_exp)
    w3 = expert_w3.reshape(depth * n_exp, d, d_exp)
    w2 = expert_w2.reshape(depth * n_exp, d_exp, d)
    outs = [[] for _ in range(10)]
    for l in range(depth):
        p = dict(conv_w=conv_w[l], conv_b=conv_b[l], lru_wa=lru_wa[l], lru_ba=lru_ba[l],
                 lru_wx=lru_wx[l], lru_bx=lru_bx[l], lru_lambda=lru_lambda[l], pool_w=pool_w[l],
                 pool_scale=pool_scale[l], attn_sinks=attn_sinks[l])
        consts = _layer_consts(p, dims)
        if l == 0:
            proj = _in_proj(x_main, x_tail, norm1_g[l].reshape(1, d), w_in, l)
        else:
            proj, x_nat = _in_proj_gather(pos, x_sorted, norm1_g[l].reshape(1, d), w_in, l, n_alloc)
            x_main, x_tail, tail_rows, tail_block = x_nat, x_nat, ROW_BLOCK, npb
        mix_p, pk, pv, ph, pc, pp = _mixer_prompt(proj, consts, cos_p, sin_p, bsz, seq, dims, tile)
        mix_s, sk, sv, sh, sc, sp = _mixer_sample(
            proj, n_prompt, consts, cos_s, sin_s, ck, cv, st_h, state_conv, state_pool, l, dims)
        wr = jnp.concatenate([router_group_w[l], router_expert_w[l]], axis=1)
        wr = jnp.pad(wr, ((0, 0), (0, LANES - wr.shape[1]))).astype(BF16)
        br = jnp.concatenate([router_group_b[l], router_expert_b[l]])
        br = jnp.pad(br, (0, LANES - br.shape[0])).reshape(1, LANES)
        xr, info, cnt = _out_proj(mix_p, mix_s, x_main, x_tail, tail_rows, tail_block, w_out, l,
                                  norm2_g[l].reshape(1, d), wr, br)
        route = info[:n_rows, 2:4].astype(I32)
        cls, rank = route[:, 0], route[:, 1]
        chunk_tables = _chunk_tables(cnt[0, :N_CLASSES].astype(I32), c_max, l * n_exp)
        x_sorted = _moe(xr, norm2_g[l].reshape(1, d), w1, w3, w2, (cls, rank) + chunk_tables, n_rows, c_max)
        pos = jnp.pad(chunk_tables[0][cls] * MOE_CHUNK + rank, (0, n_alloc + ROW_BLOCK - n_rows))
        for lst, val in zip(outs, (pk.reshape(bsz, WINDOW, nkv, HEAD_DIM), pv.reshape(bsz, WINDOW, nkv, HEAD_DIM),
                                   ph.reshape(bsz, d_lru), pc, pp,
                                   sk.reshape(db, win, nkv, HEAD_DIM), sv.reshape(db, win, nkv, HEAD_DIM),
                                   sh.reshape(db, d_lru), sc, sp)):
            lst.append(val)
    y_p, y_s = _final_norm(pos, x_sorted, final_norm_g.reshape(1, d), n_prompt, n_sample)
    return (y_p.reshape(bsz, seq, d), y_s.reshape(db, ds, d)) + tuple(jnp.stack(o) for o in outs)
```

```python
import functools

import jax
import jax.numpy as jnp
from jax import lax
from jax.experimental import pallas as pl
from jax.experimental.pallas import tpu as pltpu

F32 = jnp.float32
BF16 = jnp.bfloat16
I32 = jnp.int32

LRU_HEADS = 8
CONV_WIDTH = 4
LRU_C = 8.0
POOL_WINDOWS = (2, 4, 8, 16)
POOL_PAD = max(POOL_WINDOWS) - 1
HEAD_DIM = 64
N_KV_HEADS = 4
WINDOW = 128
ROPE_THETA = 10000.0
ATTN_SCALE = HEAD_DIM ** -0.5
N_EXPERT_GROUPS = 4
EXPERTS_PER_GROUP = 4
N_EXPERTS = N_EXPERT_GROUPS * EXPERTS_PER_GROUP
RMS_EPS = 1e-6
PAST_LEN = 16384

LANES = 128
HALF = LANES // 2
HALO = 16
ROW_BLOCK = 256
W_STAGE_ROWS = 128
MOE_CHUNK = 256
PAIR_SLOT_A = (0, 0, 0, 1, 1, 3)
PAIR_SLOT_B = (1, 2, 3, 3, 2, 2)
N_PAIRS = len(PAIR_SLOT_A)
N_CLASSES = N_EXPERT_GROUPS * N_PAIRS
VMEM_LIMIT = 52 * 1024 * 1024


def _rms(x, g):
    return (x * lax.rsqrt(jnp.mean(x * x, axis=-1, keepdims=True) + RMS_EPS)) * g


def _load_weight_bf16(w_hbm, w_bf, stage, sem):
    rows = stage.shape[1]
    n = w_hbm.shape[0] // rows

    def copy(i, sl):
        return pltpu.make_async_copy(w_hbm.at[pl.ds(i * rows, rows)], stage.at[sl], sem.at[sl])

    copy(0, 0).start()
    for i in range(n):
        sl = i % 2
        if i + 1 < n:
            copy(i + 1, 1 - sl).start()
        copy(i, sl).wait()
        w_bf[i * rows:(i + 1) * rows, :] = stage[sl].astype(BF16)


def _slab_pitch(d):
    return d // LANES + 1


def _read_slabs(ref, lead, n_tok, n_slabs, pitch):
    return jnp.concatenate(
        [ref[lead + (pl.ds(k, n_tok, stride=pitch), slice(None))] for k in range(n_slabs)], axis=1)


def _write_slabs(ref, lead, val, pitch):
    n_tok = val.shape[0]
    for k in range(val.shape[1] // LANES):
        ref[lead + (pl.ds(k, n_tok, stride=pitch), slice(None))] = val[:, k * LANES:(k + 1) * LANES]


def _pad_rows(x, n):
    if x.shape[0] == n:
        return x
    return jnp.concatenate([x, jnp.zeros((n - x.shape[0], x.shape[1]), x.dtype)], axis=0)


def _pick_rows(i, n_main_blocks, main_ref, tail_ref):
    return jnp.where(i < n_main_blocks, main_ref[...], _pad_rows(tail_ref[...], ROW_BLOCK))


def _row_specs(d, n_main_blocks, tail_rows, tail_block):
    return [pl.BlockSpec((ROW_BLOCK, d), lambda i: (jnp.minimum(i, n_main_blocks - 1), 0)),
            pl.BlockSpec((tail_rows, d), lambda i: (tail_block, 0))]


def _in_proj_kernel(xm_ref, xt_ref, g_ref, w_hbm, o_ref, w_bf, stage, sem, *, n_main_blocks, layer):
    i = pl.program_id(0)

    @pl.when(i == 0)
    def _():
        _load_weight_bf16(w_hbm.at[layer], w_bf, stage, sem)

    h = _rms(_pick_rows(i, n_main_blocks, xm_ref, xt_ref), g_ref[...])
    o_ref[...] = jnp.dot(h.astype(BF16), w_bf[...], preferred_element_type=F32)


def _in_proj(x_main, x_tail, g, w, layer):
    d = g.shape[1]
    d_in = w.shape[2]
    n_main = x_main.shape[0]
    nmb = n_main // ROW_BLOCK
    kern = functools.partial(_in_proj_kernel, n_main_blocks=nmb, layer=layer)
    return pl.pallas_call(
        kern,
        grid=(nmb + 1,),
        in_specs=_row_specs(d, nmb, x_tail.shape[0], 0) + [
            pl.BlockSpec((1, d), lambda i: (0, 0)),
            pl.BlockSpec(memory_space=pl.ANY),
        ],
        out_specs=pl.BlockSpec((ROW_BLOCK, d_in), lambda i: (i, 0)),
        out_shape=jax.ShapeDtypeStruct((n_main + ROW_BLOCK, d_in), F32),
        scratch_shapes=[
            pltpu.VMEM((d, d_in), BF16),
            pltpu.VMEM((2, W_STAGE_ROWS, d_in), F32),
            pltpu.SemaphoreType.DMA((2,)),
        ],
        compiler_params=pltpu.CompilerParams(
            dimension_semantics=("arbitrary",), vmem_limit_bytes=VMEM_LIMIT),
        name="in_proj",
    )(x_main, x_tail, g, w)


def _in_proj_slab_kernel(x_ref, g_ref, w_hbm, o_ref, xnat_ref, w_bf, stage, sem, *, layer):
    i = pl.program_id(0)
    d = g_ref.shape[1]

    @pl.when(i == 0)
    def _():
        _load_weight_bf16(w_hbm.at[layer], w_bf, stage, sem)

    x = _read_slabs(x_ref, (), ROW_BLOCK, d // LANES, _slab_pitch(d))
    xnat_ref[...] = x
    o_ref[...] = jnp.dot(_rms(x, g_ref[...]).astype(BF16), w_bf[...], preferred_element_type=F32)


def _in_proj_slab(xs, g, w, layer, n_rows):
    d = g.shape[1]
    d_in = w.shape[2]
    return pl.pallas_call(
        functools.partial(_in_proj_slab_kernel, layer=layer),
        grid=(n_rows // ROW_BLOCK,),
        in_specs=[
            pl.BlockSpec((ROW_BLOCK * _slab_pitch(d), LANES), lambda i: (i, 0)),
            pl.BlockSpec((1, d), lambda i: (0, 0)),
            pl.BlockSpec(memory_space=pl.ANY),
        ],
        out_specs=[pl.BlockSpec((ROW_BLOCK, d_in), lambda i: (i, 0)),
                   pl.BlockSpec((ROW_BLOCK, d), lambda i: (i, 0))],
        out_shape=[jax.ShapeDtypeStruct((n_rows, d_in), F32),
                   jax.ShapeDtypeStruct((n_rows, d), F32)],
        scratch_shapes=[
            pltpu.VMEM((d, d_in), BF16),
            pltpu.VMEM((2, W_STAGE_ROWS, d_in), F32),
            pltpu.SemaphoreType.DMA((2,)),
        ],
        compiler_params=pltpu.CompilerParams(
            dimension_semantics=("arbitrary",), vmem_limit_bytes=VMEM_LIMIT),
        name="in_proj_slab",
    )(xs, g, w)


def _scan_linear(a, b):
    t = a.shape[0]
    row = lax.broadcasted_iota(I32, a.shape, 0)
    d = 1
    while d < min(t, 8):
        a_sh = pltpu.roll(a, d, axis=0)
        b_sh = pltpu.roll(b, d, axis=0)
        m = row >= d
        b = jnp.where(m, a * b_sh + b, b)
        a = jnp.where(m, a * a_sh, a)
        d *= 2
    while d < t:
        b = jnp.concatenate([b[:d], a[d:] * b[:t - d] + b[d:]], axis=0)
        a = jnp.concatenate([a[:d], a[d:] * a[:t - d]], axis=0)
        d *= 2
    return a, b


def _lru_chunk(xc, g, h_prev, wg, ba, bx, sp):
    pre = jnp.dot(xc.astype(BF16), wg, preferred_element_type=F32)
    r = jax.nn.sigmoid(pre[:, :LANES] + ba)
    ig = jax.nn.sigmoid(pre[:, LANES:] + bx)
    log_a = (-LRU_C * r) * sp
    a = jnp.exp(log_a)
    bterm = jnp.sqrt(-jnp.tanh(log_a) * (a * a + 1.0)) * ig * xc
    a_cum, h0 = _scan_linear(a, bterm)
    hs = a_cum * h_prev + h0
    return hs * jax.nn.gelu(g), hs


def _rope(x, cos, sin_signed):
    n = x.shape[1] // LANES
    lane = lax.broadcasted_iota(I32, (x.shape[0], LANES), 1)
    first = (lane % HEAD_DIM) < (HEAD_DIM // 2)
    outs = []
    for c in range(n):
        xc = x[:, c * LANES:(c + 1) * LANES]
        swapped = jnp.where(first, pltpu.roll(xc, LANES - HEAD_DIM // 2, axis=1),
                            pltpu.roll(xc, HEAD_DIM // 2, axis=1))
        outs.append(xc * cos + swapped * sin_signed)
    return outs


def _store_head_variants(var_ref, row0, chunks):
    t = chunks[0].shape[0]
    lane = lax.broadcasted_iota(I32, (t, LANES), 1)
    for kc, x in enumerate(chunks):
        swapped = pltpu.roll(x, HALF, axis=1)
        for hh in range(2):
            for p in range(2):
                src = x if p == hh else swapped
                keep = (lane < HALF) if p == 0 else (lane >= HALF)
                var_ref[2 * kc + hh, p, row0:row0 + t, :] = jnp.where(keep, src, 0.0).astype(BF16)


def _attend_block(q_chunks, kvar, vvar, key0, sinks, lim, out_ref, out_rows, out_col0):
    qb = q_chunks[0].shape[0]
    nk = 2 * WINDOW
    gq = (2 * len(q_chunks)) // N_KV_HEADS
    rows = 2 * qb
    qi = lax.broadcasted_iota(I32, (rows, nk), 0) % qb
    kj = lax.broadcasted_iota(I32, (rows, nk), 1)
    valid = ((kj < WINDOW) & (kj > qi + lim)) | ((kj >= WINDOW) & (kj - WINDOW <= qi))
    top = lax.broadcasted_iota(I32, (rows, 1), 0) < qb
    lane = lax.broadcasted_iota(I32, (nk, LANES), 1)
    ones_lo = jnp.where(lane < HALF, 1.0, 0.0).astype(BF16)
    ones_hi = jnp.where(lane >= HALF, 1.0, 0.0).astype(BF16)
    lane_o = lax.broadcasted_iota(I32, (rows, LANES), 1)
    nt = (((1,), (1,)), ((), ()))
    for c in range(N_KV_HEADS):
        c0 = c * gq // 2
        qs = jnp.concatenate([q_chunks[c0], q_chunks[c0 + 1]], axis=0)
        es, sink_terms = [], []
        for p in range(2):
            kc = kvar[c, p, key0:key0 + nk, :]
            s = lax.dot_general(qs, kc, nt, preferred_element_type=F32)
            s = jnp.where(valid, s, -jnp.inf)
            h0 = c * gq + p
            sink = jnp.where(top, sinks[:, h0:h0 + 1], sinks[:, h0 + 2:h0 + 3])
            m = jnp.maximum(jnp.max(s, axis=-1, keepdims=True), sink)
            es.append(jnp.exp(s - m).astype(BF16))
            sink_terms.append(jnp.exp(sink - m))
        r0 = jnp.concatenate([vvar[c, 0, key0:key0 + nk, :], ones_lo], axis=1)
        r1 = jnp.concatenate([vvar[c, 1, key0:key0 + nk, :], ones_hi], axis=1)
        od = (jnp.dot(es[0], r0, preferred_element_type=F32)
              + jnp.dot(es[1], r1, preferred_element_type=F32))
        den = od[:, LANES:] + jnp.where(lane_o < HALF, sink_terms[0], sink_terms[1])
        o = (od[:, :LANES] / den).astype(out_ref.dtype)
        out_ref[out_rows, out_col0 + c0 * LANES:out_col0 + (c0 + 1) * LANES] = o[0:qb]
        out_ref[out_rows, out_col0 + (c0 + 1) * LANES:out_col0 + (c0 + 2) * LANES] = o[qb:rows]


def _mixer_kernel(*refs, tile, qblock, is_prompt, pos0, d_lru, d_pool, d_attn, d_kv):
    (proj_ref, cos_ref, sin_ref, convw_ref, convb_ref, wg_ref, ba_ref, bx_ref, lam_ref,
     poolw_ref, pscale_ref, sink_ref) = refs[:12]
    if is_prompt:
        (mix_ref, ko_ref, vo_ref, ho_ref, co_ref, po_ref,
         lru_ext, pool_ext, kvar, vvar, hcar) = refs[12:]
    else:
        (ck_ref, cv_ref, sh_ref, sc_ref, sp_ref,
         mix_ref, ko_ref, vo_ref, ho_ref, co_ref, po_ref,
         lru_ext, pool_ext, kvar, vvar) = refs[12:]

    s = pl.program_id(1) if is_prompt else 0
    o_glru = d_lru
    o_pool = 2 * d_lru
    o_q = o_pool + d_pool
    o_k = o_q + d_attn
    o_v = o_k + d_kv
    n_kc = d_kv // LANES

    if is_prompt:
        @pl.when(s == 0)
        def _():
            lru_ext[0:HALO, :] = jnp.zeros((HALO, d_lru), F32)
            pool_ext[0:HALO, :] = jnp.zeros((HALO, d_pool), F32)
            hcar[...] = jnp.zeros_like(hcar)
            kvar[:, :, 0:WINDOW, :] = jnp.zeros((N_KV_HEADS, 2, WINDOW, LANES), BF16)
            vvar[:, :, 0:WINDOW, :] = jnp.zeros((N_KV_HEADS, 2, WINDOW, LANES), BF16)
        h_prev = hcar[0:1, :]
    else:
        lru_ext[0:HALO, :] = jnp.zeros((HALO, d_lru), F32)
        pool_ext[0:HALO, :] = jnp.zeros((HALO, d_pool), F32)
        lru_ext[HALO - (CONV_WIDTH - 1):HALO, :] = sc_ref[0]
        pool_ext[HALO - POOL_PAD:HALO, :] = sp_ref[0]
        h_prev = sh_ref[0]
        kvar[:, :, WINDOW:2 * WINDOW, :] = jnp.zeros((N_KV_HEADS, 2, WINDOW, LANES), BF16)
        vvar[:, :, WINDOW:2 * WINDOW, :] = jnp.zeros((N_KV_HEADS, 2, WINDOW, LANES), BF16)
        _store_head_variants(kvar, 0, [ck_ref[0, :, c * LANES:(c + 1) * LANES] for c in range(n_kc)])
        _store_head_variants(vvar, 0, [cv_ref[0, :, c * LANES:(c + 1) * LANES] for c in range(n_kc)])

    lru_ext[HALO:HALO + tile, :] = proj_ref[:, 0:d_lru]
    pool_ext[HALO:HALO + tile, :] = proj_ref[:, o_pool:o_pool + d_pool]

    xc = convb_ref[...] + convw_ref[0:1, :] * lru_ext[HALO - 3:HALO - 3 + tile, :]
    for j in range(1, CONV_WIDTH):
        xc = xc + convw_ref[j:j + 1, :] * lru_ext[HALO - 3 + j:HALO - 3 + j + tile, :]
    neg = -lam_ref[...]
    sp_all = jnp.maximum(neg, 0.0) + jnp.log1p(jnp.exp(-jnp.abs(neg)))
    h_last = []
    for c in range(d_lru // LANES):
        cs = slice(c * LANES, (c + 1) * LANES)
        y, hs = _lru_chunk(xc[:, cs], proj_ref[:, o_glru + c * LANES:o_glru + (c + 1) * LANES],
                           h_prev[:, cs], wg_ref[c], ba_ref[:, cs], bx_ref[:, cs], sp_all[:, cs])
        mix_ref[:, cs] = y.astype(mix_ref.dtype)
        h_last.append(hs[tile - 1:tile, :])
    h_last = jnp.concatenate(h_last, axis=-1)

    row = lax.broadcasted_iota(I32, (tile, LANES), 0)
    pos = pos0 + s * tile + row
    for gi, w in enumerate(POOL_WINDOWS):
        cs = slice(gi * LANES, (gi + 1) * LANES)
        e = pool_ext[:, cs]
        acc = e
        step = 1
        while step < w:
            acc = acc + pltpu.roll(acc, step, axis=0)
            step *= 2
        cnt = jnp.minimum(pos + 1, w).astype(F32)
        dlt = acc[HALO:HALO + tile, :] / cnt - e[HALO:HALO + tile, :]
        y = jnp.dot(dlt.astype(BF16), poolw_ref[gi], preferred_element_type=F32) * pscale_ref[:, cs]
        mix_ref[:, d_lru + gi * LANES:d_lru + (gi + 1) * LANES] = y.astype(mix_ref.dtype)

    sinks = sink_ref[...]
    o_attn = d_lru + d_pool
    cos = cos_ref[...]
    sin = sin_ref[...]
    q_chunks = [(qc * ATTN_SCALE).astype(BF16) for qc in _rope(proj_ref[:, o_q:o_q + d_attn], cos, sin)]
    k_rot = _rope(proj_ref[:, o_k:o_k + d_kv], cos, sin)
    _store_head_variants(kvar, WINDOW, k_rot)
    _store_head_variants(vvar, WINDOW, [proj_ref[:, o_v + c * LANES:o_v + (c + 1) * LANES]
                                        for c in range(n_kc)])
    for blk in range(tile // qblock):
        rs = slice(blk * qblock, (blk + 1) * qblock)
        if is_prompt and blk == 0:
            lim = jnp.where(s == 0, WINDOW, 0)
        else:
            lim = 0
        _attend_block([qc[rs] for qc in q_chunks], kvar, vvar, blk * qblock, sinks, lim,
                      mix_ref, rs, o_attn)

    k_last = jnp.concatenate([kc[tile - qblock:tile] for kc in k_rot], axis=-1)
    if is_prompt:
        lru_ext[0:HALO, :] = lru_ext[tile:tile + HALO, :]
        pool_ext[0:HALO, :] = pool_ext[tile:tile + HALO, :]
        hcar[0:1, :] = h_last
        kvar[:, :, 0:WINDOW, :] = kvar[:, :, tile:tile + WINDOW, :]
        vvar[:, :, 0:WINDOW, :] = vvar[:, :, tile:tile + WINDOW, :]

        @pl.when(s == pl.num_programs(1) - 1)
        def _():
            ko_ref[0] = k_last
            vo_ref[0] = proj_ref[tile - qblock:tile, o_v:o_v + d_kv]
            ho_ref[0] = h_last
            co_ref[0] = lru_ext[HALO + tile - (CONV_WIDTH - 1):HALO + tile, :]
            po_ref[0] = pool_ext[HALO + tile - POOL_PAD:HALO + tile, :]
    else:
        ko_ref[0, 0:WINDOW - tile, :] = ck_ref[0, tile:WINDOW, :]
        ko_ref[0, WINDOW - tile:WINDOW, :] = k_last
        vo_ref[0, 0:WINDOW - tile, :] = cv_ref[0, tile:WINDOW, :]
        vo_ref[0, WINDOW - tile:WINDOW, :] = proj_ref[:, o_v:o_v + d_kv]
        ho_ref[0] = h_last
        co_ref[0] = lru_ext[HALO + tile - (CONV_WIDTH - 1):HALO + tile, :]
        po_ref[0] = pool_ext[HALO + tile - POOL_PAD:HALO + tile, :]


def _layer_consts(p, dims):
    d_lru, d_pool = dims["d_lru"], dims["d_pool"]
    hd = d_lru // LRU_HEADS
    per = LANES // hd
    nchunk = d_lru // LANES

    def blockdiag(w):
        w = w.reshape(nchunk, per, hd, hd)
        eye = jnp.eye(per, dtype=w.dtype)
        return jnp.einsum("cpij,pq->cpiqj", w, eye).reshape(nchunk, LANES, LANES)

    wg = jnp.concatenate([blockdiag(p["lru_wa"]), blockdiag(p["lru_wx"])], axis=-1).astype(BF16)
    return dict(
        convw=p["conv_w"], convb=p["conv_b"].reshape(1, d_lru), wg=wg,
        ba=p["lru_ba"].reshape(1, d_lru), bx=p["lru_bx"].reshape(1, d_lru),
        lam=p["lru_lambda"].reshape(1, d_lru), poolw=p["pool_w"].astype(BF16),
        pscale=p["pool_scale"].reshape(1, d_pool), sinks=p["attn_sinks"].reshape(1, -1))


def _rope_tables(pos):
    half = HEAD_DIM // 2
    inv = ROPE_THETA ** (-jnp.arange(half, dtype=F32) / half)
    ang = pos.astype(F32)[:, None] * inv[None, :]
    cos = jnp.cos(ang)
    sin = jnp.sin(ang)
    cos2 = jnp.concatenate([cos, cos], axis=-1)
    sin2 = jnp.concatenate([-sin, sin], axis=-1)
    reps = LANES // HEAD_DIM
    return jnp.tile(cos2, (1, reps)), jnp.tile(sin2, (1, reps))


CONST_NAMES = ("convw", "convb", "wg", "ba", "bx", "lam", "poolw", "pscale", "sinks")


def _const_specs(consts):
    return [pl.BlockSpec(consts[n].shape, functools.partial(lambda nd, *_: (0,) * nd, consts[n].ndim))
            for n in CONST_NAMES]


def _mixer_prompt(proj, consts, cos, sin, bsz, seq, dims, tile):
    d_lru, d_pool, d_attn, d_kv = dims["d_lru"], dims["d_pool"], dims["d_attn"], dims["d_kv"]
    d_in = proj.shape[1]
    d_mix = d_lru + d_pool + d_attn
    ns = seq // tile
    kern = functools.partial(_mixer_kernel, tile=tile, qblock=WINDOW, is_prompt=True, pos0=0,
                             d_lru=d_lru, d_pool=d_pool, d_attn=d_attn, d_kv=d_kv)
    return pl.pallas_call(
        kern,
        grid=(bsz, ns),
        in_specs=[
            pl.BlockSpec((tile, d_in), lambda b, s: (b * ns + s, 0)),
            pl.BlockSpec((tile, LANES), lambda b, s: (s, 0)),
            pl.BlockSpec((tile, LANES), lambda b, s: (s, 0)),
        ] + _const_specs(consts),
        out_specs=[
            pl.BlockSpec((tile, d_mix), lambda b, s: (b * ns + s, 0)),
            pl.BlockSpec((1, WINDOW, d_kv), lambda b, s: (b, 0, 0)),
            pl.BlockSpec((1, WINDOW, d_kv), lambda b, s: (b, 0, 0)),
            pl.BlockSpec((1, 1, d_lru), lambda b, s: (b, 0, 0)),
            pl.BlockSpec((1, CONV_WIDTH - 1, d_lru), lambda b, s: (b, 0, 0)),
            pl.BlockSpec((1, POOL_PAD, d_pool), lambda b, s: (b, 0, 0)),
        ],
        out_shape=[
            jax.ShapeDtypeStruct((bsz * seq, d_mix), BF16),
            jax.ShapeDtypeStruct((bsz, WINDOW, d_kv), F32),
            jax.ShapeDtypeStruct((bsz, WINDOW, d_kv), F32),
            jax.ShapeDtypeStruct((bsz, 1, d_lru), F32),
            jax.ShapeDtypeStruct((bsz, CONV_WIDTH - 1, d_lru), F32),
            jax.ShapeDtypeStruct((bsz, POOL_PAD, d_pool), F32),
        ],
        scratch_shapes=[
            pltpu.VMEM((tile + HALO, d_lru), F32),
            pltpu.VMEM((tile + HALO, d_pool), F32),
            pltpu.VMEM((N_KV_HEADS, 2, WINDOW + tile, LANES), BF16),
            pltpu.VMEM((N_KV_HEADS, 2, WINDOW + tile, LANES), BF16),
            pltpu.VMEM((8, d_lru), F32),
        ],
        compiler_params=pltpu.CompilerParams(
            dimension_semantics=("arbitrary", "arbitrary"), vmem_limit_bytes=VMEM_LIMIT),
        name="mixer_prompt",
    )(proj, cos, sin, *[consts[n] for n in CONST_NAMES])


def _mixer_sample(proj, row0, consts, cos, sin, cache_k, cache_v, st_h, st_conv, st_pool, layer, dims):
    d_lru, d_pool, d_attn, d_kv = dims["d_lru"], dims["d_pool"], dims["d_attn"], dims["d_kv"]
    d_in = proj.shape[1]
    d_mix = d_lru + d_pool + d_attn
    db, win = cache_k.shape[1], cache_k.shape[2]
    t = cos.shape[0]
    blk0 = row0 // t
    kern = functools.partial(_mixer_kernel, tile=t, qblock=t, is_prompt=False, pos0=PAST_LEN,
                             d_lru=d_lru, d_pool=d_pool, d_attn=d_attn, d_kv=d_kv)
    return pl.pallas_call(
        kern,
        grid=(db,),
        in_specs=[
            pl.BlockSpec((t, d_in), lambda b: (blk0 + b, 0)),
            pl.BlockSpec((t, LANES), lambda b: (0, 0)),
            pl.BlockSpec((t, LANES), lambda b: (0, 0)),
        ] + _const_specs(consts) + [
            pl.BlockSpec((None, 1, win, d_kv), lambda b: (layer, b, 0, 0)),
            pl.BlockSpec((None, 1, win, d_kv), lambda b: (layer, b, 0, 0)),
            pl.BlockSpec((None, 1, 1, d_lru), lambda b: (layer, b, 0, 0)),
            pl.BlockSpec((None, 1, CONV_WIDTH - 1, d_lru), lambda b: (layer, b, 0, 0)),
            pl.BlockSpec((None, 1, POOL_PAD, d_pool), lambda b: (layer, b, 0, 0)),
        ],
        out_specs=[
            pl.BlockSpec((t, d_mix), lambda b: (b, 0)),
            pl.BlockSpec((1, win, d_kv), lambda b: (b, 0, 0)),
            pl.BlockSpec((1, win, d_kv), lambda b: (b, 0, 0)),
            pl.BlockSpec((1, 1, d_lru), lambda b: (b, 0, 0)),
            pl.BlockSpec((1, CONV_WIDTH - 1, d_lru), lambda b: (b, 0, 0)),
            pl.BlockSpec((1, POOL_PAD, d_pool), lambda b: (b, 0, 0)),
        ],
        out_shape=[
            jax.ShapeDtypeStruct((db * t, d_mix), F32),
            jax.ShapeDtypeStruct((db, win, d_kv), F32),
            jax.ShapeDtypeStruct((db, win, d_kv), F32),
            jax.ShapeDtypeStruct((db, 1, d_lru), F32),
            jax.ShapeDtypeStruct((db, CONV_WIDTH - 1, d_lru), F32),
            jax.ShapeDtypeStruct((db, POOL_PAD, d_pool), F32),
        ],
        scratch_shapes=[
            pltpu.VMEM((t + HALO, d_lru), F32),
            pltpu.VMEM((t + HALO, d_pool), F32),
            pltpu.VMEM((N_KV_HEADS, 2, 2 * WINDOW, LANES), BF16),
            pltpu.VMEM((N_KV_HEADS, 2, 2 * WINDOW, LANES), BF16),
        ],
        compiler_params=pltpu.CompilerParams(
            dimension_semantics=("arbitrary",), vmem_limit_bytes=VMEM_LIMIT),
        name="mixer_sample",
    )(proj, cos, sin, *[consts[n] for n in CONST_NAMES], cache_k, cache_v, st_h, st_conv, st_pool)


def _route(logits, run_cnt, n_valid):
    t = logits.shape[0]
    lane = lax.broadcasted_iota(I32, (t, LANES), 1)
    lane_f = lane.astype(F32)
    ninf = -jnp.inf
    big = float(LANES)
    is_g = lane < N_EXPERT_GROUPS
    lg = jnp.where(is_g, logits, ninf)
    mg = jnp.max(lg, axis=-1, keepdims=True)
    g_top = jnp.min(jnp.where(lg == mg, lane_f, big), axis=-1, keepdims=True).astype(I32)
    pg_top = 1.0 / jnp.sum(jnp.exp(lg - mg), axis=-1, keepdims=True)
    base = N_EXPERT_GROUPS + EXPERTS_PER_GROUP * g_top
    in_grp = (lane >= base) & (lane < base + EXPERTS_PER_GROUP)
    le = jnp.where(in_grp, logits, ninf)
    m1 = jnp.max(le, axis=-1, keepdims=True)
    i1 = jnp.min(jnp.where(le == m1, lane_f, big), axis=-1, keepdims=True).astype(I32)
    le2 = jnp.where(lane == i1, ninf, le)
    m2 = jnp.max(le2, axis=-1, keepdims=True)
    i2 = jnp.min(jnp.where((le2 == m2) & in_grp & (lane != i1), lane_f, big),
                 axis=-1, keepdims=True).astype(I32)
    se = jnp.sum(jnp.exp(le - m1), axis=-1, keepdims=True)
    p1 = 1.0 / se
    p2 = jnp.exp(m2 - m1) / se
    tot = p1 + p2
    w1 = (p1 / tot) * pg_top
    w2 = (p2 / tot) * pg_top
    a = i1 - base
    b = i2 - base
    lo = jnp.minimum(a, b)
    hi = jnp.maximum(a, b)
    w_lo = jnp.where(a < b, w1, w2)
    w_hi = jnp.where(a < b, w2, w1)
    pid = jnp.where(lo == 0, hi - 1, jnp.where(lo == 1, jnp.where(hi == 3, 3, 4), 5))
    swap = pid == 5
    w_a = jnp.where(swap, w_hi, w_lo)
    w_b = jnp.where(swap, w_lo, w_hi)
    cls = g_top * N_PAIRS + pid
    onehot = (lane == cls) & (lax.broadcasted_iota(I32, (t, LANES), 0) < n_valid)
    ti = lax.broadcasted_iota(I32, (t, t), 0)
    tj = lax.broadcasted_iota(I32, (t, t), 1)
    lower = jnp.where(tj <= ti, 1.0, 0.0).astype(BF16)
    prefix = jnp.dot(lower, jnp.where(onehot, 1.0, 0.0).astype(BF16), preferred_element_type=F32)
    rank = jnp.sum(jnp.where(onehot, prefix - 1.0 + run_cnt, 0.0), axis=-1, keepdims=True)
    info = jnp.where(lane == 0, w_a, jnp.where(lane == 1, w_b, jnp.where(
        lane == 2, cls.astype(F32), jnp.where(lane == 3, rank, 0.0))))
    return info, run_cnt + prefix[t - 1:t, :]


def _out_proj_kernel(mixp_ref, mixs_ref, xm_ref, xt_ref, w_hbm, g_ref, wr_ref, br_ref,
                     o_ref, info_ref, cnt_ref, w_bf, stage, sem, run_cnt, *, n_main_blocks, d, layer):
    i = pl.program_id(0)

    @pl.when(i == 0)
    def _():
        _load_weight_bf16(w_hbm.at[layer], w_bf, stage, sem)
        run_cnt[...] = jnp.zeros_like(run_cnt)

    n_tail = mixs_ref.shape[0]
    mix = jnp.where(i < n_main_blocks, mixp_ref[...], _pad_rows(mixs_ref[...].astype(BF16), ROW_BLOCK))
    xres = _pick_rows(i, n_main_blocks, xm_ref, xt_ref) + jnp.dot(mix, w_bf[...], preferred_element_type=F32)
    h2 = _rms(xres, g_ref[...])
    logits = jnp.dot(h2.astype(BF16), wr_ref[...], preferred_element_type=F32) + br_ref[...]
    info, cnt = _route(logits, run_cnt[0:1, :], jnp.where(i < n_main_blocks, ROW_BLOCK, n_tail))
    run_cnt[0:1, :] = cnt
    pitch = _slab_pitch(d)
    _write_slabs(o_ref, (), xres, pitch)
    o_ref[pl.ds(d // LANES, ROW_BLOCK, stride=pitch), :] = info
    info_ref[...] = info
    cnt_ref[...] = jnp.broadcast_to(cnt, cnt_ref.shape)


def _out_proj(mix_p, mix_s, x_main, x_tail, tail_rows, tail_block, w, layer, g, wr, br):
    d = g.shape[1]
    d_mix = w.shape[1]
    n_main, n_tail = mix_p.shape[0], mix_s.shape[0]
    npb = n_main // ROW_BLOCK
    pitch = _slab_pitch(d)
    kern = functools.partial(_out_proj_kernel, n_main_blocks=npb, d=d, layer=layer)
    return pl.pallas_call(
        kern,
        grid=(npb + 1,),
        in_specs=[
            pl.BlockSpec((ROW_BLOCK, d_mix), lambda i: (jnp.minimum(i, npb - 1), 0)),
            pl.BlockSpec((n_tail, d_mix), lambda i: (0, 0)),
        ] + _row_specs(d, npb, tail_rows, tail_block) + [
            pl.BlockSpec(memory_space=pl.ANY),
            pl.BlockSpec((1, d), lambda i: (0, 0)),
            pl.BlockSpec((d, LANES), lambda i: (0, 0)),
            pl.BlockSpec((1, LANES), lambda i: (0, 0)),
        ],
        out_specs=[pl.BlockSpec((ROW_BLOCK * pitch, LANES), lambda i: (i, 0)),
                   pl.BlockSpec((ROW_BLOCK, LANES), lambda i: (i, 0)),
                   pl.BlockSpec((8, LANES), lambda i: (0, 0))],
        out_shape=[jax.ShapeDtypeStruct(((n_main + ROW_BLOCK) * pitch, LANES), F32),
                   jax.ShapeDtypeStruct((n_main + ROW_BLOCK, LANES), F32),
                   jax.ShapeDtypeStruct((8, LANES), F32)],
        scratch_shapes=[
            pltpu.VMEM((d_mix, d), BF16),
            pltpu.VMEM((2, W_STAGE_ROWS, d), F32),
            pltpu.SemaphoreType.DMA((2,)),
            pltpu.VMEM((8, LANES), F32),
        ],
        compiler_params=pltpu.CompilerParams(
            dimension_semantics=("arbitrary",), vmem_limit_bytes=VMEM_LIMIT),
        name="out_proj",
    )(mix_p, mix_s, x_main, x_tail, w, g, wr, br)


def _moe_sort_kernel(cls_ref, rank_ref, off_ref, cnt_ref, ea_ref, eb_ref, chga_ref, chgb_ref, nch_ref,
                     xr_ref, g_ref, w1a_ref, w3a_ref, w2a_ref, w1b_ref, w3b_ref, w2b_ref,
                     xs_hbm, xo_hbm,
                     perm, xbuf, obuf, wa1, wa3, wa2, wb1, wb3, wb2, asem, gsem, ssem,
                     *, d, n_rows, n_sort_steps, c_max):
    del ea_ref, eb_ref
    m = MOE_CHUNK
    pitch = _slab_pitch(d)
    n_slabs = d // LANES
    s = pl.program_id(0)
    n = nch_ref[0]
    c = s - n_sort_steps
    slot = c % 2
    other = 1 - slot

    def chunk_read(chunk, sl):
        return pltpu.make_async_copy(xs_hbm.at[pl.ds(chunk * (m * pitch), m * pitch)], xbuf.at[sl],
                                     gsem.at[sl])

    def start_scatter(chunk, sl, j):
        tok = perm[(chunk + 1) * m + j]
        r = jnp.where(tok < 0, n_rows + sl * m + j, tok)
        pltpu.make_async_copy(obuf.at[sl, pl.ds(j * pitch, pitch)],
                              xo_hbm.at[pl.ds(r * pitch, pitch)], ssem.at[sl]).start()

    def wait_scatter(sl):
        pltpu.make_async_copy(obuf.at[sl], xo_hbm.at[pl.ds(0, m * pitch)], ssem.at[sl]).wait()

    @pl.when(s == 0)
    def _():
        obuf[...] = jnp.zeros_like(obuf)

        def zero_rows(dst_hbm, row0):
            return pltpu.make_async_copy(obuf.at[0], dst_hbm.at[pl.ds(row0 * pitch, m * pitch)], ssem.at[0])

        first = zero_rows(xo_hbm, n_rows)
        first.start()
        first.wait()
        for k in range(N_CLASSES):
            zero_rows(xs_hbm, off_ref[k] * m + cnt_ref[k]).start()
        for k in range(N_CLASSES):
            zero_rows(xs_hbm, 0).wait()

        def zero_chunk(k, z):
            cp = zero_rows(xs_hbm, k * m)
            cp.start()
            cp.wait()
            return z
        lax.fori_loop(n, c_max + 1, zero_chunk, 0)

        def fill(lo, hi):
            def body(i, z):
                perm[i] = -1
                return z
            lax.fori_loop(lo, hi, body, 0)

        fill(0, m)
        for k in range(N_CLASSES):
            base = (off_ref[k] + 1) * m
            cnt = cnt_ref[k]
            fill(base + cnt, base + ((cnt + m - 1) // m) * m)

    @pl.when(s < n_sort_steps)
    def _():
        for j in range(ROW_BLOCK):
            tok = s * ROW_BLOCK + j
            p = off_ref[cls_ref[tok]] * m + rank_ref[tok]
            perm[p + m] = tok
            pltpu.make_async_copy(xr_ref.at[pl.ds(j * pitch, pitch)],
                                  xs_hbm.at[pl.ds(p * pitch, pitch)], asem.at[0]).start()
        pltpu.make_async_copy(xr_ref, xs_hbm.at[pl.ds(0, ROW_BLOCK * pitch)], asem.at[0]).wait()

        @pl.when(s == n_sort_steps - 1)
        def _():
            chunk_read(0, 0).start()

    def copy_back_previous():
        for j in range(m):
            start_scatter(c - 1, other, j)

    @pl.when((c >= 0) & (c < n))
    def _():
        chunk_read(c, slot).wait()

        @pl.when(c >= 1)
        def _():
            wait_scatter(slot)

        @pl.when(c + 1 < n)
        def _():
            chunk_read(c + 1, other).start()

        @pl.when(chga_ref[c] == 1)
        def _():
            wa1[...] = w1a_ref[0].astype(BF16)
            wa3[...] = w3a_ref[0].astype(BF16)
            wa2[...] = w2a_ref[0].astype(BF16)

        @pl.when(chgb_ref[c] == 1)
        def _():
            wb1[...] = w1b_ref[0].astype(BF16)
            wb3[...] = w3b_ref[0].astype(BF16)
            wb2[...] = w2b_ref[0].astype(BF16)

        copy_back_previous()
        x = _read_slabs(xbuf, (slot,), m, n_slabs, pitch)
        info = xbuf[slot, pl.ds(n_slabs, m, stride=pitch), :]
        w_a = info[:, 0:1]
        w_b = info[:, 1:2]
        h = _rms(x, g_ref[...]).astype(BF16)
        hid_a = (jax.nn.silu(jnp.dot(h, wa1[...], preferred_element_type=F32))
                 * jnp.dot(h, wa3[...], preferred_element_type=F32) * w_a).astype(BF16)
        hid_b = (jax.nn.silu(jnp.dot(h, wb1[...], preferred_element_type=F32))
                 * jnp.dot(h, wb3[...], preferred_element_type=F32) * w_b).astype(BF16)
        y = (jnp.dot(hid_a, wa2[...], preferred_element_type=F32)
             + jnp.dot(hid_b, wb2[...], preferred_element_type=F32))
        _write_slabs(obuf, (slot,), x + y, pitch)

    @pl.when(c == n)
    def _():
        wait_scatter(slot)
        copy_back_previous()
        wait_scatter(other)


def _moe_sort(xr, g, w1, w3, w2, tables, n_rows, c_max):
    d = g.shape[1]
    f = w1.shape[2]
    m = MOE_CHUNK
    pitch = _slab_pitch(d)
    n_sort = n_rows // ROW_BLOCK
    kern = functools.partial(_moe_sort_kernel, d=d, n_rows=n_rows, n_sort_steps=n_sort, c_max=c_max)

    def wspec(shape, which):
        return pl.BlockSpec(shape, lambda s, *pref: (pref[which][jnp.maximum(s - n_sort, 0)], 0, 0))

    grid_spec = pltpu.PrefetchScalarGridSpec(
        num_scalar_prefetch=9,
        grid=(n_sort + c_max + 1,),
        in_specs=[
            pl.BlockSpec((ROW_BLOCK * pitch, LANES), lambda s, *pref: (jnp.minimum(s, n_sort - 1), 0)),
            pl.BlockSpec((1, d), lambda s, *pref: (0, 0)),
            wspec((1, d, f), 4), wspec((1, d, f), 4), wspec((1, f, d), 4),
            wspec((1, d, f), 5), wspec((1, d, f), 5), wspec((1, f, d), 5),
        ],
        out_specs=[pl.BlockSpec(memory_space=pl.ANY), pl.BlockSpec(memory_space=pl.ANY)],
        scratch_shapes=[
            pltpu.SMEM(((c_max + 1) * m,), I32),
            pltpu.VMEM((2, m * pitch, LANES), F32),
            pltpu.VMEM((2, m * pitch, LANES), F32),
            pltpu.VMEM((d, f), BF16), pltpu.VMEM((d, f), BF16), pltpu.VMEM((f, d), BF16),
            pltpu.VMEM((d, f), BF16), pltpu.VMEM((d, f), BF16), pltpu.VMEM((f, d), BF16),
            pltpu.SemaphoreType.DMA((1,)),
            pltpu.SemaphoreType.DMA((2,)),
            pltpu.SemaphoreType.DMA((2,)),
        ],
    )
    return pl.pallas_call(
        kern,
        grid_spec=grid_spec,
        out_shape=[jax.ShapeDtypeStruct(((c_max + 1) * m * pitch, LANES), F32),
                   jax.ShapeDtypeStruct(((n_rows + 2 * m) * pitch, LANES), F32)],
        compiler_params=pltpu.CompilerParams(
            dimension_semantics=("arbitrary",), vmem_limit_bytes=VMEM_LIMIT),
        name="moe",
    )(*tables, xr, g, w1, w3, w2, w1, w3, w2)


def _chunk_tables(counts, c_max, expert0):
    m = MOE_CHUNK
    nch_c = (counts + m - 1) // m
    ch_end = jnp.cumsum(nch_c)
    ch_off = ch_end - nch_c
    n_chunks = ch_end[-1]
    chunk = jnp.minimum(jnp.arange(c_max + 1, dtype=I32), jnp.maximum(n_chunks - 1, 0))
    ccls = jnp.sum((chunk[:, None] >= ch_end[None, :]).astype(I32), axis=1)
    ccls = jnp.minimum(ccls, N_CLASSES - 1)
    grp = ccls // N_PAIRS
    pid = ccls % N_PAIRS
    ea = expert0 + grp * EXPERTS_PER_GROUP + jnp.asarray(PAIR_SLOT_A, I32)[pid]
    eb = expert0 + grp * EXPERTS_PER_GROUP + jnp.asarray(PAIR_SLOT_B, I32)[pid]
    first = jnp.ones((1,), I32)
    chga = jnp.concatenate([first, (ea[1:] != ea[:-1]).astype(I32)])
    chgb = jnp.concatenate([first, (eb[1:] != eb[:-1]).astype(I32)])
    return ch_off.astype(I32), counts, ea, eb, chga, chgb, n_chunks.reshape(1).astype(I32)


def _final_norm_kernel(x_ref, g_ref, op_ref, os_ref, *, n_prompt_blocks):
    i = pl.program_id(0)
    d = g_ref.shape[1]
    y = _rms(_read_slabs(x_ref, (), ROW_BLOCK, d // LANES, _slab_pitch(d)), g_ref[...])

    @pl.when(i < n_prompt_blocks)
    def _():
        op_ref[...] = y

    @pl.when(i >= n_prompt_blocks)
    def _():
        os_ref[...] = y[0:os_ref.shape[0]]


def _final_norm(x, g, n_prompt, n_sample):
    d = g.shape[1]
    npb = n_prompt // ROW_BLOCK
    kern = functools.partial(_final_norm_kernel, n_prompt_blocks=npb)
    return pl.pallas_call(
        kern,
        grid=(npb + 1,),
        in_specs=[pl.BlockSpec((ROW_BLOCK * _slab_pitch(d), LANES), lambda i: (i, 0)),
                  pl.BlockSpec((1, d), lambda i: (0, 0))],
        out_specs=[pl.BlockSpec((ROW_BLOCK, d), lambda i: (jnp.minimum(i, npb - 1), 0)),
                   pl.BlockSpec((n_sample, d), lambda i: (0, 0))],
        out_shape=[jax.ShapeDtypeStruct((n_prompt, d), F32),
                   jax.ShapeDtypeStruct((n_sample, d), F32)],
        compiler_params=pltpu.CompilerParams(dimension_semantics=("arbitrary",)),
        name="final_norm",
    )(x, g)


def kernel(x_prompt, x_sample, cache_k, cache_v, state_lru_h, state_conv, state_pool, norm1_g, w_in, conv_w, conv_b, lru_wa, lru_ba, lru_wx, lru_bx, lru_lambda, pool_w, pool_scale, attn_sinks, w_out, norm2_g, router_group_w, router_group_b, router_expert_w, router_expert_b, expert_w1, expert_w3, expert_w2, final_norm_g):
    bsz, seq, d = x_prompt.shape
    db, ds, _ = x_sample.shape
    depth = w_in.shape[0]
    win = cache_k.shape[2]
    nkv = cache_k.shape[3]
    d_lru = lru_lambda.shape[1]
    d_pool = pool_scale.shape[1]
    d_kv = nkv * cache_k.shape[4]
    d_attn = attn_sinks.shape[1] * HEAD_DIM
    dims = dict(d_lru=d_lru, d_pool=d_pool, d_attn=d_attn, d_kv=d_kv)
    n_prompt = bsz * seq
    n_sample = db * ds
    n_rows = n_prompt + n_sample
    assert n_prompt % ROW_BLOCK == 0 and n_sample == ROW_BLOCK
    assert win == WINDOW and nkv == N_KV_HEADS and ds <= 8 and WINDOW % ds == 0
    tile = 256 if seq % 256 == 0 else WINDOW
    c_max = -(-n_rows // MOE_CHUNK) + N_CLASSES
    npb = n_prompt // ROW_BLOCK

    cos_p, sin_p = _rope_tables(jnp.arange(seq))
    cos_s, sin_s = _rope_tables(PAST_LEN + jnp.arange(ds))

    x_main, x_tail = x_prompt.reshape(n_prompt, d), x_sample.reshape(n_sample, d)
    tail_block = 0
    x_tok = None
    ck = cache_k.reshape(depth, db, win, d_kv)
    cv = cache_v.reshape(depth, db, win, d_kv)
    st_h = state_lru_h.reshape(depth, db, 1, d_lru)
    n_exp, _, d_exp = expert_w1.shape[1:]
    w1 = expert_w1.reshape(depth * n_exp, d, d_exp)
    w3 = expert_w3.reshape(depth * n_exp, d, d_exp)
    w2 = expert_w2.reshape(depth * n_exp, d_exp, d)
    outs = [[] for _ in range(10)]
    for l in range(depth):
        p = dict(conv_w=conv_w[l], conv_b=conv_b[l], lru_wa=lru_wa[l], lru_ba=lru_ba[l],
                 lru_wx=lru_wx[l], lru_bx=lru_bx[l], lru_lambda=lru_lambda[l], pool_w=pool_w[l],
                 pool_scale=pool_scale[l], attn_sinks=attn_sinks[l])
        consts = _layer_consts(p, dims)
        if l == 0:
            proj = _in_proj(x_main, x_tail, norm1_g[l].reshape(1, d), w_in, l)
        else:
            proj, x_nat = _in_proj_slab(x_tok, norm1_g[l].reshape(1, d), w_in, l, n_rows)
            x_main, x_tail, tail_block = x_nat, x_nat, npb
        mix_p, pk, pv, ph, pc, pp = _mixer_prompt(proj, consts, cos_p, sin_p, bsz, seq, dims, tile)
        mix_s, sk, sv, sh, sc, sp = _mixer_sample(
            proj, n_prompt, consts, cos_s, sin_s, ck, cv, st_h, state_conv, state_pool, l, dims)
        wr = jnp.concatenate([router_group_w[l], router_expert_w[l]], axis=1)
        wr = jnp.pad(wr, ((0, 0), (0, LANES - wr.shape[1]))).astype(BF16)
        br = jnp.concatenate([router_group_b[l], router_expert_b[l]])
        br = jnp.pad(br, (0, LANES - br.shape[0])).reshape(1, LANES)
        xr, info, cnt = _out_proj(mix_p, mix_s, x_main, x_tail, ROW_BLOCK, tail_block, w_out, l,
                                  norm2_g[l].reshape(1, d), wr, br)
        route = info[:, 2:4].astype(I32)
        chunk_tables = _chunk_tables(cnt[0, :N_CLASSES].astype(I32), c_max, l * n_exp)
        _, x_tok = _moe_sort(xr, norm2_g[l].reshape(1, d), w1, w3, w2,
                             (route[:, 0], route[:, 1]) + chunk_tables, n_rows, c_max)
        for lst, val in zip(outs, (pk.reshape(bsz, WINDOW, nkv, HEAD_DIM), pv.reshape(bsz, WINDOW, nkv, HEAD_DIM),
                                   ph.reshape(bsz, d_lru), pc, pp,
                                   sk.reshape(db, win, nkv, HEAD_DIM), sv.reshape(db, win, nkv, HEAD_DIM),
                                   sh.reshape(db, d_lru), sc, sp)):
            lst.append(val)
    y_p, y_s = _final_norm(x_tok, final_norm_g.reshape(1, d), n_prompt, n_sample)
    return (y_p.reshape(bsz, seq, d), y_s.reshape(db, ds, d)) + tuple(jnp.stack(o) for o in outs)
```

```python
import functools

import jax
import jax.numpy as jnp
from jax import lax
from jax.experimental import pallas as pl
from jax.experimental.pallas import tpu as pltpu

F32 = jnp.float32
BF16 = jnp.bfloat16
I32 = jnp.int32

LRU_HEADS = 8
CONV_WIDTH = 4
LRU_C = 8.0
POOL_WINDOWS = (2, 4, 8, 16)
POOL_PAD = max(POOL_WINDOWS) - 1
HEAD_DIM = 64
N_KV_HEADS = 4
WINDOW = 128
ROPE_THETA = 10000.0
ATTN_SCALE = HEAD_DIM ** -0.5
N_EXPERT_GROUPS = 4
EXPERTS_PER_GROUP = 4
N_EXPERTS = N_EXPERT_GROUPS * EXPERTS_PER_GROUP
RMS_EPS = 1e-6
PAST_LEN = 16384

LANES = 128
HALF = LANES // 2
HALO = 16
ROW_BLOCK = 256
W_STAGE_ROWS = 64
MOE_CHUNK = 128
PAIR_SLOT_A = (0, 0, 0, 1, 1, 3)
PAIR_SLOT_B = (1, 2, 3, 3, 2, 2)
N_PAIRS = len(PAIR_SLOT_A)
N_CLASSES = N_EXPERT_GROUPS * N_PAIRS
VMEM_LIMIT = 52 * 1024 * 1024


def _rms(x, g):
    return (x * lax.rsqrt(jnp.mean(x * x, axis=-1, keepdims=True) + RMS_EPS)) * g


def _load_weight_bf16(w_hbm, w_bf, stage, sem):
    rows = stage.shape[1]
    n = w_hbm.shape[0] // rows

    def copy(i, sl):
        return pltpu.make_async_copy(w_hbm.at[pl.ds(i * rows, rows)], stage.at[sl], sem.at[sl])

    copy(0, 0).start()
    for i in range(n):
        sl = i % 2
        if i + 1 < n:
            copy(i + 1, 1 - sl).start()
        copy(i, sl).wait()
        w_bf[i * rows:(i + 1) * rows, :] = stage[sl].astype(BF16)


def _slab_pitch(d):
    return d // LANES + 1


def _read_slabs(ref, lead, n_tok, n_slabs, pitch):
    return jnp.concatenate(
        [ref[lead + (pl.ds(k, n_tok, stride=pitch), slice(None))] for k in range(n_slabs)], axis=1)


def _write_slabs(ref, lead, val, pitch):
    n_tok = val.shape[0]
    for k in range(val.shape[1] // LANES):
        ref[lead + (pl.ds(k, n_tok, stride=pitch), slice(None))] = val[:, k * LANES:(k + 1) * LANES]


def _pick_rows(i, n_main_blocks, main_ref, tail_ref, d, slab):
    if slab:
        main = _read_slabs(main_ref, (), ROW_BLOCK, d // LANES, _slab_pitch(d))
        tail = _read_slabs(tail_ref, (), ROW_BLOCK, d // LANES, _slab_pitch(d))
    else:
        main, tail = main_ref[...], tail_ref[...]
    return jnp.where(i < n_main_blocks, main, tail)


def _in_proj_kernel(xm_ref, xt_ref, g_ref, w_hbm, o_ref, w_bf, stage, sem, *, n_main_blocks, layer, slab):
    i = pl.program_id(0)

    @pl.when(i == 0)
    def _():
        _load_weight_bf16(w_hbm.at[layer], w_bf, stage, sem)

    h = _rms(_pick_rows(i, n_main_blocks, xm_ref, xt_ref, g_ref.shape[1], slab), g_ref[...])
    o_ref[...] = jnp.dot(h.astype(BF16), w_bf[...], preferred_element_type=F32)


def _row_specs(d, n_main_blocks, tail_block0, slab):
    shape = (ROW_BLOCK * _slab_pitch(d), LANES) if slab else (ROW_BLOCK, d)
    return [pl.BlockSpec(shape, lambda i: (jnp.minimum(i, n_main_blocks - 1), 0)),
            pl.BlockSpec(shape, lambda i: (jnp.maximum(i - n_main_blocks, 0) + tail_block0, 0))]


def _in_proj(x_main, x_tail, tail_block0, slab, g, w, layer, n_rows, n_main):
    d = g.shape[1]
    d_in = w.shape[2]
    nmb = n_main // ROW_BLOCK
    kern = functools.partial(_in_proj_kernel, n_main_blocks=nmb, layer=layer, slab=slab)
    return pl.pallas_call(
        kern,
        grid=(n_rows // ROW_BLOCK,),
        in_specs=_row_specs(d, nmb, tail_block0, slab) + [
            pl.BlockSpec((1, d), lambda i: (0, 0)),
            pl.BlockSpec(memory_space=pl.ANY),
        ],
        out_specs=pl.BlockSpec((ROW_BLOCK, d_in), lambda i: (i, 0)),
        out_shape=jax.ShapeDtypeStruct((n_rows, d_in), F32),
        scratch_shapes=[
            pltpu.VMEM((d, d_in), BF16),
            pltpu.VMEM((2, ROW_BLOCK, d_in), F32),
            pltpu.SemaphoreType.DMA((2,)),
        ],
        compiler_params=pltpu.CompilerParams(
            dimension_semantics=("arbitrary",), vmem_limit_bytes=VMEM_LIMIT),
        name="in_proj",
    )(x_main, x_tail, g, w)


def _scan_linear(a, b):
    t = a.shape[0]
    row = lax.broadcasted_iota(I32, a.shape, 0)
    d = 1
    while d < min(t, 8):
        a_sh = pltpu.roll(a, d, axis=0)
        b_sh = pltpu.roll(b, d, axis=0)
        m = row >= d
        b = jnp.where(m, a * b_sh + b, b)
        a = jnp.where(m, a * a_sh, a)
        d *= 2
    while d < t:
        b = jnp.concatenate([b[:d], a[d:] * b[:t - d] + b[d:]], axis=0)
        a = jnp.concatenate([a[:d], a[d:] * a[:t - d]], axis=0)
        d *= 2
    return a, b


def _lru_chunk(xc, g, h_prev, wg, ba, bx, sp):
    pre = jnp.dot(xc.astype(BF16), wg, preferred_element_type=F32)
    r = jax.nn.sigmoid(pre[:, :LANES] + ba)
    ig = jax.nn.sigmoid(pre[:, LANES:] + bx)
    log_a = (-LRU_C * r) * sp
    a = jnp.exp(log_a)
    v = 1.0 - a * a
    bterm = jnp.where(v > 0.0, v * lax.rsqrt(v), 0.0) * ig * xc
    a_cum, h0 = _scan_linear(a, bterm)
    hs = a_cum * h_prev + h0
    return hs * jax.nn.gelu(g), hs


def _rope(x, cos, sin_signed):
    n = x.shape[1] // LANES
    lane = lax.broadcasted_iota(I32, (x.shape[0], LANES), 1)
    first = (lane % HEAD_DIM) < (HEAD_DIM // 2)
    outs = []
    for c in range(n):
        xc = x[:, c * LANES:(c + 1) * LANES]
        swapped = jnp.where(first, pltpu.roll(xc, LANES - HEAD_DIM // 2, axis=1),
                            pltpu.roll(xc, HEAD_DIM // 2, axis=1))
        outs.append(xc * cos + swapped * sin_signed)
    return outs


def _store_head_variants(var_ref, row0, chunks):
    t = chunks[0].shape[0]
    lane = lax.broadcasted_iota(I32, (t, LANES), 1)
    for kc, x in enumerate(chunks):
        swapped = pltpu.roll(x, HALF, axis=1)
        for hh in range(2):
            for p in range(2):
                src = x if p == hh else swapped
                keep = (lane < HALF) if p == 0 else (lane >= HALF)
                var_ref[2 * kc + hh, p, row0:row0 + t, :] = jnp.where(keep, src, 0.0).astype(BF16)


def _attend_block(q_chunks, kvar, vvar, key0, sinks, lim, out_ref, out_rows, out_col0):
    qb = q_chunks[0].shape[0]
    nk = 2 * WINDOW
    gq = (2 * len(q_chunks)) // N_KV_HEADS
    rows = 2 * qb
    qi = lax.broadcasted_iota(I32, (rows, nk), 0) % qb
    kj = lax.broadcasted_iota(I32, (rows, nk), 1)
    valid = ((kj < WINDOW) & (kj > qi + lim)) | ((kj >= WINDOW) & (kj - WINDOW <= qi))
    top = lax.broadcasted_iota(I32, (rows, 1), 0) < qb
    lane = lax.broadcasted_iota(I32, (nk, LANES), 1)
    ones_lo = jnp.where(lane < HALF, 1.0, 0.0).astype(BF16)
    ones_hi = jnp.where(lane >= HALF, 1.0, 0.0).astype(BF16)
    lane_o = lax.broadcasted_iota(I32, (rows, LANES), 1)
    nt = (((1,), (1,)), ((), ()))
    for c in range(N_KV_HEADS):
        c0 = c * gq // 2
        qs = jnp.concatenate([q_chunks[c0], q_chunks[c0 + 1]], axis=0)
        es, sink_terms = [], []
        for p in range(2):
            kc = kvar[c, p, key0:key0 + nk, :]
            s = lax.dot_general(qs, kc, nt, preferred_element_type=F32)
            s = jnp.where(valid, s, -jnp.inf)
            h0 = c * gq + p
            sink = jnp.where(top, sinks[:, h0:h0 + 1], sinks[:, h0 + 2:h0 + 3])
            m = jnp.maximum(jnp.max(s, axis=-1, keepdims=True), sink)
            es.append(jnp.exp(s - m).astype(BF16))
            sink_terms.append(jnp.exp(sink - m))
        r0 = jnp.concatenate([vvar[c, 0, key0:key0 + nk, :], ones_lo], axis=1)
        r1 = jnp.concatenate([vvar[c, 1, key0:key0 + nk, :], ones_hi], axis=1)
        od = (jnp.dot(es[0], r0, preferred_element_type=F32)
              + jnp.dot(es[1], r1, preferred_element_type=F32))
        den = od[:, LANES:] + jnp.where(lane_o < HALF, sink_terms[0], sink_terms[1])
        o = (od[:, :LANES] / den).astype(out_ref.dtype)
        out_ref[out_rows, out_col0 + c0 * LANES:out_col0 + (c0 + 1) * LANES] = o[0:qb]
        out_ref[out_rows, out_col0 + (c0 + 1) * LANES:out_col0 + (c0 + 2) * LANES] = o[qb:rows]


def _mixer_kernel(*refs, tile, qblock, is_prompt, pos0, d_lru, d_pool, d_attn, d_kv):
    (proj_ref, cos_ref, sin_ref, convw_ref, convb_ref, wg_ref, ba_ref, bx_ref, lam_ref,
     poolw_ref, pscale_ref, sink_ref) = refs[:12]
    if is_prompt:
        (mix_ref, ko_ref, vo_ref, ho_ref, co_ref, po_ref,
         lru_ext, pool_ext, kvar, vvar, hcar) = refs[12:]
    else:
        (ck_ref, cv_ref, sh_ref, sc_ref, sp_ref,
         mix_ref, ko_ref, vo_ref, ho_ref, co_ref, po_ref,
         lru_ext, pool_ext, kvar, vvar) = refs[12:]

    s = pl.program_id(1) if is_prompt else 0
    o_glru = d_lru
    o_pool = 2 * d_lru
    o_q = o_pool + d_pool
    o_k = o_q + d_attn
    o_v = o_k + d_kv
    n_kc = d_kv // LANES

    if is_prompt:
        @pl.when(s == 0)
        def _():
            lru_ext[0:HALO, :] = jnp.zeros((HALO, d_lru), F32)
            pool_ext[0:HALO, :] = jnp.zeros((HALO, d_pool), F32)
            hcar[...] = jnp.zeros_like(hcar)
            kvar[:, :, 0:WINDOW, :] = jnp.zeros((N_KV_HEADS, 2, WINDOW, LANES), BF16)
            vvar[:, :, 0:WINDOW, :] = jnp.zeros((N_KV_HEADS, 2, WINDOW, LANES), BF16)
        h_prev = hcar[0:1, :]
    else:
        lru_ext[0:HALO, :] = jnp.zeros((HALO, d_lru), F32)
        pool_ext[0:HALO, :] = jnp.zeros((HALO, d_pool), F32)
        lru_ext[HALO - (CONV_WIDTH - 1):HALO, :] = sc_ref[0]
        pool_ext[HALO - POOL_PAD:HALO, :] = sp_ref[0]
        h_prev = sh_ref[0]
        kvar[:, :, WINDOW:2 * WINDOW, :] = jnp.zeros((N_KV_HEADS, 2, WINDOW, LANES), BF16)
        vvar[:, :, WINDOW:2 * WINDOW, :] = jnp.zeros((N_KV_HEADS, 2, WINDOW, LANES), BF16)
        _store_head_variants(kvar, 0, [ck_ref[0, :, c * LANES:(c + 1) * LANES] for c in range(n_kc)])
        _store_head_variants(vvar, 0, [cv_ref[0, :, c * LANES:(c + 1) * LANES] for c in range(n_kc)])

    lru_ext[HALO:HALO + tile, :] = proj_ref[:, 0:d_lru]
    pool_ext[HALO:HALO + tile, :] = proj_ref[:, o_pool:o_pool + d_pool]

    xc = convb_ref[...] + convw_ref[0:1, :] * lru_ext[HALO - 3:HALO - 3 + tile, :]
    for j in range(1, CONV_WIDTH):
        xc = xc + convw_ref[j:j + 1, :] * lru_ext[HALO - 3 + j:HALO - 3 + j + tile, :]
    neg = -lam_ref[...]
    sp_all = jnp.maximum(neg, 0.0) + jnp.log1p(jnp.exp(-jnp.abs(neg)))
    h_last = []
    for c in range(d_lru // LANES):
        cs = slice(c * LANES, (c + 1) * LANES)
        y, hs = _lru_chunk(xc[:, cs], proj_ref[:, o_glru + c * LANES:o_glru + (c + 1) * LANES],
                           h_prev[:, cs], wg_ref[c], ba_ref[:, cs], bx_ref[:, cs], sp_all[:, cs])
        mix_ref[:, cs] = y.astype(mix_ref.dtype)
        h_last.append(hs[tile - 1:tile, :])
    h_last = jnp.concatenate(h_last, axis=-1)

    row = lax.broadcasted_iota(I32, (tile, LANES), 0)
    pos = pos0 + s * tile + row
    for gi, w in enumerate(POOL_WINDOWS):
        cs = slice(gi * LANES, (gi + 1) * LANES)
        e = pool_ext[:, cs]
        acc = e
        step = 1
        while step < w:
            acc = acc + pltpu.roll(acc, step, axis=0)
            step *= 2
        cnt = jnp.minimum(pos + 1, w).astype(F32)
        dlt = acc[HALO:HALO + tile, :] / cnt - e[HALO:HALO + tile, :]
        y = jnp.dot(dlt.astype(BF16), poolw_ref[gi], preferred_element_type=F32) * pscale_ref[:, cs]
        mix_ref[:, d_lru + gi * LANES:d_lru + (gi + 1) * LANES] = y.astype(mix_ref.dtype)

    sinks = sink_ref[...]
    o_attn = d_lru + d_pool
    cos = cos_ref[...]
    sin = sin_ref[...]
    q_chunks = [(qc * ATTN_SCALE).astype(BF16) for qc in _rope(proj_ref[:, o_q:o_q + d_attn], cos, sin)]
    k_rot = _rope(proj_ref[:, o_k:o_k + d_kv], cos, sin)
    _store_head_variants(kvar, WINDOW, k_rot)
    _store_head_variants(vvar, WINDOW, [proj_ref[:, o_v + c * LANES:o_v + (c + 1) * LANES]
                                        for c in range(n_kc)])
    for blk in range(tile // qblock):
        rs = slice(blk * qblock, (blk + 1) * qblock)
        if is_prompt and blk == 0:
            lim = jnp.where(s == 0, WINDOW, 0)
        else:
            lim = 0
        _attend_block([qc[rs] for qc in q_chunks], kvar, vvar, blk * qblock, sinks, lim,
                      mix_ref, rs, o_attn)

    k_last = jnp.concatenate([kc[tile - qblock:tile] for kc in k_rot], axis=-1)
    if is_prompt:
        lru_ext[0:HALO, :] = lru_ext[tile:tile + HALO, :]
        pool_ext[0:HALO, :] = pool_ext[tile:tile + HALO, :]
        hcar[0:1, :] = h_last
        kvar[:, :, 0:WINDOW, :] = kvar[:, :, tile:tile + WINDOW, :]
        vvar[:, :, 0:WINDOW, :] = vvar[:, :, tile:tile + WINDOW, :]

        @pl.when(s == pl.num_programs(1) - 1)
        def _():
            ko_ref[0] = k_last
            vo_ref[0] = proj_ref[tile - qblock:tile, o_v:o_v + d_kv]
            ho_ref[0] = h_last
            co_ref[0] = lru_ext[HALO + tile - (CONV_WIDTH - 1):HALO + tile, :]
            po_ref[0] = pool_ext[HALO + tile - POOL_PAD:HALO + tile, :]
    else:
        ko_ref[0, 0:WINDOW - tile, :] = ck_ref[0, tile:WINDOW, :]
        ko_ref[0, WINDOW - tile:WINDOW, :] = k_last
        vo_ref[0, 0:WINDOW - tile, :] = cv_ref[0, tile:WINDOW, :]
        vo_ref[0, WINDOW - tile:WINDOW, :] = proj_ref[:, o_v:o_v + d_kv]
        ho_ref[0] = h_last
        co_ref[0] = lru_ext[HALO + tile - (CONV_WIDTH - 1):HALO + tile, :]
        po_ref[0] = pool_ext[HALO + tile - POOL_PAD:HALO + tile, :]


def _layer_consts(p, dims):
    d_lru, d_pool = dims["d_lru"], dims["d_pool"]
    hd = d_lru // LRU_HEADS
    per = LANES // hd
    nchunk = d_lru // LANES

    def blockdiag(w):
        w = w.reshape(nchunk, per, hd, hd)
        eye = jnp.eye(per, dtype=w.dtype)
        return jnp.einsum("cpij,pq->cpiqj", w, eye).reshape(nchunk, LANES, LANES)

    wg = jnp.concatenate([blockdiag(p["lru_wa"]), blockdiag(p["lru_wx"])], axis=-1).astype(BF16)
    return dict(
        convw=p["conv_w"], convb=p["conv_b"].reshape(1, d_lru), wg=wg,
        ba=p["lru_ba"].reshape(1, d_lru), bx=p["lru_bx"].reshape(1, d_lru),
        lam=p["lru_lambda"].reshape(1, d_lru), poolw=p["pool_w"].astype(BF16),
        pscale=p["pool_scale"].reshape(1, d_pool), sinks=p["attn_sinks"].reshape(1, -1))


def _rope_tables(pos):
    half = HEAD_DIM // 2
    inv = ROPE_THETA ** (-jnp.arange(half, dtype=F32) / half)
    ang = pos.astype(F32)[:, None] * inv[None, :]
    cos = jnp.cos(ang)
    sin = jnp.sin(ang)
    cos2 = jnp.concatenate([cos, cos], axis=-1)
    sin2 = jnp.concatenate([-sin, sin], axis=-1)
    reps = LANES // HEAD_DIM
    return jnp.tile(cos2, (1, reps)), jnp.tile(sin2, (1, reps))


CONST_NAMES = ("convw", "convb", "wg", "ba", "bx", "lam", "poolw", "pscale", "sinks")


def _const_specs(consts):
    return [pl.BlockSpec(consts[n].shape, functools.partial(lambda nd, *_: (0,) * nd, consts[n].ndim))
            for n in CONST_NAMES]


def _mixer_prompt(proj, consts, cos, sin, bsz, seq, dims, tile):
    d_lru, d_pool, d_attn, d_kv = dims["d_lru"], dims["d_pool"], dims["d_attn"], dims["d_kv"]
    d_in = proj.shape[1]
    d_mix = d_lru + d_pool + d_attn
    ns = seq // tile
    kern = functools.partial(_mixer_kernel, tile=tile, qblock=WINDOW, is_prompt=True, pos0=0,
                             d_lru=d_lru, d_pool=d_pool, d_attn=d_attn, d_kv=d_kv)
    return pl.pallas_call(
        kern,
        grid=(bsz, ns),
        in_specs=[
            pl.BlockSpec((tile, d_in), lambda b, s: (b * ns + s, 0)),
            pl.BlockSpec((tile, LANES), lambda b, s: (s, 0)),
            pl.BlockSpec((tile, LANES), lambda b, s: (s, 0)),
        ] + _const_specs(consts),
        out_specs=[
            pl.BlockSpec((tile, d_mix), lambda b, s: (b * ns + s, 0)),
            pl.BlockSpec((1, WINDOW, d_kv), lambda b, s: (b, 0, 0)),
            pl.BlockSpec((1, WINDOW, d_kv), lambda b, s: (b, 0, 0)),
            pl.BlockSpec((1, 1, d_lru), lambda b, s: (b, 0, 0)),
            pl.BlockSpec((1, CONV_WIDTH - 1, d_lru), lambda b, s: (b, 0, 0)),
            pl.BlockSpec((1, POOL_PAD, d_pool), lambda b, s: (b, 0, 0)),
        ],
        out_shape=[
            jax.ShapeDtypeStruct((bsz * seq, d_mix), BF16),
            jax.ShapeDtypeStruct((bsz, WINDOW, d_kv), F32),
            jax.ShapeDtypeStruct((bsz, WINDOW, d_kv), F32),
            jax.ShapeDtypeStruct((bsz, 1, d_lru), F32),
            jax.ShapeDtypeStruct((bsz, CONV_WIDTH - 1, d_lru), F32),
            jax.ShapeDtypeStruct((bsz, POOL_PAD, d_pool), F32),
        ],
        scratch_shapes=[
            pltpu.VMEM((tile + HALO, d_lru), F32),
            pltpu.VMEM((tile + HALO, d_pool), F32),
            pltpu.VMEM((N_KV_HEADS, 2, WINDOW + tile, LANES), BF16),
            pltpu.VMEM((N_KV_HEADS, 2, WINDOW + tile, LANES), BF16),
            pltpu.VMEM((8, d_lru), F32),
        ],
        compiler_params=pltpu.CompilerParams(
            dimension_semantics=("arbitrary", "arbitrary"), vmem_limit_bytes=VMEM_LIMIT),
        name="mixer_prompt",
    )(proj, cos, sin, *[consts[n] for n in CONST_NAMES])


def _mixer_sample(proj, row0, consts, cos, sin, cache_k, cache_v, st_h, st_conv, st_pool, layer, dims):
    d_lru, d_pool, d_attn, d_kv = dims["d_lru"], dims["d_pool"], dims["d_attn"], dims["d_kv"]
    d_in = proj.shape[1]
    d_mix = d_lru + d_pool + d_attn
    db, win = cache_k.shape[1], cache_k.shape[2]
    t = cos.shape[0]
    blk0 = row0 // t
    kern = functools.partial(_mixer_kernel, tile=t, qblock=t, is_prompt=False, pos0=PAST_LEN,
                             d_lru=d_lru, d_pool=d_pool, d_attn=d_attn, d_kv=d_kv)
    return pl.pallas_call(
        kern,
        grid=(db,),
        in_specs=[
            pl.BlockSpec((t, d_in), lambda b: (blk0 + b, 0)),
            pl.BlockSpec((t, LANES), lambda b: (0, 0)),
            pl.BlockSpec((t, LANES), lambda b: (0, 0)),
        ] + _const_specs(consts) + [
            pl.BlockSpec((None, 1, win, d_kv), lambda b: (layer, b, 0, 0)),
            pl.BlockSpec((None, 1, win, d_kv), lambda b: (layer, b, 0, 0)),
            pl.BlockSpec((None, 1, 1, d_lru), lambda b: (layer, b, 0, 0)),
            pl.BlockSpec((None, 1, CONV_WIDTH - 1, d_lru), lambda b: (layer, b, 0, 0)),
            pl.BlockSpec((None, 1, POOL_PAD, d_pool), lambda b: (layer, b, 0, 0)),
        ],
        out_specs=[
            pl.BlockSpec((t, d_mix), lambda b: (b, 0)),
            pl.BlockSpec((1, win, d_kv), lambda b: (b, 0, 0)),
            pl.BlockSpec((1, win, d_kv), lambda b: (b, 0, 0)),
            pl.BlockSpec((1, 1, d_lru), lambda b: (b, 0, 0)),
            pl.BlockSpec((1, CONV_WIDTH - 1, d_lru), lambda b: (b, 0, 0)),
            pl.BlockSpec((1, POOL_PAD, d_pool), lambda b: (b, 0, 0)),
        ],
        out_shape=[
            jax.ShapeDtypeStruct((db * t, d_mix), F32),
            jax.ShapeDtypeStruct((db, win, d_kv), F32),
            jax.ShapeDtypeStruct((db, win, d_kv), F32),
            jax.ShapeDtypeStruct((db, 1, d_lru), F32),
            jax.ShapeDtypeStruct((db, CONV_WIDTH - 1, d_lru), F32),
            jax.ShapeDtypeStruct((db, POOL_PAD, d_pool), F32),
        ],
        scratch_shapes=[
            pltpu.VMEM((t + HALO, d_lru), F32),
            pltpu.VMEM((t + HALO, d_pool), F32),
            pltpu.VMEM((N_KV_HEADS, 2, 2 * WINDOW, LANES), BF16),
            pltpu.VMEM((N_KV_HEADS, 2, 2 * WINDOW, LANES), BF16),
        ],
        compiler_params=pltpu.CompilerParams(
            dimension_semantics=("arbitrary",), vmem_limit_bytes=VMEM_LIMIT),
        name="mixer_sample",
    )(proj, cos, sin, *[consts[n] for n in CONST_NAMES], cache_k, cache_v, st_h, st_conv, st_pool)


def _route(logits, run_cnt):
    t = logits.shape[0]
    lane = lax.broadcasted_iota(I32, (t, LANES), 1)
    lane_f = lane.astype(F32)
    ninf = -jnp.inf
    big = float(LANES)
    is_g = lane < N_EXPERT_GROUPS
    lg = jnp.where(is_g, logits, ninf)
    mg = jnp.max(lg, axis=-1, keepdims=True)
    g_top = jnp.min(jnp.where(lg == mg, lane_f, big), axis=-1, keepdims=True).astype(I32)
    pg_top = 1.0 / jnp.sum(jnp.exp(lg - mg), axis=-1, keepdims=True)
    base = N_EXPERT_GROUPS + EXPERTS_PER_GROUP * g_top
    in_grp = (lane >= base) & (lane < base + EXPERTS_PER_GROUP)
    le = jnp.where(in_grp, logits, ninf)
    m1 = jnp.max(le, axis=-1, keepdims=True)
    i1 = jnp.min(jnp.where(le == m1, lane_f, big), axis=-1, keepdims=True).astype(I32)
    le2 = jnp.where(lane == i1, ninf, le)
    m2 = jnp.max(le2, axis=-1, keepdims=True)
    i2 = jnp.min(jnp.where((le2 == m2) & in_grp & (lane != i1), lane_f, big),
                 axis=-1, keepdims=True).astype(I32)
    se = jnp.sum(jnp.exp(le - m1), axis=-1, keepdims=True)
    p1 = 1.0 / se
    p2 = jnp.exp(m2 - m1) / se
    tot = p1 + p2
    w1 = (p1 / tot) * pg_top
    w2 = (p2 / tot) * pg_top
    a = i1 - base
    b = i2 - base
    lo = jnp.minimum(a, b)
    hi = jnp.maximum(a, b)
    w_lo = jnp.where(a < b, w1, w2)
    w_hi = jnp.where(a < b, w2, w1)
    pid = jnp.where(lo == 0, hi - 1, jnp.where(lo == 1, jnp.where(hi == 3, 3, 4), 5))
    swap = pid == 5
    w_a = jnp.where(swap, w_hi, w_lo)
    w_b = jnp.where(swap, w_lo, w_hi)
    cls = g_top * N_PAIRS + pid
    onehot = lane == cls
    ti = lax.broadcasted_iota(I32, (t, t), 0)
    tj = lax.broadcasted_iota(I32, (t, t), 1)
    lower = jnp.where(tj <= ti, 1.0, 0.0).astype(BF16)
    prefix = jnp.dot(lower, jnp.where(onehot, 1.0, 0.0).astype(BF16), preferred_element_type=F32)
    rank = jnp.sum(jnp.where(onehot, prefix - 1.0 + run_cnt, 0.0), axis=-1, keepdims=True)
    info = jnp.where(lane == 0, w_a, jnp.where(lane == 1, w_b, jnp.where(
        lane == 2, cls.astype(F32), jnp.where(lane == 3, rank, 0.0))))
    return info, run_cnt + prefix[t - 1:t, :]


def _out_proj_kernel(mixp_ref, mixs_ref, xm_ref, xt_ref, w_hbm, g_ref, wr_ref, br_ref,
                     o_ref, info_ref, cnt_ref, w_bf, stage, sem, run_cnt, *, n_main_blocks, d, layer, slab):
    i = pl.program_id(0)

    @pl.when(i == 0)
    def _():
        _load_weight_bf16(w_hbm.at[layer], w_bf, stage, sem)
        run_cnt[...] = jnp.zeros_like(run_cnt)

    mix = jnp.where(i < n_main_blocks, mixp_ref[...], mixs_ref[...].astype(BF16))
    xres = (_pick_rows(i, n_main_blocks, xm_ref, xt_ref, d, slab)
            + jnp.dot(mix, w_bf[...], preferred_element_type=F32))
    h2 = _rms(xres, g_ref[...])
    logits = jnp.dot(h2.astype(BF16), wr_ref[...], preferred_element_type=F32) + br_ref[...]
    info, cnt = _route(logits, run_cnt[0:1, :])
    run_cnt[0:1, :] = cnt
    pitch = _slab_pitch(d)
    _write_slabs(o_ref, (), xres, pitch)
    o_ref[pl.ds(d // LANES, ROW_BLOCK, stride=pitch), :] = info
    info_ref[...] = info
    cnt_ref[...] = jnp.broadcast_to(cnt, cnt_ref.shape)


def _out_proj(mix_p, mix_s, x_main, x_tail, tail_block0, slab, w, layer, g, wr, br, n_rows):
    d = g.shape[1]
    d_mix = w.shape[1]
    npb = mix_p.shape[0] // ROW_BLOCK
    pitch = _slab_pitch(d)
    kern = functools.partial(_out_proj_kernel, n_main_blocks=npb, d=d, layer=layer, slab=slab)
    return pl.pallas_call(
        kern,
        grid=(n_rows // ROW_BLOCK,),
        in_specs=[
            pl.BlockSpec((ROW_BLOCK, d_mix), lambda i: (jnp.minimum(i, npb - 1), 0)),
            pl.BlockSpec((ROW_BLOCK, d_mix), lambda i: (jnp.maximum(i - npb, 0), 0)),
        ] + _row_specs(d, npb, tail_block0, slab) + [
            pl.BlockSpec(memory_space=pl.ANY),
            pl.BlockSpec((1, d), lambda i: (0, 0)),
            pl.BlockSpec((d, LANES), lambda i: (0, 0)),
            pl.BlockSpec((1, LANES), lambda i: (0, 0)),
        ],
        out_specs=[pl.BlockSpec((ROW_BLOCK * pitch, LANES), lambda i: (i, 0)),
                   pl.BlockSpec((ROW_BLOCK, LANES), lambda i: (i, 0)),
                   pl.BlockSpec((8, LANES), lambda i: (0, 0))],
        out_shape=[jax.ShapeDtypeStruct((n_rows * pitch, LANES), F32),
                   jax.ShapeDtypeStruct((n_rows, LANES), F32),
                   jax.ShapeDtypeStruct((8, LANES), F32)],
        scratch_shapes=[
            pltpu.VMEM((d_mix, d), BF16),
            pltpu.VMEM((2, ROW_BLOCK, d), F32),
            pltpu.SemaphoreType.DMA((2,)),
            pltpu.VMEM((8, LANES), F32),
        ],
        compiler_params=pltpu.CompilerParams(
            dimension_semantics=("arbitrary",), vmem_limit_bytes=VMEM_LIMIT),
        name="out_proj",
    )(mix_p, mix_s, x_main, x_tail, w, g, wr, br)


def _layer_prompt_kernel(*refs, tile, slab_in, layer, d, d_lru, d_pool, d_attn, d_kv):
    n_const = len(CONST_NAMES)
    x_ref, xs_ref, mixs_ref, g1_ref, win_hbm, cos_ref, sin_ref = refs[:7]
    consts = refs[7:7 + n_const]
    wout_hbm, g2_ref, wr_ref, br_ref = refs[7 + n_const:11 + n_const]
    xr_ref, info_ref, cnt_ref, ko_ref, vo_ref, ho_ref, co_ref, po_ref = refs[11 + n_const:19 + n_const]
    (win_bf, wout_bf, stage_in, stage_out, wsem, proj_s, mix_s,
     lru_ext, pool_ext, kvar, vvar, hcar, run_cnt) = refs[19 + n_const:]
    pitch = _slab_pitch(d)
    b = pl.program_id(0)
    n_seq = pl.num_programs(0) - 1

    @pl.when((b == 0) & (pl.program_id(1) == 0))
    def _():
        _load_weight_bf16(win_hbm.at[layer], win_bf, stage_in, wsem)
        _load_weight_bf16(wout_hbm.at[layer], wout_bf, stage_out, wsem)
        run_cnt[...] = jnp.zeros_like(run_cnt)

    def project_out(x, mix):
        xres = x + jnp.dot(mix, wout_bf[...], preferred_element_type=F32)
        h2 = _rms(xres, g2_ref[...])
        logits = jnp.dot(h2.astype(BF16), wr_ref[...], preferred_element_type=F32) + br_ref[...]
        info, cnt = _route(logits, run_cnt[0:1, :])
        run_cnt[0:1, :] = cnt
        _write_slabs(xr_ref, (), xres, pitch)
        xr_ref[pl.ds(d // LANES, tile, stride=pitch), :] = info
        info_ref[...] = info
        cnt_ref[...] = jnp.broadcast_to(cnt, cnt_ref.shape)

    def rows(ref):
        return _read_slabs(ref, (), tile, d // LANES, pitch) if slab_in else ref[...]

    @pl.when(b < n_seq)
    def _():
        x = rows(x_ref)
        proj_s[...] = jnp.dot(_rms(x, g1_ref[...]).astype(BF16), win_bf[...], preferred_element_type=F32)
        _mixer_kernel(proj_s, cos_ref, sin_ref, *consts, mix_s, ko_ref, vo_ref, ho_ref, co_ref, po_ref,
                      lru_ext, pool_ext, kvar, vvar, hcar, tile=tile, qblock=WINDOW, is_prompt=True,
                      pos0=0, d_lru=d_lru, d_pool=d_pool, d_attn=d_attn, d_kv=d_kv)
        project_out(x, mix_s[...])

    @pl.when((b == n_seq) & (pl.program_id(1) == 0))
    def _():
        project_out(rows(xs_ref), mixs_ref[...].astype(BF16))


def _layer_prompt(x, x_s, xs_block, slab_in, mix_s, g1, w_in, cos, sin, consts, w_out, g2, wr, br, layer,
                  bsz, seq, dims, tile):
    d_lru, d_pool, d_attn, d_kv = dims["d_lru"], dims["d_pool"], dims["d_attn"], dims["d_kv"]
    d = g1.shape[1]
    d_in = w_in.shape[2]
    d_mix = w_out.shape[1]
    ns = seq // tile
    n_rows = bsz * seq + tile
    assert mix_s.shape[0] == tile
    pitch = _slab_pitch(d)
    kern = functools.partial(_layer_prompt_kernel, tile=tile, slab_in=slab_in, layer=layer, d=d,
                             d_lru=d_lru, d_pool=d_pool, d_attn=d_attn, d_kv=d_kv)
    last = bsz * ns - 1
    row_shape = (tile * pitch, LANES) if slab_in else (tile, d)
    const2 = lambda b, s: (0, 0)
    state = lambda b, s: (jnp.minimum(b, bsz - 1), 0, 0)
    return pl.pallas_call(
        kern,
        grid=(bsz + 1, ns),
        in_specs=[
            pl.BlockSpec(row_shape, lambda b, s: (jnp.minimum(b * ns + s, last), 0)),
            pl.BlockSpec(row_shape, lambda b, s: (xs_block, 0)),
            pl.BlockSpec((tile, d_mix), const2),
            pl.BlockSpec((1, d), const2),
            pl.BlockSpec(memory_space=pl.ANY),
            pl.BlockSpec((tile, LANES), lambda b, s: (s, 0)),
            pl.BlockSpec((tile, LANES), lambda b, s: (s, 0)),
        ] + _const_specs(consts) + [
            pl.BlockSpec(memory_space=pl.ANY),
            pl.BlockSpec((1, d), const2),
            pl.BlockSpec((d, LANES), const2),
            pl.BlockSpec((1, LANES), const2),
        ],
        out_specs=[
            pl.BlockSpec((tile * pitch, LANES), lambda b, s: (jnp.minimum(b * ns + s, last + 1), 0)),
            pl.BlockSpec((tile, LANES), lambda b, s: (jnp.minimum(b * ns + s, last + 1), 0)),
            pl.BlockSpec((8, LANES), const2),
            pl.BlockSpec((1, WINDOW, d_kv), state),
            pl.BlockSpec((1, WINDOW, d_kv), state),
            pl.BlockSpec((1, 1, d_lru), state),
            pl.BlockSpec((1, CONV_WIDTH - 1, d_lru), state),
            pl.BlockSpec((1, POOL_PAD, d_pool), state),
        ],
        out_shape=[
            jax.ShapeDtypeStruct((n_rows * pitch, LANES), F32),
            jax.ShapeDtypeStruct((n_rows, LANES), F32),
            jax.ShapeDtypeStruct((8, LANES), F32),
            jax.ShapeDtypeStruct((bsz, WINDOW, d_kv), F32),
            jax.ShapeDtypeStruct((bsz, WINDOW, d_kv), F32),
            jax.ShapeDtypeStruct((bsz, 1, d_lru), F32),
            jax.ShapeDtypeStruct((bsz, CONV_WIDTH - 1, d_lru), F32),
            jax.ShapeDtypeStruct((bsz, POOL_PAD, d_pool), F32),
        ],
        scratch_shapes=[
            pltpu.VMEM((d, d_in), BF16),
            pltpu.VMEM((d_mix, d), BF16),
            pltpu.VMEM((2, W_STAGE_ROWS, d_in), F32),
            pltpu.VMEM((2, W_STAGE_ROWS, d), F32),
            pltpu.SemaphoreType.DMA((2,)),
            pltpu.VMEM((tile, d_in), F32),
            pltpu.VMEM((tile, d_mix), BF16),
            pltpu.VMEM((tile + HALO, d_lru), F32),
            pltpu.VMEM((tile + HALO, d_pool), F32),
            pltpu.VMEM((N_KV_HEADS, 2, WINDOW + tile, LANES), BF16),
            pltpu.VMEM((N_KV_HEADS, 2, WINDOW + tile, LANES), BF16),
            pltpu.VMEM((8, d_lru), F32),
            pltpu.VMEM((8, LANES), F32),
        ],
        compiler_params=pltpu.CompilerParams(
            dimension_semantics=("arbitrary", "arbitrary"), vmem_limit_bytes=VMEM_LIMIT),
        name="layer_prompt",
    )(x, x_s, mix_s, g1, w_in, cos, sin, *[consts[n] for n in CONST_NAMES], w_out, g2, wr, br)


def _sample_in_kernel(x_ref, g_ref, w_hbm, o_ref, w_bf, stage, sem, *, layer, slab_in):
    d = g_ref.shape[1]
    _load_weight_bf16(w_hbm.at[layer], w_bf, stage, sem)
    n = o_ref.shape[0]
    x = _read_slabs(x_ref, (), n, d // LANES, _slab_pitch(d)) if slab_in else x_ref[...]
    o_ref[...] = jnp.dot(_rms(x, g_ref[...]).astype(BF16), w_bf[...], preferred_element_type=F32)


def _sample_in(x, x_block, slab_in, n, g, w, layer):
    d = g.shape[1]
    d_in = w.shape[2]
    x_spec = (pl.BlockSpec((n * _slab_pitch(d), LANES), lambda i: (x_block, 0)) if slab_in
              else pl.BlockSpec((n, d), lambda i: (x_block, 0)))
    return pl.pallas_call(
        functools.partial(_sample_in_kernel, layer=layer, slab_in=slab_in),
        grid=(1,),
        in_specs=[x_spec, pl.BlockSpec((1, d), lambda i: (0, 0)), pl.BlockSpec(memory_space=pl.ANY)],
        out_specs=pl.BlockSpec((n, d_in), lambda i: (0, 0)),
        out_shape=jax.ShapeDtypeStruct((n, d_in), F32),
        scratch_shapes=[
            pltpu.VMEM((d, d_in), BF16),
            pltpu.VMEM((2, W_STAGE_ROWS, d_in), F32),
            pltpu.SemaphoreType.DMA((2,)),
        ],
        compiler_params=pltpu.CompilerParams(
            dimension_semantics=("arbitrary",), vmem_limit_bytes=VMEM_LIMIT),
        name="sample_in",
    )(x, g, w)


def _moe_kernel(cls_ref, rank_ref, off_ref, cnt_ref, ea_ref, eb_ref, chga_ref, chgb_ref, nch_ref,
                xr_hbm, g_ref, w1a_ref, w3a_ref, w2a_ref, w1b_ref, w3b_ref, w2b_ref,
                xo_hbm,
                perm, xbuf, obuf, wa1, wa3, wa2, wb1, wb3, wb2, gsem, ssem, *, d, n_rows):
    del ea_ref, eb_ref
    m = MOE_CHUNK
    pitch = _slab_pitch(d)
    n_slabs = d // LANES
    c = pl.program_id(0)
    n = nch_ref[0]
    slot = c % 2
    other = 1 - slot

    def start_gather(chunk, sl, j):
        tok = jnp.maximum(perm[(chunk + 1) * m + j], 0)
        pltpu.make_async_copy(xr_hbm.at[pl.ds(tok * pitch, pitch)],
                              xbuf.at[sl, pl.ds(j * pitch, pitch)], gsem.at[sl]).start()

    def start_scatter(chunk, sl, j):
        tok = perm[(chunk + 1) * m + j]
        r = jnp.where(tok < 0, n_rows + sl * m + j, tok)
        pltpu.make_async_copy(obuf.at[sl, pl.ds(j * pitch, pitch)],
                              xo_hbm.at[pl.ds(r * pitch, pitch)], ssem.at[sl]).start()

    def wait_gather(sl):
        pltpu.make_async_copy(xr_hbm.at[pl.ds(0, m * pitch)], xbuf.at[sl], gsem.at[sl]).wait()

    def wait_scatter(sl):
        pltpu.make_async_copy(obuf.at[sl], xo_hbm.at[pl.ds(0, m * pitch)], ssem.at[sl]).wait()

    @pl.when(c == 0)
    def _():
        obuf[...] = jnp.zeros_like(obuf)
        pad = pltpu.make_async_copy(obuf.at[0], xo_hbm.at[pl.ds(n_rows * pitch, m * pitch)], ssem.at[0])
        pad.start()
        pad.wait()

        def fill(lo, hi):
            def body(i, z):
                perm[i] = -1
                return z
            lax.fori_loop(lo, hi, body, 0)

        def first_gather(j, z):
            start_gather(0, 0, j)
            return z

        fill(0, m)
        fill((n + 1) * m, (n + 3) * m)
        for k in range(N_CLASSES):
            base = (off_ref[k] + 1) * m
            cnt = cnt_ref[k]
            fill(base + cnt, base + ((cnt + m - 1) // m) * m)

        def place(t, z):
            perm[(off_ref[cls_ref[t]] + 1) * m + rank_ref[t]] = t
            return z
        lax.fori_loop(0, n_rows, place, 0, unroll=8)
        lax.fori_loop(0, m, first_gather, 0)

    @pl.when(c <= n)
    def _():
        wait_gather(slot)

        @pl.when(c >= 1)
        def _():
            wait_scatter(slot)

        @pl.when(chga_ref[c] == 1)
        def _():
            wa1[...] = w1a_ref[0].astype(BF16)
            wa3[...] = w3a_ref[0].astype(BF16)
            wa2[...] = w2a_ref[0].astype(BF16)

        @pl.when(chgb_ref[c] == 1)
        def _():
            wb1[...] = w1b_ref[0].astype(BF16)
            wb3[...] = w3b_ref[0].astype(BF16)
            wb2[...] = w2b_ref[0].astype(BF16)

        for j in range(m):
            start_gather(c + 1, other, j)
        for j in range(m):
            start_scatter(c - 1, other, j)
        x = _read_slabs(xbuf, (slot,), m, n_slabs, pitch)
        info = xbuf[slot, pl.ds(n_slabs, m, stride=pitch), :]
        w_a = info[:, 0:1]
        w_b = info[:, 1:2]
        h = _rms(x, g_ref[...]).astype(BF16)
        hid_a = (jax.nn.silu(jnp.dot(h, wa1[...], preferred_element_type=F32))
                 * jnp.dot(h, wa3[...], preferred_element_type=F32) * w_a).astype(BF16)
        hid_b = (jax.nn.silu(jnp.dot(h, wb1[...], preferred_element_type=F32))
                 * jnp.dot(h, wb3[...], preferred_element_type=F32) * w_b).astype(BF16)
        y = (jnp.dot(hid_a, wa2[...], preferred_element_type=F32)
             + jnp.dot(hid_b, wb2[...], preferred_element_type=F32))
        _write_slabs(obuf, (slot,), x + y, pitch)

        @pl.when(c == n)
        def _():
            wait_gather(other)
            wait_scatter(other)


def _moe(xr, g, w1, w3, w2, tables, n_rows, c_max):
    d = g.shape[1]
    f = w1.shape[2]
    m = MOE_CHUNK
    pitch = _slab_pitch(d)
    kern = functools.partial(_moe_kernel, d=d, n_rows=n_rows)

    def wspec(shape, which):
        return pl.BlockSpec(shape, lambda c, *pref: (pref[which][c], 0, 0))

    grid_spec = pltpu.PrefetchScalarGridSpec(
        num_scalar_prefetch=9,
        grid=(c_max + 1,),
        in_specs=[
            pl.BlockSpec(memory_space=pl.ANY),
            pl.BlockSpec((1, d), lambda c, *pref: (0, 0)),
            wspec((1, d, f), 4), wspec((1, d, f), 4), wspec((1, f, d), 4),
            wspec((1, d, f), 5), wspec((1, d, f), 5), wspec((1, f, d), 5),
        ],
        out_specs=pl.BlockSpec(memory_space=pl.ANY),
        scratch_shapes=[
            pltpu.SMEM(((c_max + 3) * m,), I32),
            pltpu.VMEM((2, m * pitch, LANES), F32),
            pltpu.VMEM((2, m * pitch, LANES), F32),
            pltpu.VMEM((d, f), BF16), pltpu.VMEM((d, f), BF16), pltpu.VMEM((f, d), BF16),
            pltpu.VMEM((d, f), BF16), pltpu.VMEM((d, f), BF16), pltpu.VMEM((f, d), BF16),
            pltpu.SemaphoreType.DMA((2,)),
            pltpu.SemaphoreType.DMA((2,)),
        ],
    )
    return pl.pallas_call(
        kern,
        grid_spec=grid_spec,
        out_shape=jax.ShapeDtypeStruct(((n_rows + 2 * m) * pitch, LANES), F32),
        compiler_params=pltpu.CompilerParams(
            dimension_semantics=("arbitrary",), vmem_limit_bytes=VMEM_LIMIT),
        name="moe",
    )(*tables, xr, g, w1, w3, w2, w1, w3, w2)


def _chunk_tables(counts, c_max, expert0):
    m = MOE_CHUNK
    nch_c = (counts + m - 1) // m
    ch_end = jnp.cumsum(nch_c)
    ch_off = ch_end - nch_c
    n_chunks = ch_end[-1]
    chunk = jnp.minimum(jnp.arange(c_max + 1, dtype=I32), jnp.maximum(n_chunks - 1, 0))
    ccls = jnp.sum((chunk[:, None] >= ch_end[None, :]).astype(I32), axis=1)
    ccls = jnp.minimum(ccls, N_CLASSES - 1)
    grp = ccls // N_PAIRS
    pid = ccls % N_PAIRS
    ea = expert0 + grp * EXPERTS_PER_GROUP + jnp.asarray(PAIR_SLOT_A, I32)[pid]
    eb = expert0 + grp * EXPERTS_PER_GROUP + jnp.asarray(PAIR_SLOT_B, I32)[pid]
    first = jnp.ones((1,), I32)
    chga = jnp.concatenate([first, (ea[1:] != ea[:-1]).astype(I32)])
    chgb = jnp.concatenate([first, (eb[1:] != eb[:-1]).astype(I32)])
    return ch_off.astype(I32), counts, ea, eb, chga, chgb, n_chunks.reshape(1).astype(I32)


def _final_norm_kernel(x_ref, g_ref, op_ref, os_ref, *, n_prompt_blocks):
    i = pl.program_id(0)
    d = g_ref.shape[1]
    y = _rms(_read_slabs(x_ref, (), ROW_BLOCK, d // LANES, _slab_pitch(d)), g_ref[...])

    @pl.when(i < n_prompt_blocks)
    def _():
        op_ref[...] = y

    @pl.when(i >= n_prompt_blocks)
    def _():
        os_ref[...] = y


def _final_norm(x, g, n_prompt, n_sample):
    d = g.shape[1]
    npb = n_prompt // ROW_BLOCK
    nsb = n_sample // ROW_BLOCK
    kern = functools.partial(_final_norm_kernel, n_prompt_blocks=npb)
    return pl.pallas_call(
        kern,
        grid=(npb + nsb,),
        in_specs=[pl.BlockSpec((ROW_BLOCK * _slab_pitch(d), LANES), lambda i: (i, 0)),
                  pl.BlockSpec((1, d), lambda i: (0, 0))],
        out_specs=[pl.BlockSpec((ROW_BLOCK, d), lambda i: (jnp.minimum(i, npb - 1), 0)),
                   pl.BlockSpec((ROW_BLOCK, d), lambda i: (jnp.maximum(i - npb, 0), 0))],
        out_shape=[jax.ShapeDtypeStruct((n_prompt, d), F32),
                   jax.ShapeDtypeStruct((n_sample, d), F32)],
        compiler_params=pltpu.CompilerParams(dimension_semantics=("arbitrary",)),
        name="final_norm",
    )(x, g)


def kernel(x_prompt, x_sample, cache_k, cache_v, state_lru_h, state_conv, state_pool, norm1_g, w_in, conv_w, conv_b, lru_wa, lru_ba, lru_wx, lru_bx, lru_lambda, pool_w, pool_scale, attn_sinks, w_out, norm2_g, router_group_w, router_group_b, router_expert_w, router_expert_b, expert_w1, expert_w3, expert_w2, final_norm_g):
    bsz, seq, d = x_prompt.shape
    db, ds, _ = x_sample.shape
    depth = w_in.shape[0]
    win = cache_k.shape[2]
    nkv = cache_k.shape[3]
    d_lru = lru_lambda.shape[1]
    d_pool = pool_scale.shape[1]
    d_kv = nkv * cache_k.shape[4]
    d_attn = attn_sinks.shape[1] * HEAD_DIM
    dims = dict(d_lru=d_lru, d_pool=d_pool, d_attn=d_attn, d_kv=d_kv)
    n_prompt = bsz * seq
    n_sample = db * ds
    n_rows = n_prompt + n_sample
    tile = ROW_BLOCK
    assert seq % tile == 0 and n_sample == tile
    assert win == WINDOW and nkv == N_KV_HEADS and ds <= 8 and WINDOW % ds == 0
    c_max = -(-n_rows // MOE_CHUNK) + N_CLASSES
    npb = n_prompt // ROW_BLOCK

    cos_p, sin_p = _rope_tables(jnp.arange(seq))
    cos_s, sin_s = _rope_tables(PAST_LEN + jnp.arange(ds))

    x_p, x_s, xs_block, slab = x_prompt.reshape(n_prompt, d), x_sample.reshape(n_sample, d), 0, False
    ck = cache_k.reshape(depth, db, win, d_kv)
    cv = cache_v.reshape(depth, db, win, d_kv)
    st_h = state_lru_h.reshape(depth, db, 1, d_lru)
    n_exp, _, d_exp = expert_w1.shape[1:]
    w1 = expert_w1.reshape(depth * n_exp, d, d_exp)
    w3 = expert_w3.reshape(depth * n_exp, d, d_exp)
    w2 = expert_w2.reshape(depth * n_exp, d_exp, d)
    outs = [[] for _ in range(10)]
    for l in range(depth):
        p = dict(conv_w=conv_w[l], conv_b=conv_b[l], lru_wa=lru_wa[l], lru_ba=lru_ba[l],
                 lru_wx=lru_wx[l], lru_bx=lru_bx[l], lru_lambda=lru_lambda[l], pool_w=pool_w[l],
                 pool_scale=pool_scale[l], attn_sinks=attn_sinks[l])
        consts = _layer_consts(p, dims)
        g1 = norm1_g[l].reshape(1, d)
        g2 = norm2_g[l].reshape(1, d)
        proj_s = _sample_in(x_s, xs_block, slab, n_sample, g1, w_in, l)
        mix_s, sk, sv, sh, sc, sp = _mixer_sample(
            proj_s, 0, consts, cos_s, sin_s, ck, cv, st_h, state_conv, state_pool, l, dims)
        wr = jnp.concatenate([router_group_w[l], router_expert_w[l]], axis=1)
        wr = jnp.pad(wr, ((0, 0), (0, LANES - wr.shape[1]))).astype(BF16)
        br = jnp.concatenate([router_group_b[l], router_expert_b[l]])
        br = jnp.pad(br, (0, LANES - br.shape[0])).reshape(1, LANES)
        xr, info, cnt, pk, pv, ph, pc, pp = _layer_prompt(
            x_p, x_s, xs_block, slab, mix_s, g1, w_in, cos_p, sin_p, consts, w_out, g2, wr, br, l,
            bsz, seq, dims, tile)
        route = info[:, 2:4].astype(I32)
        tables = (route[:, 0], route[:, 1]) + _chunk_tables(cnt[0, :N_CLASSES].astype(I32), c_max, l * n_exp)
        x = _moe(xr, g2, w1, w3, w2, tables, n_rows, c_max)
        x_p, x_s, xs_block, slab = x, x, npb, True
        for lst, val in zip(outs, (pk.reshape(bsz, WINDOW, nkv, HEAD_DIM), pv.reshape(bsz, WINDOW, nkv, HEAD_DIM),
                                   ph.reshape(bsz, d_lru), pc, pp,
                                   sk.reshape(db, win, nkv, HEAD_DIM), sv.reshape(db, win, nkv, HEAD_DIM),
                                   sh.reshape(db, d_lru), sc, sp)):
            lst.append(val)
    y_p, y_s = _final_norm(x_p, final_norm_g.reshape(1, d), n_prompt, n_sample)
    return (y_p.reshape(bsz, seq, d), y_s.reshape(db, ds, d)) + tuple(jnp.stack(o) for o in outs)
```

```python
import functools

import jax
import jax.numpy as jnp
from jax import lax
from jax.experimental import pallas as pl
from jax.experimental.pallas import tpu as pltpu

F32 = jnp.float32
BF16 = jnp.bfloat16
I32 = jnp.int32

LRU_HEADS = 8
CONV_WIDTH = 4
LRU_C = 8.0
POOL_WINDOWS = (2, 4, 8, 16)
POOL_PAD = max(POOL_WINDOWS) - 1
HEAD_DIM = 64
N_KV_HEADS = 4
WINDOW = 128
ROPE_THETA = 10000.0
ATTN_SCALE = HEAD_DIM ** -0.5
N_EXPERT_GROUPS = 4
EXPERTS_PER_GROUP = 4
N_EXPERTS = N_EXPERT_GROUPS * EXPERTS_PER_GROUP
RMS_EPS = 1e-6
PAST_LEN = 16384

LANES = 128
HALF = LANES // 2
HALO = 16
ROW_BLOCK = 256
SAMPLE_SEQS_PER_STEP = 4
MOE_CHUNK = 128
PAIR_SLOT_A = (0, 0, 0, 1, 1, 3)
PAIR_SLOT_B = (1, 2, 3, 3, 2, 2)
N_PAIRS = len(PAIR_SLOT_A)
N_CLASSES = N_EXPERT_GROUPS * N_PAIRS
VMEM_LIMIT = 52 * 1024 * 1024


def _rms(x, g):
    return (x * lax.rsqrt(jnp.mean(x * x, axis=-1, keepdims=True) + RMS_EPS)) * g


def _load_weight_bf16(w_hbm, w_bf, stage, sem):
    rows = stage.shape[1]
    n = w_hbm.shape[0] // rows

    def copy(i, sl):
        return pltpu.make_async_copy(w_hbm.at[pl.ds(i * rows, rows)], stage.at[sl], sem.at[sl])

    copy(0, 0).start()
    for i in range(n):
        sl = i % 2
        if i + 1 < n:
            copy(i + 1, 1 - sl).start()
        copy(i, sl).wait()
        w_bf[i * rows:(i + 1) * rows, :] = stage[sl].astype(BF16)


def _slab_pitch(d):
    return d // LANES + 1


def _read_slabs(ref, lead, n_tok, n_slabs, pitch):
    return jnp.concatenate(
        [ref[lead + (pl.ds(k, n_tok, stride=pitch), slice(None))] for k in range(n_slabs)], axis=1)


def _write_slabs(ref, lead, val, pitch):
    n_tok = val.shape[0]
    for k in range(val.shape[1] // LANES):
        ref[lead + (pl.ds(k, n_tok, stride=pitch), slice(None))] = val[:, k * LANES:(k + 1) * LANES]


def _pick_rows(i, n_main_blocks, main_ref, tail_ref, d, slab):
    if slab:
        main = _read_slabs(main_ref, (), ROW_BLOCK, d // LANES, _slab_pitch(d))
        tail = _read_slabs(tail_ref, (), ROW_BLOCK, d // LANES, _slab_pitch(d))
    else:
        main, tail = main_ref[...], tail_ref[...]
    return jnp.where(i < n_main_blocks, main, tail)


def _in_proj_kernel(xm_ref, xt_ref, g_ref, w_hbm, o_ref, w_bf, stage, sem, *, n_main_blocks, layer, slab):
    i = pl.program_id(0)

    @pl.when(i == 0)
    def _():
        _load_weight_bf16(w_hbm.at[layer], w_bf, stage, sem)

    h = _rms(_pick_rows(i, n_main_blocks, xm_ref, xt_ref, g_ref.shape[1], slab), g_ref[...])
    o_ref[...] = jnp.dot(h.astype(BF16), w_bf[...], preferred_element_type=F32)


def _row_specs(d, n_main_blocks, tail_block0, slab):
    shape = (ROW_BLOCK * _slab_pitch(d), LANES) if slab else (ROW_BLOCK, d)
    return [pl.BlockSpec(shape, lambda i: (jnp.minimum(i, n_main_blocks - 1), 0)),
            pl.BlockSpec(shape, lambda i: (jnp.maximum(i - n_main_blocks, 0) + tail_block0, 0))]


def _in_proj(x_main, x_tail, tail_block0, slab, g, w, layer, n_rows, n_main):
    d = g.shape[1]
    d_in = w.shape[2]
    nmb = n_main // ROW_BLOCK
    kern = functools.partial(_in_proj_kernel, n_main_blocks=nmb, layer=layer, slab=slab)
    return pl.pallas_call(
        kern,
        grid=(n_rows // ROW_BLOCK,),
        in_specs=_row_specs(d, nmb, tail_block0, slab) + [
            pl.BlockSpec((1, d), lambda i: (0, 0)),
            pl.BlockSpec(memory_space=pl.ANY),
        ],
        out_specs=pl.BlockSpec((ROW_BLOCK, d_in), lambda i: (i, 0)),
        out_shape=jax.ShapeDtypeStruct((n_rows, d_in), F32),
        scratch_shapes=[
            pltpu.VMEM((d, d_in), BF16),
            pltpu.VMEM((2, ROW_BLOCK, d_in), F32),
            pltpu.SemaphoreType.DMA((2,)),
        ],
        compiler_params=pltpu.CompilerParams(
            dimension_semantics=("arbitrary",), vmem_limit_bytes=VMEM_LIMIT),
        name="in_proj",
    )(x_main, x_tail, g, w)


def _scan_linear(a, b):
    t = a.shape[0]
    row = lax.broadcasted_iota(I32, a.shape, 0)
    d = 1
    while d < min(t, 8):
        a_sh = pltpu.roll(a, d, axis=0)
        b_sh = pltpu.roll(b, d, axis=0)
        m = row >= d
        b = jnp.where(m, a * b_sh + b, b)
        a = jnp.where(m, a * a_sh, a)
        d *= 2
    while d < t:
        b = jnp.concatenate([b[:d], a[d:] * b[:t - d] + b[d:]], axis=0)
        a = jnp.concatenate([a[:d], a[d:] * a[:t - d]], axis=0)
        d *= 2
    return a, b


def _lru_chunk(xc, g, h_prev, wg, ba, bx, sp):
    pre = jnp.dot(xc.astype(BF16), wg, preferred_element_type=F32)
    r = jax.nn.sigmoid(pre[:, :LANES] + ba)
    ig = jax.nn.sigmoid(pre[:, LANES:] + bx)
    log_a = (-LRU_C * r) * sp
    a = jnp.exp(log_a)
    v = 1.0 - a * a
    bterm = jnp.where(v > 0.0, v * lax.rsqrt(v), 0.0) * ig * xc
    a_cum, h0 = _scan_linear(a, bterm)
    hs = a_cum * h_prev + h0
    return hs * jax.nn.gelu(g), hs


def _rope(x, cos, sin_signed):
    n = x.shape[1] // LANES
    lane = lax.broadcasted_iota(I32, (x.shape[0], LANES), 1)
    first = (lane % HEAD_DIM) < (HEAD_DIM // 2)
    outs = []
    for c in range(n):
        xc = x[:, c * LANES:(c + 1) * LANES]
        swapped = jnp.where(first, pltpu.roll(xc, LANES - HEAD_DIM // 2, axis=1),
                            pltpu.roll(xc, HEAD_DIM // 2, axis=1))
        outs.append(xc * cos + swapped * sin_signed)
    return outs


def _store_head_variants(var_ref, row0, chunks):
    t = chunks[0].shape[0]
    lane = lax.broadcasted_iota(I32, (t, LANES), 1)
    for kc, x in enumerate(chunks):
        swapped = pltpu.roll(x, HALF, axis=1)
        for hh in range(2):
            for p in range(2):
                src = x if p == hh else swapped
                keep = (lane < HALF) if p == 0 else (lane >= HALF)
                var_ref[2 * kc + hh, p, row0:row0 + t, :] = jnp.where(keep, src, 0.0).astype(BF16)


def _attend_block(q_chunks, kvar, vvar, key0, sinks, lim, out_ref, out_rows, out_col0):
    qb = q_chunks[0].shape[0]
    nk = 2 * WINDOW
    gq = (2 * len(q_chunks)) // N_KV_HEADS
    rows = 2 * qb
    qi = lax.broadcasted_iota(I32, (rows, nk), 0) % qb
    kj = lax.broadcasted_iota(I32, (rows, nk), 1)
    valid = ((kj < WINDOW) & (kj > qi + lim)) | ((kj >= WINDOW) & (kj - WINDOW <= qi))
    top = lax.broadcasted_iota(I32, (rows, 1), 0) < qb
    lane = lax.broadcasted_iota(I32, (nk, LANES), 1)
    ones_lo = jnp.where(lane < HALF, 1.0, 0.0).astype(BF16)
    ones_hi = jnp.where(lane >= HALF, 1.0, 0.0).astype(BF16)
    lane_o = lax.broadcasted_iota(I32, (rows, LANES), 1)
    nt = (((1,), (1,)), ((), ()))
    for c in range(N_KV_HEADS):
        c0 = c * gq // 2
        qs = jnp.concatenate([q_chunks[c0], q_chunks[c0 + 1]], axis=0)
        es, sink_terms = [], []
        for p in range(2):
            kc = kvar[c, p, key0:key0 + nk, :]
            s = lax.dot_general(qs, kc, nt, preferred_element_type=F32)
            s = jnp.where(valid, s, -jnp.inf)
            h0 = c * gq + p
            sink = jnp.where(top, sinks[:, h0:h0 + 1], sinks[:, h0 + 2:h0 + 3])
            m = jnp.maximum(jnp.max(s, axis=-1, keepdims=True), sink)
            es.append(jnp.exp(s - m).astype(BF16))
            sink_terms.append(jnp.exp(sink - m))
        r0 = jnp.concatenate([vvar[c, 0, key0:key0 + nk, :], ones_lo], axis=1)
        r1 = jnp.concatenate([vvar[c, 1, key0:key0 + nk, :], ones_hi], axis=1)
        od = (jnp.dot(es[0], r0, preferred_element_type=F32)
              + jnp.dot(es[1], r1, preferred_element_type=F32))
        den = od[:, LANES:] + jnp.where(lane_o < HALF, sink_terms[0], sink_terms[1])
        o = (od[:, :LANES] / den).astype(out_ref.dtype)
        out_ref[out_rows, out_col0 + c0 * LANES:out_col0 + (c0 + 1) * LANES] = o[0:qb]
        out_ref[out_rows, out_col0 + (c0 + 1) * LANES:out_col0 + (c0 + 2) * LANES] = o[qb:rows]


def _mixer_kernel(*refs, tile, qblock, is_prompt, pos0, d_lru, d_pool, d_attn, d_kv):
    (proj_ref, cos_ref, sin_ref, convw_ref, convb_ref, wg_ref, ba_ref, bx_ref, lam_ref,
     poolw_ref, pscale_ref, sink_ref) = refs[:12]
    if is_prompt:
        (mix_ref, ko_ref, vo_ref, ho_ref, co_ref, po_ref,
         lru_ext, pool_ext, kvar, vvar, hcar) = refs[12:]
    else:
        (ck_ref, cv_ref, sh_ref, sc_ref, sp_ref,
         mix_ref, ko_ref, vo_ref, ho_ref, co_ref, po_ref,
         lru_ext, pool_ext, kvar, vvar) = refs[12:]

    s = pl.program_id(1) if is_prompt else 0
    o_glru = d_lru
    o_pool = 2 * d_lru
    o_q = o_pool + d_pool
    o_k = o_q + d_attn
    o_v = o_k + d_kv
    n_kc = d_kv // LANES

    if is_prompt:
        @pl.when(s == 0)
        def _():
            lru_ext[0:HALO, :] = jnp.zeros((HALO, d_lru), F32)
            pool_ext[0:HALO, :] = jnp.zeros((HALO, d_pool), F32)
            hcar[...] = jnp.zeros_like(hcar)
            kvar[:, :, 0:WINDOW, :] = jnp.zeros((N_KV_HEADS, 2, WINDOW, LANES), BF16)
            vvar[:, :, 0:WINDOW, :] = jnp.zeros((N_KV_HEADS, 2, WINDOW, LANES), BF16)
        h_prev = hcar[0:1, :]
    else:
        lru_ext[0:HALO, :] = jnp.zeros((HALO, d_lru), F32)
        pool_ext[0:HALO, :] = jnp.zeros((HALO, d_pool), F32)
        lru_ext[HALO - (CONV_WIDTH - 1):HALO, :] = sc_ref[0]
        pool_ext[HALO - POOL_PAD:HALO, :] = sp_ref[0]
        h_prev = sh_ref[0]
        kvar[:, :, WINDOW:2 * WINDOW, :] = jnp.zeros((N_KV_HEADS, 2, WINDOW, LANES), BF16)
        vvar[:, :, WINDOW:2 * WINDOW, :] = jnp.zeros((N_KV_HEADS, 2, WINDOW, LANES), BF16)
        _store_head_variants(kvar, 0, [ck_ref[0, :, c * LANES:(c + 1) * LANES] for c in range(n_kc)])
        _store_head_variants(vvar, 0, [cv_ref[0, :, c * LANES:(c + 1) * LANES] for c in range(n_kc)])

    lru_ext[HALO:HALO + tile, :] = proj_ref[:, 0:d_lru]
    pool_ext[HALO:HALO + tile, :] = proj_ref[:, o_pool:o_pool + d_pool]

    xc = convb_ref[...] + convw_ref[0:1, :] * lru_ext[HALO - 3:HALO - 3 + tile, :]
    for j in range(1, CONV_WIDTH):
        xc = xc + convw_ref[j:j + 1, :] * lru_ext[HALO - 3 + j:HALO - 3 + j + tile, :]
    neg = -lam_ref[...]
    sp_all = jnp.maximum(neg, 0.0) + jnp.log1p(jnp.exp(-jnp.abs(neg)))
    h_last = []
    for c in range(d_lru // LANES):
        cs = slice(c * LANES, (c + 1) * LANES)
        y, hs = _lru_chunk(xc[:, cs], proj_ref[:, o_glru + c * LANES:o_glru + (c + 1) * LANES],
                           h_prev[:, cs], wg_ref[c], ba_ref[:, cs], bx_ref[:, cs], sp_all[:, cs])
        mix_ref[:, cs] = y.astype(mix_ref.dtype)
        h_last.append(hs[tile - 1:tile, :])
    h_last = jnp.concatenate(h_last, axis=-1)

    row = lax.broadcasted_iota(I32, (tile, LANES), 0)
    pos = pos0 + s * tile + row
    for gi, w in enumerate(POOL_WINDOWS):
        cs = slice(gi * LANES, (gi + 1) * LANES)
        e = pool_ext[:, cs]
        acc = e
        step = 1
        while step < w:
            acc = acc + pltpu.roll(acc, step, axis=0)
            step *= 2
        cnt = jnp.minimum(pos + 1, w).astype(F32)
        dlt = acc[HALO:HALO + tile, :] / cnt - e[HALO:HALO + tile, :]
        y = jnp.dot(dlt.astype(BF16), poolw_ref[gi], preferred_element_type=F32) * pscale_ref[:, cs]
        mix_ref[:, d_lru + gi * LANES:d_lru + (gi + 1) * LANES] = y.astype(mix_ref.dtype)

    sinks = sink_ref[...]
    o_attn = d_lru + d_pool
    cos = cos_ref[...]
    sin = sin_ref[...]
    q_chunks = [(qc * ATTN_SCALE).astype(BF16) for qc in _rope(proj_ref[:, o_q:o_q + d_attn], cos, sin)]
    k_rot = _rope(proj_ref[:, o_k:o_k + d_kv], cos, sin)
    _store_head_variants(kvar, WINDOW, k_rot)
    _store_head_variants(vvar, WINDOW, [proj_ref[:, o_v + c * LANES:o_v + (c + 1) * LANES]
                                        for c in range(n_kc)])
    for blk in range(tile // qblock):
        rs = slice(blk * qblock, (blk + 1) * qblock)
        if is_prompt and blk == 0:
            lim = jnp.where(s == 0, WINDOW, 0)
        else:
            lim = 0
        _attend_block([qc[rs] for qc in q_chunks], kvar, vvar, blk * qblock, sinks, lim,
                      mix_ref, rs, o_attn)

    k_last = jnp.concatenate([kc[tile - qblock:tile] for kc in k_rot], axis=-1)
    if is_prompt:
        lru_ext[0:HALO, :] = lru_ext[tile:tile + HALO, :]
        pool_ext[0:HALO, :] = pool_ext[tile:tile + HALO, :]
        hcar[0:1, :] = h_last
        kvar[:, :, 0:WINDOW, :] = kvar[:, :, tile:tile + WINDOW, :]
        vvar[:, :, 0:WINDOW, :] = vvar[:, :, tile:tile + WINDOW, :]

        @pl.when(s == pl.num_programs(1) - 1)
        def _():
            ko_ref[0] = k_last
            vo_ref[0] = proj_ref[tile - qblock:tile, o_v:o_v + d_kv]
            ho_ref[0] = h_last
            co_ref[0] = lru_ext[HALO + tile - (CONV_WIDTH - 1):HALO + tile, :]
            po_ref[0] = pool_ext[HALO + tile - POOL_PAD:HALO + tile, :]
    else:
        ko_ref[0, 0:WINDOW - tile, :] = ck_ref[0, tile:WINDOW, :]
        ko_ref[0, WINDOW - tile:WINDOW, :] = k_last
        vo_ref[0, 0:WINDOW - tile, :] = cv_ref[0, tile:WINDOW, :]
        vo_ref[0, WINDOW - tile:WINDOW, :] = proj_ref[:, o_v:o_v + d_kv]
        ho_ref[0] = h_last
        co_ref[0] = lru_ext[HALO + tile - (CONV_WIDTH - 1):HALO + tile, :]
        po_ref[0] = pool_ext[HALO + tile - POOL_PAD:HALO + tile, :]


def _layer_consts(p, dims):
    d_lru, d_pool = dims["d_lru"], dims["d_pool"]
    hd = d_lru // LRU_HEADS
    per = LANES // hd
    nchunk = d_lru // LANES

    def blockdiag(w):
        w = w.reshape(nchunk, per, hd, hd)
        eye = jnp.eye(per, dtype=w.dtype)
        return jnp.einsum("cpij,pq->cpiqj", w, eye).reshape(nchunk, LANES, LANES)

    wg = jnp.concatenate([blockdiag(p["lru_wa"]), blockdiag(p["lru_wx"])], axis=-1).astype(BF16)
    return dict(
        convw=p["conv_w"], convb=p["conv_b"].reshape(1, d_lru), wg=wg,
        ba=p["lru_ba"].reshape(1, d_lru), bx=p["lru_bx"].reshape(1, d_lru),
        lam=p["lru_lambda"].reshape(1, d_lru), poolw=p["pool_w"].astype(BF16),
        pscale=p["pool_scale"].reshape(1, d_pool), sinks=p["attn_sinks"].reshape(1, -1))


def _rope_tables(pos):
    half = HEAD_DIM // 2
    inv = ROPE_THETA ** (-jnp.arange(half, dtype=F32) / half)
    ang = pos.astype(F32)[:, None] * inv[None, :]
    cos = jnp.cos(ang)
    sin = jnp.sin(ang)
    cos2 = jnp.concatenate([cos, cos], axis=-1)
    sin2 = jnp.concatenate([-sin, sin], axis=-1)
    reps = LANES // HEAD_DIM
    return jnp.tile(cos2, (1, reps)), jnp.tile(sin2, (1, reps))


CONST_NAMES = ("convw", "convb", "wg", "ba", "bx", "lam", "poolw", "pscale", "sinks")


def _const_specs(consts):
    return [pl.BlockSpec(consts[n].shape, functools.partial(lambda nd, *_: (0,) * nd, consts[n].ndim))
            for n in CONST_NAMES]


def _mixer_prompt(proj, consts, cos, sin, bsz, seq, dims, tile):
    d_lru, d_pool, d_attn, d_kv = dims["d_lru"], dims["d_pool"], dims["d_attn"], dims["d_kv"]
    d_in = proj.shape[1]
    d_mix = d_lru + d_pool + d_attn
    ns = seq // tile
    kern = functools.partial(_mixer_kernel, tile=tile, qblock=WINDOW, is_prompt=True, pos0=0,
                             d_lru=d_lru, d_pool=d_pool, d_attn=d_attn, d_kv=d_kv)
    return pl.pallas_call(
        kern,
        grid=(bsz, ns),
        in_specs=[
            pl.BlockSpec((tile, d_in), lambda b, s: (b * ns + s, 0)),
            pl.BlockSpec((tile, LANES), lambda b, s: (s, 0)),
            pl.BlockSpec((tile, LANES), lambda b, s: (s, 0)),
        ] + _const_specs(consts),
        out_specs=[
            pl.BlockSpec((tile, d_mix), lambda b, s: (b * ns + s, 0)),
            pl.BlockSpec((1, WINDOW, d_kv), lambda b, s: (b, 0, 0)),
            pl.BlockSpec((1, WINDOW, d_kv), lambda b, s: (b, 0, 0)),
            pl.BlockSpec((1, 1, d_lru), lambda b, s: (b, 0, 0)),
            pl.BlockSpec((1, CONV_WIDTH - 1, d_lru), lambda b, s: (b, 0, 0)),
            pl.BlockSpec((1, POOL_PAD, d_pool), lambda b, s: (b, 0, 0)),
        ],
        out_shape=[
            jax.ShapeDtypeStruct((bsz * seq, d_mix), BF16),
            jax.ShapeDtypeStruct((bsz, WINDOW, d_kv), F32),
            jax.ShapeDtypeStruct((bsz, WINDOW, d_kv), F32),
            jax.ShapeDtypeStruct((bsz, 1, d_lru), F32),
            jax.ShapeDtypeStruct((bsz, CONV_WIDTH - 1, d_lru), F32),
            jax.ShapeDtypeStruct((bsz, POOL_PAD, d_pool), F32),
        ],
        scratch_shapes=[
            pltpu.VMEM((tile + HALO, d_lru), F32),
            pltpu.VMEM((tile + HALO, d_pool), F32),
            pltpu.VMEM((N_KV_HEADS, 2, WINDOW + tile, LANES), BF16),
            pltpu.VMEM((N_KV_HEADS, 2, WINDOW + tile, LANES), BF16),
            pltpu.VMEM((8, d_lru), F32),
        ],
        compiler_params=pltpu.CompilerParams(
            dimension_semantics=("arbitrary", "arbitrary"), vmem_limit_bytes=VMEM_LIMIT),
        name="mixer_prompt",
    )(proj, cos, sin, *[consts[n] for n in CONST_NAMES])


def _mixer_sample_kernel(*refs, n_seq, t, d_lru, d_pool, d_attn, d_kv):
    n_in = 3 + len(CONST_NAMES)
    proj_ref, shared = refs[0], refs[1:n_in]
    state_in = refs[n_in:n_in + 5]
    mix_ref = refs[n_in + 5]
    state_out = refs[n_in + 6:n_in + 11]
    scratch = refs[n_in + 11:]
    for q in range(n_seq):
        one = lambda r: r.at[pl.ds(q, 1)]
        _mixer_kernel(proj_ref.at[pl.ds(q * t, t)], *shared, *[one(r) for r in state_in],
                      mix_ref.at[pl.ds(q * t, t)], *[one(r) for r in state_out],
                      *[r.at[q] for r in scratch],
                      tile=t, qblock=t, is_prompt=False, pos0=PAST_LEN,
                      d_lru=d_lru, d_pool=d_pool, d_attn=d_attn, d_kv=d_kv)


def _mixer_sample(proj, row0, consts, cos, sin, cache_k, cache_v, st_h, st_conv, st_pool, layer, dims):
    d_lru, d_pool, d_attn, d_kv = dims["d_lru"], dims["d_pool"], dims["d_attn"], dims["d_kv"]
    d_in = proj.shape[1]
    d_mix = d_lru + d_pool + d_attn
    db, win = cache_k.shape[1], cache_k.shape[2]
    t = cos.shape[0]
    nq = SAMPLE_SEQS_PER_STEP
    assert db % nq == 0 and row0 % (nq * t) == 0
    blk0 = row0 // (nq * t)
    kern = functools.partial(_mixer_sample_kernel, n_seq=nq, t=t,
                             d_lru=d_lru, d_pool=d_pool, d_attn=d_attn, d_kv=d_kv)
    return pl.pallas_call(
        kern,
        grid=(db // nq,),
        in_specs=[
            pl.BlockSpec((nq * t, d_in), lambda b: (blk0 + b, 0)),
            pl.BlockSpec((t, LANES), lambda b: (0, 0)),
            pl.BlockSpec((t, LANES), lambda b: (0, 0)),
        ] + _const_specs(consts) + [
            pl.BlockSpec((None, nq, win, d_kv), lambda b: (layer, b, 0, 0)),
            pl.BlockSpec((None, nq, win, d_kv), lambda b: (layer, b, 0, 0)),
            pl.BlockSpec((None, nq, 1, d_lru), lambda b: (layer, b, 0, 0)),
            pl.BlockSpec((None, nq, CONV_WIDTH - 1, d_lru), lambda b: (layer, b, 0, 0)),
            pl.BlockSpec((None, nq, POOL_PAD, d_pool), lambda b: (layer, b, 0, 0)),
        ],
        out_specs=[
            pl.BlockSpec((nq * t, d_mix), lambda b: (b, 0)),
            pl.BlockSpec((nq, win, d_kv), lambda b: (b, 0, 0)),
            pl.BlockSpec((nq, win, d_kv), lambda b: (b, 0, 0)),
            pl.BlockSpec((nq, 1, d_lru), lambda b: (b, 0, 0)),
            pl.BlockSpec((nq, CONV_WIDTH - 1, d_lru), lambda b: (b, 0, 0)),
            pl.BlockSpec((nq, POOL_PAD, d_pool), lambda b: (b, 0, 0)),
        ],
        out_shape=[
            jax.ShapeDtypeStruct((db * t, d_mix), F32),
            jax.ShapeDtypeStruct((db, win, d_kv), F32),
            jax.ShapeDtypeStruct((db, win, d_kv), F32),
            jax.ShapeDtypeStruct((db, 1, d_lru), F32),
            jax.ShapeDtypeStruct((db, CONV_WIDTH - 1, d_lru), F32),
            jax.ShapeDtypeStruct((db, POOL_PAD, d_pool), F32),
        ],
        scratch_shapes=[
            pltpu.VMEM((nq, t + HALO, d_lru), F32),
            pltpu.VMEM((nq, t + HALO, d_pool), F32),
            pltpu.VMEM((nq, N_KV_HEADS, 2, 2 * WINDOW, LANES), BF16),
            pltpu.VMEM((nq, N_KV_HEADS, 2, 2 * WINDOW, LANES), BF16),
        ],
        compiler_params=pltpu.CompilerParams(
            dimension_semantics=("arbitrary",), vmem_limit_bytes=VMEM_LIMIT),
        name="mixer_sample",
    )(proj, cos, sin, *[consts[n] for n in CONST_NAMES], cache_k, cache_v, st_h, st_conv, st_pool)


def _route(logits, run_cnt):
    t = logits.shape[0]
    lane = lax.broadcasted_iota(I32, (t, LANES), 1)
    lane_f = lane.astype(F32)
    ninf = -jnp.inf
    big = float(LANES)
    is_g = lane < N_EXPERT_GROUPS
    lg = jnp.where(is_g, logits, ninf)
    mg = jnp.max(lg, axis=-1, keepdims=True)
    g_top = jnp.min(jnp.where(lg == mg, lane_f, big), axis=-1, keepdims=True).astype(I32)
    pg_top = 1.0 / jnp.sum(jnp.exp(lg - mg), axis=-1, keepdims=True)
    base = N_EXPERT_GROUPS + EXPERTS_PER_GROUP * g_top
    in_grp = (lane >= base) & (lane < base + EXPERTS_PER_GROUP)
    le = jnp.where(in_grp, logits, ninf)
    m1 = jnp.max(le, axis=-1, keepdims=True)
    i1 = jnp.min(jnp.where(le == m1, lane_f, big), axis=-1, keepdims=True).astype(I32)
    le2 = jnp.where(lane == i1, ninf, le)
    m2 = jnp.max(le2, axis=-1, keepdims=True)
    i2 = jnp.min(jnp.where((le2 == m2) & in_grp & (lane != i1), lane_f, big),
                 axis=-1, keepdims=True).astype(I32)
    se = jnp.sum(jnp.exp(le - m1), axis=-1, keepdims=True)
    p1 = 1.0 / se
    p2 = jnp.exp(m2 - m1) / se
    tot = p1 + p2
    w1 = (p1 / tot) * pg_top
    w2 = (p2 / tot) * pg_top
    a = i1 - base
    b = i2 - base
    lo = jnp.minimum(a, b)
    hi = jnp.maximum(a, b)
    w_lo = jnp.where(a < b, w1, w2)
    w_hi = jnp.where(a < b, w2, w1)
    pid = jnp.where(lo == 0, hi - 1, jnp.where(lo == 1, jnp.where(hi == 3, 3, 4), 5))
    swap = pid == 5
    w_a = jnp.where(swap, w_hi, w_lo)
    w_b = jnp.where(swap, w_lo, w_hi)
    cls = g_top * N_PAIRS + pid
    onehot = lane == cls
    ti = lax.broadcasted_iota(I32, (t, t), 0)
    tj = lax.broadcasted_iota(I32, (t, t), 1)
    lower = jnp.where(tj <= ti, 1.0, 0.0).astype(BF16)
    prefix = jnp.dot(lower, jnp.where(onehot, 1.0, 0.0).astype(BF16), preferred_element_type=F32)
    rank = jnp.sum(jnp.where(onehot, prefix - 1.0 + run_cnt, 0.0), axis=-1, keepdims=True)
    info = jnp.where(lane == 0, w_a, jnp.where(lane == 1, w_b, jnp.where(
        lane == 2, cls.astype(F32), jnp.where(lane == 3, rank, 0.0))))
    return info, run_cnt + prefix[t - 1:t, :]


def _out_proj_kernel(mixp_ref, mixs_ref, xm_ref, xt_ref, w_hbm, g_ref, wr_ref, br_ref,
                     o_ref, info_ref, cnt_ref, w_bf, stage, sem, run_cnt, *, n_main_blocks, d, layer, slab):
    i = pl.program_id(0)

    @pl.when(i == 0)
    def _():
        _load_weight_bf16(w_hbm.at[layer], w_bf, stage, sem)
        run_cnt[...] = jnp.zeros_like(run_cnt)

    mix = jnp.where(i < n_main_blocks, mixp_ref[...], mixs_ref[...].astype(BF16))
    xres = (_pick_rows(i, n_main_blocks, xm_ref, xt_ref, d, slab)
            + jnp.dot(mix, w_bf[...], preferred_element_type=F32))
    h2 = _rms(xres, g_ref[...])
    logits = jnp.dot(h2.astype(BF16), wr_ref[...], preferred_element_type=F32) + br_ref[...]
    info, cnt = _route(logits, run_cnt[0:1, :])
    run_cnt[0:1, :] = cnt
    pitch = _slab_pitch(d)
    _write_slabs(o_ref, (), xres, pitch)
    o_ref[pl.ds(d // LANES, ROW_BLOCK, stride=pitch), :] = info
    info_ref[...] = info
    cnt_ref[...] = jnp.broadcast_to(cnt, cnt_ref.shape)


def _out_proj(mix_p, mix_s, x_main, x_tail, tail_block0, slab, w, layer, g, wr, br, n_rows):
    d = g.shape[1]
    d_mix = w.shape[1]
    npb = mix_p.shape[0] // ROW_BLOCK
    pitch = _slab_pitch(d)
    kern = functools.partial(_out_proj_kernel, n_main_blocks=npb, d=d, layer=layer, slab=slab)
    return pl.pallas_call(
        kern,
        grid=(n_rows // ROW_BLOCK,),
        in_specs=[
            pl.BlockSpec((ROW_BLOCK, d_mix), lambda i: (jnp.minimum(i, npb - 1), 0)),
            pl.BlockSpec((ROW_BLOCK, d_mix), lambda i: (jnp.maximum(i - npb, 0), 0)),
        ] + _row_specs(d, npb, tail_block0, slab) + [
            pl.BlockSpec(memory_space=pl.ANY),
            pl.BlockSpec((1, d), lambda i: (0, 0)),
            pl.BlockSpec((d, LANES), lambda i: (0, 0)),
            pl.BlockSpec((1, LANES), lambda i: (0, 0)),
        ],
        out_specs=[pl.BlockSpec((ROW_BLOCK * pitch, LANES), lambda i: (i, 0)),
                   pl.BlockSpec((ROW_BLOCK, LANES), lambda i: (i, 0)),
                   pl.BlockSpec((8, LANES), lambda i: (0, 0))],
        out_shape=[jax.ShapeDtypeStruct((n_rows * pitch, LANES), F32),
                   jax.ShapeDtypeStruct((n_rows, LANES), F32),
                   jax.ShapeDtypeStruct((8, LANES), F32)],
        scratch_shapes=[
            pltpu.VMEM((d_mix, d), BF16),
            pltpu.VMEM((2, ROW_BLOCK, d), F32),
            pltpu.SemaphoreType.DMA((2,)),
            pltpu.VMEM((8, LANES), F32),
        ],
        compiler_params=pltpu.CompilerParams(
            dimension_semantics=("arbitrary",), vmem_limit_bytes=VMEM_LIMIT),
        name="out_proj",
    )(mix_p, mix_s, x_main, x_tail, w, g, wr, br)


def _moe_kernel(cls_ref, rank_ref, off_ref, cnt_ref, ea_ref, eb_ref, chga_ref, chgb_ref, nch_ref,
                xr_hbm, g_ref, w1a_ref, w3a_ref, w2a_ref, w1b_ref, w3b_ref, w2b_ref,
                xo_hbm,
                perm, xbuf, obuf, wa1, wa3, wa2, wb1, wb3, wb2, gsem, ssem, *, d, n_rows):
    del ea_ref, eb_ref
    m = MOE_CHUNK
    pitch = _slab_pitch(d)
    n_slabs = d // LANES
    c = pl.program_id(0)
    n = nch_ref[0]
    slot = c % 2
    other = 1 - slot

    def start_gather(chunk, sl, j):
        tok = jnp.maximum(perm[(chunk + 1) * m + j], 0)
        pltpu.make_async_copy(xr_hbm.at[pl.ds(tok * pitch, pitch)],
                              xbuf.at[sl, pl.ds(j * pitch, pitch)], gsem.at[sl]).start()

    def start_scatter(chunk, sl, j):
        tok = perm[(chunk + 1) * m + j]
        r = jnp.where(tok < 0, n_rows + sl * m + j, tok)
        pltpu.make_async_copy(obuf.at[sl, pl.ds(j * pitch, pitch)],
                              xo_hbm.at[pl.ds(r * pitch, pitch)], ssem.at[sl]).start()

    def wait_gather(sl):
        pltpu.make_async_copy(xr_hbm.at[pl.ds(0, m * pitch)], xbuf.at[sl], gsem.at[sl]).wait()

    def wait_scatter(sl):
        pltpu.make_async_copy(obuf.at[sl], xo_hbm.at[pl.ds(0, m * pitch)], ssem.at[sl]).wait()

    @pl.when(c == 0)
    def _():
        obuf[...] = jnp.zeros_like(obuf)
        pad = pltpu.make_async_copy(obuf.at[0], xo_hbm.at[pl.ds(n_rows * pitch, m * pitch)], ssem.at[0])
        pad.start()
        pad.wait()

        def fill(lo, hi):
            def body(i, z):
                perm[i] = -1
                return z
            lax.fori_loop(lo, hi, body, 0)

        def first_gather(j, z):
            start_gather(0, 0, j)
            return z

        fill(0, m)
        fill((n + 1) * m, (n + 3) * m)
        for k in range(N_CLASSES):
            base = (off_ref[k] + 1) * m
            cnt = cnt_ref[k]
            fill(base + cnt, base + ((cnt + m - 1) // m) * m)

        def place(t, z):
            perm[(off_ref[cls_ref[t]] + 1) * m + rank_ref[t]] = t
            return z
        lax.fori_loop(0, n_rows, place, 0, unroll=8)
        lax.fori_loop(0, m, first_gather, 0)

    @pl.when(c <= n)
    def _():
        wait_gather(slot)

        @pl.when(c >= 1)
        def _():
            wait_scatter(slot)

        @pl.when(chga_ref[c] == 1)
        def _():
            wa1[...] = w1a_ref[0].astype(BF16)
            wa3[...] = w3a_ref[0].astype(BF16)
            wa2[...] = w2a_ref[0].astype(BF16)

        @pl.when(chgb_ref[c] == 1)
        def _():
            wb1[...] = w1b_ref[0].astype(BF16)
            wb3[...] = w3b_ref[0].astype(BF16)
            wb2[...] = w2b_ref[0].astype(BF16)

        for j in range(m):
            start_gather(c + 1, other, j)
        for j in range(m):
            start_scatter(c - 1, other, j)
        x = _read_slabs(xbuf, (slot,), m, n_slabs, pitch)
        info = xbuf[slot, pl.ds(n_slabs, m, stride=pitch), :]
        w_a = info[:, 0:1]
        w_b = info[:, 1:2]
        h = _rms(x, g_ref[...]).astype(BF16)
        hid_a = (jax.nn.silu(jnp.dot(h, wa1[...], preferred_element_type=F32))
                 * jnp.dot(h, wa3[...], preferred_element_type=F32) * w_a).astype(BF16)
        hid_b = (jax.nn.silu(jnp.dot(h, wb1[...], preferred_element_type=F32))
                 * jnp.dot(h, wb3[...], preferred_element_type=F32) * w_b).astype(BF16)
        y = (jnp.dot(hid_a, wa2[...], preferred_element_type=F32)
             + jnp.dot(hid_b, wb2[...], preferred_element_type=F32))
        _write_slabs(obuf, (slot,), x + y, pitch)

        @pl.when(c == n)
        def _():
            wait_gather(other)
            wait_scatter(other)


def _moe(xr, g, w1, w3, w2, tables, n_rows, c_max):
    d = g.shape[1]
    f = w1.shape[2]
    m = MOE_CHUNK
    pitch = _slab_pitch(d)
    kern = functools.partial(_moe_kernel, d=d, n_rows=n_rows)

    def wspec(shape, which):
        return pl.BlockSpec(shape, lambda c, *pref: (pref[which][c], 0, 0))

    grid_spec = pltpu.PrefetchScalarGridSpec(
        num_scalar_prefetch=9,
        grid=(c_max + 1,),
        in_specs=[
            pl.BlockSpec(memory_space=pl.ANY),
            pl.BlockSpec((1, d), lambda c, *pref: (0, 0)),
            wspec((1, d, f), 4), wspec((1, d, f), 4), wspec((1, f, d), 4),
            wspec((1, d, f), 5), wspec((1, d, f), 5), wspec((1, f, d), 5),
        ],
        out_specs=pl.BlockSpec(memory_space=pl.ANY),
        scratch_shapes=[
            pltpu.SMEM(((c_max + 3) * m,), I32),
            pltpu.VMEM((2, m * pitch, LANES), F32),
            pltpu.VMEM((2, m * pitch, LANES), F32),
            pltpu.VMEM((d, f), BF16), pltpu.VMEM((d, f), BF16), pltpu.VMEM((f, d), BF16),
            pltpu.VMEM((d, f), BF16), pltpu.VMEM((d, f), BF16), pltpu.VMEM((f, d), BF16),
            pltpu.SemaphoreType.DMA((2,)),
            pltpu.SemaphoreType.DMA((2,)),
        ],
    )
    return pl.pallas_call(
        kern,
        grid_spec=grid_spec,
        out_shape=jax.ShapeDtypeStruct(((n_rows + 2 * m) * pitch, LANES), F32),
        compiler_params=pltpu.CompilerParams(
            dimension_semantics=("arbitrary",), vmem_limit_bytes=VMEM_LIMIT),
        name="moe",
    )(*tables, xr, g, w1, w3, w2, w1, w3, w2)


def _chunk_tables(counts, c_max, expert0):
    m = MOE_CHUNK
    nch_c = (counts + m - 1) // m
    ch_end = jnp.cumsum(nch_c)
    ch_off = ch_end - nch_c
    n_chunks = ch_end[-1]
    chunk = jnp.minimum(jnp.arange(c_max + 1, dtype=I32), jnp.maximum(n_chunks - 1, 0))
    ccls = jnp.sum((chunk[:, None] >= ch_end[None, :]).astype(I32), axis=1)
    ccls = jnp.minimum(ccls, N_CLASSES - 1)
    grp = ccls // N_PAIRS
    pid = ccls % N_PAIRS
    ea = expert0 + grp * EXPERTS_PER_GROUP + jnp.asarray(PAIR_SLOT_A, I32)[pid]
    eb = expert0 + grp * EXPERTS_PER_GROUP + jnp.asarray(PAIR_SLOT_B, I32)[pid]
    first = jnp.ones((1,), I32)
    chga = jnp.concatenate([first, (ea[1:] != ea[:-1]).astype(I32)])
    chgb = jnp.concatenate([first, (eb[1:] != eb[:-1]).astype(I32)])
    return ch_off.astype(I32), counts, ea, eb, chga, chgb, n_chunks.reshape(1).astype(I32)


def _final_norm_kernel(x_ref, g_ref, op_ref, os_ref, *, n_prompt_blocks):
    i = pl.program_id(0)
    d = g_ref.shape[1]
    y = _rms(_read_slabs(x_ref, (), ROW_BLOCK, d // LANES, _slab_pitch(d)), g_ref[...])

    @pl.when(i < n_prompt_blocks)
    def _():
        op_ref[...] = y

    @pl.when(i >= n_prompt_blocks)
    def _():
        os_ref[...] = y


def _final_norm(x, g, n_prompt, n_sample):
    d = g.shape[1]
    npb = n_prompt // ROW_BLOCK
    nsb = n_sample // ROW_BLOCK
    kern = functools.partial(_final_norm_kernel, n_prompt_blocks=npb)
    return pl.pallas_call(
        kern,
        grid=(npb + nsb,),
        in_specs=[pl.BlockSpec((ROW_BLOCK * _slab_pitch(d), LANES), lambda i: (i, 0)),
                  pl.BlockSpec((1, d), lambda i: (0, 0))],
        out_specs=[pl.BlockSpec((ROW_BLOCK, d), lambda i: (jnp.minimum(i, npb - 1), 0)),
                   pl.BlockSpec((ROW_BLOCK, d), lambda i: (jnp.maximum(i - npb, 0), 0))],
        out_shape=[jax.ShapeDtypeStruct((n_prompt, d), F32),
                   jax.ShapeDtypeStruct((n_sample, d), F32)],
        compiler_params=pltpu.CompilerParams(dimension_semantics=("arbitrary",)),
        name="final_norm",
    )(x, g)


def kernel(x_prompt, x_sample, cache_k, cache_v, state_lru_h, state_conv, state_pool, norm1_g, w_in, conv_w, conv_b, lru_wa, lru_ba, lru_wx, lru_bx, lru_lambda, pool_w, pool_scale, attn_sinks, w_out, norm2_g, router_group_w, router_group_b, router_expert_w, router_expert_b, expert_w1, expert_w3, expert_w2, final_norm_g):
    bsz, seq, d = x_prompt.shape
    db, ds, _ = x_sample.shape
    depth = w_in.shape[0]
    win = cache_k.shape[2]
    nkv = cache_k.shape[3]
    d_lru = lru_lambda.shape[1]
    d_pool = pool_scale.shape[1]
    d_kv = nkv * cache_k.shape[4]
    d_attn = attn_sinks.shape[1] * HEAD_DIM
    dims = dict(d_lru=d_lru, d_pool=d_pool, d_attn=d_attn, d_kv=d_kv)
    n_prompt = bsz * seq
    n_sample = db * ds
    n_rows = n_prompt + n_sample
    assert n_prompt % ROW_BLOCK == 0 and n_sample % ROW_BLOCK == 0
    assert win == WINDOW and nkv == N_KV_HEADS and ds <= 8 and WINDOW % ds == 0
    tile = 256 if seq % 256 == 0 else WINDOW
    c_max = -(-n_rows // MOE_CHUNK) + N_CLASSES
    npb = n_prompt // ROW_BLOCK

    cos_p, sin_p = _rope_tables(jnp.arange(seq))
    cos_s, sin_s = _rope_tables(PAST_LEN + jnp.arange(ds))
    wr_all = jnp.concatenate([router_group_w, router_expert_w], axis=2)
    wr_all = jnp.pad(wr_all, ((0, 0), (0, 0), (0, LANES - wr_all.shape[2]))).astype(BF16)
    br_all = jnp.concatenate([router_group_b, router_expert_b], axis=1)
    br_all = jnp.pad(br_all, ((0, 0), (0, LANES - br_all.shape[1])))

    x_main, x_tail, tail0, slab = x_prompt.reshape(n_prompt, d), x_sample.reshape(n_sample, d), 0, False
    ck = cache_k.reshape(depth, db, win, d_kv)
    cv = cache_v.reshape(depth, db, win, d_kv)
    st_h = state_lru_h.reshape(depth, db, 1, d_lru)
    n_exp, _, d_exp = expert_w1.shape[1:]
    w1 = expert_w1.reshape(depth * n_exp, d, d_exp)
    w3 = expert_w3.reshape(depth * n_exp, d, d_exp)
    w2 = expert_w2.reshape(depth * n_exp, d_exp, d)
    outs = [[] for _ in range(10)]
    for l in range(depth):
        p = dict(conv_w=conv_w[l], conv_b=conv_b[l], lru_wa=lru_wa[l], lru_ba=lru_ba[l],
                 lru_wx=lru_wx[l], lru_bx=lru_bx[l], lru_lambda=lru_lambda[l], pool_w=pool_w[l],
                 pool_scale=pool_scale[l], attn_sinks=attn_sinks[l])
        consts = _layer_consts(p, dims)
        proj = _in_proj(x_main, x_tail, tail0, slab, norm1_g[l].reshape(1, d), w_in, l, n_rows, n_prompt)
        mix_p, pk, pv, ph, pc, pp = _mixer_prompt(proj, consts, cos_p, sin_p, bsz, seq, dims, tile)
        mix_s, sk, sv, sh, sc, sp = _mixer_sample(
            proj, n_prompt, consts, cos_s, sin_s, ck, cv, st_h, state_conv, state_pool, l, dims)
        xr, info, cnt = _out_proj(mix_p, mix_s, x_main, x_tail, tail0, slab, w_out, l,
                                  norm2_g[l].reshape(1, d), wr_all[l], br_all[l:l + 1], n_rows)
        route = info[:, 2:4].astype(I32)
        tables = (route[:, 0], route[:, 1]) + _chunk_tables(cnt[0, :N_CLASSES].astype(I32), c_max, l * n_exp)
        x = _moe(xr, norm2_g[l].reshape(1, d), w1, w3, w2, tables, n_rows, c_max)
        x_main, x_tail, tail0, slab = x, x, npb, True
        for lst, val in zip(outs, (pk.reshape(bsz, WINDOW, nkv, HEAD_DIM), pv.reshape(bsz, WINDOW, nkv, HEAD_DIM),
                                   ph.reshape(bsz, d_lru), pc, pp,
                                   sk.reshape(db, win, nkv, HEAD_DIM), sv.reshape(db, win, nkv, HEAD_DIM),
                                   sh.reshape(db, d_lru), sc, sp)):
            lst.append(val)
    y_p, y_s = _final_norm(x_main, final_norm_g.reshape(1, d), n_prompt, n_sample)
    return (y_p.reshape(bsz, seq, d), y_s.reshape(db, ds, d)) + tuple(jnp.stack(o) for o in outs)
```

```python
import functools

import jax
import jax.numpy as jnp
from jax import lax
from jax.experimental import pallas as pl
from jax.experimental.pallas import tpu as pltpu

F32 = jnp.float32
BF16 = jnp.bfloat16
I32 = jnp.int32

LRU_HEADS = 8
CONV_WIDTH = 4
LRU_C = 8.0
POOL_WINDOWS = (2, 4, 8, 16)
POOL_PAD = max(POOL_WINDOWS) - 1
HEAD_DIM = 64
N_KV_HEADS = 4
WINDOW = 128
ROPE_THETA = 10000.0
ATTN_SCALE = HEAD_DIM ** -0.5
LOG2_E = 1.4426950408889634
N_EXPERT_GROUPS = 4
EXPERTS_PER_GROUP = 4
N_EXPERTS = N_EXPERT_GROUPS * EXPERTS_PER_GROUP
RMS_EPS = 1e-6
PAST_LEN = 16384

LANES = 128
HALF = LANES // 2
HALO = 16
ROW_BLOCK = 256
SAMPLE_SEQS_PER_STEP = 4
MOE_CHUNK = 128
PAIR_SLOT_A = (0, 0, 0, 1, 1, 3)
PAIR_SLOT_B = (1, 2, 3, 3, 2, 2)
N_PAIRS = len(PAIR_SLOT_A)
N_CLASSES = N_EXPERT_GROUPS * N_PAIRS
VMEM_LIMIT = 52 * 1024 * 1024


def _rms(x, g):
    return (x * lax.rsqrt(jnp.mean(x * x, axis=-1, keepdims=True) + RMS_EPS)) * g


def _load_weight_bf16(w_hbm, w_bf, stage, sem):
    rows = stage.shape[1]
    n = w_hbm.shape[0] // rows

    def copy(i, sl):
        return pltpu.make_async_copy(w_hbm.at[pl.ds(i * rows, rows)], stage.at[sl], sem.at[sl])

    copy(0, 0).start()
    for i in range(n):
        sl = i % 2
        if i + 1 < n:
            copy(i + 1, 1 - sl).start()
        copy(i, sl).wait()
        w_bf[i * rows:(i + 1) * rows, :] = stage[sl].astype(BF16)


def _slab_pitch(d):
    return d // LANES + 1


def _read_slabs(ref, lead, n_tok, n_slabs, pitch):
    return jnp.concatenate(
        [ref[lead + (pl.ds(k, n_tok, stride=pitch), slice(None))] for k in range(n_slabs)], axis=1)


def _write_slabs(ref, lead, val, pitch):
    n_tok = val.shape[0]
    for k in range(val.shape[1] // LANES):
        ref[lead + (pl.ds(k, n_tok, stride=pitch), slice(None))] = val[:, k * LANES:(k + 1) * LANES]


def _pick_rows(i, n_main_blocks, main_ref, tail_ref, d, slab):
    if slab:
        return _read_slabs(main_ref, (), ROW_BLOCK, d // LANES, _slab_pitch(d))
    return jnp.where(i < n_main_blocks, main_ref[...], tail_ref[...])


def _in_proj_kernel(xm_ref, xt_ref, g_ref, w_hbm, o_ref, ot_ref, w_bf, stage, sem,
                    *, n_main_blocks, layer, slab):
    i = pl.program_id(0)

    @pl.when(i == 0)
    def _():
        _load_weight_bf16(w_hbm.at[layer], w_bf, stage, sem)

    h = _rms(_pick_rows(i, n_main_blocks, xm_ref, xt_ref, g_ref.shape[1], slab), g_ref[...])
    proj = jnp.dot(h.astype(BF16), w_bf[...], preferred_element_type=F32)
    o_ref[...] = proj.astype(o_ref.dtype)

    @pl.when(i >= n_main_blocks)
    def _():
        ot_ref[...] = proj


def _row_specs(d, n_main_blocks, tail_block0, slab):
    if slab:
        shape = (ROW_BLOCK * _slab_pitch(d), LANES)
        return [pl.BlockSpec(shape, lambda i: (i, 0)), pl.BlockSpec(shape, lambda i: (tail_block0, 0))]
    shape = (ROW_BLOCK, d)
    return [pl.BlockSpec(shape, lambda i: (jnp.minimum(i, n_main_blocks - 1), 0)),
            pl.BlockSpec(shape, lambda i: (jnp.maximum(i - n_main_blocks, 0) + tail_block0, 0))]


def _in_proj(x_main, x_tail, tail_block0, slab, g, w, layer, n_rows, n_main):
    d = g.shape[1]
    d_in = w.shape[2]
    nmb = n_main // ROW_BLOCK
    kern = functools.partial(_in_proj_kernel, n_main_blocks=nmb, layer=layer, slab=slab)
    return pl.pallas_call(
        kern,
        grid=(n_rows // ROW_BLOCK,),
        in_specs=_row_specs(d, nmb, tail_block0, slab) + [
            pl.BlockSpec((1, d), lambda i: (0, 0)),
            pl.BlockSpec(memory_space=pl.ANY),
        ],
        out_specs=[pl.BlockSpec((ROW_BLOCK, d_in), lambda i: (i, 0)),
                   pl.BlockSpec((ROW_BLOCK, d_in), lambda i: (jnp.maximum(i - nmb, 0), 0))],
        out_shape=[jax.ShapeDtypeStruct((n_rows, d_in), BF16),
                   jax.ShapeDtypeStruct((n_rows - n_main, d_in), F32)],
        scratch_shapes=[
            pltpu.VMEM((d, d_in), BF16),
            pltpu.VMEM((2, ROW_BLOCK, d_in), F32),
            pltpu.SemaphoreType.DMA((2,)),
        ],
        compiler_params=pltpu.CompilerParams(
            dimension_semantics=("arbitrary",), vmem_limit_bytes=VMEM_LIMIT),
        name="in_proj",
    )(x_main, x_tail, g, w)


def _scan_linear(a, b):
    t = a.shape[0]
    row = lax.broadcasted_iota(I32, a.shape, 0)
    d = 1
    while d < min(t, 8):
        a_sh = pltpu.roll(a, d, axis=0)
        b_sh = pltpu.roll(b, d, axis=0)
        m = row >= d
        b = jnp.where(m, a * b_sh + b, b)
        a = jnp.where(m, a * a_sh, a)
        d *= 2
    while d < t:
        b = jnp.concatenate([b[:d], a[d:] * b[:t - d] + b[d:]], axis=0)
        a = jnp.concatenate([a[:d], a[d:] * a[:t - d]], axis=0)
        d *= 2
    return a, b


def _lru_chunk(xc, g, h_prev, wg, ba, bx, sp):
    pre = jnp.dot(xc.astype(BF16), wg, preferred_element_type=F32)
    r = jax.nn.sigmoid(pre[:, :LANES] + ba)
    ig = jax.nn.sigmoid(pre[:, LANES:] + bx)
    log_a = (-LRU_C * r) * sp
    a = jnp.exp(log_a)
    v = 1.0 - a * a
    bterm = jnp.where(v > 0.0, v * lax.rsqrt(v), 0.0) * ig * xc
    a_cum, h0 = _scan_linear(a, bterm)
    hs = a_cum * h_prev + h0
    return hs * jax.nn.gelu(g), hs


def _rope(x, cos, sin_signed):
    n = x.shape[1] // LANES
    lane = lax.broadcasted_iota(I32, (x.shape[0], LANES), 1)
    first = (lane % HEAD_DIM) < (HEAD_DIM // 2)
    outs = []
    for c in range(n):
        xc = x[:, c * LANES:(c + 1) * LANES]
        swapped = jnp.where(first, pltpu.roll(xc, LANES - HEAD_DIM // 2, axis=1),
                            pltpu.roll(xc, HEAD_DIM // 2, axis=1))
        outs.append(xc * cos + swapped * sin_signed)
    return outs


def _store_head_variants(var_ref, row0, chunks):
    t = chunks[0].shape[0]
    lane = lax.broadcasted_iota(I32, (t, LANES), 1)
    for kc, x in enumerate(chunks):
        swapped = pltpu.roll(x, HALF, axis=1)
        for hh in range(2):
            for p in range(2):
                src = x if p == hh else swapped
                keep = (lane < HALF) if p == 0 else (lane >= HALF)
                var_ref[2 * kc + hh, p, row0:row0 + t, :] = jnp.where(keep, src, 0.0).astype(BF16)


def _attend_block(q_chunks, kvar, vvar, key0, sinks, lim, out_ref, out_rows, out_col0):
    qb = q_chunks[0].shape[0]
    nk = 2 * WINDOW
    gq = (2 * len(q_chunks)) // N_KV_HEADS
    rows = 2 * qb
    qi = lax.broadcasted_iota(I32, (rows, nk), 0) % qb
    kj = lax.broadcasted_iota(I32, (rows, nk), 1)
    valid = ((kj < WINDOW) & (kj > qi + lim)) | ((kj >= WINDOW) & (kj - WINDOW <= qi))
    top = lax.broadcasted_iota(I32, (rows, 1), 0) < qb
    lane = lax.broadcasted_iota(I32, (nk, LANES), 1)
    ones_lo = jnp.where(lane < HALF, 1.0, 0.0).astype(BF16)
    ones_hi = jnp.where(lane >= HALF, 1.0, 0.0).astype(BF16)
    lane_o = lax.broadcasted_iota(I32, (rows, LANES), 1)
    nt = (((1,), (1,)), ((), ()))
    for c in range(N_KV_HEADS):
        c0 = c * gq // 2
        qs = jnp.concatenate([q_chunks[c0], q_chunks[c0 + 1]], axis=0)
        es, sink_terms = [], []
        for p in range(2):
            kc = kvar[c, p, key0:key0 + nk, :]
            s = lax.dot_general(qs, kc, nt, preferred_element_type=F32)
            s = jnp.where(valid, s, -jnp.inf)
            h0 = c * gq + p
            sink = jnp.where(top, sinks[:, h0:h0 + 1], sinks[:, h0 + 2:h0 + 3])
            m = jnp.maximum(jnp.max(s, axis=-1, keepdims=True), sink)
            es.append(jnp.exp2(s - m).astype(BF16))
            sink_terms.append(jnp.exp2(sink - m))
        r0 = jnp.concatenate([vvar[c, 0, key0:key0 + nk, :], ones_lo], axis=1)
        r1 = jnp.concatenate([vvar[c, 1, key0:key0 + nk, :], ones_hi], axis=1)
        od = (jnp.dot(es[0], r0, preferred_element_type=F32)
              + jnp.dot(es[1], r1, preferred_element_type=F32))
        den = od[:, LANES:] + jnp.where(lane_o < HALF, sink_terms[0], sink_terms[1])
        o = (od[:, :LANES] / den).astype(out_ref.dtype)
        out_ref[out_rows, out_col0 + c0 * LANES:out_col0 + (c0 + 1) * LANES] = o[0:qb]
        out_ref[out_rows, out_col0 + (c0 + 1) * LANES:out_col0 + (c0 + 2) * LANES] = o[qb:rows]


def _mixer_kernel(*refs, tile, qblock, is_prompt, pos0, d_lru, d_pool, d_attn, d_kv):
    (proj_ref, cos_ref, sin_ref, convw_ref, convb_ref, wg_ref, ba_ref, bx_ref, lam_ref,
     poolw_ref, pscale_ref, sink_ref) = refs[:12]
    if is_prompt:
        (mix_ref, ko_ref, vo_ref, ho_ref, co_ref, po_ref,
         lru_ext, pool_ext, kvar, vvar, hcar) = refs[12:]
    else:
        (ck_ref, cv_ref, sh_ref, sc_ref, sp_ref,
         mix_ref, ko_ref, vo_ref, ho_ref, co_ref, po_ref,
         lru_ext, pool_ext, kvar, vvar) = refs[12:]

    s = pl.program_id(1) if is_prompt else 0
    o_glru = d_lru
    o_pool = 2 * d_lru
    o_q = o_pool + d_pool
    o_k = o_q + d_attn
    o_v = o_k + d_kv
    n_kc = d_kv // LANES

    if is_prompt:
        @pl.when(s == 0)
        def _():
            lru_ext[0:HALO, :] = jnp.zeros((HALO, d_lru), F32)
            pool_ext[0:HALO, :] = jnp.zeros((HALO, d_pool), F32)
            hcar[...] = jnp.zeros_like(hcar)
            kvar[:, :, 0:WINDOW, :] = jnp.zeros((N_KV_HEADS, 2, WINDOW, LANES), BF16)
            vvar[:, :, 0:WINDOW, :] = jnp.zeros((N_KV_HEADS, 2, WINDOW, LANES), BF16)
        h_prev = hcar[0:1, :]
    else:
        lru_ext[0:HALO, :] = jnp.zeros((HALO, d_lru), F32)
        pool_ext[0:HALO, :] = jnp.zeros((HALO, d_pool), F32)
        lru_ext[HALO - (CONV_WIDTH - 1):HALO, :] = sc_ref[0]
        pool_ext[HALO - POOL_PAD:HALO, :] = sp_ref[0]
        h_prev = sh_ref[0]
        kvar[:, :, WINDOW:2 * WINDOW, :] = jnp.zeros((N_KV_HEADS, 2, WINDOW, LANES), BF16)
        vvar[:, :, WINDOW:2 * WINDOW, :] = jnp.zeros((N_KV_HEADS, 2, WINDOW, LANES), BF16)
        _store_head_variants(kvar, 0, [ck_ref[0, :, c * LANES:(c + 1) * LANES] for c in range(n_kc)])
        _store_head_variants(vvar, 0, [cv_ref[0, :, c * LANES:(c + 1) * LANES] for c in range(n_kc)])

    def proj(rows, c0, c1):
        return proj_ref[rows, c0:c1].astype(F32)

    every = slice(None)
    lru_ext[HALO:HALO + tile, :] = proj(every, 0, d_lru)
    pool_ext[HALO:HALO + tile, :] = proj(every, o_pool, o_pool + d_pool)

    xc = convb_ref[...] + convw_ref[0:1, :] * lru_ext[HALO - 3:HALO - 3 + tile, :]
    for j in range(1, CONV_WIDTH):
        xc = xc + convw_ref[j:j + 1, :] * lru_ext[HALO - 3 + j:HALO - 3 + j + tile, :]
    neg = -lam_ref[...]
    sp_all = jnp.maximum(neg, 0.0) + jnp.log1p(jnp.exp(-jnp.abs(neg)))
    h_last = []
    for c in range(d_lru // LANES):
        cs = slice(c * LANES, (c + 1) * LANES)
        y, hs = _lru_chunk(xc[:, cs], proj(every, o_glru + c * LANES, o_glru + (c + 1) * LANES),
                           h_prev[:, cs], wg_ref[c], ba_ref[:, cs], bx_ref[:, cs], sp_all[:, cs])
        mix_ref[:, cs] = y.astype(mix_ref.dtype)
        h_last.append(hs[tile - 1:tile, :])
    h_last = jnp.concatenate(h_last, axis=-1)

    row = lax.broadcasted_iota(I32, (tile, LANES), 0)
    pos = pos0 + s * tile + row
    for gi, w in enumerate(POOL_WINDOWS):
        cs = slice(gi * LANES, (gi + 1) * LANES)
        e = pool_ext[:, cs]
        acc = e
        step = 1
        while step < w:
            acc = acc + pltpu.roll(acc, step, axis=0)
            step *= 2
        cnt = jnp.minimum(pos + 1, w).astype(F32)
        dlt = acc[HALO:HALO + tile, :] / cnt - e[HALO:HALO + tile, :]
        y = jnp.dot(dlt.astype(BF16), poolw_ref[gi], preferred_element_type=F32) * pscale_ref[:, cs]
        mix_ref[:, d_lru + gi * LANES:d_lru + (gi + 1) * LANES] = y.astype(mix_ref.dtype)

    sinks = sink_ref[...] * LOG2_E
    o_attn = d_lru + d_pool
    cos = cos_ref[...]
    sin = sin_ref[...]
    q_chunks = [(qc * (ATTN_SCALE * LOG2_E)).astype(BF16)
                for qc in _rope(proj(every, o_q, o_q + d_attn), cos, sin)]
    k_rot = _rope(proj(every, o_k, o_k + d_kv), cos, sin)
    _store_head_variants(kvar, WINDOW, k_rot)
    _store_head_variants(vvar, WINDOW, [proj(every, o_v + c * LANES, o_v + (c + 1) * LANES)
                                        for c in range(n_kc)])
    for blk in range(tile // qblock):
        rs = slice(blk * qblock, (blk + 1) * qblock)
        if is_prompt and blk == 0:
            lim = jnp.where(s == 0, WINDOW, 0)
        else:
            lim = 0
        _attend_block([qc[rs] for qc in q_chunks], kvar, vvar, blk * qblock, sinks, lim,
                      mix_ref, rs, o_attn)

    k_last = jnp.concatenate([kc[tile - qblock:tile] for kc in k_rot], axis=-1)
    if is_prompt:
        lru_ext[0:HALO, :] = lru_ext[tile:tile + HALO, :]
        pool_ext[0:HALO, :] = pool_ext[tile:tile + HALO, :]
        hcar[0:1, :] = h_last
        kvar[:, :, 0:WINDOW, :] = kvar[:, :, tile:tile + WINDOW, :]
        vvar[:, :, 0:WINDOW, :] = vvar[:, :, tile:tile + WINDOW, :]

        @pl.when(s == pl.num_programs(1) - 1)
        def _():
            ko_ref[0] = k_last
            vo_ref[0] = proj(slice(tile - qblock, tile), o_v, o_v + d_kv)
            ho_ref[0] = h_last
            co_ref[0] = lru_ext[HALO + tile - (CONV_WIDTH - 1):HALO + tile, :]
            po_ref[0] = pool_ext[HALO + tile - POOL_PAD:HALO + tile, :]
    else:
        ko_ref[0, 0:WINDOW - tile, :] = ck_ref[0, tile:WINDOW, :]
        ko_ref[0, WINDOW - tile:WINDOW, :] = k_last
        vo_ref[0, 0:WINDOW - tile, :] = cv_ref[0, tile:WINDOW, :]
        vo_ref[0, WINDOW - tile:WINDOW, :] = proj(every, o_v, o_v + d_kv)
        ho_ref[0] = h_last
        co_ref[0] = lru_ext[HALO + tile - (CONV_WIDTH - 1):HALO + tile, :]
        po_ref[0] = pool_ext[HALO + tile - POOL_PAD:HALO + tile, :]


def _layer_consts(p, dims):
    d_lru, d_pool = dims["d_lru"], dims["d_pool"]
    hd = d_lru // LRU_HEADS
    per = LANES // hd
    nchunk = d_lru // LANES

    def blockdiag(w):
        w = w.reshape(nchunk, per, hd, hd)
        eye = jnp.eye(per, dtype=w.dtype)
        return jnp.einsum("cpij,pq->cpiqj", w, eye).reshape(nchunk, LANES, LANES)

    wg = jnp.concatenate([blockdiag(p["lru_wa"]), blockdiag(p["lru_wx"])], axis=-1).astype(BF16)
    return dict(
        convw=p["conv_w"], convb=p["conv_b"].reshape(1, d_lru), wg=wg,
        ba=p["lru_ba"].reshape(1, d_lru), bx=p["lru_bx"].reshape(1, d_lru),
        lam=p["lru_lambda"].reshape(1, d_lru), poolw=p["pool_w"].astype(BF16),
        pscale=p["pool_scale"].reshape(1, d_pool), sinks=p["attn_sinks"].reshape(1, -1))


def _rope_tables(pos):
    half = HEAD_DIM // 2
    inv = ROPE_THETA ** (-jnp.arange(half, dtype=F32) / half)
    ang = pos.astype(F32)[:, None] * inv[None, :]
    cos = jnp.cos(ang)
    sin = jnp.sin(ang)
    cos2 = jnp.concatenate([cos, cos], axis=-1)
    sin2 = jnp.concatenate([-sin, sin], axis=-1)
    reps = LANES // HEAD_DIM
    return jnp.tile(cos2, (1, reps)), jnp.tile(sin2, (1, reps))


CONST_NAMES = ("convw", "convb", "wg", "ba", "bx", "lam", "poolw", "pscale", "sinks")


def _const_specs(consts):
    return [pl.BlockSpec(consts[n].shape, functools.partial(lambda nd, *_: (0,) * nd, consts[n].ndim))
            for n in CONST_NAMES]


def _mixer_prompt(proj, consts, cos, sin, bsz, seq, dims, tile):
    d_lru, d_pool, d_attn, d_kv = dims["d_lru"], dims["d_pool"], dims["d_attn"], dims["d_kv"]
    d_in = proj.shape[1]
    d_mix = d_lru + d_pool + d_attn
    ns = seq // tile
    kern = functools.partial(_mixer_kernel, tile=tile, qblock=WINDOW, is_prompt=True, pos0=0,
                             d_lru=d_lru, d_pool=d_pool, d_attn=d_attn, d_kv=d_kv)
    return pl.pallas_call(
        kern,
        grid=(bsz, ns),
        in_specs=[
            pl.BlockSpec((tile, d_in), lambda b, s: (b * ns + s, 0)),
            pl.BlockSpec((tile, LANES), lambda b, s: (s, 0)),
            pl.BlockSpec((tile, LANES), lambda b, s: (s, 0)),
        ] + _const_specs(consts),
        out_specs=[
            pl.BlockSpec((tile, d_mix), lambda b, s: (b * ns + s, 0)),
            pl.BlockSpec((1, WINDOW, d_kv), lambda b, s: (b, 0, 0)),
            pl.BlockSpec((1, WINDOW, d_kv), lambda b, s: (b, 0, 0)),
            pl.BlockSpec((1, 1, d_lru), lambda b, s: (b, 0, 0)),
            pl.BlockSpec((1, CONV_WIDTH - 1, d_lru), lambda b, s: (b, 0, 0)),
            pl.BlockSpec((1, POOL_PAD, d_pool), lambda b, s: (b, 0, 0)),
        ],
        out_shape=[
            jax.ShapeDtypeStruct((bsz * seq, d_mix), BF16),
            jax.ShapeDtypeStruct((bsz, WINDOW, d_kv), F32),
            jax.ShapeDtypeStruct((bsz, WINDOW, d_kv), F32),
            jax.ShapeDtypeStruct((bsz, 1, d_lru), F32),
            jax.ShapeDtypeStruct((bsz, CONV_WIDTH - 1, d_lru), F32),
            jax.ShapeDtypeStruct((bsz, POOL_PAD, d_pool), F32),
        ],
        scratch_shapes=[
            pltpu.VMEM((tile + HALO, d_lru), F32),
            pltpu.VMEM((tile + HALO, d_pool), F32),
            pltpu.VMEM((N_KV_HEADS, 2, WINDOW + tile, LANES), BF16),
            pltpu.VMEM((N_KV_HEADS, 2, WINDOW + tile, LANES), BF16),
            pltpu.VMEM((8, d_lru), F32),
        ],
        compiler_params=pltpu.CompilerParams(
            dimension_semantics=("arbitrary", "arbitrary"), vmem_limit_bytes=VMEM_LIMIT),
        name="mixer_prompt",
    )(proj, cos, sin, *[consts[n] for n in CONST_NAMES])


def _mixer_sample_kernel(*refs, n_seq, t, d_lru, d_pool, d_attn, d_kv):
    n_in = 3 + len(CONST_NAMES)
    proj_ref, shared = refs[0], refs[1:n_in]
    state_in = refs[n_in:n_in + 5]
    mix_ref = refs[n_in + 5]
    state_out = refs[n_in + 6:n_in + 11]
    scratch = refs[n_in + 11:]
    for q in range(n_seq):
        one = lambda r: r.at[pl.ds(q, 1)]
        _mixer_kernel(proj_ref.at[pl.ds(q * t, t)], *shared, *[one(r) for r in state_in],
                      mix_ref.at[pl.ds(q * t, t)], *[one(r) for r in state_out],
                      *[r.at[q] for r in scratch],
                      tile=t, qblock=t, is_prompt=False, pos0=PAST_LEN,
                      d_lru=d_lru, d_pool=d_pool, d_attn=d_attn, d_kv=d_kv)


def _mixer_sample(proj, row0, consts, cos, sin, cache_k, cache_v, st_h, st_conv, st_pool, layer, dims):
    d_lru, d_pool, d_attn, d_kv = dims["d_lru"], dims["d_pool"], dims["d_attn"], dims["d_kv"]
    d_in = proj.shape[1]
    d_mix = d_lru + d_pool + d_attn
    db, win = cache_k.shape[1], cache_k.shape[2]
    t = cos.shape[0]
    nq = SAMPLE_SEQS_PER_STEP
    assert db % nq == 0 and row0 % (nq * t) == 0
    blk0 = row0 // (nq * t)
    kern = functools.partial(_mixer_sample_kernel, n_seq=nq, t=t,
                             d_lru=d_lru, d_pool=d_pool, d_attn=d_attn, d_kv=d_kv)
    return pl.pallas_call(
        kern,
        grid=(db // nq,),
        in_specs=[
            pl.BlockSpec((nq * t, d_in), lambda b: (blk0 + b, 0)),
            pl.BlockSpec((t, LANES), lambda b: (0, 0)),
            pl.BlockSpec((t, LANES), lambda b: (0, 0)),
        ] + _const_specs(consts) + [
            pl.BlockSpec((None, nq, win, d_kv), lambda b: (layer, b, 0, 0)),
            pl.BlockSpec((None, nq, win, d_kv), lambda b: (layer, b, 0, 0)),
            pl.BlockSpec((None, nq, 1, d_lru), lambda b: (layer, b, 0, 0)),
            pl.BlockSpec((None, nq, CONV_WIDTH - 1, d_lru), lambda b: (layer, b, 0, 0)),
            pl.BlockSpec((None, nq, POOL_PAD, d_pool), lambda b: (layer, b, 0, 0)),
        ],
        out_specs=[
            pl.BlockSpec((nq * t, d_mix), lambda b: (b, 0)),
            pl.BlockSpec((nq, win, d_kv), lambda b: (b, 0, 0)),
            pl.BlockSpec((nq, win, d_kv), lambda b: (b, 0, 0)),
            pl.BlockSpec((nq, 1, d_lru), lambda b: (b, 0, 0)),
            pl.BlockSpec((nq, CONV_WIDTH - 1, d_lru), lambda b: (b, 0, 0)),
            pl.BlockSpec((nq, POOL_PAD, d_pool), lambda b: (b, 0, 0)),
        ],
        out_shape=[
            jax.ShapeDtypeStruct((db * t, d_mix), F32),
            jax.ShapeDtypeStruct((db, win, d_kv), F32),
            jax.ShapeDtypeStruct((db, win, d_kv), F32),
            jax.ShapeDtypeStruct((db, 1, d_lru), F32),
            jax.ShapeDtypeStruct((db, CONV_WIDTH - 1, d_lru), F32),
            jax.ShapeDtypeStruct((db, POOL_PAD, d_pool), F32),
        ],
        scratch_shapes=[
            pltpu.VMEM((nq, t + HALO, d_lru), F32),
            pltpu.VMEM((nq, t + HALO, d_pool), F32),
            pltpu.VMEM((nq, N_KV_HEADS, 2, 2 * WINDOW, LANES), BF16),
            pltpu.VMEM((nq, N_KV_HEADS, 2, 2 * WINDOW, LANES), BF16),
        ],
        compiler_params=pltpu.CompilerParams(
            dimension_semantics=("arbitrary",), vmem_limit_bytes=VMEM_LIMIT),
        name="mixer_sample",
    )(proj, cos, sin, *[consts[n] for n in CONST_NAMES], cache_k, cache_v, st_h, st_conv, st_pool)


def _route(logits, run_cnt):
    t = logits.shape[0]
    lane = lax.broadcasted_iota(I32, (t, LANES), 1)
    lane_f = lane.astype(F32)
    ninf = -jnp.inf
    big = float(LANES)
    is_g = lane < N_EXPERT_GROUPS
    lg = jnp.where(is_g, logits, ninf)
    mg = jnp.max(lg, axis=-1, keepdims=True)
    g_top = jnp.min(jnp.where(lg == mg, lane_f, big), axis=-1, keepdims=True).astype(I32)
    pg_top = 1.0 / jnp.sum(jnp.exp(lg - mg), axis=-1, keepdims=True)
    base = N_EXPERT_GROUPS + EXPERTS_PER_GROUP * g_top
    in_grp = (lane >= base) & (lane < base + EXPERTS_PER_GROUP)
    le = jnp.where(in_grp, logits, ninf)
    m1 = jnp.max(le, axis=-1, keepdims=True)
    i1 = jnp.min(jnp.where(le == m1, lane_f, big), axis=-1, keepdims=True).astype(I32)
    le2 = jnp.where(lane == i1, ninf, le)
    m2 = jnp.max(le2, axis=-1, keepdims=True)
    i2 = jnp.min(jnp.where((le2 == m2) & in_grp & (lane != i1), lane_f, big),
                 axis=-1, keepdims=True).astype(I32)
    se = jnp.sum(jnp.exp(le - m1), axis=-1, keepdims=True)
    p1 = 1.0 / se
    p2 = jnp.exp(m2 - m1) / se
    tot = p1 + p2
    w1 = (p1 / tot) * pg_top
    w2 = (p2 / tot) * pg_top
    a = i1 - base
    b = i2 - base
    lo = jnp.minimum(a, b)
    hi = jnp.maximum(a, b)
    w_lo = jnp.where(a < b, w1, w2)
    w_hi = jnp.where(a < b, w2, w1)
    pid = jnp.where(lo == 0, hi - 1, jnp.where(lo == 1, jnp.where(hi == 3, 3, 4), 5))
    swap = pid == 5
    w_a = jnp.where(swap, w_hi, w_lo)
    w_b = jnp.where(swap, w_lo, w_hi)
    cls = g_top * N_PAIRS + pid
    onehot = lane == cls
    ti = lax.broadcasted_iota(I32, (t, t), 0)
    tj = lax.broadcasted_iota(I32, (t, t), 1)
    lower = jnp.where(tj <= ti, 1.0, 0.0).astype(BF16)
    prefix = jnp.dot(lower, jnp.where(onehot, 1.0, 0.0).astype(BF16), preferred_element_type=F32)
    rank = jnp.sum(jnp.where(onehot, prefix - 1.0 + run_cnt, 0.0), axis=-1, keepdims=True)
    info = jnp.where(lane == 0, w_a, jnp.where(lane == 1, w_b, jnp.where(
        lane == 2, cls.astype(F32), jnp.where(lane == 3, rank, 0.0))))
    return info, run_cnt + prefix[t - 1:t, :]


def _out_proj_kernel(mixp_ref, mixs_ref, xm_ref, xt_ref, w_hbm, g_ref, wr_ref, br_ref,
                     o_ref, info_ref, cnt_ref, w_bf, stage, sem, run_cnt, *, n_main_blocks, d, layer, slab):
    i = pl.program_id(0)

    @pl.when(i == 0)
    def _():
        _load_weight_bf16(w_hbm.at[layer], w_bf, stage, sem)
        run_cnt[...] = jnp.zeros_like(run_cnt)

    mix = jnp.where(i < n_main_blocks, mixp_ref[...], mixs_ref[...].astype(BF16))
    xres = (_pick_rows(i, n_main_blocks, xm_ref, xt_ref, d, slab)
            + jnp.dot(mix, w_bf[...], preferred_element_type=F32))
    h2 = _rms(xres, g_ref[...])
    logits = jnp.dot(h2.astype(BF16), wr_ref[...], preferred_element_type=F32) + br_ref[...]
    info, cnt = _route(logits, run_cnt[0:1, :])
    run_cnt[0:1, :] = cnt
    pitch = _slab_pitch(d)
    _write_slabs(o_ref, (), xres, pitch)
    o_ref[pl.ds(d // LANES, ROW_BLOCK, stride=pitch), :] = info
    info_ref[...] = info
    cnt_ref[...] = jnp.broadcast_to(cnt, cnt_ref.shape)


def _out_proj(mix_p, mix_s, x_main, x_tail, tail_block0, slab, w, layer, g, wr, br, n_rows):
    d = g.shape[1]
    d_mix = w.shape[1]
    npb = mix_p.shape[0] // ROW_BLOCK
    pitch = _slab_pitch(d)
    kern = functools.partial(_out_proj_kernel, n_main_blocks=npb, d=d, layer=layer, slab=slab)
    return pl.pallas_call(
        kern,
        grid=(n_rows // ROW_BLOCK,),
        in_specs=[
            pl.BlockSpec((ROW_BLOCK, d_mix), lambda i: (jnp.minimum(i, npb - 1), 0)),
            pl.BlockSpec((ROW_BLOCK, d_mix), lambda i: (jnp.maximum(i - npb, 0), 0)),
        ] + _row_specs(d, npb, tail_block0, slab) + [
            pl.BlockSpec(memory_space=pl.ANY),
            pl.BlockSpec((1, d), lambda i: (0, 0)),
            pl.BlockSpec((d, LANES), lambda i: (0, 0)),
            pl.BlockSpec((1, LANES), lambda i: (0, 0)),
        ],
        out_specs=[pl.BlockSpec((ROW_BLOCK * pitch, LANES), lambda i: (i, 0)),
                   pl.BlockSpec((ROW_BLOCK, LANES), lambda i: (i, 0)),
                   pl.BlockSpec((8, LANES), lambda i: (0, 0))],
        out_shape=[jax.ShapeDtypeStruct((n_rows * pitch, LANES), F32),
                   jax.ShapeDtypeStruct((n_rows, LANES), F32),
                   jax.ShapeDtypeStruct((8, LANES), F32)],
        scratch_shapes=[
            pltpu.VMEM((d_mix, d), BF16),
            pltpu.VMEM((2, ROW_BLOCK, d), F32),
            pltpu.SemaphoreType.DMA((2,)),
            pltpu.VMEM((8, LANES), F32),
        ],
        compiler_params=pltpu.CompilerParams(
            dimension_semantics=("arbitrary",), vmem_limit_bytes=VMEM_LIMIT),
        name="out_proj",
    )(mix_p, mix_s, x_main, x_tail, w, g, wr, br)


def _moe_kernel(cls_ref, rank_ref, off_ref, cnt_ref, ea_ref, eb_ref, chga_ref, chgb_ref, nch_ref,
                xr_hbm, g_ref, w1a_ref, w3a_ref, w2a_ref, w1b_ref, w3b_ref, w2b_ref,
                xo_hbm,
                perm, xbuf, obuf, wa1, wa3, wa2, wb1, wb3, wb2, gsem, ssem, *, d, n_rows):
    del ea_ref, eb_ref
    m = MOE_CHUNK
    pitch = _slab_pitch(d)
    n_slabs = d // LANES
    c = pl.program_id(0)
    n = nch_ref[0]
    slot = c % 2
    other = 1 - slot

    def start_gather(chunk, sl, j):
        tok = jnp.maximum(perm[(chunk + 1) * m + j], 0)
        pltpu.make_async_copy(xr_hbm.at[pl.ds(tok * pitch, pitch)],
                              xbuf.at[sl, pl.ds(j * pitch, pitch)], gsem.at[sl]).start()

    def start_scatter(chunk, sl, j):
        tok = perm[(chunk + 1) * m + j]
        r = jnp.where(tok < 0, n_rows + sl * m + j, tok)
        pltpu.make_async_copy(obuf.at[sl, pl.ds(j * pitch, pitch)],
                              xo_hbm.at[pl.ds(r * pitch, pitch)], ssem.at[sl]).start()

    def wait_gather(sl):
        pltpu.make_async_copy(xr_hbm.at[pl.ds(0, m * pitch)], xbuf.at[sl], gsem.at[sl]).wait()

    def wait_scatter(sl):
        pltpu.make_async_copy(obuf.at[sl], xo_hbm.at[pl.ds(0, m * pitch)], ssem.at[sl]).wait()

    @pl.when(c == 0)
    def _():
        obuf[...] = jnp.zeros_like(obuf)
        pad = pltpu.make_async_copy(obuf.at[0], xo_hbm.at[pl.ds(n_rows * pitch, m * pitch)], ssem.at[0])
        pad.start()
        pad.wait()

        def fill(lo, hi):
            def body(i, z):
                perm[i] = -1
                return z
            lax.fori_loop(lo, hi, body, 0)

        def first_gather(j, z):
            start_gather(0, 0, j)
            return z

        fill(0, m)
        fill((n + 1) * m, (n + 3) * m)
        for k in range(N_CLASSES):
            base = (off_ref[k] + 1) * m
            cnt = cnt_ref[k]
            fill(base + cnt, base + ((cnt + m - 1) // m) * m)

        def place(t, z):
            perm[(off_ref[cls_ref[t]] + 1) * m + rank_ref[t]] = t
            return z
        lax.fori_loop(0, n_rows, place, 0, unroll=8)
        lax.fori_loop(0, m, first_gather, 0)

    @pl.when(c <= n)
    def _():
        wait_gather(slot)

        @pl.when(c >= 1)
        def _():
            wait_scatter(slot)

        @pl.when(chga_ref[c] == 1)
        def _():
            wa1[...] = w1a_ref[0].astype(BF16)
            wa3[...] = w3a_ref[0].astype(BF16)
            wa2[...] = w2a_ref[0].astype(BF16)

        @pl.when(chgb_ref[c] == 1)
        def _():
            wb1[...] = w1b_ref[0].astype(BF16)
            wb3[...] = w3b_ref[0].astype(BF16)
            wb2[...] = w2b_ref[0].astype(BF16)

        for j in range(m):
            start_gather(c + 1, other, j)
        for j in range(m):
            start_scatter(c - 1, other, j)
        x = _read_slabs(xbuf, (slot,), m, n_slabs, pitch)
        info = xbuf[slot, pl.ds(n_slabs, m, stride=pitch), :]
        w_a = info[:, 0:1]
        w_b = info[:, 1:2]
        h = _rms(x, g_ref[...]).astype(BF16)
        hid_a = (jax.nn.silu(jnp.dot(h, wa1[...], preferred_element_type=F32))
                 * jnp.dot(h, wa3[...], preferred_element_type=F32) * w_a).astype(BF16)
        hid_b = (jax.nn.silu(jnp.dot(h, wb1[...], preferred_element_type=F32))
                 * jnp.dot(h, wb3[...], preferred_element_type=F32) * w_b).astype(BF16)
        y = (jnp.dot(hid_a, wa2[...], preferred_element_type=F32)
             + jnp.dot(hid_b, wb2[...], preferred_element_type=F32))
        _write_slabs(obuf, (slot,), x + y, pitch)

        @pl.when(c == n)
        def _():
            wait_gather(other)
            wait_scatter(other)


def _moe(xr, g, w1, w3, w2, tables, n_rows, c_max):
    d = g.shape[1]
    f = w1.shape[2]
    m = MOE_CHUNK
    pitch = _slab_pitch(d)
    kern = functools.partial(_moe_kernel, d=d, n_rows=n_rows)

    def wspec(shape, which):
        return pl.BlockSpec(shape, lambda c, *pref: (pref[which][c], 0, 0))

    grid_spec = pltpu.PrefetchScalarGridSpec(
        num_scalar_prefetch=9,
        grid=(c_max + 1,),
        in_specs=[
            pl.BlockSpec(memory_space=pl.ANY),
            pl.BlockSpec((1, d), lambda c, *pref: (0, 0)),
            wspec((1, d, f), 4), wspec((1, d, f), 4), wspec((1, f, d), 4),
            wspec((1, d, f), 5), wspec((1, d, f), 5), wspec((1, f, d), 5),
        ],
        out_specs=pl.BlockSpec(memory_space=pl.ANY),
        scratch_shapes=[
            pltpu.SMEM(((c_max + 3) * m,), I32),
            pltpu.VMEM((2, m * pitch, LANES), F32),
            pltpu.VMEM((2, m * pitch, LANES), F32),
            pltpu.VMEM((d, f), BF16), pltpu.VMEM((d, f), BF16), pltpu.VMEM((f, d), BF16),
            pltpu.VMEM((d, f), BF16), pltpu.VMEM((d, f), BF16), pltpu.VMEM((f, d), BF16),
            pltpu.SemaphoreType.DMA((2,)),
            pltpu.SemaphoreType.DMA((2,)),
        ],
    )
    return pl.pallas_call(
        kern,
        grid_spec=grid_spec,
        out_shape=jax.ShapeDtypeStruct(((n_rows + 2 * m) * pitch, LANES), F32),
        compiler_params=pltpu.CompilerParams(
            dimension_semantics=("arbitrary",), vmem_limit_bytes=VMEM_LIMIT),
        name="moe",
    )(*tables, xr, g, w1, w3, w2, w1, w3, w2)


def _chunk_tables(counts, c_max, expert0):
    m = MOE_CHUNK
    nch_c = (counts + m - 1) // m
    ch_end = jnp.cumsum(nch_c)
    ch_off = ch_end - nch_c
    n_chunks = ch_end[-1]
    chunk = jnp.minimum(jnp.arange(c_max + 1, dtype=I32), jnp.maximum(n_chunks - 1, 0))
    ccls = jnp.sum((chunk[:, None] >= ch_end[None, :]).astype(I32), axis=1)
    ccls = jnp.minimum(ccls, N_CLASSES - 1)
    grp = ccls // N_PAIRS
    pid = ccls % N_PAIRS
    ea = expert0 + grp * EXPERTS_PER_GROUP + jnp.asarray(PAIR_SLOT_A, I32)[pid]
    eb = expert0 + grp * EXPERTS_PER_GROUP + jnp.asarray(PAIR_SLOT_B, I32)[pid]
    first = jnp.ones((1,), I32)
    chga = jnp.concatenate([first, (ea[1:] != ea[:-1]).astype(I32)])
    chgb = jnp.concatenate([first, (eb[1:] != eb[:-1]).astype(I32)])
    return ch_off.astype(I32), counts, ea, eb, chga, chgb, n_chunks.reshape(1).astype(I32)


def _final_norm_kernel(x_ref, g_ref, op_ref, os_ref, *, n_prompt_blocks):
    i = pl.program_id(0)
    d = g_ref.shape[1]
    y = _rms(_read_slabs(x_ref, (), ROW_BLOCK, d // LANES, _slab_pitch(d)), g_ref[...])

    @pl.when(i < n_prompt_blocks)
    def _():
        op_ref[...] = y

    @pl.when(i >= n_prompt_blocks)
    def _():
        os_ref[...] = y


def _final_norm(x, g, n_prompt, n_sample):
    d = g.shape[1]
    npb = n_prompt // ROW_BLOCK
    nsb = n_sample // ROW_BLOCK
    kern = functools.partial(_final_norm_kernel, n_prompt_blocks=npb)
    return pl.pallas_call(
        kern,
        grid=(npb + nsb,),
        in_specs=[pl.BlockSpec((ROW_BLOCK * _slab_pitch(d), LANES), lambda i: (i, 0)),
                  pl.BlockSpec((1, d), lambda i: (0, 0))],
        out_specs=[pl.BlockSpec((ROW_BLOCK, d), lambda i: (jnp.minimum(i, npb - 1), 0)),
                   pl.BlockSpec((ROW_BLOCK, d), lambda i: (jnp.maximum(i - npb, 0), 0))],
        out_shape=[jax.ShapeDtypeStruct((n_prompt, d), F32),
                   jax.ShapeDtypeStruct((n_sample, d), F32)],
        compiler_params=pltpu.CompilerParams(dimension_semantics=("arbitrary",)),
        name="final_norm",
    )(x, g)


def kernel(x_prompt, x_sample, cache_k, cache_v, state_lru_h, state_conv, state_pool, norm1_g, w_in, conv_w, conv_b, lru_wa, lru_ba, lru_wx, lru_bx, lru_lambda, pool_w, pool_scale, attn_sinks, w_out, norm2_g, router_group_w, router_group_b, router_expert_w, router_expert_b, expert_w1, expert_w3, expert_w2, final_norm_g):
    bsz, seq, d = x_prompt.shape
    db, ds, _ = x_sample.shape
    depth = w_in.shape[0]
    win = cache_k.shape[2]
    nkv = cache_k.shape[3]
    d_lru = lru_lambda.shape[1]
    d_pool = pool_scale.shape[1]
    d_kv = nkv * cache_k.shape[4]
    d_attn = attn_sinks.shape[1] * HEAD_DIM
    dims = dict(d_lru=d_lru, d_pool=d_pool, d_attn=d_attn, d_kv=d_kv)
    n_prompt = bsz * seq
    n_sample = db * ds
    n_rows = n_prompt + n_sample
    assert n_prompt % ROW_BLOCK == 0 and n_sample % ROW_BLOCK == 0
    assert win == WINDOW and nkv == N_KV_HEADS and ds <= 8 and WINDOW % ds == 0
    tile = 512 if seq % 512 == 0 else WINDOW
    c_max = -(-n_rows // MOE_CHUNK) + N_CLASSES
    npb = n_prompt // ROW_BLOCK

    cos_p, sin_p = _rope_tables(jnp.arange(seq))
    cos_s, sin_s = _rope_tables(PAST_LEN + jnp.arange(ds))
    wr_all = jnp.concatenate([router_group_w, router_expert_w], axis=2)
    wr_all = jnp.pad(wr_all, ((0, 0), (0, 0), (0, LANES - wr_all.shape[2]))).astype(BF16)
    br_all = jnp.concatenate([router_group_b, router_expert_b], axis=1)
    br_all = jnp.pad(br_all, ((0, 0), (0, LANES - br_all.shape[1])))

    x_main, x_tail, tail0, slab = x_prompt.reshape(n_prompt, d), x_sample.reshape(n_sample, d), 0, False
    ck = cache_k.reshape(depth, db, win, d_kv)
    cv = cache_v.reshape(depth, db, win, d_kv)
    st_h = state_lru_h.reshape(depth, db, 1, d_lru)
    n_exp, _, d_exp = expert_w1.shape[1:]
    w1 = expert_w1.reshape(depth * n_exp, d, d_exp)
    w3 = expert_w3.reshape(depth * n_exp, d, d_exp)
    w2 = expert_w2.reshape(depth * n_exp, d_exp, d)
    outs = [[] for _ in range(10)]
    for l in range(depth):
        p = dict(conv_w=conv_w[l], conv_b=conv_b[l], lru_wa=lru_wa[l], lru_ba=lru_ba[l],
                 lru_wx=lru_wx[l], lru_bx=lru_bx[l], lru_lambda=lru_lambda[l], pool_w=pool_w[l],
                 pool_scale=pool_scale[l], attn_sinks=attn_sinks[l])
        consts = _layer_consts(p, dims)
        proj, proj_s = _in_proj(x_main, x_tail, tail0, slab, norm1_g[l].reshape(1, d), w_in, l, n_rows, n_prompt)
        mix_p, pk, pv, ph, pc, pp = _mixer_prompt(proj, consts, cos_p, sin_p, bsz, seq, dims, tile)
        mix_s, sk, sv, sh, sc, sp = _mixer_sample(
            proj_s, 0, consts, cos_s, sin_s, ck, cv, st_h, state_conv, state_pool, l, dims)
        xr, info, cnt = _out_proj(mix_p, mix_s, x_main, x_tail, tail0, slab, w_out, l,
                                  norm2_g[l].reshape(1, d), wr_all[l], br_all[l:l + 1], n_rows)
        route = info[:, 2:4].astype(I32)
        tables = (route[:, 0], route[:, 1]) + _chunk_tables(cnt[0, :N_CLASSES].astype(I32), c_max, l * n_exp)
        x = _moe(xr, norm2_g[l].reshape(1, d), w1, w3, w2, tables, n_rows, c_max)
        x_main, x_tail, tail0, slab = x, x, npb, True
        for lst, val in zip(outs, (pk.reshape(bsz, WINDOW, nkv, HEAD_DIM), pv.reshape(bsz, WINDOW, nkv, HEAD_DIM),
                                   ph.reshape(bsz, d_lru), pc, pp,
                                   sk.reshape(db, win, nkv, HEAD_DIM), sv.reshape(db, win, nkv, HEAD_DIM),
                                   sh.reshape(db, d_lru), sc, sp)):
            lst.append(val)
    y_p, y_s = _final_norm(x_main, final_norm_g.reshape(1, d), n_prompt, n_sample)
    return (y_p.reshape(bsz, seq, d), y_s.reshape(db, ds, d)) + tuple(jnp.stack(o) for o in outs)
```

```python
import functools

import jax
import jax.numpy as jnp
from jax import lax
from jax.experimental import pallas as pl
from jax.experimental.pallas import tpu as pltpu

F32 = jnp.float32
BF16 = jnp.bfloat16
I32 = jnp.int32

LRU_HEADS = 8
CONV_WIDTH = 4
LRU_C = 8.0
POOL_WINDOWS = (2, 4, 8, 16)
POOL_PAD = max(POOL_WINDOWS) - 1
HEAD_DIM = 64
N_KV_HEADS = 4
WINDOW = 128
ROPE_THETA = 10000.0
ATTN_SCALE = HEAD_DIM ** -0.5
LOG2_E = 1.4426950408889634
N_EXPERT_GROUPS = 4
EXPERTS_PER_GROUP = 4
N_EXPERTS = N_EXPERT_GROUPS * EXPERTS_PER_GROUP
RMS_EPS = 1e-6
PAST_LEN = 16384

LANES = 128
HALF = LANES // 2
HALO = 16
ROW_BLOCK = 256
SAMPLE_SEQS_PER_STEP = 4
MOE_CHUNK = 256
PAIR_SLOT_A = (0, 0, 0, 1, 1, 3)
PAIR_SLOT_B = (1, 2, 3, 3, 2, 2)
N_PAIRS = len(PAIR_SLOT_A)
N_CLASSES = N_EXPERT_GROUPS * N_PAIRS
VMEM_LIMIT = 52 * 1024 * 1024


def _rms(x, g):
    return (x * lax.rsqrt(jnp.mean(x * x, axis=-1, keepdims=True) + RMS_EPS)) * g


def _load_weight_bf16(w_hbm, w_bf, stage, sem):
    rows = stage.shape[1]
    n = w_hbm.shape[0] // rows

    def copy(i, sl):
        return pltpu.make_async_copy(w_hbm.at[pl.ds(i * rows, rows)], stage.at[sl], sem.at[sl])

    copy(0, 0).start()
    for i in range(n):
        sl = i % 2
        if i + 1 < n:
            copy(i + 1, 1 - sl).start()
        copy(i, sl).wait()
        w_bf[i * rows:(i + 1) * rows, :] = stage[sl].astype(BF16)


def _slab_pitch(d):
    return d // LANES + 1


def _read_slabs(ref, lead, n_tok, n_slabs, pitch):
    return jnp.concatenate(
        [ref[lead + (pl.ds(k, n_tok, stride=pitch), slice(None))] for k in range(n_slabs)], axis=1)


def _write_slabs(ref, lead, val, pitch):
    n_tok = val.shape[0]
    for k in range(val.shape[1] // LANES):
        ref[lead + (pl.ds(k, n_tok, stride=pitch), slice(None))] = val[:, k * LANES:(k + 1) * LANES]


def _pick_rows(i, n_main_blocks, main_ref, tail_ref, d, slab):
    if slab:
        return _read_slabs(main_ref, (), ROW_BLOCK, d // LANES, _slab_pitch(d))
    return jnp.where(i < n_main_blocks, main_ref[...], tail_ref[...])


def _in_proj_kernel(xm_ref, xt_ref, g_ref, w_hbm, o_ref, w_bf, stage, sem, *, n_main_blocks, layer, slab):
    i = pl.program_id(0)

    @pl.when(i == 0)
    def _():
        _load_weight_bf16(w_hbm.at[layer], w_bf, stage, sem)

    h = _rms(_pick_rows(i, n_main_blocks, xm_ref, xt_ref, g_ref.shape[1], slab), g_ref[...])
    o_ref[...] = jnp.dot(h.astype(BF16), w_bf[...], preferred_element_type=F32)


def _row_specs(d, n_main_blocks, tail_block0, slab):
    if slab:
        shape = (ROW_BLOCK * _slab_pitch(d), LANES)
        return [pl.BlockSpec(shape, lambda i: (i, 0)), pl.BlockSpec(shape, lambda i: (tail_block0, 0))]
    shape = (ROW_BLOCK, d)
    return [pl.BlockSpec(shape, lambda i: (jnp.minimum(i, n_main_blocks - 1), 0)),
            pl.BlockSpec(shape, lambda i: (jnp.maximum(i - n_main_blocks, 0) + tail_block0, 0))]


def _in_proj(x_main, x_tail, tail_block0, slab, g, w, layer, n_rows, n_main):
    d = g.shape[1]
    d_in = w.shape[2]
    nmb = n_main // ROW_BLOCK
    kern = functools.partial(_in_proj_kernel, n_main_blocks=nmb, layer=layer, slab=slab)
    return pl.pallas_call(
        kern,
        grid=(n_rows // ROW_BLOCK,),
        in_specs=_row_specs(d, nmb, tail_block0, slab) + [
            pl.BlockSpec((1, d), lambda i: (0, 0)),
            pl.BlockSpec(memory_space=pl.ANY),
        ],
        out_specs=pl.BlockSpec((ROW_BLOCK, d_in), lambda i: (i, 0)),
        out_shape=jax.ShapeDtypeStruct((n_rows, d_in), F32),
        scratch_shapes=[
            pltpu.VMEM((d, d_in), BF16),
            pltpu.VMEM((2, ROW_BLOCK, d_in), F32),
            pltpu.SemaphoreType.DMA((2,)),
        ],
        compiler_params=pltpu.CompilerParams(
            dimension_semantics=("arbitrary",), vmem_limit_bytes=VMEM_LIMIT),
        name="in_proj",
    )(x_main, x_tail, g, w)


def _scan_linear(a, b):
    t = a.shape[0]
    row = lax.broadcasted_iota(I32, a.shape, 0)
    d = 1
    while d < min(t, 8):
        a_sh = pltpu.roll(a, d, axis=0)
        b_sh = pltpu.roll(b, d, axis=0)
        m = row >= d
        b = jnp.where(m, a * b_sh + b, b)
        a = jnp.where(m, a * a_sh, a)
        d *= 2
    while d < t:
        b = jnp.concatenate([b[:d], a[d:] * b[:t - d] + b[d:]], axis=0)
        a = jnp.concatenate([a[:d], a[d:] * a[:t - d]], axis=0)
        d *= 2
    return a, b


def _lru_chunk(xc, g, h_prev, wg, ba, bx, sp):
    pre = jnp.dot(xc.astype(BF16), wg, preferred_element_type=F32)
    r = jax.nn.sigmoid(pre[:, :LANES] + ba)
    ig = jax.nn.sigmoid(pre[:, LANES:] + bx)
    log_a = (-LRU_C * r) * sp
    a = jnp.exp(log_a)
    v = 1.0 - a * a
    bterm = jnp.where(v > 0.0, v * lax.rsqrt(v), 0.0) * ig * xc
    a_cum, h0 = _scan_linear(a, bterm)
    hs = a_cum * h_prev + h0
    return hs * jax.nn.gelu(g), hs


def _rope(x, cos, sin_signed):
    n = x.shape[1] // LANES
    lane = lax.broadcasted_iota(I32, (x.shape[0], LANES), 1)
    first = (lane % HEAD_DIM) < (HEAD_DIM // 2)
    outs = []
    for c in range(n):
        xc = x[:, c * LANES:(c + 1) * LANES]
        swapped = jnp.where(first, pltpu.roll(xc, LANES - HEAD_DIM // 2, axis=1),
                            pltpu.roll(xc, HEAD_DIM // 2, axis=1))
        outs.append(xc * cos + swapped * sin_signed)
    return outs


def _store_head_variants(var_ref, row0, chunks):
    t = chunks[0].shape[0]
    lane = lax.broadcasted_iota(I32, (t, LANES), 1)
    for kc, x in enumerate(chunks):
        swapped = pltpu.roll(x, HALF, axis=1)
        for hh in range(2):
            for p in range(2):
                src = x if p == hh else swapped
                keep = (lane < HALF) if p == 0 else (lane >= HALF)
                var_ref[2 * kc + hh, p, row0:row0 + t, :] = jnp.where(keep, src, 0.0).astype(BF16)


def _attend_block(q_chunks, kvar, vvar, key0, sinks, lim, out_ref, out_rows, out_col0):
    qb = q_chunks[0].shape[0]
    nk = 2 * WINDOW
    gq = (2 * len(q_chunks)) // N_KV_HEADS
    rows = 2 * qb
    qi = lax.broadcasted_iota(I32, (rows, nk), 0) % qb
    kj = lax.broadcasted_iota(I32, (rows, nk), 1)
    valid = ((kj < WINDOW) & (kj > qi + lim)) | ((kj >= WINDOW) & (kj - WINDOW <= qi))
    top = lax.broadcasted_iota(I32, (rows, 1), 0) < qb
    lane = lax.broadcasted_iota(I32, (nk, LANES), 1)
    ones_lo = jnp.where(lane < HALF, 1.0, 0.0).astype(BF16)
    ones_hi = jnp.where(lane >= HALF, 1.0, 0.0).astype(BF16)
    lane_o = lax.broadcasted_iota(I32, (rows, LANES), 1)
    nt = (((1,), (1,)), ((), ()))
    for c in range(N_KV_HEADS):
        c0 = c * gq // 2
        qs = jnp.concatenate([q_chunks[c0], q_chunks[c0 + 1]], axis=0)
        es, sink_terms = [], []
        for p in range(2):
            kc = kvar[c, p, key0:key0 + nk, :]
            s = lax.dot_general(qs, kc, nt, preferred_element_type=F32)
            s = jnp.where(valid, s, -jnp.inf)
            h0 = c * gq + p
            sink = jnp.where(top, sinks[:, h0:h0 + 1], sinks[:, h0 + 2:h0 + 3])
            m = jnp.maximum(jnp.max(s, axis=-1, keepdims=True), sink)
            es.append(jnp.exp2(s - m).astype(BF16))
            sink_terms.append(jnp.exp2(sink - m))
        r0 = jnp.concatenate([vvar[c, 0, key0:key0 + nk, :], ones_lo], axis=1)
        r1 = jnp.concatenate([vvar[c, 1, key0:key0 + nk, :], ones_hi], axis=1)
        od = (jnp.dot(es[0], r0, preferred_element_type=F32)
              + jnp.dot(es[1], r1, preferred_element_type=F32))
        den = od[:, LANES:] + jnp.where(lane_o < HALF, sink_terms[0], sink_terms[1])
        o = (od[:, :LANES] / den).astype(out_ref.dtype)
        out_ref[out_rows, out_col0 + c0 * LANES:out_col0 + (c0 + 1) * LANES] = o[0:qb]
        out_ref[out_rows, out_col0 + (c0 + 1) * LANES:out_col0 + (c0 + 2) * LANES] = o[qb:rows]


def _mixer_kernel(*refs, tile, qblock, is_prompt, pos0, d_lru, d_pool, d_attn, d_kv):
    (proj_ref, cos_ref, sin_ref, convw_ref, convb_ref, wg_ref, ba_ref, bx_ref, lam_ref,
     poolw_ref, pscale_ref, sink_ref) = refs[:12]
    if is_prompt:
        (mix_ref, ko_ref, vo_ref, ho_ref, co_ref, po_ref,
         lru_ext, pool_ext, kvar, vvar, hcar) = refs[12:]
    else:
        (ck_ref, cv_ref, sh_ref, sc_ref, sp_ref,
         mix_ref, ko_ref, vo_ref, ho_ref, co_ref, po_ref,
         lru_ext, pool_ext, kvar, vvar) = refs[12:]

    s = pl.program_id(1) if is_prompt else 0
    o_glru = d_lru
    o_pool = 2 * d_lru
    o_q = o_pool + d_pool
    o_k = o_q + d_attn
    o_v = o_k + d_kv
    n_kc = d_kv // LANES

    if is_prompt:
        @pl.when(s == 0)
        def _():
            lru_ext[0:HALO, :] = jnp.zeros((HALO, d_lru), F32)
            pool_ext[0:HALO, :] = jnp.zeros((HALO, d_pool), F32)
            hcar[...] = jnp.zeros_like(hcar)
            kvar[:, :, 0:WINDOW, :] = jnp.zeros((N_KV_HEADS, 2, WINDOW, LANES), BF16)
            vvar[:, :, 0:WINDOW, :] = jnp.zeros((N_KV_HEADS, 2, WINDOW, LANES), BF16)
        h_prev = hcar[0:1, :]
    else:
        lru_ext[0:HALO, :] = jnp.zeros((HALO, d_lru), F32)
        pool_ext[0:HALO, :] = jnp.zeros((HALO, d_pool), F32)
        lru_ext[HALO - (CONV_WIDTH - 1):HALO, :] = sc_ref[0]
        pool_ext[HALO - POOL_PAD:HALO, :] = sp_ref[0]
        h_prev = sh_ref[0]
        kvar[:, :, WINDOW:2 * WINDOW, :] = jnp.zeros((N_KV_HEADS, 2, WINDOW, LANES), BF16)
        vvar[:, :, WINDOW:2 * WINDOW, :] = jnp.zeros((N_KV_HEADS, 2, WINDOW, LANES), BF16)
        _store_head_variants(kvar, 0, [ck_ref[0, :, c * LANES:(c + 1) * LANES] for c in range(n_kc)])
        _store_head_variants(vvar, 0, [cv_ref[0, :, c * LANES:(c + 1) * LANES] for c in range(n_kc)])

    lru_ext[HALO:HALO + tile, :] = proj_ref[:, 0:d_lru]
    pool_ext[HALO:HALO + tile, :] = proj_ref[:, o_pool:o_pool + d_pool]

    xc = convb_ref[...] + convw_ref[0:1, :] * lru_ext[HALO - 3:HALO - 3 + tile, :]
    for j in range(1, CONV_WIDTH):
        xc = xc + convw_ref[j:j + 1, :] * lru_ext[HALO - 3 + j:HALO - 3 + j + tile, :]
    neg = -lam_ref[...]
    sp_all = jnp.maximum(neg, 0.0) + jnp.log1p(jnp.exp(-jnp.abs(neg)))
    h_last = []
    for c in range(d_lru // LANES):
        cs = slice(c * LANES, (c + 1) * LANES)
        y, hs = _lru_chunk(xc[:, cs], proj_ref[:, o_glru + c * LANES:o_glru + (c + 1) * LANES],
                           h_prev[:, cs], wg_ref[c], ba_ref[:, cs], bx_ref[:, cs], sp_all[:, cs])
        mix_ref[:, cs] = y.astype(mix_ref.dtype)
        h_last.append(hs[tile - 1:tile, :])
    h_last = jnp.concatenate(h_last, axis=-1)

    row = lax.broadcasted_iota(I32, (tile, LANES), 0)
    pos = pos0 + s * tile + row
    for gi, w in enumerate(POOL_WINDOWS):
        cs = slice(gi * LANES, (gi + 1) * LANES)
        e = pool_ext[:, cs]
        acc = e
        step = 1
        while step < w:
            acc = acc + pltpu.roll(acc, step, axis=0)
            step *= 2
        cnt = jnp.minimum(pos + 1, w).astype(F32)
        dlt = acc[HALO:HALO + tile, :] / cnt - e[HALO:HALO + tile, :]
        y = jnp.dot(dlt.astype(BF16), poolw_ref[gi], preferred_element_type=F32) * pscale_ref[:, cs]
        mix_ref[:, d_lru + gi * LANES:d_lru + (gi + 1) * LANES] = y.astype(mix_ref.dtype)

    sinks = sink_ref[...] * LOG2_E
    o_attn = d_lru + d_pool
    cos = cos_ref[...]
    sin = sin_ref[...]
    q_chunks = [(qc * (ATTN_SCALE * LOG2_E)).astype(BF16)
                for qc in _rope(proj_ref[:, o_q:o_q + d_attn], cos, sin)]
    k_rot = _rope(proj_ref[:, o_k:o_k + d_kv], cos, sin)
    _store_head_variants(kvar, WINDOW, k_rot)
    _store_head_variants(vvar, WINDOW, [proj_ref[:, o_v + c * LANES:o_v + (c + 1) * LANES]
                                        for c in range(n_kc)])
    for blk in range(tile // qblock):
        rs = slice(blk * qblock, (blk + 1) * qblock)
        if is_prompt and blk == 0:
            lim = jnp.where(s == 0, WINDOW, 0)
        else:
            lim = 0
        _attend_block([qc[rs] for qc in q_chunks], kvar, vvar, blk * qblock, sinks, lim,
                      mix_ref, rs, o_attn)

    k_last = jnp.concatenate([kc[tile - qblock:tile] for kc in k_rot], axis=-1)
    if is_prompt:
        lru_ext[0:HALO, :] = lru_ext[tile:tile + HALO, :]
        pool_ext[0:HALO, :] = pool_ext[tile:tile + HALO, :]
        hcar[0:1, :] = h_last
        kvar[:, :, 0:WINDOW, :] = kvar[:, :, tile:tile + WINDOW, :]
        vvar[:, :, 0:WINDOW, :] = vvar[:, :, tile:tile + WINDOW, :]

        @pl.when(s == pl.num_programs(1) - 1)
        def _():
            ko_ref[0] = k_last
            vo_ref[0] = proj_ref[tile - qblock:tile, o_v:o_v + d_kv]
            ho_ref[0] = h_last
            co_ref[0] = lru_ext[HALO + tile - (CONV_WIDTH - 1):HALO + tile, :]
            po_ref[0] = pool_ext[HALO + tile - POOL_PAD:HALO + tile, :]
    else:
        ko_ref[0, 0:WINDOW - tile, :] = ck_ref[0, tile:WINDOW, :]
        ko_ref[0, WINDOW - tile:WINDOW, :] = k_last
        vo_ref[0, 0:WINDOW - tile, :] = cv_ref[0, tile:WINDOW, :]
        vo_ref[0, WINDOW - tile:WINDOW, :] = proj_ref[:, o_v:o_v + d_kv]
        ho_ref[0] = h_last
        co_ref[0] = lru_ext[HALO + tile - (CONV_WIDTH - 1):HALO + tile, :]
        po_ref[0] = pool_ext[HALO + tile - POOL_PAD:HALO + tile, :]


def _layer_consts(p, dims):
    d_lru, d_pool = dims["d_lru"], dims["d_pool"]
    hd = d_lru // LRU_HEADS
    per = LANES // hd
    nchunk = d_lru // LANES

    def blockdiag(w):
        w = w.reshape(nchunk, per, hd, hd)
        eye = jnp.eye(per, dtype=w.dtype)
        return jnp.einsum("cpij,pq->cpiqj", w, eye).reshape(nchunk, LANES, LANES)

    wg = jnp.concatenate([blockdiag(p["lru_wa"]), blockdiag(p["lru_wx"])], axis=-1).astype(BF16)
    return dict(
        convw=p["conv_w"], convb=p["conv_b"].reshape(1, d_lru), wg=wg,
        ba=p["lru_ba"].reshape(1, d_lru), bx=p["lru_bx"].reshape(1, d_lru),
        lam=p["lru_lambda"].reshape(1, d_lru), poolw=p["pool_w"].astype(BF16),
        pscale=p["pool_scale"].reshape(1, d_pool), sinks=p["attn_sinks"].reshape(1, -1))


def _rope_tables(pos):
    half = HEAD_DIM // 2
    inv = ROPE_THETA ** (-jnp.arange(half, dtype=F32) / half)
    ang = pos.astype(F32)[:, None] * inv[None, :]
    cos = jnp.cos(ang)
    sin = jnp.sin(ang)
    cos2 = jnp.concatenate([cos, cos], axis=-1)
    sin2 = jnp.concatenate([-sin, sin], axis=-1)
    reps = LANES // HEAD_DIM
    return jnp.tile(cos2, (1, reps)), jnp.tile(sin2, (1, reps))


CONST_NAMES = ("convw", "convb", "wg", "ba", "bx", "lam", "poolw", "pscale", "sinks")


def _const_specs(consts):
    return [pl.BlockSpec(consts[n].shape, functools.partial(lambda nd, *_: (0,) * nd, consts[n].ndim))
            for n in CONST_NAMES]


def _mixer_prompt(proj, consts, cos, sin, bsz, seq, dims, tile):
    d_lru, d_pool, d_attn, d_kv = dims["d_lru"], dims["d_pool"], dims["d_attn"], dims["d_kv"]
    d_in = proj.shape[1]
    d_mix = d_lru + d_pool + d_attn
    ns = seq // tile
    kern = functools.partial(_mixer_kernel, tile=tile, qblock=WINDOW, is_prompt=True, pos0=0,
                             d_lru=d_lru, d_pool=d_pool, d_attn=d_attn, d_kv=d_kv)
    return pl.pallas_call(
        kern,
        grid=(bsz, ns),
        in_specs=[
            pl.BlockSpec((tile, d_in), lambda b, s: (b * ns + s, 0)),
            pl.BlockSpec((tile, LANES), lambda b, s: (s, 0)),
            pl.BlockSpec((tile, LANES), lambda b, s: (s, 0)),
        ] + _const_specs(consts),
        out_specs=[
            pl.BlockSpec((tile, d_mix), lambda b, s: (b * ns + s, 0)),
            pl.BlockSpec((1, WINDOW, d_kv), lambda b, s: (b, 0, 0)),
            pl.BlockSpec((1, WINDOW, d_kv), lambda b, s: (b, 0, 0)),
            pl.BlockSpec((1, 1, d_lru), lambda b, s: (b, 0, 0)),
            pl.BlockSpec((1, CONV_WIDTH - 1, d_lru), lambda b, s: (b, 0, 0)),
            pl.BlockSpec((1, POOL_PAD, d_pool), lambda b, s: (b, 0, 0)),
        ],
        out_shape=[
            jax.ShapeDtypeStruct((bsz * seq, d_mix), BF16),
            jax.ShapeDtypeStruct((bsz, WINDOW, d_kv), F32),
            jax.ShapeDtypeStruct((bsz, WINDOW, d_kv), F32),
            jax.ShapeDtypeStruct((bsz, 1, d_lru), F32),
            jax.ShapeDtypeStruct((bsz, CONV_WIDTH - 1, d_lru), F32),
            jax.ShapeDtypeStruct((bsz, POOL_PAD, d_pool), F32),
        ],
        scratch_shapes=[
            pltpu.VMEM((tile + HALO, d_lru), F32),
            pltpu.VMEM((tile + HALO, d_pool), F32),
            pltpu.VMEM((N_KV_HEADS, 2, WINDOW + tile, LANES), BF16),
            pltpu.VMEM((N_KV_HEADS, 2, WINDOW + tile, LANES), BF16),
            pltpu.VMEM((8, d_lru), F32),
        ],
        compiler_params=pltpu.CompilerParams(
            dimension_semantics=("arbitrary", "arbitrary"), vmem_limit_bytes=VMEM_LIMIT),
        name="mixer_prompt",
    )(proj, cos, sin, *[consts[n] for n in CONST_NAMES])


def _mixer_sample_kernel(*refs, n_seq, t, d_lru, d_pool, d_attn, d_kv):
    n_in = 3 + len(CONST_NAMES)
    proj_ref, shared = refs[0], refs[1:n_in]
    state_in = refs[n_in:n_in + 5]
    mix_ref = refs[n_in + 5]
    state_out = refs[n_in + 6:n_in + 11]
    scratch = refs[n_in + 11:]
    for q in range(n_seq):
        one = lambda r: r.at[pl.ds(q, 1)]
        _mixer_kernel(proj_ref.at[pl.ds(q * t, t)], *shared, *[one(r) for r in state_in],
                      mix_ref.at[pl.ds(q * t, t)], *[one(r) for r in state_out],
                      *[r.at[q] for r in scratch],
                      tile=t, qblock=t, is_prompt=False, pos0=PAST_LEN,
                      d_lru=d_lru, d_pool=d_pool, d_attn=d_attn, d_kv=d_kv)


def _mixer_sample(proj, row0, consts, cos, sin, cache_k, cache_v, st_h, st_conv, st_pool, layer, dims):
    d_lru, d_pool, d_attn, d_kv = dims["d_lru"], dims["d_pool"], dims["d_attn"], dims["d_kv"]
    d_in = proj.shape[1]
    d_mix = d_lru + d_pool + d_attn
    db, win = cache_k.shape[1], cache_k.shape[2]
    t = cos.shape[0]
    nq = SAMPLE_SEQS_PER_STEP
    assert db % nq == 0 and row0 % (nq * t) == 0
    blk0 = row0 // (nq * t)
    kern = functools.partial(_mixer_sample_kernel, n_seq=nq, t=t,
                             d_lru=d_lru, d_pool=d_pool, d_attn=d_attn, d_kv=d_kv)
    return pl.pallas_call(
        kern,
        grid=(db // nq,),
        in_specs=[
            pl.BlockSpec((nq * t, d_in), lambda b: (blk0 + b, 0)),
            pl.BlockSpec((t, LANES), lambda b: (0, 0)),
            pl.BlockSpec((t, LANES), lambda b: (0, 0)),
        ] + _const_specs(consts) + [
            pl.BlockSpec((None, nq, win, d_kv), lambda b: (layer, b, 0, 0)),
            pl.BlockSpec((None, nq, win, d_kv), lambda b: (layer, b, 0, 0)),
            pl.BlockSpec((None, nq, 1, d_lru), lambda b: (layer, b, 0, 0)),
            pl.BlockSpec((None, nq, CONV_WIDTH - 1, d_lru), lambda b: (layer, b, 0, 0)),
            pl.BlockSpec((None, nq, POOL_PAD, d_pool), lambda b: (layer, b, 0, 0)),
        ],
        out_specs=[
            pl.BlockSpec((nq * t, d_mix), lambda b: (b, 0)),
            pl.BlockSpec((nq, win, d_kv), lambda b: (b, 0, 0)),
            pl.BlockSpec((nq, win, d_kv), lambda b: (b, 0, 0)),
            pl.BlockSpec((nq, 1, d_lru), lambda b: (b, 0, 0)),
            pl.BlockSpec((nq, CONV_WIDTH - 1, d_lru), lambda b: (b, 0, 0)),
            pl.BlockSpec((nq, POOL_PAD, d_pool), lambda b: (b, 0, 0)),
        ],
        out_shape=[
            jax.ShapeDtypeStruct((db * t, d_mix), F32),
            jax.ShapeDtypeStruct((db, win, d_kv), F32),
            jax.ShapeDtypeStruct((db, win, d_kv), F32),
            jax.ShapeDtypeStruct((db, 1, d_lru), F32),
            jax.ShapeDtypeStruct((db, CONV_WIDTH - 1, d_lru), F32),
            jax.ShapeDtypeStruct((db, POOL_PAD, d_pool), F32),
        ],
        scratch_shapes=[
            pltpu.VMEM((nq, t + HALO, d_lru), F32),
            pltpu.VMEM((nq, t + HALO, d_pool), F32),
            pltpu.VMEM((nq, N_KV_HEADS, 2, 2 * WINDOW, LANES), BF16),
            pltpu.VMEM((nq, N_KV_HEADS, 2, 2 * WINDOW, LANES), BF16),
        ],
        compiler_params=pltpu.CompilerParams(
            dimension_semantics=("arbitrary",), vmem_limit_bytes=VMEM_LIMIT),
        name="mixer_sample",
    )(proj, cos, sin, *[consts[n] for n in CONST_NAMES], cache_k, cache_v, st_h, st_conv, st_pool)


def _route(logits, run_cnt):
    t = logits.shape[0]
    lane = lax.broadcasted_iota(I32, (t, LANES), 1)
    lane_f = lane.astype(F32)
    ninf = -jnp.inf
    big = float(LANES)
    is_g = lane < N_EXPERT_GROUPS
    lg = jnp.where(is_g, logits, ninf)
    mg = jnp.max(lg, axis=-1, keepdims=True)
    g_top = jnp.min(jnp.where(lg == mg, lane_f, big), axis=-1, keepdims=True).astype(I32)
    pg_top = 1.0 / jnp.sum(jnp.exp(lg - mg), axis=-1, keepdims=True)
    base = N_EXPERT_GROUPS + EXPERTS_PER_GROUP * g_top
    in_grp = (lane >= base) & (lane < base + EXPERTS_PER_GROUP)
    le = jnp.where(in_grp, logits, ninf)
    m1 = jnp.max(le, axis=-1, keepdims=True)
    i1 = jnp.min(jnp.where(le == m1, lane_f, big), axis=-1, keepdims=True).astype(I32)
    le2 = jnp.where(lane == i1, ninf, le)
    m2 = jnp.max(le2, axis=-1, keepdims=True)
    i2 = jnp.min(jnp.where((le2 == m2) & in_grp & (lane != i1), lane_f, big),
                 axis=-1, keepdims=True).astype(I32)
    se = jnp.sum(jnp.exp(le - m1), axis=-1, keepdims=True)
    p1 = 1.0 / se
    p2 = jnp.exp(m2 - m1) / se
    tot = p1 + p2
    w1 = (p1 / tot) * pg_top
    w2 = (p2 / tot) * pg_top
    a = i1 - base
    b = i2 - base
    lo = jnp.minimum(a, b)
    hi = jnp.maximum(a, b)
    w_lo = jnp.where(a < b, w1, w2)
    w_hi = jnp.where(a < b, w2, w1)
    pid = jnp.where(lo == 0, hi - 1, jnp.where(lo == 1, jnp.where(hi == 3, 3, 4), 5))
    swap = pid == 5
    w_a = jnp.where(swap, w_hi, w_lo)
    w_b = jnp.where(swap, w_lo, w_hi)
    cls = g_top * N_PAIRS + pid
    onehot = lane == cls
    ti = lax.broadcasted_iota(I32, (t, t), 0)
    tj = lax.broadcasted_iota(I32, (t, t), 1)
    lower = jnp.where(tj <= ti, 1.0, 0.0).astype(BF16)
    prefix = jnp.dot(lower, jnp.where(onehot, 1.0, 0.0).astype(BF16), preferred_element_type=F32)
    rank = jnp.sum(jnp.where(onehot, prefix - 1.0 + run_cnt, 0.0), axis=-1, keepdims=True)
    info = jnp.where(lane == 0, w_a, jnp.where(lane == 1, w_b, jnp.where(
        lane == 2, cls.astype(F32), jnp.where(lane == 3, rank, 0.0))))
    return info, run_cnt + prefix[t - 1:t, :]


def _out_proj_kernel(mixp_ref, mixs_ref, xm_ref, xt_ref, w_hbm, g_ref, wr_ref, br_ref,
                     o_ref, info_ref, cnt_ref, w_bf, stage, sem, run_cnt, *, n_main_blocks, d, layer, slab):
    i = pl.program_id(0)

    @pl.when(i == 0)
    def _():
        _load_weight_bf16(w_hbm.at[layer], w_bf, stage, sem)
        run_cnt[...] = jnp.zeros_like(run_cnt)

    mix = jnp.where(i < n_main_blocks, mixp_ref[...], mixs_ref[...].astype(BF16))
    xres = (_pick_rows(i, n_main_blocks, xm_ref, xt_ref, d, slab)
            + jnp.dot(mix, w_bf[...], preferred_element_type=F32))
    h2 = _rms(xres, g_ref[...])
    logits = jnp.dot(h2.astype(BF16), wr_ref[...], preferred_element_type=F32) + br_ref[...]
    info, cnt = _route(logits, run_cnt[0:1, :])
    run_cnt[0:1, :] = cnt
    pitch = _slab_pitch(d)
    _write_slabs(o_ref, (), xres, pitch)
    o_ref[pl.ds(d // LANES, ROW_BLOCK, stride=pitch), :] = info
    info_ref[...] = info
    cnt_ref[...] = jnp.broadcast_to(cnt, cnt_ref.shape)


def _out_proj(mix_p, mix_s, x_main, x_tail, tail_block0, slab, w, layer, g, wr, br, n_rows):
    d = g.shape[1]
    d_mix = w.shape[1]
    npb = mix_p.shape[0] // ROW_BLOCK
    pitch = _slab_pitch(d)
    kern = functools.partial(_out_proj_kernel, n_main_blocks=npb, d=d, layer=layer, slab=slab)
    return pl.pallas_call(
        kern,
        grid=(n_rows // ROW_BLOCK,),
        in_specs=[
            pl.BlockSpec((ROW_BLOCK, d_mix), lambda i: (jnp.minimum(i, npb - 1), 0)),
            pl.BlockSpec((ROW_BLOCK, d_mix), lambda i: (jnp.maximum(i - npb, 0), 0)),
        ] + _row_specs(d, npb, tail_block0, slab) + [
            pl.BlockSpec(memory_space=pl.ANY),
            pl.BlockSpec((1, d), lambda i: (0, 0)),
            pl.BlockSpec((d, LANES), lambda i: (0, 0)),
            pl.BlockSpec((1, LANES), lambda i: (0, 0)),
        ],
        out_specs=[pl.BlockSpec((ROW_BLOCK * pitch, LANES), lambda i: (i, 0)),
                   pl.BlockSpec((ROW_BLOCK, LANES), lambda i: (i, 0)),
                   pl.BlockSpec((8, LANES), lambda i: (0, 0))],
        out_shape=[jax.ShapeDtypeStruct((n_rows * pitch, LANES), F32),
                   jax.ShapeDtypeStruct((n_rows, LANES), F32),
                   jax.ShapeDtypeStruct((8, LANES), F32)],
        scratch_shapes=[
            pltpu.VMEM((d_mix, d), BF16),
            pltpu.VMEM((2, ROW_BLOCK, d), F32),
            pltpu.SemaphoreType.DMA((2,)),
            pltpu.VMEM((8, LANES), F32),
        ],
        compiler_params=pltpu.CompilerParams(
            dimension_semantics=("arbitrary",), vmem_limit_bytes=VMEM_LIMIT),
        name="out_proj",
    )(mix_p, mix_s, x_main, x_tail, w, g, wr, br)


def _moe_kernel(cls_ref, rank_ref, off_ref, cnt_ref, ea_ref, eb_ref, chga_ref, chgb_ref, nch_ref,
                xr_hbm, g_ref, w1a_ref, w3a_ref, w2a_ref, w1b_ref, w3b_ref, w2b_ref,
                xo_hbm,
                perm, xbuf, obuf, wa1, wa3, wa2, wb1, wb3, wb2, gsem, ssem, *, d, n_rows):
    del ea_ref, eb_ref
    m = MOE_CHUNK
    pitch = _slab_pitch(d)
    n_slabs = d // LANES
    c = pl.program_id(0)
    n = nch_ref[0]
    slot = c % 2
    other = 1 - slot

    def start_gather(chunk, sl, j):
        tok = jnp.maximum(perm[(chunk + 1) * m + j], 0)
        pltpu.make_async_copy(xr_hbm.at[pl.ds(tok * pitch, pitch)],
                              xbuf.at[sl, pl.ds(j * pitch, pitch)], gsem.at[sl]).start()

    def start_scatter(chunk, sl, j):
        tok = perm[(chunk + 1) * m + j]
        r = jnp.where(tok < 0, n_rows + sl * m + j, tok)
        pltpu.make_async_copy(obuf.at[sl, pl.ds(j * pitch, pitch)],
                              xo_hbm.at[pl.ds(r * pitch, pitch)], ssem.at[sl]).start()

    def wait_gather(sl):
        pltpu.make_async_copy(xr_hbm.at[pl.ds(0, m * pitch)], xbuf.at[sl], gsem.at[sl]).wait()

    def wait_scatter(sl):
        pltpu.make_async_copy(obuf.at[sl], xo_hbm.at[pl.ds(0, m * pitch)], ssem.at[sl]).wait()

    @pl.when(c == 0)
    def _():
        obuf[...] = jnp.zeros_like(obuf)
        pad = pltpu.make_async_copy(obuf.at[0], xo_hbm.at[pl.ds(n_rows * pitch, m * pitch)], ssem.at[0])
        pad.start()
        pad.wait()

        def fill(lo, hi):
            def body(i, z):
                perm[i] = -1
                return z
            lax.fori_loop(lo, hi, body, 0)

        def first_gather(j, z):
            start_gather(0, 0, j)
            return z

        fill(0, m)
        fill((n + 1) * m, (n + 3) * m)
        for k in range(N_CLASSES):
            base = (off_ref[k] + 1) * m
            cnt = cnt_ref[k]
            fill(base + cnt, base + ((cnt + m - 1) // m) * m)

        def place(t, z):
            perm[(off_ref[cls_ref[t]] + 1) * m + rank_ref[t]] = t
            return z
        lax.fori_loop(0, n_rows, place, 0, unroll=8)
        lax.fori_loop(0, m, first_gather, 0)

    @pl.when(c <= n)
    def _():
        wait_gather(slot)

        @pl.when(c >= 1)
        def _():
            wait_scatter(slot)

        @pl.when(chga_ref[c] == 1)
        def _():
            wa1[...] = w1a_ref[0].astype(BF16)
            wa3[...] = w3a_ref[0].astype(BF16)
            wa2[...] = w2a_ref[0].astype(BF16)

        @pl.when(chgb_ref[c] == 1)
        def _():
            wb1[...] = w1b_ref[0].astype(BF16)
            wb3[...] = w3b_ref[0].astype(BF16)
            wb2[...] = w2b_ref[0].astype(BF16)

        for j in range(m):
            start_gather(c + 1, other, j)
        for j in range(m):
            start_scatter(c - 1, other, j)
        x = _read_slabs(xbuf, (slot,), m, n_slabs, pitch)
        info = xbuf[slot, pl.ds(n_slabs, m, stride=pitch), :]
        w_a = info[:, 0:1]
        w_b = info[:, 1:2]
        h = _rms(x, g_ref[...]).astype(BF16)
        hid_a = (jax.nn.silu(jnp.dot(h, wa1[...], preferred_element_type=F32))
                 * jnp.dot(h, wa3[...], preferred_element_type=F32) * w_a).astype(BF16)
        hid_b = (jax.nn.silu(jnp.dot(h, wb1[...], preferred_element_type=F32))
                 * jnp.dot(h, wb3[...], preferred_element_type=F32) * w_b).astype(BF16)
        y = (jnp.dot(hid_a, wa2[...], preferred_element_type=F32)
             + jnp.dot(hid_b, wb2[...], preferred_element_type=F32))
        _write_slabs(obuf, (slot,), x + y, pitch)

        @pl.when(c == n)
        def _():
            wait_gather(other)
            wait_scatter(other)


def _moe(xr, g, w1, w3, w2, tables, n_rows, c_max):
    d = g.shape[1]
    f = w1.shape[2]
    m = MOE_CHUNK
    pitch = _slab_pitch(d)
    kern = functools.partial(_moe_kernel, d=d, n_rows=n_rows)

    def wspec(shape, which):
        return pl.BlockSpec(shape, lambda c, *pref: (pref[which][c], 0, 0))

    grid_spec = pltpu.PrefetchScalarGridSpec(
        num_scalar_prefetch=9,
        grid=(c_max + 1,),
        in_specs=[
            pl.BlockSpec(memory_space=pl.ANY),
            pl.BlockSpec((1, d), lambda c, *pref: (0, 0)),
            wspec((1, d, f), 4), wspec((1, d, f), 4), wspec((1, f, d), 4),
            wspec((1, d, f), 5), wspec((1, d, f), 5), wspec((1, f, d), 5),
        ],
        out_specs=pl.BlockSpec(memory_space=pl.ANY),
        scratch_shapes=[
            pltpu.SMEM(((c_max + 3) * m,), I32),
            pltpu.VMEM((2, m * pitch, LANES), F32),
            pltpu.VMEM((2, m * pitch, LANES), F32),
            pltpu.VMEM((d, f), BF16), pltpu.VMEM((d, f), BF16), pltpu.VMEM((f, d), BF16),
            pltpu.VMEM((d, f), BF16), pltpu.VMEM((d, f), BF16), pltpu.VMEM((f, d), BF16),
            pltpu.SemaphoreType.DMA((2,)),
            pltpu.SemaphoreType.DMA((2,)),
        ],
    )
    return pl.pallas_call(
        kern,
        grid_spec=grid_spec,
        out_shape=jax.ShapeDtypeStruct(((n_rows + 2 * m) * pitch, LANES), F32),
        compiler_params=pltpu.CompilerParams(
            dimension_semantics=("arbitrary",), vmem_limit_bytes=VMEM_LIMIT),
        name="moe",
    )(*tables, xr, g, w1, w3, w2, w1, w3, w2)


def _chunk_tables(counts, c_max, expert0):
    m = MOE_CHUNK
    nch_c = (counts + m - 1) // m
    ch_end = jnp.cumsum(nch_c)
    ch_off = ch_end - nch_c
    n_chunks = ch_end[-1]
    chunk = jnp.minimum(jnp.arange(c_max + 1, dtype=I32), jnp.maximum(n_chunks - 1, 0))
    ccls = jnp.sum((chunk[:, None] >= ch_end[None, :]).astype(I32), axis=1)
    ccls = jnp.minimum(ccls, N_CLASSES - 1)
    grp = ccls // N_PAIRS
    pid = ccls % N_PAIRS
    ea = expert0 + grp * EXPERTS_PER_GROUP + jnp.asarray(PAIR_SLOT_A, I32)[pid]
    eb = expert0 + grp * EXPERTS_PER_GROUP + jnp.asarray(PAIR_SLOT_B, I32)[pid]
    first = jnp.ones((1,), I32)
    chga = jnp.concatenate([first, (ea[1:] != ea[:-1]).astype(I32)])
    chgb = jnp.concatenate([first, (eb[1:] != eb[:-1]).astype(I32)])
    return ch_off.astype(I32), counts, ea, eb, chga, chgb, n_chunks.reshape(1).astype(I32)


def _final_norm_kernel(x_ref, g_ref, op_ref, os_ref, *, n_prompt_blocks):
    i = pl.program_id(0)
    d = g_ref.shape[1]
    y = _rms(_read_slabs(x_ref, (), ROW_BLOCK, d // LANES, _slab_pitch(d)), g_ref[...])

    @pl.when(i < n_prompt_blocks)
    def _():
        op_ref[...] = y

    @pl.when(i >= n_prompt_blocks)
    def _():
        os_ref[...] = y


def _final_norm(x, g, n_prompt, n_sample):
    d = g.shape[1]
    npb = n_prompt // ROW_BLOCK
    nsb = n_sample // ROW_BLOCK
    kern = functools.partial(_final_norm_kernel, n_prompt_blocks=npb)
    return pl.pallas_call(
        kern,
        grid=(npb + nsb,),
        in_specs=[pl.BlockSpec((ROW_BLOCK * _slab_pitch(d), LANES), lambda i: (i, 0)),
                  pl.BlockSpec((1, d), lambda i: (0, 0))],
        out_specs=[pl.BlockSpec((ROW_BLOCK, d), lambda i: (jnp.minimum(i, npb - 1), 0)),
                   pl.BlockSpec((ROW_BLOCK, d), lambda i: (jnp.maximum(i - npb, 0), 0))],
        out_shape=[jax.ShapeDtypeStruct((n_prompt, d), F32),
                   jax.ShapeDtypeStruct((n_sample, d), F32)],
        compiler_params=pltpu.CompilerParams(dimension_semantics=("arbitrary",)),
        name="final_norm",
    )(x, g)


def kernel(x_prompt, x_sample, cache_k, cache_v, state_lru_h, state_conv, state_pool, norm1_g, w_in, conv_w, conv_b, lru_wa, lru_ba, lru_wx, lru_bx, lru_lambda, pool_w, pool_scale, attn_sinks, w_out, norm2_g, router_group_w, router_group_b, router_expert_w, router_expert_b, expert_w1, expert_w3, expert_w2, final_norm_g):
    bsz, seq, d = x_prompt.shape
    db, ds, _ = x_sample.shape
    depth = w_in.shape[0]
    win = cache_k.shape[2]
    nkv = cache_k.shape[3]
    d_lru = lru_lambda.shape[1]
    d_pool = pool_scale.shape[1]
    d_kv = nkv * cache_k.shape[4]
    d_attn = attn_sinks.shape[1] * HEAD_DIM
    dims = dict(d_lru=d_lru, d_pool=d_pool, d_attn=d_attn, d_kv=d_kv)
    n_prompt = bsz * seq
    n_sample = db * ds
    n_rows = n_prompt + n_sample
    assert n_prompt % ROW_BLOCK == 0 and n_sample % ROW_BLOCK == 0
    assert win == WINDOW and nkv == N_KV_HEADS and ds <= 8 and WINDOW % ds == 0
    tile = 512 if seq % 512 == 0 else WINDOW
    c_max = -(-n_rows // MOE_CHUNK) + N_CLASSES
    npb = n_prompt // ROW_BLOCK

    cos_p, sin_p = _rope_tables(jnp.arange(seq))
    cos_s, sin_s = _rope_tables(PAST_LEN + jnp.arange(ds))
    wr_all = jnp.concatenate([router_group_w, router_expert_w], axis=2)
    wr_all = jnp.pad(wr_all, ((0, 0), (0, 0), (0, LANES - wr_all.shape[2]))).astype(BF16)
    br_all = jnp.concatenate([router_group_b, router_expert_b], axis=1)
    br_all = jnp.pad(br_all, ((0, 0), (0, LANES - br_all.shape[1])))

    x_main, x_tail, tail0, slab = x_prompt.reshape(n_prompt, d), x_sample.reshape(n_sample, d), 0, False
    ck = cache_k.reshape(depth, db, win, d_kv)
    cv = cache_v.reshape(depth, db, win, d_kv)
    st_h = state_lru_h.reshape(depth, db, 1, d_lru)
    n_exp, _, d_exp = expert_w1.shape[1:]
    w1 = expert_w1.reshape(depth * n_exp, d, d_exp)
    w3 = expert_w3.reshape(depth * n_exp, d, d_exp)
    w2 = expert_w2.reshape(depth * n_exp, d_exp, d)
    outs = [[] for _ in range(10)]
    for l in range(depth):
        p = dict(conv_w=conv_w[l], conv_b=conv_b[l], lru_wa=lru_wa[l], lru_ba=lru_ba[l],
                 lru_wx=lru_wx[l], lru_bx=lru_bx[l], lru_lambda=lru_lambda[l], pool_w=pool_w[l],
                 pool_scale=pool_scale[l], attn_sinks=attn_sinks[l])
        consts = _layer_consts(p, dims)
        proj = _in_proj(x_main, x_tail, tail0, slab, norm1_g[l].reshape(1, d), w_in, l, n_rows, n_prompt)
        mix_p, pk, pv, ph, pc, pp = _mixer_prompt(proj, consts, cos_p, sin_p, bsz, seq, dims, tile)
        mix_s, sk, sv, sh, sc, sp = _mixer_sample(
            proj, n_prompt, consts, cos_s, sin_s, ck, cv, st_h, state_conv, state_pool, l, dims)
        xr, info, cnt = _out_proj(mix_p, mix_s, x_main, x_tail, tail0, slab, w_out, l,
                                  norm2_g[l].reshape(1, d), wr_all[l], br_all[l:l + 1], n_rows)
        route = info[:, 2:4].astype(I32)
        tables = (route[:, 0], route[:, 1]) + _chunk_tables(cnt[0, :N_CLASSES].astype(I32), c_max, l * n_exp)
        x = _moe(xr, norm2_g[l].reshape(1, d), w1, w3, w2, tables, n_rows, c_max)
        x_main, x_tail, tail0, slab = x, x, npb, True
        for lst, val in zip(outs, (pk.reshape(bsz, WINDOW, nkv, HEAD_DIM), pv.reshape(bsz, WINDOW, nkv, HEAD_DIM),
                                   ph.reshape(bsz, d_lru), pc, pp,
                                   sk.reshape(db, win, nkv, HEAD_DIM), sv.reshape(db, win, nkv, HEAD_DIM),
                                   sh.reshape(db, d_lru), sc, sp)):
            lst.append(val)
    y_p, y_s = _final_norm(x_main, final_norm_g.reshape(1, d), n_prompt, n_sample)
    return (y_p.reshape(bsz, seq, d), y_s.reshape(db, ds, d)) + tuple(jnp.stack(o) for o in outs)
```

```python
import functools

import jax
import jax.numpy as jnp
from jax import lax
from jax.experimental import pallas as pl
from jax.experimental.pallas import tpu as pltpu

F32 = jnp.float32
BF16 = jnp.bfloat16
I32 = jnp.int32

LRU_HEADS = 8
CONV_WIDTH = 4
LRU_C = 8.0
POOL_WINDOWS = (2, 4, 8, 16)
POOL_PAD = max(POOL_WINDOWS) - 1
HEAD_DIM = 64
N_KV_HEADS = 4
WINDOW = 128
ROPE_THETA = 10000.0
ATTN_SCALE = HEAD_DIM ** -0.5
LOG2_E = 1.4426950408889634
N_EXPERT_GROUPS = 4
EXPERTS_PER_GROUP = 4
N_EXPERTS = N_EXPERT_GROUPS * EXPERTS_PER_GROUP
RMS_EPS = 1e-6
PAST_LEN = 16384

LANES = 128
HALF = LANES // 2
HALO = 16
ROW_BLOCK = 256
SAMPLE_SEQS_PER_STEP = 4
MOE_CHUNK = 64
PAIR_SLOT_A = (0, 0, 0, 1, 1, 3)
PAIR_SLOT_B = (1, 2, 3, 3, 2, 2)
N_PAIRS = len(PAIR_SLOT_A)
N_CLASSES = N_EXPERT_GROUPS * N_PAIRS
VMEM_LIMIT = 52 * 1024 * 1024


def _rms(x, g):
    return (x * lax.rsqrt(jnp.mean(x * x, axis=-1, keepdims=True) + RMS_EPS)) * g


def _load_weight_bf16(w_hbm, w_bf, stage, sem):
    rows = stage.shape[1]
    n = w_hbm.shape[0] // rows

    def copy(i, sl):
        return pltpu.make_async_copy(w_hbm.at[pl.ds(i * rows, rows)], stage.at[sl], sem.at[sl])

    copy(0, 0).start()
    for i in range(n):
        sl = i % 2
        if i + 1 < n:
            copy(i + 1, 1 - sl).start()
        copy(i, sl).wait()
        w_bf[i * rows:(i + 1) * rows, :] = stage[sl].astype(BF16)


def _slab_pitch(d):
    return d // LANES + 1


def _read_slabs(ref, lead, n_tok, n_slabs, pitch):
    return jnp.concatenate(
        [ref[lead + (pl.ds(k, n_tok, stride=pitch), slice(None))] for k in range(n_slabs)], axis=1)


def _write_slabs(ref, lead, val, pitch):
    n_tok = val.shape[0]
    for k in range(val.shape[1] // LANES):
        ref[lead + (pl.ds(k, n_tok, stride=pitch), slice(None))] = val[:, k * LANES:(k + 1) * LANES]


def _pick_rows(i, n_main_blocks, main_ref, tail_ref, d, slab):
    if slab:
        return _read_slabs(main_ref, (), ROW_BLOCK, d // LANES, _slab_pitch(d))
    return jnp.where(i < n_main_blocks, main_ref[...], tail_ref[...])


def _in_proj_kernel(xm_ref, xt_ref, g_ref, w_hbm, o_ref, w_bf, stage, sem, *, n_main_blocks, layer, slab):
    i = pl.program_id(0)

    @pl.when(i == 0)
    def _():
        _load_weight_bf16(w_hbm.at[layer], w_bf, stage, sem)

    h = _rms(_pick_rows(i, n_main_blocks, xm_ref, xt_ref, g_ref.shape[1], slab), g_ref[...])
    o_ref[...] = jnp.dot(h.astype(BF16), w_bf[...], preferred_element_type=F32)


def _row_specs(d, n_main_blocks, tail_block0, slab):
    if slab:
        shape = (ROW_BLOCK * _slab_pitch(d), LANES)
        return [pl.BlockSpec(shape, lambda i: (i, 0)), pl.BlockSpec(shape, lambda i: (tail_block0, 0))]
    shape = (ROW_BLOCK, d)
    return [pl.BlockSpec(shape, lambda i: (jnp.minimum(i, n_main_blocks - 1), 0)),
            pl.BlockSpec(shape, lambda i: (jnp.maximum(i - n_main_blocks, 0) + tail_block0, 0))]


def _in_proj(x_main, x_tail, tail_block0, slab, g, w, layer, n_rows, n_main):
    d = g.shape[1]
    d_in = w.shape[2]
    nmb = n_main // ROW_BLOCK
    kern = functools.partial(_in_proj_kernel, n_main_blocks=nmb, layer=layer, slab=slab)
    return pl.pallas_call(
        kern,
        grid=(n_rows // ROW_BLOCK,),
        in_specs=_row_specs(d, nmb, tail_block0, slab) + [
            pl.BlockSpec((1, d), lambda i: (0, 0)),
            pl.BlockSpec(memory_space=pl.ANY),
        ],
        out_specs=pl.BlockSpec((ROW_BLOCK, d_in), lambda i: (i, 0)),
        out_shape=jax.ShapeDtypeStruct((n_rows, d_in), F32),
        scratch_shapes=[
            pltpu.VMEM((d, d_in), BF16),
            pltpu.VMEM((2, ROW_BLOCK, d_in), F32),
            pltpu.SemaphoreType.DMA((2,)),
        ],
        compiler_params=pltpu.CompilerParams(
            dimension_semantics=("arbitrary",), vmem_limit_bytes=VMEM_LIMIT),
        name="in_proj",
    )(x_main, x_tail, g, w)


def _scan_linear(a, b):
    t = a.shape[0]
    row = lax.broadcasted_iota(I32, a.shape, 0)
    d = 1
    while d < min(t, 8):
        a_sh = pltpu.roll(a, d, axis=0)
        b_sh = pltpu.roll(b, d, axis=0)
        m = row >= d
        b = jnp.where(m, a * b_sh + b, b)
        a = jnp.where(m, a * a_sh, a)
        d *= 2
    while d < t:
        b = jnp.concatenate([b[:d], a[d:] * b[:t - d] + b[d:]], axis=0)
        a = jnp.concatenate([a[:d], a[d:] * a[:t - d]], axis=0)
        d *= 2
    return a, b


def _lru_chunk(xc, g, h_prev, wg, ba, bx, sp):
    pre = jnp.dot(xc.astype(BF16), wg, preferred_element_type=F32)
    r = jax.nn.sigmoid(pre[:, :LANES] + ba)
    ig = jax.nn.sigmoid(pre[:, LANES:] + bx)
    log_a = (-LRU_C * r) * sp
    a = jnp.exp(log_a)
    v = 1.0 - a * a
    bterm = jnp.where(v > 0.0, v * lax.rsqrt(v), 0.0) * ig * xc
    a_cum, h0 = _scan_linear(a, bterm)
    hs = a_cum * h_prev + h0
    return hs * jax.nn.gelu(g), hs


def _rope(x, cos, sin_signed):
    n = x.shape[1] // LANES
    lane = lax.broadcasted_iota(I32, (x.shape[0], LANES), 1)
    first = (lane % HEAD_DIM) < (HEAD_DIM // 2)
    outs = []
    for c in range(n):
        xc = x[:, c * LANES:(c + 1) * LANES]
        swapped = jnp.where(first, pltpu.roll(xc, LANES - HEAD_DIM // 2, axis=1),
                            pltpu.roll(xc, HEAD_DIM // 2, axis=1))
        outs.append(xc * cos + swapped * sin_signed)
    return outs


def _store_head_variants(var_ref, row0, chunks):
    t = chunks[0].shape[0]
    lane = lax.broadcasted_iota(I32, (t, LANES), 1)
    for kc, x in enumerate(chunks):
        swapped = pltpu.roll(x, HALF, axis=1)
        for hh in range(2):
            for p in range(2):
                src = x if p == hh else swapped
                keep = (lane < HALF) if p == 0 else (lane >= HALF)
                var_ref[2 * kc + hh, p, row0:row0 + t, :] = jnp.where(keep, src, 0.0).astype(BF16)


def _attend_block(q_chunks, kvar, vvar, key0, sinks, lim, out_ref, out_rows, out_col0):
    qb = q_chunks[0].shape[0]
    nk = 2 * WINDOW
    gq = (2 * len(q_chunks)) // N_KV_HEADS
    rows = 2 * qb
    qi = lax.broadcasted_iota(I32, (rows, nk), 0) % qb
    kj = lax.broadcasted_iota(I32, (rows, nk), 1)
    valid = ((kj < WINDOW) & (kj > qi + lim)) | ((kj >= WINDOW) & (kj - WINDOW <= qi))
    top = lax.broadcasted_iota(I32, (rows, 1), 0) < qb
    lane = lax.broadcasted_iota(I32, (nk, LANES), 1)
    ones_lo = jnp.where(lane < HALF, 1.0, 0.0).astype(BF16)
    ones_hi = jnp.where(lane >= HALF, 1.0, 0.0).astype(BF16)
    lane_o = lax.broadcasted_iota(I32, (rows, LANES), 1)
    nt = (((1,), (1,)), ((), ()))
    for c in range(N_KV_HEADS):
        c0 = c * gq // 2
        qs = jnp.concatenate([q_chunks[c0], q_chunks[c0 + 1]], axis=0)
        es, sink_terms = [], []
        for p in range(2):
            kc = kvar[c, p, key0:key0 + nk, :]
            s = lax.dot_general(qs, kc, nt, preferred_element_type=F32)
            s = jnp.where(valid, s, -jnp.inf)
            h0 = c * gq + p
            sink = jnp.where(top, sinks[:, h0:h0 + 1], sinks[:, h0 + 2:h0 + 3])
            m = jnp.maximum(jnp.max(s, axis=-1, keepdims=True), sink)
            es.append(jnp.exp2(s - m).astype(BF16))
            sink_terms.append(jnp.exp2(sink - m))
        r0 = jnp.concatenate([vvar[c, 0, key0:key0 + nk, :], ones_lo], axis=1)
        r1 = jnp.concatenate([vvar[c, 1, key0:key0 + nk, :], ones_hi], axis=1)
        od = (jnp.dot(es[0], r0, preferred_element_type=F32)
              + jnp.dot(es[1], r1, preferred_element_type=F32))
        den = od[:, LANES:] + jnp.where(lane_o < HALF, sink_terms[0], sink_terms[1])
        o = (od[:, :LANES] / den).astype(out_ref.dtype)
        out_ref[out_rows, out_col0 + c0 * LANES:out_col0 + (c0 + 1) * LANES] = o[0:qb]
        out_ref[out_rows, out_col0 + (c0 + 1) * LANES:out_col0 + (c0 + 2) * LANES] = o[qb:rows]


def _mixer_kernel(*refs, tile, qblock, is_prompt, pos0, d_lru, d_pool, d_attn, d_kv):
    (proj_ref, cos_ref, sin_ref, convw_ref, convb_ref, wg_ref, ba_ref, bx_ref, lam_ref,
     poolw_ref, pscale_ref, sink_ref) = refs[:12]
    if is_prompt:
        (mix_ref, ko_ref, vo_ref, ho_ref, co_ref, po_ref,
         lru_ext, pool_ext, kvar, vvar, hcar) = refs[12:]
    else:
        (ck_ref, cv_ref, sh_ref, sc_ref, sp_ref,
         mix_ref, ko_ref, vo_ref, ho_ref, co_ref, po_ref,
         lru_ext, pool_ext, kvar, vvar) = refs[12:]

    s = pl.program_id(1) if is_prompt else 0
    o_glru = d_lru
    o_pool = 2 * d_lru
    o_q = o_pool + d_pool
    o_k = o_q + d_attn
    o_v = o_k + d_kv
    n_kc = d_kv // LANES

    if is_prompt:
        @pl.when(s == 0)
        def _():
            lru_ext[0:HALO, :] = jnp.zeros((HALO, d_lru), F32)
            pool_ext[0:HALO, :] = jnp.zeros((HALO, d_pool), F32)
            hcar[...] = jnp.zeros_like(hcar)
            kvar[:, :, 0:WINDOW, :] = jnp.zeros((N_KV_HEADS, 2, WINDOW, LANES), BF16)
            vvar[:, :, 0:WINDOW, :] = jnp.zeros((N_KV_HEADS, 2, WINDOW, LANES), BF16)
        h_prev = hcar[0:1, :]
    else:
        lru_ext[0:HALO, :] = jnp.zeros((HALO, d_lru), F32)
        pool_ext[0:HALO, :] = jnp.zeros((HALO, d_pool), F32)
        lru_ext[HALO - (CONV_WIDTH - 1):HALO, :] = sc_ref[0]
        pool_ext[HALO - POOL_PAD:HALO, :] = sp_ref[0]
        h_prev = sh_ref[0]
        kvar[:, :, WINDOW:2 * WINDOW, :] = jnp.zeros((N_KV_HEADS, 2, WINDOW, LANES), BF16)
        vvar[:, :, WINDOW:2 * WINDOW, :] = jnp.zeros((N_KV_HEADS, 2, WINDOW, LANES), BF16)
        _store_head_variants(kvar, 0, [ck_ref[0, :, c * LANES:(c + 1) * LANES] for c in range(n_kc)])
        _store_head_variants(vvar, 0, [cv_ref[0, :, c * LANES:(c + 1) * LANES] for c in range(n_kc)])

    lru_ext[HALO:HALO + tile, :] = proj_ref[:, 0:d_lru]
    pool_ext[HALO:HALO + tile, :] = proj_ref[:, o_pool:o_pool + d_pool]

    xc = convb_ref[...] + convw_ref[0:1, :] * lru_ext[HALO - 3:HALO - 3 + tile, :]
    for j in range(1, CONV_WIDTH):
        xc = xc + convw_ref[j:j + 1, :] * lru_ext[HALO - 3 + j:HALO - 3 + j + tile, :]
    neg = -lam_ref[...]
    sp_all = jnp.maximum(neg, 0.0) + jnp.log1p(jnp.exp(-jnp.abs(neg)))
    h_last = []
    for c in range(d_lru // LANES):
        cs = slice(c * LANES, (c + 1) * LANES)
        y, hs = _lru_chunk(xc[:, cs], proj_ref[:, o_glru + c * LANES:o_glru + (c + 1) * LANES],
                           h_prev[:, cs], wg_ref[c], ba_ref[:, cs], bx_ref[:, cs], sp_all[:, cs])
        mix_ref[:, cs] = y.astype(mix_ref.dtype)
        h_last.append(hs[tile - 1:tile, :])
    h_last = jnp.concatenate(h_last, axis=-1)

    row = lax.broadcasted_iota(I32, (tile, LANES), 0)
    pos = pos0 + s * tile + row
    for gi, w in enumerate(POOL_WINDOWS):
        cs = slice(gi * LANES, (gi + 1) * LANES)
        e = pool_ext[:, cs]
        acc = e
        step = 1
        while step < w:
            acc = acc + pltpu.roll(acc, step, axis=0)
            step *= 2
        cnt = jnp.minimum(pos + 1, w).astype(F32)
        dlt = acc[HALO:HALO + tile, :] / cnt - e[HALO:HALO + tile, :]
        y = jnp.dot(dlt.astype(BF16), poolw_ref[gi], preferred_element_type=F32) * pscale_ref[:, cs]
        mix_ref[:, d_lru + gi * LANES:d_lru + (gi + 1) * LANES] = y.astype(mix_ref.dtype)

    sinks = sink_ref[...] * LOG2_E
    o_attn = d_lru + d_pool
    cos = cos_ref[...]
    sin = sin_ref[...]
    q_chunks = [(qc * (ATTN_SCALE * LOG2_E)).astype(BF16)
                for qc in _rope(proj_ref[:, o_q:o_q + d_attn], cos, sin)]
    k_rot = _rope(proj_ref[:, o_k:o_k + d_kv], cos, sin)
    _store_head_variants(kvar, WINDOW, k_rot)
    _store_head_variants(vvar, WINDOW, [proj_ref[:, o_v + c * LANES:o_v + (c + 1) * LANES]
                                        for c in range(n_kc)])
    for blk in range(tile // qblock):
        rs = slice(blk * qblock, (blk + 1) * qblock)
        if is_prompt and blk == 0:
            lim = jnp.where(s == 0, WINDOW, 0)
        else:
            lim = 0
        _attend_block([qc[rs] for qc in q_chunks], kvar, vvar, blk * qblock, sinks, lim,
                      mix_ref, rs, o_attn)

    k_last = jnp.concatenate([kc[tile - qblock:tile] for kc in k_rot], axis=-1)
    if is_prompt:
        lru_ext[0:HALO, :] = lru_ext[tile:tile + HALO, :]
        pool_ext[0:HALO, :] = pool_ext[tile:tile + HALO, :]
        hcar[0:1, :] = h_last
        kvar[:, :, 0:WINDOW, :] = kvar[:, :, tile:tile + WINDOW, :]
        vvar[:, :, 0:WINDOW, :] = vvar[:, :, tile:tile + WINDOW, :]

        @pl.when(s == pl.num_programs(1) - 1)
        def _():
            ko_ref[0] = k_last
            vo_ref[0] = proj_ref[tile - qblock:tile, o_v:o_v + d_kv]
            ho_ref[0] = h_last
            co_ref[0] = lru_ext[HALO + tile - (CONV_WIDTH - 1):HALO + tile, :]
            po_ref[0] = pool_ext[HALO + tile - POOL_PAD:HALO + tile, :]
    else:
        ko_ref[0, 0:WINDOW - tile, :] = ck_ref[0, tile:WINDOW, :]
        ko_ref[0, WINDOW - tile:WINDOW, :] = k_last
        vo_ref[0, 0:WINDOW - tile, :] = cv_ref[0, tile:WINDOW, :]
        vo_ref[0, WINDOW - tile:WINDOW, :] = proj_ref[:, o_v:o_v + d_kv]
        ho_ref[0] = h_last
        co_ref[0] = lru_ext[HALO + tile - (CONV_WIDTH - 1):HALO + tile, :]
        po_ref[0] = pool_ext[HALO + tile - POOL_PAD:HALO + tile, :]


def _layer_consts(p, dims):
    d_lru, d_pool = dims["d_lru"], dims["d_pool"]
    hd = d_lru // LRU_HEADS
    per = LANES // hd
    nchunk = d_lru // LANES

    def blockdiag(w):
        w = w.reshape(nchunk, per, hd, hd)
        eye = jnp.eye(per, dtype=w.dtype)
        return jnp.einsum("cpij,pq->cpiqj", w, eye).reshape(nchunk, LANES, LANES)

    wg = jnp.concatenate([blockdiag(p["lru_wa"]), blockdiag(p["lru_wx"])], axis=-1).astype(BF16)
    return dict(
        convw=p["conv_w"], convb=p["conv_b"].reshape(1, d_lru), wg=wg,
        ba=p["lru_ba"].reshape(1, d_lru), bx=p["lru_bx"].reshape(1, d_lru),
        lam=p["lru_lambda"].reshape(1, d_lru), poolw=p["pool_w"].astype(BF16),
        pscale=p["pool_scale"].reshape(1, d_pool), sinks=p["attn_sinks"].reshape(1, -1))


def _rope_tables(pos):
    half = HEAD_DIM // 2
    inv = ROPE_THETA ** (-jnp.arange(half, dtype=F32) / half)
    ang = pos.astype(F32)[:, None] * inv[None, :]
    cos = jnp.cos(ang)
    sin = jnp.sin(ang)
    cos2 = jnp.concatenate([cos, cos], axis=-1)
    sin2 = jnp.concatenate([-sin, sin], axis=-1)
    reps = LANES // HEAD_DIM
    return jnp.tile(cos2, (1, reps)), jnp.tile(sin2, (1, reps))


CONST_NAMES = ("convw", "convb", "wg", "ba", "bx", "lam", "poolw", "pscale", "sinks")


def _const_specs(consts):
    return [pl.BlockSpec(consts[n].shape, functools.partial(lambda nd, *_: (0,) * nd, consts[n].ndim))
            for n in CONST_NAMES]


def _mixer_prompt(proj, consts, cos, sin, bsz, seq, dims, tile):
    d_lru, d_pool, d_attn, d_kv = dims["d_lru"], dims["d_pool"], dims["d_attn"], dims["d_kv"]
    d_in = proj.shape[1]
    d_mix = d_lru + d_pool + d_attn
    ns = seq // tile
    kern = functools.partial(_mixer_kernel, tile=tile, qblock=WINDOW, is_prompt=True, pos0=0,
                             d_lru=d_lru, d_pool=d_pool, d_attn=d_attn, d_kv=d_kv)
    return pl.pallas_call(
        kern,
        grid=(bsz, ns),
        in_specs=[
            pl.BlockSpec((tile, d_in), lambda b, s: (b * ns + s, 0)),
            pl.BlockSpec((tile, LANES), lambda b, s: (s, 0)),
            pl.BlockSpec((tile, LANES), lambda b, s: (s, 0)),
        ] + _const_specs(consts),
        out_specs=[
            pl.BlockSpec((tile, d_mix), lambda b, s: (b * ns + s, 0)),
            pl.BlockSpec((1, WINDOW, d_kv), lambda b, s: (b, 0, 0)),
            pl.BlockSpec((1, WINDOW, d_kv), lambda b, s: (b, 0, 0)),
            pl.BlockSpec((1, 1, d_lru), lambda b, s: (b, 0, 0)),
            pl.BlockSpec((1, CONV_WIDTH - 1, d_lru), lambda b, s: (b, 0, 0)),
            pl.BlockSpec((1, POOL_PAD, d_pool), lambda b, s: (b, 0, 0)),
        ],
        out_shape=[
            jax.ShapeDtypeStruct((bsz * seq, d_mix), BF16),
            jax.ShapeDtypeStruct((bsz, WINDOW, d_kv), F32),
            jax.ShapeDtypeStruct((bsz, WINDOW, d_kv), F32),
            jax.ShapeDtypeStruct((bsz, 1, d_lru), F32),
            jax.ShapeDtypeStruct((bsz, CONV_WIDTH - 1, d_lru), F32),
            jax.ShapeDtypeStruct((bsz, POOL_PAD, d_pool), F32),
        ],
        scratch_shapes=[
            pltpu.VMEM((tile + HALO, d_lru), F32),
            pltpu.VMEM((tile + HALO, d_pool), F32),
            pltpu.VMEM((N_KV_HEADS, 2, WINDOW + tile, LANES), BF16),
            pltpu.VMEM((N_KV_HEADS, 2, WINDOW + tile, LANES), BF16),
            pltpu.VMEM((8, d_lru), F32),
        ],
        compiler_params=pltpu.CompilerParams(
            dimension_semantics=("arbitrary", "arbitrary"), vmem_limit_bytes=VMEM_LIMIT),
        name="mixer_prompt",
    )(proj, cos, sin, *[consts[n] for n in CONST_NAMES])


def _mixer_sample_kernel(*refs, n_seq, t, d_lru, d_pool, d_attn, d_kv):
    n_in = 3 + len(CONST_NAMES)
    proj_ref, shared = refs[0], refs[1:n_in]
    state_in = refs[n_in:n_in + 5]
    mix_ref = refs[n_in + 5]
    state_out = refs[n_in + 6:n_in + 11]
    scratch = refs[n_in + 11:]
    for q in range(n_seq):
        one = lambda r: r.at[pl.ds(q, 1)]
        _mixer_kernel(proj_ref.at[pl.ds(q * t, t)], *shared, *[one(r) for r in state_in],
                      mix_ref.at[pl.ds(q * t, t)], *[one(r) for r in state_out],
                      *[r.at[q] for r in scratch],
                      tile=t, qblock=t, is_prompt=False, pos0=PAST_LEN,
                      d_lru=d_lru, d_pool=d_pool, d_attn=d_attn, d_kv=d_kv)


def _mixer_sample(proj, row0, consts, cos, sin, cache_k, cache_v, st_h, st_conv, st_pool, layer, dims):
    d_lru, d_pool, d_attn, d_kv = dims["d_lru"], dims["d_pool"], dims["d_attn"], dims["d_kv"]
    d_in = proj.shape[1]
    d_mix = d_lru + d_pool + d_attn
    db, win = cache_k.shape[1], cache_k.shape[2]
    t = cos.shape[0]
    nq = SAMPLE_SEQS_PER_STEP
    assert db % nq == 0 and row0 % (nq * t) == 0
    blk0 = row0 // (nq * t)
    kern = functools.partial(_mixer_sample_kernel, n_seq=nq, t=t,
                             d_lru=d_lru, d_pool=d_pool, d_attn=d_attn, d_kv=d_kv)
    return pl.pallas_call(
        kern,
        grid=(db // nq,),
        in_specs=[
            pl.BlockSpec((nq * t, d_in), lambda b: (blk0 + b, 0)),
            pl.BlockSpec((t, LANES), lambda b: (0, 0)),
            pl.BlockSpec((t, LANES), lambda b: (0, 0)),
        ] + _const_specs(consts) + [
            pl.BlockSpec((None, nq, win, d_kv), lambda b: (layer, b, 0, 0)),
            pl.BlockSpec((None, nq, win, d_kv), lambda b: (layer, b, 0, 0)),
            pl.BlockSpec((None, nq, 1, d_lru), lambda b: (layer, b, 0, 0)),
            pl.BlockSpec((None, nq, CONV_WIDTH - 1, d_lru), lambda b: (layer, b, 0, 0)),
            pl.BlockSpec((None, nq, POOL_PAD, d_pool), lambda b: (layer, b, 0, 0)),
        ],
        out_specs=[
            pl.BlockSpec((nq * t, d_mix), lambda b: (b, 0)),
            pl.BlockSpec((nq, win, d_kv), lambda b: (b, 0, 0)),
            pl.BlockSpec((nq, win, d_kv), lambda b: (b, 0, 0)),
            pl.BlockSpec((nq, 1, d_lru), lambda b: (b, 0, 0)),
            pl.BlockSpec((nq, CONV_WIDTH - 1, d_lru), lambda b: (b, 0, 0)),
            pl.BlockSpec((nq, POOL_PAD, d_pool), lambda b: (b, 0, 0)),
        ],
        out_shape=[
            jax.ShapeDtypeStruct((db * t, d_mix), F32),
            jax.ShapeDtypeStruct((db, win, d_kv), F32),
            jax.ShapeDtypeStruct((db, win, d_kv), F32),
            jax.ShapeDtypeStruct((db, 1, d_lru), F32),
            jax.ShapeDtypeStruct((db, CONV_WIDTH - 1, d_lru), F32),
            jax.ShapeDtypeStruct((db, POOL_PAD, d_pool), F32),
        ],
        scratch_shapes=[
            pltpu.VMEM((nq, t + HALO, d_lru), F32),
            pltpu.VMEM((nq, t + HALO, d_pool), F32),
            pltpu.VMEM((nq, N_KV_HEADS, 2, 2 * WINDOW, LANES), BF16),
            pltpu.VMEM((nq, N_KV_HEADS, 2, 2 * WINDOW, LANES), BF16),
        ],
        compiler_params=pltpu.CompilerParams(
            dimension_semantics=("arbitrary",), vmem_limit_bytes=VMEM_LIMIT),
        name="mixer_sample",
    )(proj, cos, sin, *[consts[n] for n in CONST_NAMES], cache_k, cache_v, st_h, st_conv, st_pool)


def _route(logits, run_cnt):
    t = logits.shape[0]
    lane = lax.broadcasted_iota(I32, (t, LANES), 1)
    lane_f = lane.astype(F32)
    ninf = -jnp.inf
    big = float(LANES)
    is_g = lane < N_EXPERT_GROUPS
    lg = jnp.where(is_g, logits, ninf)
    mg = jnp.max(lg, axis=-1, keepdims=True)
    g_top = jnp.min(jnp.where(lg == mg, lane_f, big), axis=-1, keepdims=True).astype(I32)
    pg_top = 1.0 / jnp.sum(jnp.exp(lg - mg), axis=-1, keepdims=True)
    base = N_EXPERT_GROUPS + EXPERTS_PER_GROUP * g_top
    in_grp = (lane >= base) & (lane < base + EXPERTS_PER_GROUP)
    le = jnp.where(in_grp, logits, ninf)
    m1 = jnp.max(le, axis=-1, keepdims=True)
    i1 = jnp.min(jnp.where(le == m1, lane_f, big), axis=-1, keepdims=True).astype(I32)
    le2 = jnp.where(lane == i1, ninf, le)
    m2 = jnp.max(le2, axis=-1, keepdims=True)
    i2 = jnp.min(jnp.where((le2 == m2) & in_grp & (lane != i1), lane_f, big),
                 axis=-1, keepdims=True).astype(I32)
    se = jnp.sum(jnp.exp(le - m1), axis=-1, keepdims=True)
    p1 = 1.0 / se
    p2 = jnp.exp(m2 - m1) / se
    tot = p1 + p2
    w1 = (p1 / tot) * pg_top
    w2 = (p2 / tot) * pg_top
    a = i1 - base
    b = i2 - base
    lo = jnp.minimum(a, b)
    hi = jnp.maximum(a, b)
    w_lo = jnp.where(a < b, w1, w2)
    w_hi = jnp.where(a < b, w2, w1)
    pid = jnp.where(lo == 0, hi - 1, jnp.where(lo == 1, jnp.where(hi == 3, 3, 4), 5))
    swap = pid == 5
    w_a = jnp.where(swap, w_hi, w_lo)
    w_b = jnp.where(swap, w_lo, w_hi)
    cls = g_top * N_PAIRS + pid
    onehot = lane == cls
    ti = lax.broadcasted_iota(I32, (t, t), 0)
    tj = lax.broadcasted_iota(I32, (t, t), 1)
    lower = jnp.where(tj <= ti, 1.0, 0.0).astype(BF16)
    prefix = jnp.dot(lower, jnp.where(onehot, 1.0, 0.0).astype(BF16), preferred_element_type=F32)
    rank = jnp.sum(jnp.where(onehot, prefix - 1.0 + run_cnt, 0.0), axis=-1, keepdims=True)
    info = jnp.where(lane == 0, w_a, jnp.where(lane == 1, w_b, jnp.where(
        lane == 2, cls.astype(F32), jnp.where(lane == 3, rank, 0.0))))
    return info, run_cnt + prefix[t - 1:t, :]


def _out_proj_kernel(mixp_ref, mixs_ref, xm_ref, xt_ref, w_hbm, g_ref, wr_ref, br_ref,
                     o_ref, info_ref, cnt_ref, w_bf, stage, sem, run_cnt, *, n_main_blocks, d, layer, slab):
    i = pl.program_id(0)

    @pl.when(i == 0)
    def _():
        _load_weight_bf16(w_hbm.at[layer], w_bf, stage, sem)
        run_cnt[...] = jnp.zeros_like(run_cnt)

    mix = jnp.where(i < n_main_blocks, mixp_ref[...], mixs_ref[...].astype(BF16))
    xres = (_pick_rows(i, n_main_blocks, xm_ref, xt_ref, d, slab)
            + jnp.dot(mix, w_bf[...], preferred_element_type=F32))
    h2 = _rms(xres, g_ref[...])
    logits = jnp.dot(h2.astype(BF16), wr_ref[...], preferred_element_type=F32) + br_ref[...]
    info, cnt = _route(logits, run_cnt[0:1, :])
    run_cnt[0:1, :] = cnt
    pitch = _slab_pitch(d)
    _write_slabs(o_ref, (), xres, pitch)
    o_ref[pl.ds(d // LANES, ROW_BLOCK, stride=pitch), :] = info
    info_ref[...] = info
    cnt_ref[...] = jnp.broadcast_to(cnt, cnt_ref.shape)


def _out_proj(mix_p, mix_s, x_main, x_tail, tail_block0, slab, w, layer, g, wr, br, n_rows):
    d = g.shape[1]
    d_mix = w.shape[1]
    npb = mix_p.shape[0] // ROW_BLOCK
    pitch = _slab_pitch(d)
    kern = functools.partial(_out_proj_kernel, n_main_blocks=npb, d=d, layer=layer, slab=slab)
    return pl.pallas_call(
        kern,
        grid=(n_rows // ROW_BLOCK,),
        in_specs=[
            pl.BlockSpec((ROW_BLOCK, d_mix), lambda i: (jnp.minimum(i, npb - 1), 0)),
            pl.BlockSpec((ROW_BLOCK, d_mix), lambda i: (jnp.maximum(i - npb, 0), 0)),
        ] + _row_specs(d, npb, tail_block0, slab) + [
            pl.BlockSpec(memory_space=pl.ANY),
            pl.BlockSpec((1, d), lambda i: (0, 0)),
            pl.BlockSpec((d, LANES), lambda i: (0, 0)),
            pl.BlockSpec((1, LANES), lambda i: (0, 0)),
        ],
        out_specs=[pl.BlockSpec((ROW_BLOCK * pitch, LANES), lambda i: (i, 0)),
                   pl.BlockSpec((ROW_BLOCK, LANES), lambda i: (i, 0)),
                   pl.BlockSpec((8, LANES), lambda i: (0, 0))],
        out_shape=[jax.ShapeDtypeStruct((n_rows * pitch, LANES), F32),
                   jax.ShapeDtypeStruct((n_rows, LANES), F32),
                   jax.ShapeDtypeStruct((8, LANES), F32)],
        scratch_shapes=[
            pltpu.VMEM((d_mix, d), BF16),
            pltpu.VMEM((2, ROW_BLOCK, d), F32),
            pltpu.SemaphoreType.DMA((2,)),
            pltpu.VMEM((8, LANES), F32),
        ],
        compiler_params=pltpu.CompilerParams(
            dimension_semantics=("arbitrary",), vmem_limit_bytes=VMEM_LIMIT),
        name="out_proj",
    )(mix_p, mix_s, x_main, x_tail, w, g, wr, br)


def _moe_kernel(cls_ref, rank_ref, off_ref, cnt_ref, ea_ref, eb_ref, chga_ref, chgb_ref, nch_ref,
                xr_hbm, g_ref, w1a_ref, w3a_ref, w2a_ref, w1b_ref, w3b_ref, w2b_ref,
                xo_hbm,
                perm, xbuf, obuf, wa1, wa3, wa2, wb1, wb3, wb2, gsem, ssem, *, d, n_rows):
    del ea_ref, eb_ref
    m = MOE_CHUNK
    pitch = _slab_pitch(d)
    n_slabs = d // LANES
    c = pl.program_id(0)
    n = nch_ref[0]
    slot = c % 2
    other = 1 - slot

    def start_gather(chunk, sl, j):
        tok = jnp.maximum(perm[(chunk + 1) * m + j], 0)
        pltpu.make_async_copy(xr_hbm.at[pl.ds(tok * pitch, pitch)],
                              xbuf.at[sl, pl.ds(j * pitch, pitch)], gsem.at[sl]).start()

    def start_scatter(chunk, sl, j):
        tok = perm[(chunk + 1) * m + j]
        r = jnp.where(tok < 0, n_rows + sl * m + j, tok)
        pltpu.make_async_copy(obuf.at[sl, pl.ds(j * pitch, pitch)],
                              xo_hbm.at[pl.ds(r * pitch, pitch)], ssem.at[sl]).start()

    def wait_gather(sl):
        pltpu.make_async_copy(xr_hbm.at[pl.ds(0, m * pitch)], xbuf.at[sl], gsem.at[sl]).wait()

    def wait_scatter(sl):
        pltpu.make_async_copy(obuf.at[sl], xo_hbm.at[pl.ds(0, m * pitch)], ssem.at[sl]).wait()

    @pl.when(c == 0)
    def _():
        obuf[...] = jnp.zeros_like(obuf)
        pad = pltpu.make_async_copy(obuf.at[0], xo_hbm.at[pl.ds(n_rows * pitch, m * pitch)], ssem.at[0])
        pad.start()
        pad.wait()

        def fill(lo, hi):
            def body(i, z):
                perm[i] = -1
                return z
            lax.fori_loop(lo, hi, body, 0)

        def first_gather(j, z):
            start_gather(0, 0, j)
            return z

        fill(0, m)
        fill((n + 1) * m, (n + 3) * m)
        for k in range(N_CLASSES):
            base = (off_ref[k] + 1) * m
            cnt = cnt_ref[k]
            fill(base + cnt, base + ((cnt + m - 1) // m) * m)

        def place(t, z):
            perm[(off_ref[cls_ref[t]] + 1) * m + rank_ref[t]] = t
            return z
        lax.fori_loop(0, n_rows, place, 0, unroll=8)
        lax.fori_loop(0, m, first_gather, 0)

    @pl.when(c <= n)
    def _():
        wait_gather(slot)

        @pl.when(c >= 1)
        def _():
            wait_scatter(slot)

        @pl.when(chga_ref[c] == 1)
        def _():
            wa1[...] = w1a_ref[0].astype(BF16)
            wa3[...] = w3a_ref[0].astype(BF16)
            wa2[...] = w2a_ref[0].astype(BF16)

        @pl.when(chgb_ref[c] == 1)
        def _():
            wb1[...] = w1b_ref[0].astype(BF16)
            wb3[...] = w3b_ref[0].astype(BF16)
            wb2[...] = w2b_ref[0].astype(BF16)

        for j in range(m):
            start_gather(c + 1, other, j)
        for j in range(m):
            start_scatter(c - 1, other, j)
        x = _read_slabs(xbuf, (slot,), m, n_slabs, pitch)
        info = xbuf[slot, pl.ds(n_slabs, m, stride=pitch), :]
        w_a = info[:, 0:1]
        w_b = info[:, 1:2]
        h = _rms(x, g_ref[...]).astype(BF16)
        hid_a = (jax.nn.silu(jnp.dot(h, wa1[...], preferred_element_type=F32))
                 * jnp.dot(h, wa3[...], preferred_element_type=F32) * w_a).astype(BF16)
        hid_b = (jax.nn.silu(jnp.dot(h, wb1[...], preferred_element_type=F32))
                 * jnp.dot(h, wb3[...], preferred_element_type=F32) * w_b).astype(BF16)
        y = (jnp.dot(hid_a, wa2[...], preferred_element_type=F32)
             + jnp.dot(hid_b, wb2[...], preferred_element_type=F32))
        _write_slabs(obuf, (slot,), x + y, pitch)

        @pl.when(c == n)
        def _():
            wait_gather(other)
            wait_scatter(other)


def _moe(xr, g, w1, w3, w2, tables, n_rows, c_max):
    d = g.shape[1]
    f = w1.shape[2]
    m = MOE_CHUNK
    pitch = _slab_pitch(d)
    kern = functools.partial(_moe_kernel, d=d, n_rows=n_rows)

    def wspec(shape, which):
        return pl.BlockSpec(shape, lambda c, *pref: (pref[which][c], 0, 0))

    grid_spec = pltpu.PrefetchScalarGridSpec(
        num_scalar_prefetch=9,
        grid=(c_max + 1,),
        in_specs=[
            pl.BlockSpec(memory_space=pl.ANY),
            pl.BlockSpec((1, d), lambda c, *pref: (0, 0)),
            wspec((1, d, f), 4), wspec((1, d, f), 4), wspec((1, f, d), 4),
            wspec((1, d, f), 5), wspec((1, d, f), 5), wspec((1, f, d), 5),
        ],
        out_specs=pl.BlockSpec(memory_space=pl.ANY),
        scratch_shapes=[
            pltpu.SMEM(((c_max + 3) * m,), I32),
            pltpu.VMEM((2, m * pitch, LANES), F32),
            pltpu.VMEM((2, m * pitch, LANES), F32),
            pltpu.VMEM((d, f), BF16), pltpu.VMEM((d, f), BF16), pltpu.VMEM((f, d), BF16),
            pltpu.VMEM((d, f), BF16), pltpu.VMEM((d, f), BF16), pltpu.VMEM((f, d), BF16),
            pltpu.SemaphoreType.DMA((2,)),
            pltpu.SemaphoreType.DMA((2,)),
        ],
    )
    return pl.pallas_call(
        kern,
        grid_spec=grid_spec,
        out_shape=jax.ShapeDtypeStruct(((n_rows + 2 * m) * pitch, LANES), F32),
        compiler_params=pltpu.CompilerParams(
            dimension_semantics=("arbitrary",), vmem_limit_bytes=VMEM_LIMIT),
        name="moe",
    )(*tables, xr, g, w1, w3, w2, w1, w3, w2)


def _chunk_tables(counts, c_max, expert0):
    m = MOE_CHUNK
    nch_c = (counts + m - 1) // m
    ch_end = jnp.cumsum(nch_c)
    ch_off = ch_end - nch_c
    n_chunks = ch_end[-1]
    chunk = jnp.minimum(jnp.arange(c_max + 1, dtype=I32), jnp.maximum(n_chunks - 1, 0))
    ccls = jnp.sum((chunk[:, None] >= ch_end[None, :]).astype(I32), axis=1)
    ccls = jnp.minimum(ccls, N_CLASSES - 1)
    grp = ccls // N_PAIRS
    pid = ccls % N_PAIRS
    ea = expert0 + grp * EXPERTS_PER_GROUP + jnp.asarray(PAIR_SLOT_A, I32)[pid]
    eb = expert0 + grp * EXPERTS_PER_GROUP + jnp.asarray(PAIR_SLOT_B, I32)[pid]
    first = jnp.ones((1,), I32)
    chga = jnp.concatenate([first, (ea[1:] != ea[:-1]).astype(I32)])
    chgb = jnp.concatenate([first, (eb[1:] != eb[:-1]).astype(I32)])
    return ch_off.astype(I32), counts, ea, eb, chga, chgb, n_chunks.reshape(1).astype(I32)


def _final_norm_kernel(x_ref, g_ref, op_ref, os_ref, *, n_prompt_blocks):
    i = pl.program_id(0)
    d = g_ref.shape[1]
    y = _rms(_read_slabs(x_ref, (), ROW_BLOCK, d // LANES, _slab_pitch(d)), g_ref[...])

    @pl.when(i < n_prompt_blocks)
    def _():
        op_ref[...] = y

    @pl.when(i >= n_prompt_blocks)
    def _():
        os_ref[...] = y


def _final_norm(x, g, n_prompt, n_sample):
    d = g.shape[1]
    npb = n_prompt // ROW_BLOCK
    nsb = n_sample // ROW_BLOCK
    kern = functools.partial(_final_norm_kernel, n_prompt_blocks=npb)
    return pl.pallas_call(
        kern,
        grid=(npb + nsb,),
        in_specs=[pl.BlockSpec((ROW_BLOCK * _slab_pitch(d), LANES), lambda i: (i, 0)),
                  pl.BlockSpec((1, d), lambda i: (0, 0))],
        out_specs=[pl.BlockSpec((ROW_BLOCK, d), lambda i: (jnp.minimum(i, npb - 1), 0)),
                   pl.BlockSpec((ROW_BLOCK, d), lambda i: (jnp.maximum(i - npb, 0), 0))],
        out_shape=[jax.ShapeDtypeStruct((n_prompt, d), F32),
                   jax.ShapeDtypeStruct((n_sample, d), F32)],
        compiler_params=pltpu.CompilerParams(dimension_semantics=("arbitrary",)),
        name="final_norm",
    )(x, g)


def kernel(x_prompt, x_sample, cache_k, cache_v, state_lru_h, state_conv, state_pool, norm1_g, w_in, conv_w, conv_b, lru_wa, lru_ba, lru_wx, lru_bx, lru_lambda, pool_w, pool_scale, attn_sinks, w_out, norm2_g, router_group_w, router_group_b, router_expert_w, router_expert_b, expert_w1, expert_w3, expert_w2, final_norm_g):
    bsz, seq, d = x_prompt.shape
    db, ds, _ = x_sample.shape
    depth = w_in.shape[0]
    win = cache_k.shape[2]
    nkv = cache_k.shape[3]
    d_lru = lru_lambda.shape[1]
    d_pool = pool_scale.shape[1]
    d_kv = nkv * cache_k.shape[4]
    d_attn = attn_sinks.shape[1] * HEAD_DIM
    dims = dict(d_lru=d_lru, d_pool=d_pool, d_attn=d_attn, d_kv=d_kv)
    n_prompt = bsz * seq
    n_sample = db * ds
    n_rows = n_prompt + n_sample
    assert n_prompt % ROW_BLOCK == 0 and n_sample % ROW_BLOCK == 0
    assert win == WINDOW and nkv == N_KV_HEADS and ds <= 8 and WINDOW % ds == 0
    tile = 512 if seq % 512 == 0 else WINDOW
    c_max = -(-n_rows // MOE_CHUNK) + N_CLASSES
    npb = n_prompt // ROW_BLOCK

    cos_p, sin_p = _rope_tables(jnp.arange(seq))
    cos_s, sin_s = _rope_tables(PAST_LEN + jnp.arange(ds))
    wr_all = jnp.concatenate([router_group_w, router_expert_w], axis=2)
    wr_all = jnp.pad(wr_all, ((0, 0), (0, 0), (0, LANES - wr_all.shape[2]))).astype(BF16)
    br_all = jnp.concatenate([router_group_b, router_expert_b], axis=1)
    br_all = jnp.pad(br_all, ((0, 0), (0, LANES - br_all.shape[1])))

    x_main, x_tail, tail0, slab = x_prompt.reshape(n_prompt, d), x_sample.reshape(n_sample, d), 0, False
    ck = cache_k.reshape(depth, db, win, d_kv)
    cv = cache_v.reshape(depth, db, win, d_kv)
    st_h = state_lru_h.reshape(depth, db, 1, d_lru)
    n_exp, _, d_exp = expert_w1.shape[1:]
    w1 = expert_w1.reshape(depth * n_exp, d, d_exp)
    w3 = expert_w3.reshape(depth * n_exp, d, d_exp)
    w2 = expert_w2.reshape(depth * n_exp, d_exp, d)
    outs = [[] for _ in range(10)]
    for l in range(depth):
        p = dict(conv_w=conv_w[l], conv_b=conv_b[l], lru_wa=lru_wa[l], lru_ba=lru_ba[l],
                 lru_wx=lru_wx[l], lru_bx=lru_bx[l], lru_lambda=lru_lambda[l], pool_w=pool_w[l],
                 pool_scale=pool_scale[l], attn_sinks=attn_sinks[l])
        consts = _layer_consts(p, dims)
        proj = _in_proj(x_main, x_tail, tail0, slab, norm1_g[l].reshape(1, d), w_in, l, n_rows, n_prompt)
        mix_p, pk, pv, ph, pc, pp = _mixer_prompt(proj, consts, cos_p, sin_p, bsz, seq, dims, tile)
        mix_s, sk, sv, sh, sc, sp = _mixer_sample(
            proj, n_prompt, consts, cos_s, sin_s, ck, cv, st_h, state_conv, state_pool, l, dims)
        xr, info, cnt = _out_proj(mix_p, mix_s, x_main, x_tail, tail0, slab, w_out, l,
                                  norm2_g[l].reshape(1, d), wr_all[l], br_all[l:l + 1], n_rows)
        route = info[:, 2:4].astype(I32)
        tables = (route[:, 0], route[:, 1]) + _chunk_tables(cnt[0, :N_CLASSES].astype(I32), c_max, l * n_exp)
        x = _moe(xr, norm2_g[l].reshape(1, d), w1, w3, w2, tables, n_rows, c_max)
        x_main, x_tail, tail0, slab = x, x, npb, True
        for lst, val in zip(outs, (pk.reshape(bsz, WINDOW, nkv, HEAD_DIM), pv.reshape(bsz, WINDOW, nkv, HEAD_DIM),
                                   ph.reshape(bsz, d_lru), pc, pp,
                                   sk.reshape(db, win, nkv, HEAD_DIM), sv.reshape(db, win, nkv, HEAD_DIM),
                                   sh.reshape(db, d_lru), sc, sp)):
            lst.append(val)
    y_p, y_s = _final_norm(x_main, final_norm_g.reshape(1, d), n_prompt, n_sample)
    return (y_p.reshape(bsz, seq, d), y_s.reshape(db, ds, d)) + tuple(jnp.stack(o) for o in outs)
```

```python
import functools

import jax
import jax.numpy as jnp
from jax import lax
from jax.experimental import pallas as pl
from jax.experimental.pallas import tpu as pltpu

F32 = jnp.float32
BF16 = jnp.bfloat16
I32 = jnp.int32

LRU_HEADS = 8
CONV_WIDTH = 4
LRU_C = 8.0
POOL_WINDOWS = (2, 4, 8, 16)
POOL_PAD = max(POOL_WINDOWS) - 1
HEAD_DIM = 64
N_KV_HEADS = 4
WINDOW = 128
ROPE_THETA = 10000.0
ATTN_SCALE = HEAD_DIM ** -0.5
LOG2_E = 1.4426950408889634
N_EXPERT_GROUPS = 4
EXPERTS_PER_GROUP = 4
N_EXPERTS = N_EXPERT_GROUPS * EXPERTS_PER_GROUP
RMS_EPS = 1e-6
PAST_LEN = 16384

LANES = 128
HALF = LANES // 2
HALO = 16
ROW_BLOCK = 256
SAMPLE_SEQS_PER_STEP = 4
MOE_CHUNK = 128
PAIR_SLOT_A = (0, 0, 0, 1, 1, 3)
PAIR_SLOT_B = (1, 2, 3, 3, 2, 2)
N_PAIRS = len(PAIR_SLOT_A)
N_CLASSES = N_EXPERT_GROUPS * N_PAIRS
VMEM_LIMIT = 52 * 1024 * 1024


def _rms(x, g):
    return (x * lax.rsqrt(jnp.mean(x * x, axis=-1, keepdims=True) + RMS_EPS)) * g


def _load_weight_bf16(w_hbm, w_bf, stage, sem):
    rows = stage.shape[1]
    n = w_hbm.shape[0] // rows

    def copy(i, sl):
        return pltpu.make_async_copy(w_hbm.at[pl.ds(i * rows, rows)], stage.at[sl], sem.at[sl])

    copy(0, 0).start()
    for i in range(n):
        sl = i % 2
        if i + 1 < n:
            copy(i + 1, 1 - sl).start()
        copy(i, sl).wait()
        w_bf[i * rows:(i + 1) * rows, :] = stage[sl].astype(BF16)


def _slab_pitch(d):
    return d // LANES + 1


def _read_slabs(ref, lead, n_tok, n_slabs, pitch):
    return jnp.concatenate(
        [ref[lead + (pl.ds(k, n_tok, stride=pitch), slice(None))] for k in range(n_slabs)], axis=1)


def _write_slabs(ref, lead, val, pitch):
    n_tok = val.shape[0]
    for k in range(val.shape[1] // LANES):
        ref[lead + (pl.ds(k, n_tok, stride=pitch), slice(None))] = val[:, k * LANES:(k + 1) * LANES]


def _pick_rows(i, n_main_blocks, main_ref, tail_ref, d, slab):
    if slab:
        return _read_slabs(main_ref, (), ROW_BLOCK, d // LANES, _slab_pitch(d))
    return jnp.where(i < n_main_blocks, main_ref[...], tail_ref[...])


def _in_proj_kernel(xm_ref, xt_ref, g_ref, w_hbm, o_ref, w_bf, stage, sem, *, n_main_blocks, layer, slab):
    i = pl.program_id(0)

    @pl.when(i == 0)
    def _():
        _load_weight_bf16(w_hbm.at[layer], w_bf, stage, sem)

    h = _rms(_pick_rows(i, n_main_blocks, xm_ref, xt_ref, g_ref.shape[1], slab), g_ref[...])
    o_ref[...] = jnp.dot(h.astype(BF16), w_bf[...], preferred_element_type=F32)


def _row_specs(d, n_main_blocks, tail_block0, slab):
    if slab:
        shape = (ROW_BLOCK * _slab_pitch(d), LANES)
        return [pl.BlockSpec(shape, lambda i: (i, 0)), pl.BlockSpec(shape, lambda i: (tail_block0, 0))]
    shape = (ROW_BLOCK, d)
    return [pl.BlockSpec(shape, lambda i: (jnp.minimum(i, n_main_blocks - 1), 0)),
            pl.BlockSpec(shape, lambda i: (jnp.maximum(i - n_main_blocks, 0) + tail_block0, 0))]


def _in_proj(x_main, x_tail, tail_block0, slab, g, w, layer, n_rows, n_main):
    d = g.shape[1]
    d_in = w.shape[2]
    nmb = n_main // ROW_BLOCK
    kern = functools.partial(_in_proj_kernel, n_main_blocks=nmb, layer=layer, slab=slab)
    return pl.pallas_call(
        kern,
        grid=(n_rows // ROW_BLOCK,),
        in_specs=_row_specs(d, nmb, tail_block0, slab) + [
            pl.BlockSpec((1, d), lambda i: (0, 0)),
            pl.BlockSpec(memory_space=pl.ANY),
        ],
        out_specs=pl.BlockSpec((ROW_BLOCK, d_in), lambda i: (i, 0)),
        out_shape=jax.ShapeDtypeStruct((n_rows, d_in), F32),
        scratch_shapes=[
            pltpu.VMEM((d, d_in), BF16),
            pltpu.VMEM((2, ROW_BLOCK, d_in), F32),
            pltpu.SemaphoreType.DMA((2,)),
        ],
        compiler_params=pltpu.CompilerParams(
            dimension_semantics=("arbitrary",), vmem_limit_bytes=VMEM_LIMIT),
        name="in_proj",
    )(x_main, x_tail, g, w)


def _scan_linear(a, b):
    t = a.shape[0]
    row = lax.broadcasted_iota(I32, a.shape, 0)
    d = 1
    while d < min(t, 8):
        a_sh = pltpu.roll(a, d, axis=0)
        b_sh = pltpu.roll(b, d, axis=0)
        m = row >= d
        b = jnp.where(m, a * b_sh + b, b)
        a = jnp.where(m, a * a_sh, a)
        d *= 2
    while d < t:
        b = jnp.concatenate([b[:d], a[d:] * b[:t - d] + b[d:]], axis=0)
        a = jnp.concatenate([a[:d], a[d:] * a[:t - d]], axis=0)
        d *= 2
    return a, b


def _lru_chunk(xc, g, h_prev, wg, ba, bx, sp):
    pre = jnp.dot(xc.astype(BF16), wg, preferred_element_type=F32)
    r = jax.nn.sigmoid(pre[:, :LANES] + ba)
    ig = jax.nn.sigmoid(pre[:, LANES:] + bx)
    log_a = (-LRU_C * r) * sp
    a = jnp.exp(log_a)
    v = 1.0 - a * a
    bterm = jnp.where(v > 0.0, v * lax.rsqrt(v), 0.0) * ig * xc
    a_cum, h0 = _scan_linear(a, bterm)
    hs = a_cum * h_prev + h0
    return hs * jax.nn.gelu(g), hs


def _rope(x, cos, sin_signed):
    n = x.shape[1] // LANES
    lane = lax.broadcasted_iota(I32, (x.shape[0], LANES), 1)
    first = (lane % HEAD_DIM) < (HEAD_DIM // 2)
    outs = []
    for c in range(n):
        xc = x[:, c * LANES:(c + 1) * LANES]
        swapped = jnp.where(first, pltpu.roll(xc, LANES - HEAD_DIM // 2, axis=1),
                            pltpu.roll(xc, HEAD_DIM // 2, axis=1))
        outs.append(xc * cos + swapped * sin_signed)
    return outs


def _store_head_variants(var_ref, row0, chunks):
    t = chunks[0].shape[0]
    lane = lax.broadcasted_iota(I32, (t, LANES), 1)
    for kc, x in enumerate(chunks):
        swapped = pltpu.roll(x, HALF, axis=1)
        for hh in range(2):
            for p in range(2):
                src = x if p == hh else swapped
                keep = (lane < HALF) if p == 0 else (lane >= HALF)
                var_ref[2 * kc + hh, p, row0:row0 + t, :] = jnp.where(keep, src, 0.0).astype(BF16)


def _attend_block(q_chunks, kvar, vvar, key0, sinks, lim, out_ref, out_rows, out_col0):
    qb = q_chunks[0].shape[0]
    nk = 2 * WINDOW
    gq = (2 * len(q_chunks)) // N_KV_HEADS
    rows = 2 * qb
    qi = lax.broadcasted_iota(I32, (rows, nk), 0) % qb
    kj = lax.broadcasted_iota(I32, (rows, nk), 1)
    valid = ((kj < WINDOW) & (kj > qi + lim)) | ((kj >= WINDOW) & (kj - WINDOW <= qi))
    top = lax.broadcasted_iota(I32, (rows, 1), 0) < qb
    lane = lax.broadcasted_iota(I32, (nk, LANES), 1)
    ones_lo = jnp.where(lane < HALF, 1.0, 0.0).astype(BF16)
    ones_hi = jnp.where(lane >= HALF, 1.0, 0.0).astype(BF16)
    lane_o = lax.broadcasted_iota(I32, (rows, LANES), 1)
    nt = (((1,), (1,)), ((), ()))
    for c in range(N_KV_HEADS):
        c0 = c * gq // 2
        qs = jnp.concatenate([q_chunks[c0], q_chunks[c0 + 1]], axis=0)
        es, sink_terms = [], []
        for p in range(2):
            kc = kvar[c, p, key0:key0 + nk, :]
            s = lax.dot_general(qs, kc, nt, preferred_element_type=F32)
            s = jnp.where(valid, s, -jnp.inf)
            h0 = c * gq + p
            sink = jnp.where(top, sinks[:, h0:h0 + 1], sinks[:, h0 + 2:h0 + 3])
            m = jnp.maximum(jnp.max(s, axis=-1, keepdims=True), sink)
            es.append(jnp.exp2(s - m).astype(BF16))
            sink_terms.append(jnp.exp2(sink - m))
        r0 = jnp.concatenate([vvar[c, 0, key0:key0 + nk, :], ones_lo], axis=1)
        r1 = jnp.concatenate([vvar[c, 1, key0:key0 + nk, :], ones_hi], axis=1)
        od = (jnp.dot(es[0], r0, preferred_element_type=F32)
              + jnp.dot(es[1], r1, preferred_element_type=F32))
        den = od[:, LANES:] + jnp.where(lane_o < HALF, sink_terms[0], sink_terms[1])
        o = (od[:, :LANES] / den).astype(out_ref.dtype)
        out_ref[out_rows, out_col0 + c0 * LANES:out_col0 + (c0 + 1) * LANES] = o[0:qb]
        out_ref[out_rows, out_col0 + (c0 + 1) * LANES:out_col0 + (c0 + 2) * LANES] = o[qb:rows]


def _mixer_kernel(*refs, tile, qblock, is_prompt, pos0, d_lru, d_pool, d_attn, d_kv):
    (proj_ref, cos_ref, sin_ref, convw_ref, convb_ref, wg_ref, ba_ref, bx_ref, lam_ref,
     poolw_ref, pscale_ref, sink_ref) = refs[:12]
    if is_prompt:
        (mix_ref, ko_ref, vo_ref, ho_ref, co_ref, po_ref,
         lru_ext, pool_ext, kvar, vvar, hcar) = refs[12:]
    else:
        (ck_ref, cv_ref, sh_ref, sc_ref, sp_ref,
         mix_ref, ko_ref, vo_ref, ho_ref, co_ref, po_ref,
         lru_ext, pool_ext, kvar, vvar) = refs[12:]

    s = pl.program_id(1) if is_prompt else 0
    o_glru = d_lru
    o_pool = 2 * d_lru
    o_q = o_pool + d_pool
    o_k = o_q + d_attn
    o_v = o_k + d_kv
    n_kc = d_kv // LANES

    if is_prompt:
        @pl.when(s == 0)
        def _():
            lru_ext[0:HALO, :] = jnp.zeros((HALO, d_lru), F32)
            pool_ext[0:HALO, :] = jnp.zeros((HALO, d_pool), F32)
            hcar[...] = jnp.zeros_like(hcar)
            kvar[:, :, 0:WINDOW, :] = jnp.zeros((N_KV_HEADS, 2, WINDOW, LANES), BF16)
            vvar[:, :, 0:WINDOW, :] = jnp.zeros((N_KV_HEADS, 2, WINDOW, LANES), BF16)
        h_prev = hcar[0:1, :]
    else:
        lru_ext[0:HALO, :] = jnp.zeros((HALO, d_lru), F32)
        pool_ext[0:HALO, :] = jnp.zeros((HALO, d_pool), F32)
        lru_ext[HALO - (CONV_WIDTH - 1):HALO, :] = sc_ref[0]
        pool_ext[HALO - POOL_PAD:HALO, :] = sp_ref[0]
        h_prev = sh_ref[0]
        kvar[:, :, WINDOW:2 * WINDOW, :] = jnp.zeros((N_KV_HEADS, 2, WINDOW, LANES), BF16)
        vvar[:, :, WINDOW:2 * WINDOW, :] = jnp.zeros((N_KV_HEADS, 2, WINDOW, LANES), BF16)
        _store_head_variants(kvar, 0, [ck_ref[0, :, c * LANES:(c + 1) * LANES] for c in range(n_kc)])
        _store_head_variants(vvar, 0, [cv_ref[0, :, c * LANES:(c + 1) * LANES] for c in range(n_kc)])

    lru_ext[HALO:HALO + tile, :] = proj_ref[:, 0:d_lru]
    pool_ext[HALO:HALO + tile, :] = proj_ref[:, o_pool:o_pool + d_pool]

    xc = convb_ref[...] + convw_ref[0:1, :] * lru_ext[HALO - 3:HALO - 3 + tile, :]
    for j in range(1, CONV_WIDTH):
        xc = xc + convw_ref[j:j + 1, :] * lru_ext[HALO - 3 + j:HALO - 3 + j + tile, :]
    neg = -lam_ref[...]
    sp_all = jnp.maximum(neg, 0.0) + jnp.log1p(jnp.exp(-jnp.abs(neg)))
    h_last = []
    for c in range(d_lru // LANES):
        cs = slice(c * LANES, (c + 1) * LANES)
        y, hs = _lru_chunk(xc[:, cs], proj_ref[:, o_glru + c * LANES:o_glru + (c + 1) * LANES],
                           h_prev[:, cs], wg_ref[c], ba_ref[:, cs], bx_ref[:, cs], sp_all[:, cs])
        mix_ref[:, cs] = y.astype(mix_ref.dtype)
        h_last.append(hs[tile - 1:tile, :])
    h_last = jnp.concatenate(h_last, axis=-1)

    row = lax.broadcasted_iota(I32, (tile, LANES), 0)
    pos = pos0 + s * tile + row
    for gi, w in enumerate(POOL_WINDOWS):
        cs = slice(gi * LANES, (gi + 1) * LANES)
        e = pool_ext[:, cs]
        acc = e
        step = 1
        while step < w:
            acc = acc + pltpu.roll(acc, step, axis=0)
            step *= 2
        cnt = jnp.minimum(pos + 1, w).astype(F32)
        dlt = acc[HALO:HALO + tile, :] / cnt - e[HALO:HALO + tile, :]
        y = jnp.dot(dlt.astype(BF16), poolw_ref[gi], preferred_element_type=F32) * pscale_ref[:, cs]
        mix_ref[:, d_lru + gi * LANES:d_lru + (gi + 1) * LANES] = y.astype(mix_ref.dtype)

    sinks = sink_ref[...] * LOG2_E
    o_attn = d_lru + d_pool
    cos = cos_ref[...]
    sin = sin_ref[...]
    q_chunks = [(qc * (ATTN_SCALE * LOG2_E)).astype(BF16)
                for qc in _rope(proj_ref[:, o_q:o_q + d_attn], cos, sin)]
    k_rot = _rope(proj_ref[:, o_k:o_k + d_kv], cos, sin)
    _store_head_variants(kvar, WINDOW, k_rot)
    _store_head_variants(vvar, WINDOW, [proj_ref[:, o_v + c * LANES:o_v + (c + 1) * LANES]
                                        for c in range(n_kc)])
    for blk in range(tile // qblock):
        rs = slice(blk * qblock, (blk + 1) * qblock)
        if is_prompt and blk == 0:
            lim = jnp.where(s == 0, WINDOW, 0)
        else:
            lim = 0
        _attend_block([qc[rs] for qc in q_chunks], kvar, vvar, blk * qblock, sinks, lim,
                      mix_ref, rs, o_attn)

    k_last = jnp.concatenate([kc[tile - qblock:tile] for kc in k_rot], axis=-1)
    if is_prompt:
        lru_ext[0:HALO, :] = lru_ext[tile:tile + HALO, :]
        pool_ext[0:HALO, :] = pool_ext[tile:tile + HALO, :]
        hcar[0:1, :] = h_last
        kvar[:, :, 0:WINDOW, :] = kvar[:, :, tile:tile + WINDOW, :]
        vvar[:, :, 0:WINDOW, :] = vvar[:, :, tile:tile + WINDOW, :]

        @pl.when(s == pl.num_programs(1) - 1)
        def _():
            ko_ref[0] = k_last
            vo_ref[0] = proj_ref[tile - qblock:tile, o_v:o_v + d_kv]
            ho_ref[0] = h_last
            co_ref[0] = lru_ext[HALO + tile - (CONV_WIDTH - 1):HALO + tile, :]
            po_ref[0] = pool_ext[HALO + tile - POOL_PAD:HALO + tile, :]
    else:
        ko_ref[0, 0:WINDOW - tile, :] = ck_ref[0, tile:WINDOW, :]
        ko_ref[0, WINDOW - tile:WINDOW, :] = k_last
        vo_ref[0, 0:WINDOW - tile, :] = cv_ref[0, tile:WINDOW, :]
        vo_ref[0, WINDOW - tile:WINDOW, :] = proj_ref[:, o_v:o_v + d_kv]
        ho_ref[0] = h_last
        co_ref[0] = lru_ext[HALO + tile - (CONV_WIDTH - 1):HALO + tile, :]
        po_ref[0] = pool_ext[HALO + tile - POOL_PAD:HALO + tile, :]


def _layer_consts(p, dims):
    d_lru, d_pool = dims["d_lru"], dims["d_pool"]
    hd = d_lru // LRU_HEADS
    per = LANES // hd
    nchunk = d_lru // LANES

    def blockdiag(w):
        w = w.reshape(nchunk, per, hd, hd)
        eye = jnp.eye(per, dtype=w.dtype)
        return jnp.einsum("cpij,pq->cpiqj", w, eye).reshape(nchunk, LANES, LANES)

    wg = jnp.concatenate([blockdiag(p["lru_wa"]), blockdiag(p["lru_wx"])], axis=-1).astype(BF16)
    return dict(
        convw=p["conv_w"], convb=p["conv_b"].reshape(1, d_lru), wg=wg,
        ba=p["lru_ba"].reshape(1, d_lru), bx=p["lru_bx"].reshape(1, d_lru),
        lam=p["lru_lambda"].reshape(1, d_lru), poolw=p["pool_w"].astype(BF16),
        pscale=p["pool_scale"].reshape(1, d_pool), sinks=p["attn_sinks"].reshape(1, -1))


def _rope_tables(pos):
    half = HEAD_DIM // 2
    inv = ROPE_THETA ** (-jnp.arange(half, dtype=F32) / half)
    ang = pos.astype(F32)[:, None] * inv[None, :]
    cos = jnp.cos(ang)
    sin = jnp.sin(ang)
    cos2 = jnp.concatenate([cos, cos], axis=-1)
    sin2 = jnp.concatenate([-sin, sin], axis=-1)
    reps = LANES // HEAD_DIM
    return jnp.tile(cos2, (1, reps)), jnp.tile(sin2, (1, reps))


CONST_NAMES = ("convw", "convb", "wg", "ba", "bx", "lam", "poolw", "pscale", "sinks")


def _const_specs(consts):
    return [pl.BlockSpec(consts[n].shape, functools.partial(lambda nd, *_: (0,) * nd, consts[n].ndim))
            for n in CONST_NAMES]


def _mixer_prompt(proj, consts, cos, sin, bsz, seq, dims, tile):
    d_lru, d_pool, d_attn, d_kv = dims["d_lru"], dims["d_pool"], dims["d_attn"], dims["d_kv"]
    d_in = proj.shape[1]
    d_mix = d_lru + d_pool + d_attn
    ns = seq // tile
    kern = functools.partial(_mixer_kernel, tile=tile, qblock=WINDOW, is_prompt=True, pos0=0,
                             d_lru=d_lru, d_pool=d_pool, d_attn=d_attn, d_kv=d_kv)
    return pl.pallas_call(
        kern,
        grid=(bsz, ns),
        in_specs=[
            pl.BlockSpec((tile, d_in), lambda b, s: (b * ns + s, 0)),
            pl.BlockSpec((tile, LANES), lambda b, s: (s, 0)),
            pl.BlockSpec((tile, LANES), lambda b, s: (s, 0)),
        ] + _const_specs(consts),
        out_specs=[
            pl.BlockSpec((tile, d_mix), lambda b, s: (b * ns + s, 0)),
            pl.BlockSpec((1, WINDOW, d_kv), lambda b, s: (b, 0, 0)),
            pl.BlockSpec((1, WINDOW, d_kv), lambda b, s: (b, 0, 0)),
            pl.BlockSpec((1, 1, d_lru), lambda b, s: (b, 0, 0)),
            pl.BlockSpec((1, CONV_WIDTH - 1, d_lru), lambda b, s: (b, 0, 0)),
            pl.BlockSpec((1, POOL_PAD, d_pool), lambda b, s: (b, 0, 0)),
        ],
        out_shape=[
            jax.ShapeDtypeStruct((bsz * seq, d_mix), BF16),
            jax.ShapeDtypeStruct((bsz, WINDOW, d_kv), F32),
            jax.ShapeDtypeStruct((bsz, WINDOW, d_kv), F32),
            jax.ShapeDtypeStruct((bsz, 1, d_lru), F32),
            jax.ShapeDtypeStruct((bsz, CONV_WIDTH - 1, d_lru), F32),
            jax.ShapeDtypeStruct((bsz, POOL_PAD, d_pool), F32),
        ],
        scratch_shapes=[
            pltpu.VMEM((tile + HALO, d_lru), F32),
            pltpu.VMEM((tile + HALO, d_pool), F32),
            pltpu.VMEM((N_KV_HEADS, 2, WINDOW + tile, LANES), BF16),
            pltpu.VMEM((N_KV_HEADS, 2, WINDOW + tile, LANES), BF16),
            pltpu.VMEM((8, d_lru), F32),
        ],
        compiler_params=pltpu.CompilerParams(
            dimension_semantics=("arbitrary", "arbitrary"), vmem_limit_bytes=VMEM_LIMIT),
        name="mixer_prompt",
    )(proj, cos, sin, *[consts[n] for n in CONST_NAMES])


def _mixer_sample_kernel(*refs, n_seq, t, d_lru, d_pool, d_attn, d_kv):
    n_in = 3 + len(CONST_NAMES)
    proj_ref, shared = refs[0], refs[1:n_in]
    state_in = refs[n_in:n_in + 5]
    mix_ref = refs[n_in + 5]
    state_out = refs[n_in + 6:n_in + 11]
    scratch = refs[n_in + 11:]
    for q in range(n_seq):
        one = lambda r: r.at[pl.ds(q, 1)]
        _mixer_kernel(proj_ref.at[pl.ds(q * t, t)], *shared, *[one(r) for r in state_in],
                      mix_ref.at[pl.ds(q * t, t)], *[one(r) for r in state_out],
                      *[r.at[q] for r in scratch],
                      tile=t, qblock=t, is_prompt=False, pos0=PAST_LEN,
                      d_lru=d_lru, d_pool=d_pool, d_attn=d_attn, d_kv=d_kv)


def _mixer_sample(proj, row0, consts, cos, sin, cache_k, cache_v, st_h, st_conv, st_pool, layer, dims):
    d_lru, d_pool, d_attn, d_kv = dims["d_lru"], dims["d_pool"], dims["d_attn"], dims["d_kv"]
    d_in = proj.shape[1]
    d_mix = d_lru + d_pool + d_attn
    db, win = cache_k.shape[1], cache_k.shape[2]
    t = cos.shape[0]
    nq = SAMPLE_SEQS_PER_STEP
    assert db % nq == 0 and row0 % (nq * t) == 0
    blk0 = row0 // (nq * t)
    kern = functools.partial(_mixer_sample_kernel, n_seq=nq, t=t,
                             d_lru=d_lru, d_pool=d_pool, d_attn=d_attn, d_kv=d_kv)
    return pl.pallas_call(
        kern,
        grid=(db // nq,),
        in_specs=[
            pl.BlockSpec((nq * t, d_in), lambda b: (blk0 + b, 0)),
            pl.BlockSpec((t, LANES), lambda b: (0, 0)),
            pl.BlockSpec((t, LANES), lambda b: (0, 0)),
        ] + _const_specs(consts) + [
            pl.BlockSpec((None, nq, win, d_kv), lambda b: (layer, b, 0, 0)),
            pl.BlockSpec((None, nq, win, d_kv), lambda b: (layer, b, 0, 0)),
            pl.BlockSpec((None, nq, 1, d_lru), lambda b: (layer, b, 0, 0)),
            pl.BlockSpec((None, nq, CONV_WIDTH - 1, d_lru), lambda b: (layer, b, 0, 0)),
            pl.BlockSpec((None, nq, POOL_PAD, d_pool), lambda b: (layer, b, 0, 0)),
        ],
        out_specs=[
            pl.BlockSpec((nq * t, d_mix), lambda b: (b, 0)),
            pl.BlockSpec((nq, win, d_kv), lambda b: (b, 0, 0)),
            pl.BlockSpec((nq, win, d_kv), lambda b: (b, 0, 0)),
            pl.BlockSpec((nq, 1, d_lru), lambda b: (b, 0, 0)),
            pl.BlockSpec((nq, CONV_WIDTH - 1, d_lru), lambda b: (b, 0, 0)),
            pl.BlockSpec((nq, POOL_PAD, d_pool), lambda b: (b, 0, 0)),
        ],
        out_shape=[
            jax.ShapeDtypeStruct((db * t, d_mix), F32),
            jax.ShapeDtypeStruct((db, win, d_kv), F32),
            jax.ShapeDtypeStruct((db, win, d_kv), F32),
            jax.ShapeDtypeStruct((db, 1, d_lru), F32),
            jax.ShapeDtypeStruct((db, CONV_WIDTH - 1, d_lru), F32),
            jax.ShapeDtypeStruct((db, POOL_PAD, d_pool), F32),
        ],
        scratch_shapes=[
            pltpu.VMEM((nq, t + HALO, d_lru), F32),
            pltpu.VMEM((nq, t + HALO, d_pool), F32),
            pltpu.VMEM((nq, N_KV_HEADS, 2, 2 * WINDOW, LANES), BF16),
            pltpu.VMEM((nq, N_KV_HEADS, 2, 2 * WINDOW, LANES), BF16),
        ],
        compiler_params=pltpu.CompilerParams(
            dimension_semantics=("arbitrary",), vmem_limit_bytes=VMEM_LIMIT),
        name="mixer_sample",
    )(proj, cos, sin, *[consts[n] for n in CONST_NAMES], cache_k, cache_v, st_h, st_conv, st_pool)


def _route(logits, run_cnt):
    t = logits.shape[0]
    lane = lax.broadcasted_iota(I32, (t, LANES), 1)
    lane_f = lane.astype(F32)
    ninf = -jnp.inf
    big = float(LANES)
    is_g = lane < N_EXPERT_GROUPS
    lg = jnp.where(is_g, logits, ninf)
    mg = jnp.max(lg, axis=-1, keepdims=True)
    g_top = jnp.min(jnp.where(lg == mg, lane_f, big), axis=-1, keepdims=True).astype(I32)
    pg_top = 1.0 / jnp.sum(jnp.exp(lg - mg), axis=-1, keepdims=True)
    base = N_EXPERT_GROUPS + EXPERTS_PER_GROUP * g_top
    in_grp = (lane >= base) & (lane < base + EXPERTS_PER_GROUP)
    le = jnp.where(in_grp, logits, ninf)
    m1 = jnp.max(le, axis=-1, keepdims=True)
    i1 = jnp.min(jnp.where(le == m1, lane_f, big), axis=-1, keepdims=True).astype(I32)
    le2 = jnp.where(lane == i1, ninf, le)
    m2 = jnp.max(le2, axis=-1, keepdims=True)
    i2 = jnp.min(jnp.where((le2 == m2) & in_grp & (lane != i1), lane_f, big),
                 axis=-1, keepdims=True).astype(I32)
    se = jnp.sum(jnp.exp(le - m1), axis=-1, keepdims=True)
    p1 = 1.0 / se
    p2 = jnp.exp(m2 - m1) / se
    tot = p1 + p2
    w1 = (p1 / tot) * pg_top
    w2 = (p2 / tot) * pg_top
    a = i1 - base
    b = i2 - base
    lo = jnp.minimum(a, b)
    hi = jnp.maximum(a, b)
    w_lo = jnp.where(a < b, w1, w2)
    w_hi = jnp.where(a < b, w2, w1)
    pid = jnp.where(lo == 0, hi - 1, jnp.where(lo == 1, jnp.where(hi == 3, 3, 4), 5))
    swap = pid == 5
    w_a = jnp.where(swap, w_hi, w_lo)
    w_b = jnp.where(swap, w_lo, w_hi)
    cls = g_top * N_PAIRS + pid
    onehot = lane == cls
    ti = lax.broadcasted_iota(I32, (t, t), 0)
    tj = lax.broadcasted_iota(I32, (t, t), 1)
    lower = jnp.where(tj <= ti, 1.0, 0.0).astype(BF16)
    prefix = jnp.dot(lower, jnp.where(onehot, 1.0, 0.0).astype(BF16), preferred_element_type=F32)
    rank = jnp.sum(jnp.where(onehot, prefix - 1.0 + run_cnt, 0.0), axis=-1, keepdims=True)
    info = jnp.where(lane == 0, w_a, jnp.where(lane == 1, w_b, jnp.where(
        lane == 2, cls.astype(F32), jnp.where(lane == 3, rank, 0.0))))
    return info, run_cnt + prefix[t - 1:t, :]


def _out_proj_kernel(mixp_ref, mixs_ref, xm_ref, xt_ref, w_hbm, g_ref, wr_ref, br_ref,
                     o_ref, info_ref, cnt_ref, w_bf, stage, sem, run_cnt, *, n_main_blocks, d, layer, slab):
    i = pl.program_id(0)

    @pl.when(i == 0)
    def _():
        _load_weight_bf16(w_hbm.at[layer], w_bf, stage, sem)
        run_cnt[...] = jnp.zeros_like(run_cnt)

    mix = jnp.where(i < n_main_blocks, mixp_ref[...], mixs_ref[...].astype(BF16))
    xres = (_pick_rows(i, n_main_blocks, xm_ref, xt_ref, d, slab)
            + jnp.dot(mix, w_bf[...], preferred_element_type=F32))
    h2 = _rms(xres, g_ref[...])
    logits = jnp.dot(h2.astype(BF16), wr_ref[...], preferred_element_type=F32) + br_ref[...]
    info, cnt = _route(logits, run_cnt[0:1, :])
    run_cnt[0:1, :] = cnt
    pitch = _slab_pitch(d)
    _write_slabs(o_ref, (), xres, pitch)
    o_ref[pl.ds(d // LANES, ROW_BLOCK, stride=pitch), :] = info
    info_ref[...] = info
    cnt_ref[...] = jnp.broadcast_to(cnt, cnt_ref.shape)


def _out_proj(mix_p, mix_s, x_main, x_tail, tail_block0, slab, w, layer, g, wr, br, n_rows):
    d = g.shape[1]
    d_mix = w.shape[1]
    npb = mix_p.shape[0] // ROW_BLOCK
    pitch = _slab_pitch(d)
    kern = functools.partial(_out_proj_kernel, n_main_blocks=npb, d=d, layer=layer, slab=slab)
    return pl.pallas_call(
        kern,
        grid=(n_rows // ROW_BLOCK,),
        in_specs=[
            pl.BlockSpec((ROW_BLOCK, d_mix), lambda i: (jnp.minimum(i, npb - 1), 0)),
            pl.BlockSpec((ROW_BLOCK, d_mix), lambda i: (jnp.maximum(i - npb, 0), 0)),
        ] + _row_specs(d, npb, tail_block0, slab) + [
            pl.BlockSpec(memory_space=pl.ANY),
            pl.BlockSpec((1, d), lambda i: (0, 0)),
            pl.BlockSpec((d, LANES), lambda i: (0, 0)),
            pl.BlockSpec((1, LANES), lambda i: (0, 0)),
        ],
        out_specs=[pl.BlockSpec((ROW_BLOCK * pitch, LANES), lambda i: (i, 0)),
                   pl.BlockSpec((ROW_BLOCK, LANES), lambda i: (i, 0)),
                   pl.BlockSpec((8, LANES), lambda i: (0, 0))],
        out_shape=[jax.ShapeDtypeStruct((n_rows * pitch, LANES), F32),
                   jax.ShapeDtypeStruct((n_rows, LANES), F32),
                   jax.ShapeDtypeStruct((8, LANES), F32)],
        scratch_shapes=[
            pltpu.VMEM((d_mix, d), BF16),
            pltpu.VMEM((2, ROW_BLOCK, d), F32),
            pltpu.SemaphoreType.DMA((2,)),
            pltpu.VMEM((8, LANES), F32),
        ],
        compiler_params=pltpu.CompilerParams(
            dimension_semantics=("arbitrary",), vmem_limit_bytes=VMEM_LIMIT),
        name="out_proj",
    )(mix_p, mix_s, x_main, x_tail, w, g, wr, br)


def _moe_kernel(off_ref, cnt_ref, ea_ref, eb_ref, chga_ref, chgb_ref, nch_ref,
                cls_ref, rank_ref, xr_hbm, g_ref, w1a_ref, w3a_ref, w2a_ref, w1b_ref, w3b_ref, w2b_ref,
                xo_hbm,
                perm, pos_v, pos_s, xbuf, obuf, wa1, wa3, wa2, wb1, wb3, wb2, gsem, ssem, psem,
                *, d, n_rows):
    del ea_ref, eb_ref
    m = MOE_CHUNK
    pitch = _slab_pitch(d)
    n_slabs = d // LANES
    c = pl.program_id(0)
    n = nch_ref[0]
    slot = c % 2
    other = 1 - slot

    def start_gather(chunk, sl, j):
        tok = jnp.maximum(perm[(chunk + 1) * m + j], 0)
        pltpu.make_async_copy(xr_hbm.at[pl.ds(tok * pitch, pitch)],
                              xbuf.at[sl, pl.ds(j * pitch, pitch)], gsem.at[sl]).start()

    def start_scatter(chunk, sl, j):
        tok = perm[(chunk + 1) * m + j]
        r = jnp.where(tok < 0, n_rows + sl * m + j, tok)
        pltpu.make_async_copy(obuf.at[sl, pl.ds(j * pitch, pitch)],
                              xo_hbm.at[pl.ds(r * pitch, pitch)], ssem.at[sl]).start()

    def wait_gather(sl):
        pltpu.make_async_copy(xr_hbm.at[pl.ds(0, m * pitch)], xbuf.at[sl], gsem.at[sl]).wait()

    def wait_scatter(sl):
        pltpu.make_async_copy(obuf.at[sl], xo_hbm.at[pl.ds(0, m * pitch)], ssem.at[sl]).wait()

    @pl.when(c == 0)
    def _():
        obuf[...] = jnp.zeros_like(obuf)
        pad = pltpu.make_async_copy(obuf.at[0], xo_hbm.at[pl.ds(n_rows * pitch, m * pitch)], ssem.at[0])
        pad.start()
        pad.wait()

        def fill(lo, hi):
            def body(i, z):
                perm[i] = -1
                return z
            lax.fori_loop(lo, hi, body, 0)

        def first_gather(j, z):
            start_gather(0, 0, j)
            return z

        fill(0, m)
        fill((n + 1) * m, (n + 3) * m)
        for k in range(N_CLASSES):
            base = (off_ref[k] + 1) * m
            cnt = cnt_ref[k]
            fill(base + cnt, base + ((cnt + m - 1) // m) * m)

        cls_v = cls_ref[...]
        base = jnp.zeros_like(cls_v)
        for k in range(N_CLASSES):
            base = jnp.where(cls_v == k, (off_ref[k] + 1) * m, base)
        pos_v[...] = base + rank_ref[...]
        to_smem = pltpu.make_async_copy(pos_v, pos_s, psem.at[0])
        to_smem.start()
        to_smem.wait()

        def place_row(r, z):
            def place(j, z2):
                perm[pos_s[r, j]] = r * LANES + j
                return z2
            return lax.fori_loop(0, LANES, place, z, unroll=8)
        lax.fori_loop(0, n_rows // LANES, place_row, 0)
        lax.fori_loop(0, m, first_gather, 0)

    @pl.when(c <= n)
    def _():
        wait_gather(slot)

        @pl.when(c >= 1)
        def _():
            wait_scatter(slot)

        @pl.when(chga_ref[c] == 1)
        def _():
            wa1[...] = w1a_ref[0].astype(BF16)
            wa3[...] = w3a_ref[0].astype(BF16)
            wa2[...] = w2a_ref[0].astype(BF16)

        @pl.when(chgb_ref[c] == 1)
        def _():
            wb1[...] = w1b_ref[0].astype(BF16)
            wb3[...] = w3b_ref[0].astype(BF16)
            wb2[...] = w2b_ref[0].astype(BF16)

        for j in range(m):
            start_gather(c + 1, other, j)
        for j in range(m):
            start_scatter(c - 1, other, j)
        x = _read_slabs(xbuf, (slot,), m, n_slabs, pitch)
        info = xbuf[slot, pl.ds(n_slabs, m, stride=pitch), :]
        w_a = info[:, 0:1]
        w_b = info[:, 1:2]
        h = _rms(x, g_ref[...]).astype(BF16)
        hid_a = (jax.nn.silu(jnp.dot(h, wa1[...], preferred_element_type=F32))
                 * jnp.dot(h, wa3[...], preferred_element_type=F32) * w_a).astype(BF16)
        hid_b = (jax.nn.silu(jnp.dot(h, wb1[...], preferred_element_type=F32))
                 * jnp.dot(h, wb3[...], preferred_element_type=F32) * w_b).astype(BF16)
        y = (jnp.dot(hid_a, wa2[...], preferred_element_type=F32)
             + jnp.dot(hid_b, wb2[...], preferred_element_type=F32))
        _write_slabs(obuf, (slot,), x + y, pitch)

        @pl.when(c == n)
        def _():
            wait_gather(other)
            wait_scatter(other)


def _moe(xr, cls, rank, g, w1, w3, w2, tables, n_rows, c_max):
    d = g.shape[1]
    f = w1.shape[2]
    m = MOE_CHUNK
    pitch = _slab_pitch(d)
    kern = functools.partial(_moe_kernel, d=d, n_rows=n_rows)

    def wspec(shape, which):
        return pl.BlockSpec(shape, lambda c, *pref: (pref[which][c], 0, 0))

    grid_spec = pltpu.PrefetchScalarGridSpec(
        num_scalar_prefetch=7,
        grid=(c_max + 1,),
        in_specs=[
            pl.BlockSpec(cls.shape, lambda c, *pref: (0, 0)),
            pl.BlockSpec(rank.shape, lambda c, *pref: (0, 0)),
            pl.BlockSpec(memory_space=pl.ANY),
            pl.BlockSpec((1, d), lambda c, *pref: (0, 0)),
            wspec((1, d, f), 2), wspec((1, d, f), 2), wspec((1, f, d), 2),
            wspec((1, d, f), 3), wspec((1, d, f), 3), wspec((1, f, d), 3),
        ],
        out_specs=pl.BlockSpec(memory_space=pl.ANY),
        scratch_shapes=[
            pltpu.SMEM(((c_max + 3) * m,), I32),
            pltpu.VMEM(cls.shape, I32),
            pltpu.SMEM(cls.shape, I32),
            pltpu.VMEM((2, m * pitch, LANES), F32),
            pltpu.VMEM((2, m * pitch, LANES), F32),
            pltpu.VMEM((d, f), BF16), pltpu.VMEM((d, f), BF16), pltpu.VMEM((f, d), BF16),
            pltpu.VMEM((d, f), BF16), pltpu.VMEM((d, f), BF16), pltpu.VMEM((f, d), BF16),
            pltpu.SemaphoreType.DMA((2,)),
            pltpu.SemaphoreType.DMA((2,)),
            pltpu.SemaphoreType.DMA((1,)),
        ],
    )
    return pl.pallas_call(
        kern,
        grid_spec=grid_spec,
        out_shape=jax.ShapeDtypeStruct(((n_rows + 2 * m) * pitch, LANES), F32),
        compiler_params=pltpu.CompilerParams(
            dimension_semantics=("arbitrary",), vmem_limit_bytes=VMEM_LIMIT),
        name="moe",
    )(*tables, cls, rank, xr, g, w1, w3, w2, w1, w3, w2)


def _chunk_tables(counts, c_max, expert0):
    m = MOE_CHUNK
    nch_c = (counts + m - 1) // m
    ch_end = jnp.cumsum(nch_c)
    ch_off = ch_end - nch_c
    n_chunks = ch_end[-1]
    chunk = jnp.minimum(jnp.arange(c_max + 1, dtype=I32), jnp.maximum(n_chunks - 1, 0))
    ccls = jnp.sum((chunk[:, None] >= ch_end[None, :]).astype(I32), axis=1)
    ccls = jnp.minimum(ccls, N_CLASSES - 1)
    grp = ccls // N_PAIRS
    pid = ccls % N_PAIRS
    ea = expert0 + grp * EXPERTS_PER_GROUP + jnp.asarray(PAIR_SLOT_A, I32)[pid]
    eb = expert0 + grp * EXPERTS_PER_GROUP + jnp.asarray(PAIR_SLOT_B, I32)[pid]
    first = jnp.ones((1,), I32)
    chga = jnp.concatenate([first, (ea[1:] != ea[:-1]).astype(I32)])
    chgb = jnp.concatenate([first, (eb[1:] != eb[:-1]).astype(I32)])
    return ch_off.astype(I32), counts, ea, eb, chga, chgb, n_chunks.reshape(1).astype(I32)


def _final_norm_kernel(x_ref, g_ref, op_ref, os_ref, *, n_prompt_blocks):
    i = pl.program_id(0)
    d = g_ref.shape[1]
    y = _rms(_read_slabs(x_ref, (), ROW_BLOCK, d // LANES, _slab_pitch(d)), g_ref[...])

    @pl.when(i < n_prompt_blocks)
    def _():
        op_ref[...] = y

    @pl.when(i >= n_prompt_blocks)
    def _():
        os_ref[...] = y


def _final_norm(x, g, n_prompt, n_sample):
    d = g.shape[1]
    npb = n_prompt // ROW_BLOCK
    nsb = n_sample // ROW_BLOCK
    kern = functools.partial(_final_norm_kernel, n_prompt_blocks=npb)
    return pl.pallas_call(
        kern,
        grid=(npb + nsb,),
        in_specs=[pl.BlockSpec((ROW_BLOCK * _slab_pitch(d), LANES), lambda i: (i, 0)),
                  pl.BlockSpec((1, d), lambda i: (0, 0))],
        out_specs=[pl.BlockSpec((ROW_BLOCK, d), lambda i: (jnp.minimum(i, npb - 1), 0)),
                   pl.BlockSpec((ROW_BLOCK, d), lambda i: (jnp.maximum(i - npb, 0), 0))],
        out_shape=[jax.ShapeDtypeStruct((n_prompt, d), F32),
                   jax.ShapeDtypeStruct((n_sample, d), F32)],
        compiler_params=pltpu.CompilerParams(dimension_semantics=("arbitrary",)),
        name="final_norm",
    )(x, g)


def kernel(x_prompt, x_sample, cache_k, cache_v, state_lru_h, state_conv, state_pool, norm1_g, w_in, conv_w, conv_b, lru_wa, lru_ba, lru_wx, lru_bx, lru_lambda, pool_w, pool_scale, attn_sinks, w_out, norm2_g, router_group_w, router_group_b, router_expert_w, router_expert_b, expert_w1, expert_w3, expert_w2, final_norm_g):
    bsz, seq, d = x_prompt.shape
    db, ds, _ = x_sample.shape
    depth = w_in.shape[0]
    win = cache_k.shape[2]
    nkv = cache_k.shape[3]
    d_lru = lru_lambda.shape[1]
    d_pool = pool_scale.shape[1]
    d_kv = nkv * cache_k.shape[4]
    d_attn = attn_sinks.shape[1] * HEAD_DIM
    dims = dict(d_lru=d_lru, d_pool=d_pool, d_attn=d_attn, d_kv=d_kv)
    n_prompt = bsz * seq
    n_sample = db * ds
    n_rows = n_prompt + n_sample
    assert n_prompt % ROW_BLOCK == 0 and n_sample % ROW_BLOCK == 0
    assert win == WINDOW and nkv == N_KV_HEADS and ds <= 8 and WINDOW % ds == 0
    tile = 512 if seq % 512 == 0 else WINDOW
    c_max = -(-n_rows // MOE_CHUNK) + N_CLASSES
    npb = n_prompt // ROW_BLOCK

    cos_p, sin_p = _rope_tables(jnp.arange(seq))
    cos_s, sin_s = _rope_tables(PAST_LEN + jnp.arange(ds))
    wr_all = jnp.concatenate([router_group_w, router_expert_w], axis=2)
    wr_all = jnp.pad(wr_all, ((0, 0), (0, 0), (0, LANES - wr_all.shape[2]))).astype(BF16)
    br_all = jnp.concatenate([router_group_b, router_expert_b], axis=1)
    br_all = jnp.pad(br_all, ((0, 0), (0, LANES - br_all.shape[1])))

    x_main, x_tail, tail0, slab = x_prompt.reshape(n_prompt, d), x_sample.reshape(n_sample, d), 0, False
    ck = cache_k.reshape(depth, db, win, d_kv)
    cv = cache_v.reshape(depth, db, win, d_kv)
    st_h = state_lru_h.reshape(depth, db, 1, d_lru)
    n_exp, _, d_exp = expert_w1.shape[1:]
    w1 = expert_w1.reshape(depth * n_exp, d, d_exp)
    w3 = expert_w3.reshape(depth * n_exp, d, d_exp)
    w2 = expert_w2.reshape(depth * n_exp, d_exp, d)
    outs = [[] for _ in range(10)]
    for l in range(depth):
        p = dict(conv_w=conv_w[l], conv_b=conv_b[l], lru_wa=lru_wa[l], lru_ba=lru_ba[l],
                 lru_wx=lru_wx[l], lru_bx=lru_bx[l], lru_lambda=lru_lambda[l], pool_w=pool_w[l],
                 pool_scale=pool_scale[l], attn_sinks=attn_sinks[l])
        consts = _layer_consts(p, dims)
        proj = _in_proj(x_main, x_tail, tail0, slab, norm1_g[l].reshape(1, d), w_in, l, n_rows, n_prompt)
        mix_p, pk, pv, ph, pc, pp = _mixer_prompt(proj, consts, cos_p, sin_p, bsz, seq, dims, tile)
        mix_s, sk, sv, sh, sc, sp = _mixer_sample(
            proj, n_prompt, consts, cos_s, sin_s, ck, cv, st_h, state_conv, state_pool, l, dims)
        xr, info, cnt = _out_proj(mix_p, mix_s, x_main, x_tail, tail0, slab, w_out, l,
                                  norm2_g[l].reshape(1, d), wr_all[l], br_all[l:l + 1], n_rows)
        route = info[:, 2:4].astype(I32)
        tables = _chunk_tables(cnt[0, :N_CLASSES].astype(I32), c_max, l * n_exp)
        x = _moe(xr, route[:, 0].reshape(-1, LANES), route[:, 1].reshape(-1, LANES),
                 norm2_g[l].reshape(1, d), w1, w3, w2, tables, n_rows, c_max)
        x_main, x_tail, tail0, slab = x, x, npb, True
        for lst, val in zip(outs, (pk.reshape(bsz, WINDOW, nkv, HEAD_DIM), pv.reshape(bsz, WINDOW, nkv, HEAD_DIM),
                                   ph.reshape(bsz, d_lru), pc, pp,
                                   sk.reshape(db, win, nkv, HEAD_DIM), sv.reshape(db, win, nkv, HEAD_DIM),
                                   sh.reshape(db, d_lru), sc, sp)):
            lst.append(val)
    y_p, y_s = _final_norm(x_main, final_norm_g.reshape(1, d), n_prompt, n_sample)
    return (y_p.reshape(bsz, seq, d), y_s.reshape(db, ds, d)) + tuple(jnp.stack(o) for o in outs)
```

```python
import functools

import jax
import jax.numpy as jnp
from jax import lax
from jax.experimental import pallas as pl
from jax.experimental.pallas import tpu as pltpu

F32 = jnp.float32
BF16 = jnp.bfloat16
I32 = jnp.int32

LRU_HEADS = 8
CONV_WIDTH = 4
LRU_C = 8.0
POOL_WINDOWS = (2, 4, 8, 16)
POOL_PAD = max(POOL_WINDOWS) - 1
HEAD_DIM = 64
N_KV_HEADS = 4
WINDOW = 128
ROPE_THETA = 10000.0
ATTN_SCALE = HEAD_DIM ** -0.5
LOG2_E = 1.4426950408889634
N_EXPERT_GROUPS = 4
EXPERTS_PER_GROUP = 4
N_EXPERTS = N_EXPERT_GROUPS * EXPERTS_PER_GROUP
RMS_EPS = 1e-6
PAST_LEN = 16384

LANES = 128
HALF = LANES // 2
HALO = 16
ROW_BLOCK = 256
SAMPLE_SEQS_PER_STEP = 4
MOE_CHUNK = 128
PAIR_SLOT_A = (0, 0, 0, 1, 1, 3)
PAIR_SLOT_B = (1, 2, 3, 3, 2, 2)
N_PAIRS = len(PAIR_SLOT_A)
N_CLASSES = N_EXPERT_GROUPS * N_PAIRS
VMEM_LIMIT = 52 * 1024 * 1024


def _rms(x, g):
    return (x * lax.rsqrt(jnp.mean(x * x, axis=-1, keepdims=True) + RMS_EPS)) * g


def _load_weight_bf16(w_hbm, w_bf, stage, sem):
    rows = stage.shape[1]
    n = w_hbm.shape[0] // rows

    def copy(i, sl):
        return pltpu.make_async_copy(w_hbm.at[pl.ds(i * rows, rows)], stage.at[sl], sem.at[sl])

    copy(0, 0).start()
    for i in range(n):
        sl = i % 2
        if i + 1 < n:
            copy(i + 1, 1 - sl).start()
        copy(i, sl).wait()
        w_bf[i * rows:(i + 1) * rows, :] = stage[sl].astype(BF16)


def _slab_pitch(d):
    return d // LANES + 1


def _read_slabs(ref, lead, n_tok, n_slabs, pitch):
    return jnp.concatenate(
        [ref[lead + (pl.ds(k, n_tok, stride=pitch), slice(None))] for k in range(n_slabs)], axis=1)


def _write_slabs(ref, lead, val, pitch):
    n_tok = val.shape[0]
    for k in range(val.shape[1] // LANES):
        ref[lead + (pl.ds(k, n_tok, stride=pitch), slice(None))] = val[:, k * LANES:(k + 1) * LANES]


def _pick_rows(i, n_main_blocks, main_ref, tail_ref, d, slab):
    if slab:
        return _read_slabs(main_ref, (), ROW_BLOCK, d // LANES, _slab_pitch(d))
    return jnp.where(i < n_main_blocks, main_ref[...], tail_ref[...])


def _in_proj_kernel(xm_ref, xt_ref, g_ref, w_hbm, o_ref, w_bf, stage, sem, *, n_main_blocks, layer, slab):
    i = pl.program_id(0)

    @pl.when(i == 0)
    def _():
        _load_weight_bf16(w_hbm.at[layer], w_bf, stage, sem)

    h = _rms(_pick_rows(i, n_main_blocks, xm_ref, xt_ref, g_ref.shape[1], slab), g_ref[...])
    o_ref[...] = jnp.dot(h.astype(BF16), w_bf[...], preferred_element_type=F32)


def _row_specs(d, n_main_blocks, tail_block0, slab):
    if slab:
        shape = (ROW_BLOCK * _slab_pitch(d), LANES)
        return [pl.BlockSpec(shape, lambda i: (i, 0)), pl.BlockSpec(shape, lambda i: (tail_block0, 0))]
    shape = (ROW_BLOCK, d)
    return [pl.BlockSpec(shape, lambda i: (jnp.minimum(i, n_main_blocks - 1), 0)),
            pl.BlockSpec(shape, lambda i: (jnp.maximum(i - n_main_blocks, 0) + tail_block0, 0))]


def _in_proj(x_main, x_tail, tail_block0, slab, g, w, layer, n_rows, n_main):
    d = g.shape[1]
    d_in = w.shape[2]
    nmb = n_main // ROW_BLOCK
    kern = functools.partial(_in_proj_kernel, n_main_blocks=nmb, layer=layer, slab=slab)
    return pl.pallas_call(
        kern,
        grid=(n_rows // ROW_BLOCK,),
        in_specs=_row_specs(d, nmb, tail_block0, slab) + [
            pl.BlockSpec((1, d), lambda i: (0, 0)),
            pl.BlockSpec(memory_space=pl.ANY),
        ],
        out_specs=pl.BlockSpec((ROW_BLOCK, d_in), lambda i: (i, 0)),
        out_shape=jax.ShapeDtypeStruct((n_rows, d_in), F32),
        scratch_shapes=[
            pltpu.VMEM((d, d_in), BF16),
            pltpu.VMEM((2, ROW_BLOCK, d_in), F32),
            pltpu.SemaphoreType.DMA((2,)),
        ],
        compiler_params=pltpu.CompilerParams(
            dimension_semantics=("arbitrary",), vmem_limit_bytes=VMEM_LIMIT),
        name="in_proj",
    )(x_main, x_tail, g, w)


def _scan_linear(a, b):
    t = a.shape[0]
    row = lax.broadcasted_iota(I32, a.shape, 0)
    d = 1
    while d < min(t, 8):
        a_sh = pltpu.roll(a, d, axis=0)
        b_sh = pltpu.roll(b, d, axis=0)
        m = row >= d
        b = jnp.where(m, a * b_sh + b, b)
        a = jnp.where(m, a * a_sh, a)
        d *= 2
    while d < t:
        b = jnp.concatenate([b[:d], a[d:] * b[:t - d] + b[d:]], axis=0)
        a = jnp.concatenate([a[:d], a[d:] * a[:t - d]], axis=0)
        d *= 2
    return a, b


def _lru_chunk(xc, g, h_prev, wg, ba, bx, sp):
    pre = jnp.dot(xc.astype(BF16), wg, preferred_element_type=F32)
    r = jax.nn.sigmoid(pre[:, :LANES] + ba)
    ig = jax.nn.sigmoid(pre[:, LANES:] + bx)
    log_a = (-LRU_C * r) * sp
    a = jnp.exp(log_a)
    v = 1.0 - a * a
    bterm = jnp.where(v > 0.0, v * lax.rsqrt(v), 0.0) * ig * xc
    a_cum, h0 = _scan_linear(a, bterm)
    hs = a_cum * h_prev + h0
    return hs * jax.nn.gelu(g), hs


def _rope(x, cos, sin_signed):
    n = x.shape[1] // LANES
    lane = lax.broadcasted_iota(I32, (x.shape[0], LANES), 1)
    first = (lane % HEAD_DIM) < (HEAD_DIM // 2)
    outs = []
    for c in range(n):
        xc = x[:, c * LANES:(c + 1) * LANES]
        swapped = jnp.where(first, pltpu.roll(xc, LANES - HEAD_DIM // 2, axis=1),
                            pltpu.roll(xc, HEAD_DIM // 2, axis=1))
        outs.append(xc * cos + swapped * sin_signed)
    return outs


def _store_head_variants(var_ref, row0, chunks):
    t = chunks[0].shape[0]
    lane = lax.broadcasted_iota(I32, (t, LANES), 1)
    for kc, x in enumerate(chunks):
        swapped = pltpu.roll(x, HALF, axis=1)
        for hh in range(2):
            for p in range(2):
                src = x if p == hh else swapped
                keep = (lane < HALF) if p == 0 else (lane >= HALF)
                var_ref[2 * kc + hh, p, row0:row0 + t, :] = jnp.where(keep, src, 0.0).astype(BF16)


def _attend_block(q_chunks, kvar, vvar, key0, sinks, lim, out_ref, out_rows, out_col0):
    qb = q_chunks[0].shape[0]
    nk = 2 * WINDOW
    gq = (2 * len(q_chunks)) // N_KV_HEADS
    rows = 2 * qb
    qi = lax.broadcasted_iota(I32, (rows, nk), 0) % qb
    kj = lax.broadcasted_iota(I32, (rows, nk), 1)
    valid = ((kj < WINDOW) & (kj > qi + lim)) | ((kj >= WINDOW) & (kj - WINDOW <= qi))
    top = lax.broadcasted_iota(I32, (rows, 1), 0) < qb
    lane = lax.broadcasted_iota(I32, (nk, LANES), 1)
    ones_lo = jnp.where(lane < HALF, 1.0, 0.0).astype(BF16)
    ones_hi = jnp.where(lane >= HALF, 1.0, 0.0).astype(BF16)
    lane_o = lax.broadcasted_iota(I32, (rows, LANES), 1)
    nt = (((1,), (1,)), ((), ()))
    for c in range(N_KV_HEADS):
        c0 = c * gq // 2
        qs = jnp.concatenate([q_chunks[c0], q_chunks[c0 + 1]], axis=0)
        es, sink_terms = [], []
        for p in range(2):
            kc = kvar[c, p, key0:key0 + nk, :]
            s = lax.dot_general(qs, kc, nt, preferred_element_type=F32)
            s = jnp.where(valid, s, -jnp.inf)
            h0 = c * gq + p
            sink = jnp.where(top, sinks[:, h0:h0 + 1], sinks[:, h0 + 2:h0 + 3])
            m = jnp.maximum(jnp.max(s, axis=-1, keepdims=True), sink)
            es.append(jnp.exp2(s - m).astype(BF16))
            sink_terms.append(jnp.exp2(sink - m))
        r0 = jnp.concatenate([vvar[c, 0, key0:key0 + nk, :], ones_lo], axis=1)
        r1 = jnp.concatenate([vvar[c, 1, key0:key0 + nk, :], ones_hi], axis=1)
        od = (jnp.dot(es[0], r0, preferred_element_type=F32)
              + jnp.dot(es[1], r1, preferred_element_type=F32))
        den = od[:, LANES:] + jnp.where(lane_o < HALF, sink_terms[0], sink_terms[1])
        o = (od[:, :LANES] / den).astype(out_ref.dtype)
        out_ref[out_rows, out_col0 + c0 * LANES:out_col0 + (c0 + 1) * LANES] = o[0:qb]
        out_ref[out_rows, out_col0 + (c0 + 1) * LANES:out_col0 + (c0 + 2) * LANES] = o[qb:rows]


def _mixer_kernel(*refs, tile, qblock, is_prompt, pos0, d_lru, d_pool, d_attn, d_kv):
    (proj_ref, cos_ref, sin_ref, convw_ref, convb_ref, wg_ref, ba_ref, bx_ref, lam_ref,
     poolw_ref, pscale_ref, sink_ref) = refs[:12]
    if is_prompt:
        (mix_ref, ko_ref, vo_ref, ho_ref, co_ref, po_ref,
         lru_ext, pool_ext, kvar, vvar, hcar) = refs[12:]
    else:
        (ck_ref, cv_ref, sh_ref, sc_ref, sp_ref,
         mix_ref, ko_ref, vo_ref, ho_ref, co_ref, po_ref,
         lru_ext, pool_ext, kvar, vvar) = refs[12:]

    s = pl.program_id(1) if is_prompt else 0
    o_glru = d_lru
    o_pool = 2 * d_lru
    o_q = o_pool + d_pool
    o_k = o_q + d_attn
    o_v = o_k + d_kv
    n_kc = d_kv // LANES

    if is_prompt:
        @pl.when(s == 0)
        def _():
            lru_ext[0:HALO, :] = jnp.zeros((HALO, d_lru), F32)
            pool_ext[0:HALO, :] = jnp.zeros((HALO, d_pool), F32)
            hcar[...] = jnp.zeros_like(hcar)
            kvar[:, :, 0:WINDOW, :] = jnp.zeros((N_KV_HEADS, 2, WINDOW, LANES), BF16)
            vvar[:, :, 0:WINDOW, :] = jnp.zeros((N_KV_HEADS, 2, WINDOW, LANES), BF16)
        h_prev = hcar[0:1, :]
    else:
        lru_ext[0:HALO, :] = jnp.zeros((HALO, d_lru), F32)
        pool_ext[0:HALO, :] = jnp.zeros((HALO, d_pool), F32)
        lru_ext[HALO - (CONV_WIDTH - 1):HALO, :] = sc_ref[0]
        pool_ext[HALO - POOL_PAD:HALO, :] = sp_ref[0]
        h_prev = sh_ref[0]
        kvar[:, :, WINDOW:2 * WINDOW, :] = jnp.zeros((N_KV_HEADS, 2, WINDOW, LANES), BF16)
        vvar[:, :, WINDOW:2 * WINDOW, :] = jnp.zeros((N_KV_HEADS, 2, WINDOW, LANES), BF16)
        _store_head_variants(kvar, 0, [ck_ref[0, :, c * LANES:(c + 1) * LANES] for c in range(n_kc)])
        _store_head_variants(vvar, 0, [cv_ref[0, :, c * LANES:(c + 1) * LANES] for c in range(n_kc)])

    lru_ext[HALO:HALO + tile, :] = proj_ref[:, 0:d_lru]
    pool_ext[HALO:HALO + tile, :] = proj_ref[:, o_pool:o_pool + d_pool]

    xc = convb_ref[...] + convw_ref[0:1, :] * lru_ext[HALO - 3:HALO - 3 + tile, :]
    for j in range(1, CONV_WIDTH):
        xc = xc + convw_ref[j:j + 1, :] * lru_ext[HALO - 3 + j:HALO - 3 + j + tile, :]
    neg = -lam_ref[...]
    sp_all = jnp.maximum(neg, 0.0) + jnp.log1p(jnp.exp(-jnp.abs(neg)))
    h_last = []
    for c in range(d_lru // LANES):
        cs = slice(c * LANES, (c + 1) * LANES)
        y, hs = _lru_chunk(xc[:, cs], proj_ref[:, o_glru + c * LANES:o_glru + (c + 1) * LANES],
                           h_prev[:, cs], wg_ref[c], ba_ref[:, cs], bx_ref[:, cs], sp_all[:, cs])
        mix_ref[:, cs] = y.astype(mix_ref.dtype)
        h_last.append(hs[tile - 1:tile, :])
    h_last = jnp.concatenate(h_last, axis=-1)

    row = lax.broadcasted_iota(I32, (tile, LANES), 0)
    pos = pos0 + s * tile + row
    for gi, w in enumerate(POOL_WINDOWS):
        cs = slice(gi * LANES, (gi + 1) * LANES)
        e = pool_ext[:, cs]
        acc = e
        step = 1
        while step < w:
            acc = acc + pltpu.roll(acc, step, axis=0)
            step *= 2
        cnt = jnp.minimum(pos + 1, w).astype(F32)
        dlt = acc[HALO:HALO + tile, :] / cnt - e[HALO:HALO + tile, :]
        y = jnp.dot(dlt.astype(BF16), poolw_ref[gi], preferred_element_type=F32) * pscale_ref[:, cs]
        mix_ref[:, d_lru + gi * LANES:d_lru + (gi + 1) * LANES] = y.astype(mix_ref.dtype)

    sinks = sink_ref[...] * LOG2_E
    o_attn = d_lru + d_pool
    cos = cos_ref[...]
    sin = sin_ref[...]
    q_chunks = [(qc * (ATTN_SCALE * LOG2_E)).astype(BF16)
                for qc in _rope(proj_ref[:, o_q:o_q + d_attn], cos, sin)]
    k_rot = _rope(proj_ref[:, o_k:o_k + d_kv], cos, sin)
    _store_head_variants(kvar, WINDOW, k_rot)
    _store_head_variants(vvar, WINDOW, [proj_ref[:, o_v + c * LANES:o_v + (c + 1) * LANES]
                                        for c in range(n_kc)])
    for blk in range(tile // qblock):
        rs = slice(blk * qblock, (blk + 1) * qblock)
        if is_prompt and blk == 0:
            lim = jnp.where(s == 0, WINDOW, 0)
        else:
            lim = 0
        _attend_block([qc[rs] for qc in q_chunks], kvar, vvar, blk * qblock, sinks, lim,
                      mix_ref, rs, o_attn)

    k_last = jnp.concatenate([kc[tile - qblock:tile] for kc in k_rot], axis=-1)
    if is_prompt:
        lru_ext[0:HALO, :] = lru_ext[tile:tile + HALO, :]
        pool_ext[0:HALO, :] = pool_ext[tile:tile + HALO, :]
        hcar[0:1, :] = h_last
        kvar[:, :, 0:WINDOW, :] = kvar[:, :, tile:tile + WINDOW, :]
        vvar[:, :, 0:WINDOW, :] = vvar[:, :, tile:tile + WINDOW, :]

        @pl.when(s == pl.num_programs(1) - 1)
        def _():
            ko_ref[0] = k_last
            vo_ref[0] = proj_ref[tile - qblock:tile, o_v:o_v + d_kv]
            ho_ref[0] = h_last
            co_ref[0] = lru_ext[HALO + tile - (CONV_WIDTH - 1):HALO + tile, :]
            po_ref[0] = pool_ext[HALO + tile - POOL_PAD:HALO + tile, :]
    else:
        ko_ref[0, 0:WINDOW - tile, :] = ck_ref[0, tile:WINDOW, :]
        ko_ref[0, WINDOW - tile:WINDOW, :] = k_last
        vo_ref[0, 0:WINDOW - tile, :] = cv_ref[0, tile:WINDOW, :]
        vo_ref[0, WINDOW - tile:WINDOW, :] = proj_ref[:, o_v:o_v + d_kv]
        ho_ref[0] = h_last
        co_ref[0] = lru_ext[HALO + tile - (CONV_WIDTH - 1):HALO + tile, :]
        po_ref[0] = pool_ext[HALO + tile - POOL_PAD:HALO + tile, :]


def _layer_consts(p, dims):
    d_lru, d_pool = dims["d_lru"], dims["d_pool"]
    hd = d_lru // LRU_HEADS
    per = LANES // hd
    nchunk = d_lru // LANES

    def blockdiag(w):
        w = w.reshape(nchunk, per, hd, hd)
        eye = jnp.eye(per, dtype=w.dtype)
        return jnp.einsum("cpij,pq->cpiqj", w, eye).reshape(nchunk, LANES, LANES)

    wg = jnp.concatenate([blockdiag(p["lru_wa"]), blockdiag(p["lru_wx"])], axis=-1).astype(BF16)
    return dict(
        convw=p["conv_w"], convb=p["conv_b"].reshape(1, d_lru), wg=wg,
        ba=p["lru_ba"].reshape(1, d_lru), bx=p["lru_bx"].reshape(1, d_lru),
        lam=p["lru_lambda"].reshape(1, d_lru), poolw=p["pool_w"].astype(BF16),
        pscale=p["pool_scale"].reshape(1, d_pool), sinks=p["attn_sinks"].reshape(1, -1))


def _rope_tables(pos):
    half = HEAD_DIM // 2
    inv = ROPE_THETA ** (-jnp.arange(half, dtype=F32) / half)
    ang = pos.astype(F32)[:, None] * inv[None, :]
    cos = jnp.cos(ang)
    sin = jnp.sin(ang)
    cos2 = jnp.concatenate([cos, cos], axis=-1)
    sin2 = jnp.concatenate([-sin, sin], axis=-1)
    reps = LANES // HEAD_DIM
    return jnp.tile(cos2, (1, reps)), jnp.tile(sin2, (1, reps))


CONST_NAMES = ("convw", "convb", "wg", "ba", "bx", "lam", "poolw", "pscale", "sinks")


def _const_specs(consts):
    return [pl.BlockSpec(consts[n].shape, functools.partial(lambda nd, *_: (0,) * nd, consts[n].ndim))
            for n in CONST_NAMES]


def _mixer_prompt(proj, consts, cos, sin, bsz, seq, dims, tile):
    d_lru, d_pool, d_attn, d_kv = dims["d_lru"], dims["d_pool"], dims["d_attn"], dims["d_kv"]
    d_in = proj.shape[1]
    d_mix = d_lru + d_pool + d_attn
    ns = seq // tile
    kern = functools.partial(_mixer_kernel, tile=tile, qblock=WINDOW, is_prompt=True, pos0=0,
                             d_lru=d_lru, d_pool=d_pool, d_attn=d_attn, d_kv=d_kv)
    return pl.pallas_call(
        kern,
        grid=(bsz, ns),
        in_specs=[
            pl.BlockSpec((tile, d_in), lambda b, s: (b * ns + s, 0)),
            pl.BlockSpec((tile, LANES), lambda b, s: (s, 0)),
            pl.BlockSpec((tile, LANES), lambda b, s: (s, 0)),
        ] + _const_specs(consts),
        out_specs=[
            pl.BlockSpec((tile, d_mix), lambda b, s: (b * ns + s, 0)),
            pl.BlockSpec((1, WINDOW, d_kv), lambda b, s: (b, 0, 0)),
            pl.BlockSpec((1, WINDOW, d_kv), lambda b, s: (b, 0, 0)),
            pl.BlockSpec((1, 1, d_lru), lambda b, s: (b, 0, 0)),
            pl.BlockSpec((1, CONV_WIDTH - 1, d_lru), lambda b, s: (b, 0, 0)),
            pl.BlockSpec((1, POOL_PAD, d_pool), lambda b, s: (b, 0, 0)),
        ],
        out_shape=[
            jax.ShapeDtypeStruct((bsz * seq, d_mix), BF16),
            jax.ShapeDtypeStruct((bsz, WINDOW, d_kv), F32),
            jax.ShapeDtypeStruct((bsz, WINDOW, d_kv), F32),
            jax.ShapeDtypeStruct((bsz, 1, d_lru), F32),
            jax.ShapeDtypeStruct((bsz, CONV_WIDTH - 1, d_lru), F32),
            jax.ShapeDtypeStruct((bsz, POOL_PAD, d_pool), F32),
        ],
        scratch_shapes=[
            pltpu.VMEM((tile + HALO, d_lru), F32),
            pltpu.VMEM((tile + HALO, d_pool), F32),
            pltpu.VMEM((N_KV_HEADS, 2, WINDOW + tile, LANES), BF16),
            pltpu.VMEM((N_KV_HEADS, 2, WINDOW + tile, LANES), BF16),
            pltpu.VMEM((8, d_lru), F32),
        ],
        compiler_params=pltpu.CompilerParams(
            dimension_semantics=("arbitrary", "arbitrary"), vmem_limit_bytes=VMEM_LIMIT),
        name="mixer_prompt",
    )(proj, cos, sin, *[consts[n] for n in CONST_NAMES])


def _mixer_sample_kernel(*refs, n_seq, t, d_lru, d_pool, d_attn, d_kv):
    n_in = 3 + len(CONST_NAMES)
    proj_ref, shared = refs[0], refs[1:n_in]
    state_in = refs[n_in:n_in + 5]
    mix_ref = refs[n_in + 5]
    state_out = refs[n_in + 6:n_in + 11]
    scratch = refs[n_in + 11:]
    for q in range(n_seq):
        one = lambda r: r.at[pl.ds(q, 1)]
        _mixer_kernel(proj_ref.at[pl.ds(q * t, t)], *shared, *[one(r) for r in state_in],
                      mix_ref.at[pl.ds(q * t, t)], *[one(r) for r in state_out],
                      *[r.at[q] for r in scratch],
                      tile=t, qblock=t, is_prompt=False, pos0=PAST_LEN,
                      d_lru=d_lru, d_pool=d_pool, d_attn=d_attn, d_kv=d_kv)


def _mixer_sample(proj, row0, consts, cos, sin, cache_k, cache_v, st_h, st_conv, st_pool, layer, dims):
    d_lru, d_pool, d_attn, d_kv = dims["d_lru"], dims["d_pool"], dims["d_attn"], dims["d_kv"]
    d_in = proj.shape[1]
    d_mix = d_lru + d_pool + d_attn
    db, win = cache_k.shape[1], cache_k.shape[2]
    t = cos.shape[0]
    nq = SAMPLE_SEQS_PER_STEP
    assert db % nq == 0 and row0 % (nq * t) == 0
    blk0 = row0 // (nq * t)
    kern = functools.partial(_mixer_sample_kernel, n_seq=nq, t=t,
                             d_lru=d_lru, d_pool=d_pool, d_attn=d_attn, d_kv=d_kv)
    return pl.pallas_call(
        kern,
        grid=(db // nq,),
        in_specs=[
            pl.BlockSpec((nq * t, d_in), lambda b: (blk0 + b, 0)),
            pl.BlockSpec((t, LANES), lambda b: (0, 0)),
            pl.BlockSpec((t, LANES), lambda b: (0, 0)),
        ] + _const_specs(consts) + [
            pl.BlockSpec((None, nq, win, d_kv), lambda b: (layer, b, 0, 0)),
            pl.BlockSpec((None, nq, win, d_kv), lambda b: (layer, b, 0, 0)),
            pl.BlockSpec((None, nq, 1, d_lru), lambda b: (layer, b, 0, 0)),
            pl.BlockSpec((None, nq, CONV_WIDTH - 1, d_lru), lambda b: (layer, b, 0, 0)),
            pl.BlockSpec((None, nq, POOL_PAD, d_pool), lambda b: (layer, b, 0, 0)),
        ],
        out_specs=[
            pl.BlockSpec((nq * t, d_mix), lambda b: (b, 0)),
            pl.BlockSpec((nq, win, d_kv), lambda b: (b, 0, 0)),
            pl.BlockSpec((nq, win, d_kv), lambda b: (b, 0, 0)),
            pl.BlockSpec((nq, 1, d_lru), lambda b: (b, 0, 0)),
            pl.BlockSpec((nq, CONV_WIDTH - 1, d_lru), lambda b: (b, 0, 0)),
            pl.BlockSpec((nq, POOL_PAD, d_pool), lambda b: (b, 0, 0)),
        ],
        out_shape=[
            jax.ShapeDtypeStruct((db * t, d_mix), F32),
            jax.ShapeDtypeStruct((db, win, d_kv), F32),
            jax.ShapeDtypeStruct((db, win, d_kv), F32),
            jax.ShapeDtypeStruct((db, 1, d_lru), F32),
            jax.ShapeDtypeStruct((db, CONV_WIDTH - 1, d_lru), F32),
            jax.ShapeDtypeStruct((db, POOL_PAD, d_pool), F32),
        ],
        scratch_shapes=[
            pltpu.VMEM((nq, t + HALO, d_lru), F32),
            pltpu.VMEM((nq, t + HALO, d_pool), F32),
            pltpu.VMEM((nq, N_KV_HEADS, 2, 2 * WINDOW, LANES), BF16),
            pltpu.VMEM((nq, N_KV_HEADS, 2, 2 * WINDOW, LANES), BF16),
        ],
        compiler_params=pltpu.CompilerParams(
            dimension_semantics=("arbitrary",), vmem_limit_bytes=VMEM_LIMIT),
        name="mixer_sample",
    )(proj, cos, sin, *[consts[n] for n in CONST_NAMES], cache_k, cache_v, st_h, st_conv, st_pool)


def _route(logits, run_cnt):
    t = logits.shape[0]
    lane = lax.broadcasted_iota(I32, (t, LANES), 1)
    lane_f = lane.astype(F32)
    ninf = -jnp.inf
    big = float(LANES)
    is_g = lane < N_EXPERT_GROUPS
    lg = jnp.where(is_g, logits, ninf)
    mg = jnp.max(lg, axis=-1, keepdims=True)
    g_top = jnp.min(jnp.where(lg == mg, lane_f, big), axis=-1, keepdims=True).astype(I32)
    pg_top = 1.0 / jnp.sum(jnp.exp(lg - mg), axis=-1, keepdims=True)
    base = N_EXPERT_GROUPS + EXPERTS_PER_GROUP * g_top
    in_grp = (lane >= base) & (lane < base + EXPERTS_PER_GROUP)
    le = jnp.where(in_grp, logits, ninf)
    m1 = jnp.max(le, axis=-1, keepdims=True)
    i1 = jnp.min(jnp.where(le == m1, lane_f, big), axis=-1, keepdims=True).astype(I32)
    le2 = jnp.where(lane == i1, ninf, le)
    m2 = jnp.max(le2, axis=-1, keepdims=True)
    i2 = jnp.min(jnp.where((le2 == m2) & in_grp & (lane != i1), lane_f, big),
                 axis=-1, keepdims=True).astype(I32)
    se = jnp.sum(jnp.exp(le - m1), axis=-1, keepdims=True)
    p1 = 1.0 / se
    p2 = jnp.exp(m2 - m1) / se
    tot = p1 + p2
    w1 = (p1 / tot) * pg_top
    w2 = (p2 / tot) * pg_top
    a = i1 - base
    b = i2 - base
    lo = jnp.minimum(a, b)
    hi = jnp.maximum(a, b)
    w_lo = jnp.where(a < b, w1, w2)
    w_hi = jnp.where(a < b, w2, w1)
    pid = jnp.where(lo == 0, hi - 1, jnp.where(lo == 1, jnp.where(hi == 3, 3, 4), 5))
    swap = pid == 5
    w_a = jnp.where(swap, w_hi, w_lo)
    w_b = jnp.where(swap, w_lo, w_hi)
    cls = g_top * N_PAIRS + pid
    onehot = lane == cls
    ti = lax.broadcasted_iota(I32, (t, t), 0)
    tj = lax.broadcasted_iota(I32, (t, t), 1)
    lower = jnp.where(tj <= ti, 1.0, 0.0).astype(BF16)
    prefix = jnp.dot(lower, jnp.where(onehot, 1.0, 0.0).astype(BF16), preferred_element_type=F32)
    rank = jnp.sum(jnp.where(onehot, prefix - 1.0 + run_cnt, 0.0), axis=-1, keepdims=True)
    info = jnp.where(lane == 0, w_a, jnp.where(lane == 1, w_b, jnp.where(
        lane == 2, cls.astype(F32), jnp.where(lane == 3, rank, 0.0))))
    return info, run_cnt + prefix[t - 1:t, :]


def _out_proj_kernel(mixp_ref, mixs_ref, xm_ref, xt_ref, w_hbm, g_ref, wr_ref, br_ref,
                     o_ref, info_ref, cnt_ref, w_bf, stage, sem, run_cnt, *, n_main_blocks, d, layer, slab):
    i = pl.program_id(0)

    @pl.when(i == 0)
    def _():
        _load_weight_bf16(w_hbm.at[layer], w_bf, stage, sem)
        run_cnt[...] = jnp.zeros_like(run_cnt)

    mix = jnp.where(i < n_main_blocks, mixp_ref[...], mixs_ref[...].astype(BF16))
    xres = (_pick_rows(i, n_main_blocks, xm_ref, xt_ref, d, slab)
            + jnp.dot(mix, w_bf[...], preferred_element_type=F32))
    h2 = _rms(xres, g_ref[...])
    logits = jnp.dot(h2.astype(BF16), wr_ref[...], preferred_element_type=F32) + br_ref[...]
    info, cnt = _route(logits, run_cnt[0:1, :])
    run_cnt[0:1, :] = cnt
    pitch = _slab_pitch(d)
    _write_slabs(o_ref, (), xres, pitch)
    o_ref[pl.ds(d // LANES, ROW_BLOCK, stride=pitch), :] = info
    info_ref[...] = info
    cnt_ref[...] = jnp.broadcast_to(cnt, cnt_ref.shape)


def _out_proj(mix_p, mix_s, x_main, x_tail, tail_block0, slab, w, layer, g, wr, br, n_rows):
    d = g.shape[1]
    d_mix = w.shape[1]
    npb = mix_p.shape[0] // ROW_BLOCK
    pitch = _slab_pitch(d)
    kern = functools.partial(_out_proj_kernel, n_main_blocks=npb, d=d, layer=layer, slab=slab)
    return pl.pallas_call(
        kern,
        grid=(n_rows // ROW_BLOCK,),
        in_specs=[
            pl.BlockSpec((ROW_BLOCK, d_mix), lambda i: (jnp.minimum(i, npb - 1), 0)),
            pl.BlockSpec((ROW_BLOCK, d_mix), lambda i: (jnp.maximum(i - npb, 0), 0)),
        ] + _row_specs(d, npb, tail_block0, slab) + [
            pl.BlockSpec(memory_space=pl.ANY),
            pl.BlockSpec((1, d), lambda i: (0, 0)),
            pl.BlockSpec((d, LANES), lambda i: (0, 0)),
            pl.BlockSpec((1, LANES), lambda i: (0, 0)),
        ],
        out_specs=[pl.BlockSpec((ROW_BLOCK * pitch, LANES), lambda i: (i, 0)),
                   pl.BlockSpec((ROW_BLOCK, LANES), lambda i: (i, 0)),
                   pl.BlockSpec((8, LANES), lambda i: (0, 0))],
        out_shape=[jax.ShapeDtypeStruct((n_rows * pitch, LANES), F32),
                   jax.ShapeDtypeStruct((n_rows, LANES), F32),
                   jax.ShapeDtypeStruct((8, LANES), F32)],
        scratch_shapes=[
            pltpu.VMEM((d_mix, d), BF16),
            pltpu.VMEM((2, ROW_BLOCK, d), F32),
            pltpu.SemaphoreType.DMA((2,)),
            pltpu.VMEM((8, LANES), F32),
        ],
        compiler_params=pltpu.CompilerParams(
            dimension_semantics=("arbitrary",), vmem_limit_bytes=VMEM_LIMIT),
        name="out_proj",
    )(mix_p, mix_s, x_main, x_tail, w, g, wr, br)


def _moe_kernel(off_ref, cnt_ref, ea_ref, eb_ref, chga_ref, chgb_ref, nch_ref,
                cls_ref, rank_ref, xr_hbm, g_ref, w1a_ref, w3a_ref, w2a_ref, w1b_ref, w3b_ref, w2b_ref,
                xo_hbm,
                perm, pos_v, pos_s, xbuf, obuf, wa1, wa3, wa2, wb1, wb3, wb2, gsem, ssem, psem,
                *, d, n_rows):
    del ea_ref, eb_ref
    m = MOE_CHUNK
    pitch = _slab_pitch(d)
    n_slabs = d // LANES
    c = pl.program_id(0)
    n = nch_ref[0]
    slot = c % 2
    other = 1 - slot

    def start_gather(chunk, sl, j):
        tok = jnp.maximum(perm[(chunk + 1) * m + j], 0)
        pltpu.make_async_copy(xr_hbm.at[pl.ds(tok * pitch, pitch)],
                              xbuf.at[sl, pl.ds(j * pitch, pitch)], gsem.at[sl]).start()

    def start_scatter(chunk, sl, j):
        tok = perm[(chunk + 1) * m + j]
        r = jnp.where(tok < 0, n_rows + sl * m + j, tok)
        pltpu.make_async_copy(obuf.at[sl, pl.ds(j * pitch, pitch)],
                              xo_hbm.at[pl.ds(r * pitch, pitch)], ssem.at[sl]).start(priority=1)

    def wait_gather(sl):
        pltpu.make_async_copy(xr_hbm.at[pl.ds(0, m * pitch)], xbuf.at[sl], gsem.at[sl]).wait()

    def wait_scatter(sl):
        pltpu.make_async_copy(obuf.at[sl], xo_hbm.at[pl.ds(0, m * pitch)], ssem.at[sl]).wait()

    @pl.when(c == 0)
    def _():
        obuf[...] = jnp.zeros_like(obuf)
        pad = pltpu.make_async_copy(obuf.at[0], xo_hbm.at[pl.ds(n_rows * pitch, m * pitch)], ssem.at[0])
        pad.start()
        pad.wait()

        def fill(lo, hi):
            def body(i, z):
                perm[i] = -1
                return z
            lax.fori_loop(lo, hi, body, 0)

        def first_gather(j, z):
            start_gather(0, 0, j)
            return z

        fill(0, m)
        fill((n + 1) * m, (n + 3) * m)
        for k in range(N_CLASSES):
            base = (off_ref[k] + 1) * m
            cnt = cnt_ref[k]
            fill(base + cnt, base + ((cnt + m - 1) // m) * m)

        cls_v = cls_ref[...]
        base = jnp.zeros_like(cls_v)
        for k in range(N_CLASSES):
            base = jnp.where(cls_v == k, (off_ref[k] + 1) * m, base)
        pos_v[...] = base + rank_ref[...]
        to_smem = pltpu.make_async_copy(pos_v, pos_s, psem.at[0])
        to_smem.start()
        to_smem.wait()

        def place_row(r, z):
            def place(j, z2):
                perm[pos_s[r, j]] = r * LANES + j
                return z2
            return lax.fori_loop(0, LANES, place, z, unroll=8)
        lax.fori_loop(0, n_rows // LANES, place_row, 0)
        lax.fori_loop(0, m, first_gather, 0)

    @pl.when(c <= n)
    def _():
        wait_gather(slot)

        @pl.when(c >= 1)
        def _():
            wait_scatter(slot)

        @pl.when(chga_ref[c] == 1)
        def _():
            wa1[...] = w1a_ref[0].astype(BF16)
            wa3[...] = w3a_ref[0].astype(BF16)
            wa2[...] = w2a_ref[0].astype(BF16)

        @pl.when(chgb_ref[c] == 1)
        def _():
            wb1[...] = w1b_ref[0].astype(BF16)
            wb3[...] = w3b_ref[0].astype(BF16)
            wb2[...] = w2b_ref[0].astype(BF16)

        for j in range(m):
            start_gather(c + 1, other, j)
        for j in range(m):
            start_scatter(c - 1, other, j)
        x = _read_slabs(xbuf, (slot,), m, n_slabs, pitch)
        info = xbuf[slot, pl.ds(n_slabs, m, stride=pitch), :]
        w_a = info[:, 0:1]
        w_b = info[:, 1:2]
        h = _rms(x, g_ref[...]).astype(BF16)
        hid_a = (jax.nn.silu(jnp.dot(h, wa1[...], preferred_element_type=F32))
                 * jnp.dot(h, wa3[...], preferred_element_type=F32) * w_a).astype(BF16)
        hid_b = (jax.nn.silu(jnp.dot(h, wb1[...], preferred_element_type=F32))
                 * jnp.dot(h, wb3[...], preferred_element_type=F32) * w_b).astype(BF16)
        y = (jnp.dot(hid_a, wa2[...], preferred_element_type=F32)
             + jnp.dot(hid_b, wb2[...], preferred_element_type=F32))
        _write_slabs(obuf, (slot,), x + y, pitch)

        @pl.when(c == n)
        def _():
            wait_gather(other)
            wait_scatter(other)


def _moe(xr, cls, rank, g, w1, w3, w2, tables, n_rows, c_max):
    d = g.shape[1]
    f = w1.shape[2]
    m = MOE_CHUNK
    pitch = _slab_pitch(d)
    kern = functools.partial(_moe_kernel, d=d, n_rows=n_rows)

    def wspec(shape, which):
        return pl.BlockSpec(shape, lambda c, *pref: (pref[which][c], 0, 0))

    grid_spec = pltpu.PrefetchScalarGridSpec(
        num_scalar_prefetch=7,
        grid=(c_max + 1,),
        in_specs=[
            pl.BlockSpec(cls.shape, lambda c, *pref: (0, 0)),
            pl.BlockSpec(rank.shape, lambda c, *pref: (0, 0)),
            pl.BlockSpec(memory_space=pl.ANY),
            pl.BlockSpec((1, d), lambda c, *pref: (0, 0)),
            wspec((1, d, f), 2), wspec((1, d, f), 2), wspec((1, f, d), 2),
            wspec((1, d, f), 3), wspec((1, d, f), 3), wspec((1, f, d), 3),
        ],
        out_specs=pl.BlockSpec(memory_space=pl.ANY),
        scratch_shapes=[
            pltpu.SMEM(((c_max + 3) * m,), I32),
            pltpu.VMEM(cls.shape, I32),
            pltpu.SMEM(cls.shape, I32),
            pltpu.VMEM((2, m * pitch, LANES), F32),
            pltpu.VMEM((2, m * pitch, LANES), F32),
            pltpu.VMEM((d, f), BF16), pltpu.VMEM((d, f), BF16), pltpu.VMEM((f, d), BF16),
            pltpu.VMEM((d, f), BF16), pltpu.VMEM((d, f), BF16), pltpu.VMEM((f, d), BF16),
            pltpu.SemaphoreType.DMA((2,)),
            pltpu.SemaphoreType.DMA((2,)),
            pltpu.SemaphoreType.DMA((1,)),
        ],
    )
    return pl.pallas_call(
        kern,
        grid_spec=grid_spec,
        out_shape=jax.ShapeDtypeStruct(((n_rows + 2 * m) * pitch, LANES), F32),
        compiler_params=pltpu.CompilerParams(
            dimension_semantics=("arbitrary",), vmem_limit_bytes=VMEM_LIMIT),
        name="moe",
    )(*tables, cls, rank, xr, g, w1, w3, w2, w1, w3, w2)


def _chunk_tables(counts, c_max, expert0):
    m = MOE_CHUNK
    nch_c = (counts + m - 1) // m
    ch_end = jnp.cumsum(nch_c)
    ch_off = ch_end - nch_c
    n_chunks = ch_end[-1]
    chunk = jnp.minimum(jnp.arange(c_max + 1, dtype=I32), jnp.maximum(n_chunks - 1, 0))
    ccls = jnp.sum((chunk[:, None] >= ch_end[None, :]).astype(I32), axis=1)
    ccls = jnp.minimum(ccls, N_CLASSES - 1)
    grp = ccls // N_PAIRS
    pid = ccls % N_PAIRS
    ea = expert0 + grp * EXPERTS_PER_GROUP + jnp.asarray(PAIR_SLOT_A, I32)[pid]
    eb = expert0 + grp * EXPERTS_PER_GROUP + jnp.asarray(PAIR_SLOT_B, I32)[pid]
    first = jnp.ones((1,), I32)
    chga = jnp.concatenate([first, (ea[1:] != ea[:-1]).astype(I32)])
    chgb = jnp.concatenate([first, (eb[1:] != eb[:-1]).astype(I32)])
    return ch_off.astype(I32), counts, ea, eb, chga, chgb, n_chunks.reshape(1).astype(I32)


def _final_norm_kernel(x_ref, g_ref, op_ref, os_ref, *, n_prompt_blocks):
    i = pl.program_id(0)
    d = g_ref.shape[1]
    y = _rms(_read_slabs(x_ref, (), ROW_BLOCK, d // LANES, _slab_pitch(d)), g_ref[...])

    @pl.when(i < n_prompt_blocks)
    def _():
        op_ref[...] = y

    @pl.when(i >= n_prompt_blocks)
    def _():
        os_ref[...] = y


def _final_norm(x, g, n_prompt, n_sample):
    d = g.shape[1]
    npb = n_prompt // ROW_BLOCK
    nsb = n_sample // ROW_BLOCK
    kern = functools.partial(_final_norm_kernel, n_prompt_blocks=npb)
    return pl.pallas_call(
        kern,
        grid=(npb + nsb,),
        in_specs=[pl.BlockSpec((ROW_BLOCK * _slab_pitch(d), LANES), lambda i: (i, 0)),
                  pl.BlockSpec((1, d), lambda i: (0, 0))],
        out_specs=[pl.BlockSpec((ROW_BLOCK, d), lambda i: (jnp.minimum(i, npb - 1), 0)),
                   pl.BlockSpec((ROW_BLOCK, d), lambda i: (jnp.maximum(i - npb, 0), 0))],
        out_shape=[jax.ShapeDtypeStruct((n_prompt, d), F32),
                   jax.ShapeDtypeStruct((n_sample, d), F32)],
        compiler_params=pltpu.CompilerParams(dimension_semantics=("arbitrary",)),
        name="final_norm",
    )(x, g)


def kernel(x_prompt, x_sample, cache_k, cache_v, state_lru_h, state_conv, state_pool, norm1_g, w_in, conv_w, conv_b, lru_wa, lru_ba, lru_wx, lru_bx, lru_lambda, pool_w, pool_scale, attn_sinks, w_out, norm2_g, router_group_w, router_group_b, router_expert_w, router_expert_b, expert_w1, expert_w3, expert_w2, final_norm_g):
    bsz, seq, d = x_prompt.shape
    db, ds, _ = x_sample.shape
    depth = w_in.shape[0]
    win = cache_k.shape[2]
    nkv = cache_k.shape[3]
    d_lru = lru_lambda.shape[1]
    d_pool = pool_scale.shape[1]
    d_kv = nkv * cache_k.shape[4]
    d_attn = attn_sinks.shape[1] * HEAD_DIM
    dims = dict(d_lru=d_lru, d_pool=d_pool, d_attn=d_attn, d_kv=d_kv)
    n_prompt = bsz * seq
    n_sample = db * ds
    n_rows = n_prompt + n_sample
    assert n_prompt % ROW_BLOCK == 0 and n_sample % ROW_BLOCK == 0
    assert win == WINDOW and nkv == N_KV_HEADS and ds <= 8 and WINDOW % ds == 0
    tile = 512 if seq % 512 == 0 else WINDOW
    c_max = -(-n_rows // MOE_CHUNK) + N_CLASSES
    npb = n_prompt // ROW_BLOCK

    cos_p, sin_p = _rope_tables(jnp.arange(seq))
    cos_s, sin_s = _rope_tables(PAST_LEN + jnp.arange(ds))
    wr_all = jnp.concatenate([router_group_w, router_expert_w], axis=2)
    wr_all = jnp.pad(wr_all, ((0, 0), (0, 0), (0, LANES - wr_all.shape[2]))).astype(BF16)
    br_all = jnp.concatenate([router_group_b, router_expert_b], axis=1)
    br_all = jnp.pad(br_all, ((0, 0), (0, LANES - br_all.shape[1])))

    x_main, x_tail, tail0, slab = x_prompt.reshape(n_prompt, d), x_sample.reshape(n_sample, d), 0, False
    ck = cache_k.reshape(depth, db, win, d_kv)
    cv = cache_v.reshape(depth, db, win, d_kv)
    st_h = state_lru_h.reshape(depth, db, 1, d_lru)
    n_exp, _, d_exp = expert_w1.shape[1:]
    w1 = expert_w1.reshape(depth * n_exp, d, d_exp)
    w3 = expert_w3.reshape(depth * n_exp, d, d_exp)
    w2 = expert_w2.reshape(depth * n_exp, d_exp, d)
    outs = [[] for _ in range(10)]
    for l in range(depth):
        p = dict(conv_w=conv_w[l], conv_b=conv_b[l], lru_wa=lru_wa[l], lru_ba=lru_ba[l],
                 lru_wx=lru_wx[l], lru_bx=lru_bx[l], lru_lambda=lru_lambda[l], pool_w=pool_w[l],
                 pool_scale=pool_scale[l], attn_sinks=attn_sinks[l])
        consts = _layer_consts(p, dims)
        proj = _in_proj(x_main, x_tail, tail0, slab, norm1_g[l].reshape(1, d), w_in, l, n_rows, n_prompt)
        mix_p, pk, pv, ph, pc, pp = _mixer_prompt(proj, consts, cos_p, sin_p, bsz, seq, dims, tile)
        mix_s, sk, sv, sh, sc, sp = _mixer_sample(
            proj, n_prompt, consts, cos_s, sin_s, ck, cv, st_h, state_conv, state_pool, l, dims)
        xr, info, cnt = _out_proj(mix_p, mix_s, x_main, x_tail, tail0, slab, w_out, l,
                                  norm2_g[l].reshape(1, d), wr_all[l], br_all[l:l + 1], n_rows)
        route = info[:, 2:4].astype(I32)
        tables = _chunk_tables(cnt[0, :N_CLASSES].astype(I32), c_max, l * n_exp)
        x = _moe(xr, route[:, 0].reshape(-1, LANES), route[:, 1].reshape(-1, LANES),
                 norm2_g[l].reshape(1, d), w1, w3, w2, tables, n_rows, c_max)
        x_main, x_tail, tail0, slab = x, x, npb, True
        for lst, val in zip(outs, (pk.reshape(bsz, WINDOW, nkv, HEAD_DIM), pv.reshape(bsz, WINDOW, nkv, HEAD_DIM),
                                   ph.reshape(bsz, d_lru), pc, pp,
                                   sk.reshape(db, win, nkv, HEAD_DIM), sv.reshape(db, win, nkv, HEAD_DIM),
                                   sh.reshape(db, d_lru), sc, sp)):
            lst.append(val)
    y_p, y_s = _final_norm(x_main, final_norm_g.reshape(1, d), n_prompt, n_sample)
    return (y_p.reshape(bsz, seq, d), y_s.reshape(db, ds, d)) + tuple(jnp.stack(o) for o in outs)
```

```python
import functools

import jax
import jax.numpy as jnp
from jax import lax
from jax.experimental import pallas as pl
from jax.experimental.pallas import tpu as pltpu

F32 = jnp.float32
BF16 = jnp.bfloat16
I32 = jnp.int32

LRU_HEADS = 8
CONV_WIDTH = 4
LRU_C = 8.0
POOL_WINDOWS = (2, 4, 8, 16)
POOL_PAD = max(POOL_WINDOWS) - 1
HEAD_DIM = 64
N_KV_HEADS = 4
WINDOW = 128
ROPE_THETA = 10000.0
ATTN_SCALE = HEAD_DIM ** -0.5
LOG2_E = 1.4426950408889634
N_EXPERT_GROUPS = 4
EXPERTS_PER_GROUP = 4
N_EXPERTS = N_EXPERT_GROUPS * EXPERTS_PER_GROUP
RMS_EPS = 1e-6
PAST_LEN = 16384

LANES = 128
HALF = LANES // 2
HALO = 16
ROW_BLOCK = 256
SAMPLE_SEQS_PER_STEP = 4
MOE_CHUNK = 128
ROW_DMA_QUEUE = 1
PAIR_SLOT_A = (0, 0, 0, 1, 1, 3)
PAIR_SLOT_B = (1, 2, 3, 3, 2, 2)
N_PAIRS = len(PAIR_SLOT_A)
N_CLASSES = N_EXPERT_GROUPS * N_PAIRS
VMEM_LIMIT = 52 * 1024 * 1024


def _rms(x, g):
    return (x * lax.rsqrt(jnp.mean(x * x, axis=-1, keepdims=True) + RMS_EPS)) * g


def _load_weight_bf16(w_hbm, w_bf, stage, sem):
    rows = stage.shape[1]
    n = w_hbm.shape[0] // rows

    def copy(i, sl):
        return pltpu.make_async_copy(w_hbm.at[pl.ds(i * rows, rows)], stage.at[sl], sem.at[sl])

    copy(0, 0).start()
    for i in range(n):
        sl = i % 2
        if i + 1 < n:
            copy(i + 1, 1 - sl).start()
        copy(i, sl).wait()
        w_bf[i * rows:(i + 1) * rows, :] = stage[sl].astype(BF16)


def _slab_pitch(d):
    return d // LANES + 1


def _read_slabs(ref, lead, n_tok, n_slabs, pitch):
    return jnp.concatenate(
        [ref[lead + (pl.ds(k, n_tok, stride=pitch), slice(None))] for k in range(n_slabs)], axis=1)


def _write_slabs(ref, lead, val, pitch):
    n_tok = val.shape[0]
    for k in range(val.shape[1] // LANES):
        ref[lead + (pl.ds(k, n_tok, stride=pitch), slice(None))] = val[:, k * LANES:(k + 1) * LANES]


def _pick_rows(i, n_main_blocks, main_ref, tail_ref, d, slab):
    if slab:
        return _read_slabs(main_ref, (), ROW_BLOCK, d // LANES, _slab_pitch(d))
    return jnp.where(i < n_main_blocks, main_ref[...], tail_ref[...])


def _in_proj_kernel(xm_ref, xt_ref, g_ref, w_hbm, o_ref, w_bf, stage, sem, *, n_main_blocks, layer, slab):
    i = pl.program_id(0)

    @pl.when(i == 0)
    def _():
        _load_weight_bf16(w_hbm.at[layer], w_bf, stage, sem)

    h = _rms(_pick_rows(i, n_main_blocks, xm_ref, xt_ref, g_ref.shape[1], slab), g_ref[...])
    o_ref[...] = jnp.dot(h.astype(BF16), w_bf[...], preferred_element_type=F32)


def _row_specs(d, n_main_blocks, tail_block0, slab):
    if slab:
        shape = (ROW_BLOCK * _slab_pitch(d), LANES)
        return [pl.BlockSpec(shape, lambda i: (i, 0)), pl.BlockSpec(shape, lambda i: (tail_block0, 0))]
    shape = (ROW_BLOCK, d)
    return [pl.BlockSpec(shape, lambda i: (jnp.minimum(i, n_main_blocks - 1), 0)),
            pl.BlockSpec(shape, lambda i: (jnp.maximum(i - n_main_blocks, 0) + tail_block0, 0))]


def _in_proj(x_main, x_tail, tail_block0, slab, g, w, layer, n_rows, n_main):
    d = g.shape[1]
    d_in = w.shape[2]
    nmb = n_main // ROW_BLOCK
    kern = functools.partial(_in_proj_kernel, n_main_blocks=nmb, layer=layer, slab=slab)
    return pl.pallas_call(
        kern,
        grid=(n_rows // ROW_BLOCK,),
        in_specs=_row_specs(d, nmb, tail_block0, slab) + [
            pl.BlockSpec((1, d), lambda i: (0, 0)),
            pl.BlockSpec(memory_space=pl.ANY),
        ],
        out_specs=pl.BlockSpec((ROW_BLOCK, d_in), lambda i: (i, 0)),
        out_shape=jax.ShapeDtypeStruct((n_rows, d_in), F32),
        scratch_shapes=[
            pltpu.VMEM((d, d_in), BF16),
            pltpu.VMEM((2, ROW_BLOCK, d_in), F32),
            pltpu.SemaphoreType.DMA((2,)),
        ],
        compiler_params=pltpu.CompilerParams(
            dimension_semantics=("arbitrary",), vmem_limit_bytes=VMEM_LIMIT),
        name="in_proj",
    )(x_main, x_tail, g, w)


def _scan_linear(a, b):
    t = a.shape[0]
    row = lax.broadcasted_iota(I32, a.shape, 0)
    d = 1
    while d < min(t, 8):
        a_sh = pltpu.roll(a, d, axis=0)
        b_sh = pltpu.roll(b, d, axis=0)
        m = row >= d
        b = jnp.where(m, a * b_sh + b, b)
        a = jnp.where(m, a * a_sh, a)
        d *= 2
    while d < t:
        b = jnp.concatenate([b[:d], a[d:] * b[:t - d] + b[d:]], axis=0)
        a = jnp.concatenate([a[:d], a[d:] * a[:t - d]], axis=0)
        d *= 2
    return a, b


def _lru_chunk(xc, g, h_prev, wg, ba, bx, sp):
    pre = jnp.dot(xc.astype(BF16), wg, preferred_element_type=F32)
    r = jax.nn.sigmoid(pre[:, :LANES] + ba)
    ig = jax.nn.sigmoid(pre[:, LANES:] + bx)
    log_a = (-LRU_C * r) * sp
    a = jnp.exp(log_a)
    v = 1.0 - a * a
    bterm = jnp.where(v > 0.0, v * lax.rsqrt(v), 0.0) * ig * xc
    a_cum, h0 = _scan_linear(a, bterm)
    hs = a_cum * h_prev + h0
    return hs * jax.nn.gelu(g), hs


def _rope(x, cos, sin_signed):
    n = x.shape[1] // LANES
    lane = lax.broadcasted_iota(I32, (x.shape[0], LANES), 1)
    first = (lane % HEAD_DIM) < (HEAD_DIM // 2)
    outs = []
    for c in range(n):
        xc = x[:, c * LANES:(c + 1) * LANES]
        swapped = jnp.where(first, pltpu.roll(xc, LANES - HEAD_DIM // 2, axis=1),
                            pltpu.roll(xc, HEAD_DIM // 2, axis=1))
        outs.append(xc * cos + swapped * sin_signed)
    return outs


def _store_head_variants(var_ref, row0, chunks):
    t = chunks[0].shape[0]
    lane = lax.broadcasted_iota(I32, (t, LANES), 1)
    for kc, x in enumerate(chunks):
        swapped = pltpu.roll(x, HALF, axis=1)
        for hh in range(2):
            for p in range(2):
                src = x if p == hh else swapped
                keep = (lane < HALF) if p == 0 else (lane >= HALF)
                var_ref[2 * kc + hh, p, row0:row0 + t, :] = jnp.where(keep, src, 0.0).astype(BF16)


def _attend_block(q_chunks, kvar, vvar, key0, sinks, lim, out_ref, out_rows, out_col0):
    qb = q_chunks[0].shape[0]
    nk = 2 * WINDOW
    gq = (2 * len(q_chunks)) // N_KV_HEADS
    rows = 2 * qb
    qi = lax.broadcasted_iota(I32, (rows, nk), 0) % qb
    kj = lax.broadcasted_iota(I32, (rows, nk), 1)
    valid = ((kj < WINDOW) & (kj > qi + lim)) | ((kj >= WINDOW) & (kj - WINDOW <= qi))
    top = lax.broadcasted_iota(I32, (rows, 1), 0) < qb
    lane = lax.broadcasted_iota(I32, (nk, LANES), 1)
    ones_lo = jnp.where(lane < HALF, 1.0, 0.0).astype(BF16)
    ones_hi = jnp.where(lane >= HALF, 1.0, 0.0).astype(BF16)
    lane_o = lax.broadcasted_iota(I32, (rows, LANES), 1)
    nt = (((1,), (1,)), ((), ()))
    for c in range(N_KV_HEADS):
        c0 = c * gq // 2
        qs = jnp.concatenate([q_chunks[c0], q_chunks[c0 + 1]], axis=0)
        es, sink_terms = [], []
        for p in range(2):
            kc = kvar[c, p, key0:key0 + nk, :]
            s = lax.dot_general(qs, kc, nt, preferred_element_type=F32)
            s = jnp.where(valid, s, -jnp.inf)
            h0 = c * gq + p
            sink = jnp.where(top, sinks[:, h0:h0 + 1], sinks[:, h0 + 2:h0 + 3])
            m = jnp.maximum(jnp.max(s, axis=-1, keepdims=True), sink)
            es.append(jnp.exp2(s - m).astype(BF16))
            sink_terms.append(jnp.exp2(sink - m))
        r0 = jnp.concatenate([vvar[c, 0, key0:key0 + nk, :], ones_lo], axis=1)
        r1 = jnp.concatenate([vvar[c, 1, key0:key0 + nk, :], ones_hi], axis=1)
        od = (jnp.dot(es[0], r0, preferred_element_type=F32)
              + jnp.dot(es[1], r1, preferred_element_type=F32))
        den = od[:, LANES:] + jnp.where(lane_o < HALF, sink_terms[0], sink_terms[1])
        o = (od[:, :LANES] / den).astype(out_ref.dtype)
        out_ref[out_rows, out_col0 + c0 * LANES:out_col0 + (c0 + 1) * LANES] = o[0:qb]
        out_ref[out_rows, out_col0 + (c0 + 1) * LANES:out_col0 + (c0 + 2) * LANES] = o[qb:rows]


def _mixer_kernel(*refs, tile, qblock, is_prompt, pos0, d_lru, d_pool, d_attn, d_kv):
    (proj_ref, cos_ref, sin_ref, convw_ref, convb_ref, wg_ref, ba_ref, bx_ref, lam_ref,
     poolw_ref, pscale_ref, sink_ref) = refs[:12]
    if is_prompt:
        (mix_ref, ko_ref, vo_ref, ho_ref, co_ref, po_ref,
         lru_ext, pool_ext, kvar, vvar, hcar) = refs[12:]
    else:
        (ck_ref, cv_ref, sh_ref, sc_ref, sp_ref,
         mix_ref, ko_ref, vo_ref, ho_ref, co_ref, po_ref,
         lru_ext, pool_ext, kvar, vvar) = refs[12:]

    s = pl.program_id(1) if is_prompt else 0
    o_glru = d_lru
    o_pool = 2 * d_lru
    o_q = o_pool + d_pool
    o_k = o_q + d_attn
    o_v = o_k + d_kv
    n_kc = d_kv // LANES

    if is_prompt:
        @pl.when(s == 0)
        def _():
            lru_ext[0:HALO, :] = jnp.zeros((HALO, d_lru), F32)
            pool_ext[0:HALO, :] = jnp.zeros((HALO, d_pool), F32)
            hcar[...] = jnp.zeros_like(hcar)
            kvar[:, :, 0:WINDOW, :] = jnp.zeros((N_KV_HEADS, 2, WINDOW, LANES), BF16)
            vvar[:, :, 0:WINDOW, :] = jnp.zeros((N_KV_HEADS, 2, WINDOW, LANES), BF16)
        h_prev = hcar[0:1, :]
    else:
        lru_ext[0:HALO, :] = jnp.zeros((HALO, d_lru), F32)
        pool_ext[0:HALO, :] = jnp.zeros((HALO, d_pool), F32)
        lru_ext[HALO - (CONV_WIDTH - 1):HALO, :] = sc_ref[0]
        pool_ext[HALO - POOL_PAD:HALO, :] = sp_ref[0]
        h_prev = sh_ref[0]
        kvar[:, :, WINDOW:2 * WINDOW, :] = jnp.zeros((N_KV_HEADS, 2, WINDOW, LANES), BF16)
        vvar[:, :, WINDOW:2 * WINDOW, :] = jnp.zeros((N_KV_HEADS, 2, WINDOW, LANES), BF16)
        _store_head_variants(kvar, 0, [ck_ref[0, :, c * LANES:(c + 1) * LANES] for c in range(n_kc)])
        _store_head_variants(vvar, 0, [cv_ref[0, :, c * LANES:(c + 1) * LANES] for c in range(n_kc)])

    lru_ext[HALO:HALO + tile, :] = proj_ref[:, 0:d_lru]
    pool_ext[HALO:HALO + tile, :] = proj_ref[:, o_pool:o_pool + d_pool]

    xc = convb_ref[...] + convw_ref[0:1, :] * lru_ext[HALO - 3:HALO - 3 + tile, :]
    for j in range(1, CONV_WIDTH):
        xc = xc + convw_ref[j:j + 1, :] * lru_ext[HALO - 3 + j:HALO - 3 + j + tile, :]
    neg = -lam_ref[...]
    sp_all = jnp.maximum(neg, 0.0) + jnp.log1p(jnp.exp(-jnp.abs(neg)))
    h_last = []
    for c in range(d_lru // LANES):
        cs = slice(c * LANES, (c + 1) * LANES)
        y, hs = _lru_chunk(xc[:, cs], proj_ref[:, o_glru + c * LANES:o_glru + (c + 1) * LANES],
                           h_prev[:, cs], wg_ref[c], ba_ref[:, cs], bx_ref[:, cs], sp_all[:, cs])
        mix_ref[:, cs] = y.astype(mix_ref.dtype)
        h_last.append(hs[tile - 1:tile, :])
    h_last = jnp.concatenate(h_last, axis=-1)

    row = lax.broadcasted_iota(I32, (tile, LANES), 0)
    pos = pos0 + s * tile + row
    for gi, w in enumerate(POOL_WINDOWS):
        cs = slice(gi * LANES, (gi + 1) * LANES)
        e = pool_ext[:, cs]
        acc = e
        step = 1
        while step < w:
            acc = acc + pltpu.roll(acc, step, axis=0)
            step *= 2
        cnt = jnp.minimum(pos + 1, w).astype(F32)
        dlt = acc[HALO:HALO + tile, :] / cnt - e[HALO:HALO + tile, :]
        y = jnp.dot(dlt.astype(BF16), poolw_ref[gi], preferred_element_type=F32) * pscale_ref[:, cs]
        mix_ref[:, d_lru + gi * LANES:d_lru + (gi + 1) * LANES] = y.astype(mix_ref.dtype)

    sinks = sink_ref[...] * LOG2_E
    o_attn = d_lru + d_pool
    cos = cos_ref[...]
    sin = sin_ref[...]
    q_chunks = [(qc * (ATTN_SCALE * LOG2_E)).astype(BF16)
                for qc in _rope(proj_ref[:, o_q:o_q + d_attn], cos, sin)]
    k_rot = _rope(proj_ref[:, o_k:o_k + d_kv], cos, sin)
    _store_head_variants(kvar, WINDOW, k_rot)
    _store_head_variants(vvar, WINDOW, [proj_ref[:, o_v + c * LANES:o_v + (c + 1) * LANES]
                                        for c in range(n_kc)])
    for blk in range(tile // qblock):
        rs = slice(blk * qblock, (blk + 1) * qblock)
        if is_prompt and blk == 0:
            lim = jnp.where(s == 0, WINDOW, 0)
        else:
            lim = 0
        _attend_block([qc[rs] for qc in q_chunks], kvar, vvar, blk * qblock, sinks, lim,
                      mix_ref, rs, o_attn)

    k_last = jnp.concatenate([kc[tile - qblock:tile] for kc in k_rot], axis=-1)
    if is_prompt:
        lru_ext[0:HALO, :] = lru_ext[tile:tile + HALO, :]
        pool_ext[0:HALO, :] = pool_ext[tile:tile + HALO, :]
        hcar[0:1, :] = h_last
        kvar[:, :, 0:WINDOW, :] = kvar[:, :, tile:tile + WINDOW, :]
        vvar[:, :, 0:WINDOW, :] = vvar[:, :, tile:tile + WINDOW, :]

        @pl.when(s == pl.num_programs(1) - 1)
        def _():
            ko_ref[0] = k_last
            vo_ref[0] = proj_ref[tile - qblock:tile, o_v:o_v + d_kv]
            ho_ref[0] = h_last
            co_ref[0] = lru_ext[HALO + tile - (CONV_WIDTH - 1):HALO + tile, :]
            po_ref[0] = pool_ext[HALO + tile - POOL_PAD:HALO + tile, :]
    else:
        ko_ref[0, 0:WINDOW - tile, :] = ck_ref[0, tile:WINDOW, :]
        ko_ref[0, WINDOW - tile:WINDOW, :] = k_last
        vo_ref[0, 0:WINDOW - tile, :] = cv_ref[0, tile:WINDOW, :]
        vo_ref[0, WINDOW - tile:WINDOW, :] = proj_ref[:, o_v:o_v + d_kv]
        ho_ref[0] = h_last
        co_ref[0] = lru_ext[HALO + tile - (CONV_WIDTH - 1):HALO + tile, :]
        po_ref[0] = pool_ext[HALO + tile - POOL_PAD:HALO + tile, :]


def _layer_consts(p, dims):
    d_lru, d_pool = dims["d_lru"], dims["d_pool"]
    hd = d_lru // LRU_HEADS
    per = LANES // hd
    nchunk = d_lru // LANES

    def blockdiag(w):
        w = w.reshape(nchunk, per, hd, hd)
        eye = jnp.eye(per, dtype=w.dtype)
        return jnp.einsum("cpij,pq->cpiqj", w, eye).reshape(nchunk, LANES, LANES)

    wg = jnp.concatenate([blockdiag(p["lru_wa"]), blockdiag(p["lru_wx"])], axis=-1).astype(BF16)
    return dict(
        convw=p["conv_w"], convb=p["conv_b"].reshape(1, d_lru), wg=wg,
        ba=p["lru_ba"].reshape(1, d_lru), bx=p["lru_bx"].reshape(1, d_lru),
        lam=p["lru_lambda"].reshape(1, d_lru), poolw=p["pool_w"].astype(BF16),
        pscale=p["pool_scale"].reshape(1, d_pool), sinks=p["attn_sinks"].reshape(1, -1))


def _rope_tables(pos):
    half = HEAD_DIM // 2
    inv = ROPE_THETA ** (-jnp.arange(half, dtype=F32) / half)
    ang = pos.astype(F32)[:, None] * inv[None, :]
    cos = jnp.cos(ang)
    sin = jnp.sin(ang)
    cos2 = jnp.concatenate([cos, cos], axis=-1)
    sin2 = jnp.concatenate([-sin, sin], axis=-1)
    reps = LANES // HEAD_DIM
    return jnp.tile(cos2, (1, reps)), jnp.tile(sin2, (1, reps))


CONST_NAMES = ("convw", "convb", "wg", "ba", "bx", "lam", "poolw", "pscale", "sinks")


def _const_specs(consts):
    return [pl.BlockSpec(consts[n].shape, functools.partial(lambda nd, *_: (0,) * nd, consts[n].ndim))
            for n in CONST_NAMES]


def _mixer_prompt(proj, consts, cos, sin, bsz, seq, dims, tile):
    d_lru, d_pool, d_attn, d_kv = dims["d_lru"], dims["d_pool"], dims["d_attn"], dims["d_kv"]
    d_in = proj.shape[1]
    d_mix = d_lru + d_pool + d_attn
    ns = seq // tile
    kern = functools.partial(_mixer_kernel, tile=tile, qblock=WINDOW, is_prompt=True, pos0=0,
                             d_lru=d_lru, d_pool=d_pool, d_attn=d_attn, d_kv=d_kv)
    return pl.pallas_call(
        kern,
        grid=(bsz, ns),
        in_specs=[
            pl.BlockSpec((tile, d_in), lambda b, s: (b * ns + s, 0)),
            pl.BlockSpec((tile, LANES), lambda b, s: (s, 0)),
            pl.BlockSpec((tile, LANES), lambda b, s: (s, 0)),
        ] + _const_specs(consts),
        out_specs=[
            pl.BlockSpec((tile, d_mix), lambda b, s: (b * ns + s, 0)),
            pl.BlockSpec((1, WINDOW, d_kv), lambda b, s: (b, 0, 0)),
            pl.BlockSpec((1, WINDOW, d_kv), lambda b, s: (b, 0, 0)),
            pl.BlockSpec((1, 1, d_lru), lambda b, s: (b, 0, 0)),
            pl.BlockSpec((1, CONV_WIDTH - 1, d_lru), lambda b, s: (b, 0, 0)),
            pl.BlockSpec((1, POOL_PAD, d_pool), lambda b, s: (b, 0, 0)),
        ],
        out_shape=[
            jax.ShapeDtypeStruct((bsz * seq, d_mix), BF16),
            jax.ShapeDtypeStruct((bsz, WINDOW, d_kv), F32),
            jax.ShapeDtypeStruct((bsz, WINDOW, d_kv), F32),
            jax.ShapeDtypeStruct((bsz, 1, d_lru), F32),
            jax.ShapeDtypeStruct((bsz, CONV_WIDTH - 1, d_lru), F32),
            jax.ShapeDtypeStruct((bsz, POOL_PAD, d_pool), F32),
        ],
        scratch_shapes=[
            pltpu.VMEM((tile + HALO, d_lru), F32),
            pltpu.VMEM((tile + HALO, d_pool), F32),
            pltpu.VMEM((N_KV_HEADS, 2, WINDOW + tile, LANES), BF16),
            pltpu.VMEM((N_KV_HEADS, 2, WINDOW + tile, LANES), BF16),
            pltpu.VMEM((8, d_lru), F32),
        ],
        compiler_params=pltpu.CompilerParams(
            dimension_semantics=("arbitrary", "arbitrary"), vmem_limit_bytes=VMEM_LIMIT),
        name="mixer_prompt",
    )(proj, cos, sin, *[consts[n] for n in CONST_NAMES])


def _mixer_sample_kernel(*refs, n_seq, t, d_lru, d_pool, d_attn, d_kv):
    n_in = 3 + len(CONST_NAMES)
    proj_ref, shared = refs[0], refs[1:n_in]
    state_in = refs[n_in:n_in + 5]
    mix_ref = refs[n_in + 5]
    state_out = refs[n_in + 6:n_in + 11]
    scratch = refs[n_in + 11:]
    for q in range(n_seq):
        one = lambda r: r.at[pl.ds(q, 1)]
        _mixer_kernel(proj_ref.at[pl.ds(q * t, t)], *shared, *[one(r) for r in state_in],
                      mix_ref.at[pl.ds(q * t, t)], *[one(r) for r in state_out],
                      *[r.at[q] for r in scratch],
                      tile=t, qblock=t, is_prompt=False, pos0=PAST_LEN,
                      d_lru=d_lru, d_pool=d_pool, d_attn=d_attn, d_kv=d_kv)


def _mixer_sample(proj, row0, consts, cos, sin, cache_k, cache_v, st_h, st_conv, st_pool, layer, dims):
    d_lru, d_pool, d_attn, d_kv = dims["d_lru"], dims["d_pool"], dims["d_attn"], dims["d_kv"]
    d_in = proj.shape[1]
    d_mix = d_lru + d_pool + d_attn
    db, win = cache_k.shape[1], cache_k.shape[2]
    t = cos.shape[0]
    nq = SAMPLE_SEQS_PER_STEP
    assert db % nq == 0 and row0 % (nq * t) == 0
    blk0 = row0 // (nq * t)
    kern = functools.partial(_mixer_sample_kernel, n_seq=nq, t=t,
                             d_lru=d_lru, d_pool=d_pool, d_attn=d_attn, d_kv=d_kv)
    return pl.pallas_call(
        kern,
        grid=(db // nq,),
        in_specs=[
            pl.BlockSpec((nq * t, d_in), lambda b: (blk0 + b, 0)),
            pl.BlockSpec((t, LANES), lambda b: (0, 0)),
            pl.BlockSpec((t, LANES), lambda b: (0, 0)),
        ] + _const_specs(consts) + [
            pl.BlockSpec((None, nq, win, d_kv), lambda b: (layer, b, 0, 0)),
            pl.BlockSpec((None, nq, win, d_kv), lambda b: (layer, b, 0, 0)),
            pl.BlockSpec((None, nq, 1, d_lru), lambda b: (layer, b, 0, 0)),
            pl.BlockSpec((None, nq, CONV_WIDTH - 1, d_lru), lambda b: (layer, b, 0, 0)),
            pl.BlockSpec((None, nq, POOL_PAD, d_pool), lambda b: (layer, b, 0, 0)),
        ],
        out_specs=[
            pl.BlockSpec((nq * t, d_mix), lambda b: (b, 0)),
            pl.BlockSpec((nq, win, d_kv), lambda b: (b, 0, 0)),
            pl.BlockSpec((nq, win, d_kv), lambda b: (b, 0, 0)),
            pl.BlockSpec((nq, 1, d_lru), lambda b: (b, 0, 0)),
            pl.BlockSpec((nq, CONV_WIDTH - 1, d_lru), lambda b: (b, 0, 0)),
            pl.BlockSpec((nq, POOL_PAD, d_pool), lambda b: (b, 0, 0)),
        ],
        out_shape=[
            jax.ShapeDtypeStruct((db * t, d_mix), F32),
            jax.ShapeDtypeStruct((db, win, d_kv), F32),
            jax.ShapeDtypeStruct((db, win, d_kv), F32),
            jax.ShapeDtypeStruct((db, 1, d_lru), F32),
            jax.ShapeDtypeStruct((db, CONV_WIDTH - 1, d_lru), F32),
            jax.ShapeDtypeStruct((db, POOL_PAD, d_pool), F32),
        ],
        scratch_shapes=[
            pltpu.VMEM((nq, t + HALO, d_lru), F32),
            pltpu.VMEM((nq, t + HALO, d_pool), F32),
            pltpu.VMEM((nq, N_KV_HEADS, 2, 2 * WINDOW, LANES), BF16),
            pltpu.VMEM((nq, N_KV_HEADS, 2, 2 * WINDOW, LANES), BF16),
        ],
        compiler_params=pltpu.CompilerParams(
            dimension_semantics=("arbitrary",), vmem_limit_bytes=VMEM_LIMIT),
        name="mixer_sample",
    )(proj, cos, sin, *[consts[n] for n in CONST_NAMES], cache_k, cache_v, st_h, st_conv, st_pool)


def _route(logits, run_cnt):
    t = logits.shape[0]
    lane = lax.broadcasted_iota(I32, (t, LANES), 1)
    lane_f = lane.astype(F32)
    ninf = -jnp.inf
    big = float(LANES)
    is_g = lane < N_EXPERT_GROUPS
    lg = jnp.where(is_g, logits, ninf)
    mg = jnp.max(lg, axis=-1, keepdims=True)
    g_top = jnp.min(jnp.where(lg == mg, lane_f, big), axis=-1, keepdims=True).astype(I32)
    pg_top = 1.0 / jnp.sum(jnp.exp(lg - mg), axis=-1, keepdims=True)
    base = N_EXPERT_GROUPS + EXPERTS_PER_GROUP * g_top
    in_grp = (lane >= base) & (lane < base + EXPERTS_PER_GROUP)
    le = jnp.where(in_grp, logits, ninf)
    m1 = jnp.max(le, axis=-1, keepdims=True)
    i1 = jnp.min(jnp.where(le == m1, lane_f, big), axis=-1, keepdims=True).astype(I32)
    le2 = jnp.where(lane == i1, ninf, le)
    m2 = jnp.max(le2, axis=-1, keepdims=True)
    i2 = jnp.min(jnp.where((le2 == m2) & in_grp & (lane != i1), lane_f, big),
                 axis=-1, keepdims=True).astype(I32)
    se = jnp.sum(jnp.exp(le - m1), axis=-1, keepdims=True)
    p1 = 1.0 / se
    p2 = jnp.exp(m2 - m1) / se
    tot = p1 + p2
    w1 = (p1 / tot) * pg_top
    w2 = (p2 / tot) * pg_top
    a = i1 - base
    b = i2 - base
    lo = jnp.minimum(a, b)
    hi = jnp.maximum(a, b)
    w_lo = jnp.where(a < b, w1, w2)
    w_hi = jnp.where(a < b, w2, w1)
    pid = jnp.where(lo == 0, hi - 1, jnp.where(lo == 1, jnp.where(hi == 3, 3, 4), 5))
    swap = pid == 5
    w_a = jnp.where(swap, w_hi, w_lo)
    w_b = jnp.where(swap, w_lo, w_hi)
    cls = g_top * N_PAIRS + pid
    onehot = lane == cls
    ti = lax.broadcasted_iota(I32, (t, t), 0)
    tj = lax.broadcasted_iota(I32, (t, t), 1)
    lower = jnp.where(tj <= ti, 1.0, 0.0).astype(BF16)
    prefix = jnp.dot(lower, jnp.where(onehot, 1.0, 0.0).astype(BF16), preferred_element_type=F32)
    rank = jnp.sum(jnp.where(onehot, prefix - 1.0 + run_cnt, 0.0), axis=-1, keepdims=True)
    info = jnp.where(lane == 0, w_a, jnp.where(lane == 1, w_b, jnp.where(
        lane == 2, cls.astype(F32), jnp.where(lane == 3, rank, 0.0))))
    return info, run_cnt + prefix[t - 1:t, :]


def _out_proj_kernel(mixp_ref, mixs_ref, xm_ref, xt_ref, w_hbm, g_ref, wr_ref, br_ref,
                     o_ref, info_ref, cnt_ref, w_bf, stage, sem, run_cnt, *, n_main_blocks, d, layer, slab):
    i = pl.program_id(0)

    @pl.when(i == 0)
    def _():
        _load_weight_bf16(w_hbm.at[layer], w_bf, stage, sem)
        run_cnt[...] = jnp.zeros_like(run_cnt)

    mix = jnp.where(i < n_main_blocks, mixp_ref[...], mixs_ref[...].astype(BF16))
    xres = (_pick_rows(i, n_main_blocks, xm_ref, xt_ref, d, slab)
            + jnp.dot(mix, w_bf[...], preferred_element_type=F32))
    h2 = _rms(xres, g_ref[...])
    logits = jnp.dot(h2.astype(BF16), wr_ref[...], preferred_element_type=F32) + br_ref[...]
    info, cnt = _route(logits, run_cnt[0:1, :])
    run_cnt[0:1, :] = cnt
    pitch = _slab_pitch(d)
    _write_slabs(o_ref, (), xres, pitch)
    o_ref[pl.ds(d // LANES, ROW_BLOCK, stride=pitch), :] = info
    info_ref[...] = info
    cnt_ref[...] = jnp.broadcast_to(cnt, cnt_ref.shape)


def _out_proj(mix_p, mix_s, x_main, x_tail, tail_block0, slab, w, layer, g, wr, br, n_rows):
    d = g.shape[1]
    d_mix = w.shape[1]
    npb = mix_p.shape[0] // ROW_BLOCK
    pitch = _slab_pitch(d)
    kern = functools.partial(_out_proj_kernel, n_main_blocks=npb, d=d, layer=layer, slab=slab)
    return pl.pallas_call(
        kern,
        grid=(n_rows // ROW_BLOCK,),
        in_specs=[
            pl.BlockSpec((ROW_BLOCK, d_mix), lambda i: (jnp.minimum(i, npb - 1), 0)),
            pl.BlockSpec((ROW_BLOCK, d_mix), lambda i: (jnp.maximum(i - npb, 0), 0)),
        ] + _row_specs(d, npb, tail_block0, slab) + [
            pl.BlockSpec(memory_space=pl.ANY),
            pl.BlockSpec((1, d), lambda i: (0, 0)),
            pl.BlockSpec((d, LANES), lambda i: (0, 0)),
            pl.BlockSpec((1, LANES), lambda i: (0, 0)),
        ],
        out_specs=[pl.BlockSpec((ROW_BLOCK * pitch, LANES), lambda i: (i, 0)),
                   pl.BlockSpec((ROW_BLOCK, LANES), lambda i: (i, 0)),
                   pl.BlockSpec((8, LANES), lambda i: (0, 0))],
        out_shape=[jax.ShapeDtypeStruct((n_rows * pitch, LANES), F32),
                   jax.ShapeDtypeStruct((n_rows, LANES), F32),
                   jax.ShapeDtypeStruct((8, LANES), F32)],
        scratch_shapes=[
            pltpu.VMEM((d_mix, d), BF16),
            pltpu.VMEM((2, ROW_BLOCK, d), F32),
            pltpu.SemaphoreType.DMA((2,)),
            pltpu.VMEM((8, LANES), F32),
        ],
        compiler_params=pltpu.CompilerParams(
            dimension_semantics=("arbitrary",), vmem_limit_bytes=VMEM_LIMIT),
        name="out_proj",
    )(mix_p, mix_s, x_main, x_tail, w, g, wr, br)


def _moe_kernel(cls_ref, rank_ref, off_ref, cnt_ref, ea_ref, eb_ref, chga_ref, chgb_ref, nch_ref,
                xr_hbm, g_ref, w1a_ref, w3a_ref, w2a_ref, w1b_ref, w3b_ref, w2b_ref,
                xo_hbm,
                perm, xbuf, obuf, wa1, wa3, wa2, wb1, wb3, wb2, gsem, ssem, *, d, n_rows):
    del ea_ref, eb_ref
    m = MOE_CHUNK
    pitch = _slab_pitch(d)
    n_slabs = d // LANES
    c = pl.program_id(0)
    n = nch_ref[0]
    slot = c % 2
    other = 1 - slot

    def start_gather(chunk, sl, j):
        tok = jnp.maximum(perm[(chunk + 1) * m + j], 0)
        pltpu.make_async_copy(xr_hbm.at[pl.ds(tok * pitch, pitch)],
                              xbuf.at[sl, pl.ds(j * pitch, pitch)], gsem.at[sl]).start(priority=ROW_DMA_QUEUE)

    def start_scatter(chunk, sl, j):
        tok = perm[(chunk + 1) * m + j]
        r = jnp.where(tok < 0, n_rows + sl * m + j, tok)
        pltpu.make_async_copy(obuf.at[sl, pl.ds(j * pitch, pitch)],
                              xo_hbm.at[pl.ds(r * pitch, pitch)], ssem.at[sl]).start(priority=ROW_DMA_QUEUE)

    def wait_gather(sl):
        pltpu.make_async_copy(xr_hbm.at[pl.ds(0, m * pitch)], xbuf.at[sl], gsem.at[sl]).wait()

    def wait_scatter(sl):
        pltpu.make_async_copy(obuf.at[sl], xo_hbm.at[pl.ds(0, m * pitch)], ssem.at[sl]).wait()

    @pl.when(c == 0)
    def _():
        obuf[...] = jnp.zeros_like(obuf)
        pad = pltpu.make_async_copy(obuf.at[0], xo_hbm.at[pl.ds(n_rows * pitch, m * pitch)], ssem.at[0])
        pad.start()
        pad.wait()

        def fill(lo, hi):
            def body(i, z):
                perm[i] = -1
                return z
            lax.fori_loop(lo, hi, body, 0)

        def first_gather(j, z):
            start_gather(0, 0, j)
            return z

        fill(0, m)
        fill((n + 1) * m, (n + 3) * m)
        for k in range(N_CLASSES):
            base = (off_ref[k] + 1) * m
            cnt = cnt_ref[k]
            fill(base + cnt, base + ((cnt + m - 1) // m) * m)

        def place(t, z):
            perm[(off_ref[cls_ref[t]] + 1) * m + rank_ref[t]] = t
            return z
        lax.fori_loop(0, n_rows, place, 0, unroll=8)
        lax.fori_loop(0, m, first_gather, 0)

    @pl.when(c <= n)
    def _():
        wait_gather(slot)

        @pl.when(c >= 1)
        def _():
            wait_scatter(slot)

        @pl.when(chga_ref[c] == 1)
        def _():
            wa1[...] = w1a_ref[0].astype(BF16)
            wa3[...] = w3a_ref[0].astype(BF16)
            wa2[...] = w2a_ref[0].astype(BF16)

        @pl.when(chgb_ref[c] == 1)
        def _():
            wb1[...] = w1b_ref[0].astype(BF16)
            wb3[...] = w3b_ref[0].astype(BF16)
            wb2[...] = w2b_ref[0].astype(BF16)

        for j in range(m):
            start_gather(c + 1, other, j)
        for j in range(m):
            start_scatter(c - 1, other, j)
        x = _read_slabs(xbuf, (slot,), m, n_slabs, pitch)
        info = xbuf[slot, pl.ds(n_slabs, m, stride=pitch), :]
        w_a = info[:, 0:1]
        w_b = info[:, 1:2]
        h = _rms(x, g_ref[...]).astype(BF16)
        hid_a = (jax.nn.silu(jnp.dot(h, wa1[...], preferred_element_type=F32))
                 * jnp.dot(h, wa3[...], preferred_element_type=F32) * w_a).astype(BF16)
        hid_b = (jax.nn.silu(jnp.dot(h, wb1[...], preferred_element_type=F32))
                 * jnp.dot(h, wb3[...], preferred_element_type=F32) * w_b).astype(BF16)
        y = (jnp.dot(hid_a, wa2[...], preferred_element_type=F32)
             + jnp.dot(hid_b, wb2[...], preferred_element_type=F32))
        _write_slabs(obuf, (slot,), x + y, pitch)

        @pl.when(c == n)
        def _():
            wait_gather(other)
            wait_scatter(other)


def _moe(xr, g, w1, w3, w2, tables, n_rows, c_max):
    d = g.shape[1]
    f = w1.shape[2]
    m = MOE_CHUNK
    pitch = _slab_pitch(d)
    kern = functools.partial(_moe_kernel, d=d, n_rows=n_rows)

    def wspec(shape, which):
        return pl.BlockSpec(shape, lambda c, *pref: (pref[which][c], 0, 0))

    grid_spec = pltpu.PrefetchScalarGridSpec(
        num_scalar_prefetch=9,
        grid=(c_max + 1,),
        in_specs=[
            pl.BlockSpec(memory_space=pl.ANY),
            pl.BlockSpec((1, d), lambda c, *pref: (0, 0)),
            wspec((1, d, f), 4), wspec((1, d, f), 4), wspec((1, f, d), 4),
            wspec((1, d, f), 5), wspec((1, d, f), 5), wspec((1, f, d), 5),
        ],
        out_specs=pl.BlockSpec(memory_space=pl.ANY),
        scratch_shapes=[
            pltpu.SMEM(((c_max + 3) * m,), I32),
            pltpu.VMEM((2, m * pitch, LANES), F32),
            pltpu.VMEM((2, m * pitch, LANES), F32),
            pltpu.VMEM((d, f), BF16), pltpu.VMEM((d, f), BF16), pltpu.VMEM((f, d), BF16),
            pltpu.VMEM((d, f), BF16), pltpu.VMEM((d, f), BF16), pltpu.VMEM((f, d), BF16),
            pltpu.SemaphoreType.DMA((2,)),
            pltpu.SemaphoreType.DMA((2,)),
        ],
    )
    return pl.pallas_call(
        kern,
        grid_spec=grid_spec,
        out_shape=jax.ShapeDtypeStruct(((n_rows + 2 * m) * pitch, LANES), F32),
        compiler_params=pltpu.CompilerParams(
            dimension_semantics=("arbitrary",), vmem_limit_bytes=VMEM_LIMIT),
        name="moe",
    )(*tables, xr, g, w1, w3, w2, w1, w3, w2)


def _chunk_tables(counts, c_max, expert0):
    m = MOE_CHUNK
    nch_c = (counts + m - 1) // m
    ch_end = jnp.cumsum(nch_c)
    ch_off = ch_end - nch_c
    n_chunks = ch_end[-1]
    chunk = jnp.minimum(jnp.arange(c_max + 1, dtype=I32), jnp.maximum(n_chunks - 1, 0))
    ccls = jnp.sum((chunk[:, None] >= ch_end[None, :]).astype(I32), axis=1)
    ccls = jnp.minimum(ccls, N_CLASSES - 1)
    grp = ccls // N_PAIRS
    pid = ccls % N_PAIRS
    ea = expert0 + grp * EXPERTS_PER_GROUP + jnp.asarray(PAIR_SLOT_A, I32)[pid]
    eb = expert0 + grp * EXPERTS_PER_GROUP + jnp.asarray(PAIR_SLOT_B, I32)[pid]
    first = jnp.ones((1,), I32)
    chga = jnp.concatenate([first, (ea[1:] != ea[:-1]).astype(I32)])
    chgb = jnp.concatenate([first, (eb[1:] != eb[:-1]).astype(I32)])
    return ch_off.astype(I32), counts, ea, eb, chga, chgb, n_chunks.reshape(1).astype(I32)


def _final_norm_kernel(x_ref, g_ref, op_ref, os_ref, *, n_prompt_blocks):
    i = pl.program_id(0)
    d = g_ref.shape[1]
    y = _rms(_read_slabs(x_ref, (), ROW_BLOCK, d // LANES, _slab_pitch(d)), g_ref[...])

    @pl.when(i < n_prompt_blocks)
    def _():
        op_ref[...] = y

    @pl.when(i >= n_prompt_blocks)
    def _():
        os_ref[...] = y


def _final_norm(x, g, n_prompt, n_sample):
    d = g.shape[1]
    npb = n_prompt // ROW_BLOCK
    nsb = n_sample // ROW_BLOCK
    kern = functools.partial(_final_norm_kernel, n_prompt_blocks=npb)
    return pl.pallas_call(
        kern,
        grid=(npb + nsb,),
        in_specs=[pl.BlockSpec((ROW_BLOCK * _slab_pitch(d), LANES), lambda i: (i, 0)),
                  pl.BlockSpec((1, d), lambda i: (0, 0))],
        out_specs=[pl.BlockSpec((ROW_BLOCK, d), lambda i: (jnp.minimum(i, npb - 1), 0)),
                   pl.BlockSpec((ROW_BLOCK, d), lambda i: (jnp.maximum(i - npb, 0), 0))],
        out_shape=[jax.ShapeDtypeStruct((n_prompt, d), F32),
                   jax.ShapeDtypeStruct((n_sample, d), F32)],
        compiler_params=pltpu.CompilerParams(dimension_semantics=("arbitrary",)),
        name="final_norm",
    )(x, g)


def kernel(x_prompt, x_sample, cache_k, cache_v, state_lru_h, state_conv, state_pool, norm1_g, w_in, conv_w, conv_b, lru_wa, lru_ba, lru_wx, lru_bx, lru_lambda, pool_w, pool_scale, attn_sinks, w_out, norm2_g, router_group_w, router_group_b, router_expert_w, router_expert_b, expert_w1, expert_w3, expert_w2, final_norm_g):
    bsz, seq, d = x_prompt.shape
    db, ds, _ = x_sample.shape
    depth = w_in.shape[0]
    win = cache_k.shape[2]
    nkv = cache_k.shape[3]
    d_lru = lru_lambda.shape[1]
    d_pool = pool_scale.shape[1]
    d_kv = nkv * cache_k.shape[4]
    d_attn = attn_sinks.shape[1] * HEAD_DIM
    dims = dict(d_lru=d_lru, d_pool=d_pool, d_attn=d_attn, d_kv=d_kv)
    n_prompt = bsz * seq
    n_sample = db * ds
    n_rows = n_prompt + n_sample
    assert n_prompt % ROW_BLOCK == 0 and n_sample % ROW_BLOCK == 0
    assert win == WINDOW and nkv == N_KV_HEADS and ds <= 8 and WINDOW % ds == 0
    tile = 512 if seq % 512 == 0 else WINDOW
    c_max = -(-n_rows // MOE_CHUNK) + N_CLASSES
    npb = n_prompt // ROW_BLOCK

    cos_p, sin_p = _rope_tables(jnp.arange(seq))
    cos_s, sin_s = _rope_tables(PAST_LEN + jnp.arange(ds))
    wr_all = jnp.concatenate([router_group_w, router_expert_w], axis=2)
    wr_all = jnp.pad(wr_all, ((0, 0), (0, 0), (0, LANES - wr_all.shape[2]))).astype(BF16)
    br_all = jnp.concatenate([router_group_b, router_expert_b], axis=1)
    br_all = jnp.pad(br_all, ((0, 0), (0, LANES - br_all.shape[1])))

    x_main, x_tail, tail0, slab = x_prompt.reshape(n_prompt, d), x_sample.reshape(n_sample, d), 0, False
    ck = cache_k.reshape(depth, db, win, d_kv)
    cv = cache_v.reshape(depth, db, win, d_kv)
    st_h = state_lru_h.reshape(depth, db, 1, d_lru)
    n_exp, _, d_exp = expert_w1.shape[1:]
    w1 = expert_w1.reshape(depth * n_exp, d, d_exp)
    w3 = expert_w3.reshape(depth * n_exp, d, d_exp)
    w2 = expert_w2.reshape(depth * n_exp, d_exp, d)
    outs = [[] for _ in range(10)]
    for l in range(depth):
        p = dict(conv_w=conv_w[l], conv_b=conv_b[l], lru_wa=lru_wa[l], lru_ba=lru_ba[l],
                 lru_wx=lru_wx[l], lru_bx=lru_bx[l], lru_lambda=lru_lambda[l], pool_w=pool_w[l],
                 pool_scale=pool_scale[l], attn_sinks=attn_sinks[l])
        consts = _layer_consts(p, dims)
        proj = _in_proj(x_main, x_tail, tail0, slab, norm1_g[l].reshape(1, d), w_in, l, n_rows, n_prompt)
        mix_p, pk, pv, ph, pc, pp = _mixer_prompt(proj, consts, cos_p, sin_p, bsz, seq, dims, tile)
        mix_s, sk, sv, sh, sc, sp = _mixer_sample(
            proj, n_prompt, consts, cos_s, sin_s, ck, cv, st_h, state_conv, state_pool, l, dims)
        xr, info, cnt = _out_proj(mix_p, mix_s, x_main, x_tail, tail0, slab, w_out, l,
                                  norm2_g[l].reshape(1, d), wr_all[l], br_all[l:l + 1], n_rows)
        route = info[:, 2:4].astype(I32)
        tables = (route[:, 0], route[:, 1]) + _chunk_tables(cnt[0, :N_CLASSES].astype(I32), c_max, l * n_exp)
        x = _moe(xr, norm2_g[l].reshape(1, d), w1, w3, w2, tables, n_rows, c_max)
        x_main, x_tail, tail0, slab = x, x, npb, True
        for lst, val in zip(outs, (pk.reshape(bsz, WINDOW, nkv, HEAD_DIM), pv.reshape(bsz, WINDOW, nkv, HEAD_DIM),
                                   ph.reshape(bsz, d_lru), pc, pp,
                                   sk.reshape(db, win, nkv, HEAD_DIM), sv.reshape(db, win, nkv, HEAD_DIM),
                                   sh.reshape(db, d_lru), sc, sp)):
            lst.append(val)
    y_p, y_s = _final_norm(x_main, final_norm_g.reshape(1, d), n_prompt, n_sample)
    return (y_p.reshape(bsz, seq, d), y_s.reshape(db, ds, d)) + tuple(jnp.stack(o) for o in outs)
```

```python
import functools

import jax
import jax.numpy as jnp
from jax import lax
from jax.experimental import pallas as pl
from jax.experimental.pallas import tpu as pltpu

F32 = jnp.float32
BF16 = jnp.bfloat16
I32 = jnp.int32

LRU_HEADS = 8
CONV_WIDTH = 4
LRU_C = 8.0
POOL_WINDOWS = (2, 4, 8, 16)
POOL_PAD = max(POOL_WINDOWS) - 1
HEAD_DIM = 64
N_KV_HEADS = 4
WINDOW = 128
ROPE_THETA = 10000.0
ATTN_SCALE = HEAD_DIM ** -0.5
LOG2_E = 1.4426950408889634
N_EXPERT_GROUPS = 4
EXPERTS_PER_GROUP = 4
N_EXPERTS = N_EXPERT_GROUPS * EXPERTS_PER_GROUP
RMS_EPS = 1e-6
PAST_LEN = 16384

LANES = 128
HALF = LANES // 2
HALO = 16
ROW_BLOCK = 256
SAMPLE_SEQS_PER_STEP = 4
MOE_CHUNK = 128
PAIR_SLOT_A = (0, 0, 0, 1, 1, 3)
PAIR_SLOT_B = (1, 2, 3, 3, 2, 2)
N_PAIRS = len(PAIR_SLOT_A)
N_CLASSES = N_EXPERT_GROUPS * N_PAIRS
VMEM_LIMIT = 52 * 1024 * 1024


def _rms(x, g):
    return (x * lax.rsqrt(jnp.mean(x * x, axis=-1, keepdims=True) + RMS_EPS)) * g


def _load_weight_bf16(w_hbm, w_bf, stage, sem):
    rows = stage.shape[1]
    n = w_hbm.shape[0] // rows

    def copy(i, sl):
        return pltpu.make_async_copy(w_hbm.at[pl.ds(i * rows, rows)], stage.at[sl], sem.at[sl])

    copy(0, 0).start()
    for i in range(n):
        sl = i % 2
        if i + 1 < n:
            copy(i + 1, 1 - sl).start()
        copy(i, sl).wait()
        w_bf[i * rows:(i + 1) * rows, :] = stage[sl].astype(BF16)


def _slab_pitch(d):
    return d // LANES + 1


def _read_slabs(ref, lead, n_tok, n_slabs, pitch):
    return jnp.concatenate(
        [ref[lead + (pl.ds(k, n_tok, stride=pitch), slice(None))] for k in range(n_slabs)], axis=1)


def _write_slabs(ref, lead, val, pitch):
    n_tok = val.shape[0]
    for k in range(val.shape[1] // LANES):
        ref[lead + (pl.ds(k, n_tok, stride=pitch), slice(None))] = val[:, k * LANES:(k + 1) * LANES]


def _pick_rows(i, n_main_blocks, main_ref, tail_ref, d, slab):
    if slab:
        return _read_slabs(main_ref, (), ROW_BLOCK, d // LANES, _slab_pitch(d))
    return jnp.where(i < n_main_blocks, main_ref[...], tail_ref[...])


def _in_proj_kernel(xm_ref, xt_ref, g_ref, w_hbm, o_ref, w_bf, stage, sem, *, n_main_blocks, layer, slab):
    i = pl.program_id(0)

    @pl.when(i == 0)
    def _():
        _load_weight_bf16(w_hbm.at[layer], w_bf, stage, sem)

    h = _rms(_pick_rows(i, n_main_blocks, xm_ref, xt_ref, g_ref.shape[1], slab), g_ref[...])
    o_ref[...] = jnp.dot(h.astype(BF16), w_bf[...], preferred_element_type=F32)


def _row_specs(d, n_main_blocks, tail_block0, slab):
    if slab:
        shape = (ROW_BLOCK * _slab_pitch(d), LANES)
        return [pl.BlockSpec(shape, lambda i: (i, 0)), pl.BlockSpec(shape, lambda i: (tail_block0, 0))]
    shape = (ROW_BLOCK, d)
    return [pl.BlockSpec(shape, lambda i: (jnp.minimum(i, n_main_blocks - 1), 0)),
            pl.BlockSpec(shape, lambda i: (jnp.maximum(i - n_main_blocks, 0) + tail_block0, 0))]


def _in_proj(x_main, x_tail, tail_block0, slab, g, w, layer, n_rows, n_main):
    d = g.shape[1]
    d_in = w.shape[2]
    nmb = n_main // ROW_BLOCK
    kern = functools.partial(_in_proj_kernel, n_main_blocks=nmb, layer=layer, slab=slab)
    return pl.pallas_call(
        kern,
        grid=(n_rows // ROW_BLOCK,),
        in_specs=_row_specs(d, nmb, tail_block0, slab) + [
            pl.BlockSpec((1, d), lambda i: (0, 0)),
            pl.BlockSpec(memory_space=pl.ANY),
        ],
        out_specs=pl.BlockSpec((ROW_BLOCK, d_in), lambda i: (i, 0)),
        out_shape=jax.ShapeDtypeStruct((n_rows, d_in), F32),
        scratch_shapes=[
            pltpu.VMEM((d, d_in), BF16),
            pltpu.VMEM((2, ROW_BLOCK, d_in), F32),
            pltpu.SemaphoreType.DMA((2,)),
        ],
        compiler_params=pltpu.CompilerParams(
            dimension_semantics=("arbitrary",), vmem_limit_bytes=VMEM_LIMIT),
        name="in_proj",
    )(x_main, x_tail, g, w)


def _scan_linear(a, b):
    t = a.shape[0]
    row = lax.broadcasted_iota(I32, a.shape, 0)
    d = 1
    while d < min(t, 8):
        a_sh = pltpu.roll(a, d, axis=0)
        b_sh = pltpu.roll(b, d, axis=0)
        m = row >= d
        b = jnp.where(m, a * b_sh + b, b)
        a = jnp.where(m, a * a_sh, a)
        d *= 2
    while d < t:
        b = jnp.concatenate([b[:d], a[d:] * b[:t - d] + b[d:]], axis=0)
        a = jnp.concatenate([a[:d], a[d:] * a[:t - d]], axis=0)
        d *= 2
    return a, b


def _lru_chunk(xc, g, h_prev, wg, ba, bx, sp):
    pre = jnp.dot(xc.astype(BF16), wg, preferred_element_type=F32)
    r = jax.nn.sigmoid(pre[:, :LANES] + ba)
    ig = jax.nn.sigmoid(pre[:, LANES:] + bx)
    log_a = (-LRU_C * r) * sp
    a = jnp.exp(log_a)
    v = 1.0 - a * a
    bterm = jnp.where(v > 0.0, v * lax.rsqrt(v), 0.0) * ig * xc
    a_cum, h0 = _scan_linear(a, bterm)
    hs = a_cum * h_prev + h0
    return hs * jax.nn.gelu(g), hs


def _rope(x, cos, sin_signed):
    n = x.shape[1] // LANES
    lane = lax.broadcasted_iota(I32, (x.shape[0], LANES), 1)
    first = (lane % HEAD_DIM) < (HEAD_DIM // 2)
    outs = []
    for c in range(n):
        xc = x[:, c * LANES:(c + 1) * LANES]
        swapped = jnp.where(first, pltpu.roll(xc, LANES - HEAD_DIM // 2, axis=1),
                            pltpu.roll(xc, HEAD_DIM // 2, axis=1))
        outs.append(xc * cos + swapped * sin_signed)
    return outs


def _store_head_variants(var_ref, row0, chunks):
    t = chunks[0].shape[0]
    lane = lax.broadcasted_iota(I32, (t, LANES), 1)
    for kc, x in enumerate(chunks):
        swapped = pltpu.roll(x, HALF, axis=1)
        for hh in range(2):
            for p in range(2):
                src = x if p == hh else swapped
                keep = (lane < HALF) if p == 0 else (lane >= HALF)
                var_ref[2 * kc + hh, p, row0:row0 + t, :] = jnp.where(keep, src, 0.0).astype(BF16)


def _attend_block(q_chunks, kvar, vvar, key0, sinks, lim, out_ref, out_rows, out_col0):
    qb = q_chunks[0].shape[0]
    nk = 2 * WINDOW
    gq = (2 * len(q_chunks)) // N_KV_HEADS
    rows = 2 * qb
    qi = lax.broadcasted_iota(I32, (rows, nk), 0) % qb
    kj = lax.broadcasted_iota(I32, (rows, nk), 1)
    valid = ((kj < WINDOW) & (kj > qi + lim)) | ((kj >= WINDOW) & (kj - WINDOW <= qi))
    top = lax.broadcasted_iota(I32, (rows, 1), 0) < qb
    lane = lax.broadcasted_iota(I32, (nk, LANES), 1)
    ones_lo = jnp.where(lane < HALF, 1.0, 0.0).astype(BF16)
    ones_hi = jnp.where(lane >= HALF, 1.0, 0.0).astype(BF16)
    lane_o = lax.broadcasted_iota(I32, (rows, LANES), 1)
    nt = (((1,), (1,)), ((), ()))
    for c in range(N_KV_HEADS):
        c0 = c * gq // 2
        qs = jnp.concatenate([q_chunks[c0], q_chunks[c0 + 1]], axis=0)
        es, sink_terms = [], []
        for p in range(2):
            kc = kvar[c, p, key0:key0 + nk, :]
            s = lax.dot_general(qs, kc, nt, preferred_element_type=F32)
            s = jnp.where(valid, s, -jnp.inf)
            h0 = c * gq + p
            sink = jnp.where(top, sinks[:, h0:h0 + 1], sinks[:, h0 + 2:h0 + 3])
            m = jnp.maximum(jnp.max(s, axis=-1, keepdims=True), sink)
            es.append(jnp.exp2(s - m).astype(BF16))
            sink_terms.append(jnp.exp2(sink - m))
        r0 = jnp.concatenate([vvar[c, 0, key0:key0 + nk, :], ones_lo], axis=1)
        r1 = jnp.concatenate([vvar[c, 1, key0:key0 + nk, :], ones_hi], axis=1)
        od = (jnp.dot(es[0], r0, preferred_element_type=F32)
              + jnp.dot(es[1], r1, preferred_element_type=F32))
        den = od[:, LANES:] + jnp.where(lane_o < HALF, sink_terms[0], sink_terms[1])
        o = (od[:, :LANES] / den).astype(out_ref.dtype)
        out_ref[out_rows, out_col0 + c0 * LANES:out_col0 + (c0 + 1) * LANES] = o[0:qb]
        out_ref[out_rows, out_col0 + (c0 + 1) * LANES:out_col0 + (c0 + 2) * LANES] = o[qb:rows]


def _mixer_kernel(*refs, tile, qblock, is_prompt, pos0, d_lru, d_pool, d_attn, d_kv):
    (proj_ref, cos_ref, sin_ref, convw_ref, convb_ref, wg_ref, ba_ref, bx_ref, lam_ref,
     poolw_ref, pscale_ref, sink_ref) = refs[:12]
    if is_prompt:
        (mix_ref, ko_ref, vo_ref, ho_ref, co_ref, po_ref,
         lru_ext, pool_ext, kvar, vvar, hcar) = refs[12:]
    else:
        (ck_ref, cv_ref, sh_ref, sc_ref, sp_ref,
         mix_ref, ko_ref, vo_ref, ho_ref, co_ref, po_ref,
         lru_ext, pool_ext, kvar, vvar) = refs[12:]

    s = pl.program_id(1) if is_prompt else 0
    o_glru = d_lru
    o_pool = 2 * d_lru
    o_q = o_pool + d_pool
    o_k = o_q + d_attn
    o_v = o_k + d_kv
    n_kc = d_kv // LANES

    if is_prompt:
        @pl.when(s == 0)
        def _():
            lru_ext[0:HALO, :] = jnp.zeros((HALO, d_lru), F32)
            pool_ext[0:HALO, :] = jnp.zeros((HALO, d_pool), F32)
            hcar[...] = jnp.zeros_like(hcar)
            kvar[:, :, 0:WINDOW, :] = jnp.zeros((N_KV_HEADS, 2, WINDOW, LANES), BF16)
            vvar[:, :, 0:WINDOW, :] = jnp.zeros((N_KV_HEADS, 2, WINDOW, LANES), BF16)
        h_prev = hcar[0:1, :]
    else:
        lru_ext[0:HALO, :] = jnp.zeros((HALO, d_lru), F32)
        pool_ext[0:HALO, :] = jnp.zeros((HALO, d_pool), F32)
        lru_ext[HALO - (CONV_WIDTH - 1):HALO, :] = sc_ref[0]
        pool_ext[HALO - POOL_PAD:HALO, :] = sp_ref[0]
        h_prev = sh_ref[0]
        kvar[:, :, WINDOW:2 * WINDOW, :] = jnp.zeros((N_KV_HEADS, 2, WINDOW, LANES), BF16)
        vvar[:, :, WINDOW:2 * WINDOW, :] = jnp.zeros((N_KV_HEADS, 2, WINDOW, LANES), BF16)
        _store_head_variants(kvar, 0, [ck_ref[0, :, c * LANES:(c + 1) * LANES] for c in range(n_kc)])
        _store_head_variants(vvar, 0, [cv_ref[0, :, c * LANES:(c + 1) * LANES] for c in range(n_kc)])

    lru_ext[HALO:HALO + tile, :] = proj_ref[:, 0:d_lru]
    pool_ext[HALO:HALO + tile, :] = proj_ref[:, o_pool:o_pool + d_pool]

    xc = convb_ref[...] + convw_ref[0:1, :] * lru_ext[HALO - 3:HALO - 3 + tile, :]
    for j in range(1, CONV_WIDTH):
        xc = xc + convw_ref[j:j + 1, :] * lru_ext[HALO - 3 + j:HALO - 3 + j + tile, :]
    neg = -lam_ref[...]
    sp_all = jnp.maximum(neg, 0.0) + jnp.log1p(jnp.exp(-jnp.abs(neg)))
    h_last = []
    for c in range(d_lru // LANES):
        cs = slice(c * LANES, (c + 1) * LANES)
        y, hs = _lru_chunk(xc[:, cs], proj_ref[:, o_glru + c * LANES:o_glru + (c + 1) * LANES],
                           h_prev[:, cs], wg_ref[c], ba_ref[:, cs], bx_ref[:, cs], sp_all[:, cs])
        mix_ref[:, cs] = y.astype(mix_ref.dtype)
        h_last.append(hs[tile - 1:tile, :])
    h_last = jnp.concatenate(h_last, axis=-1)

    row = lax.broadcasted_iota(I32, (tile, LANES), 0)
    pos = pos0 + s * tile + row
    for gi, w in enumerate(POOL_WINDOWS):
        cs = slice(gi * LANES, (gi + 1) * LANES)
        e = pool_ext[:, cs]
        acc = e
        step = 1
        while step < w:
            acc = acc + pltpu.roll(acc, step, axis=0)
            step *= 2
        cnt = jnp.minimum(pos + 1, w).astype(F32)
        dlt = acc[HALO:HALO + tile, :] / cnt - e[HALO:HALO + tile, :]
        y = jnp.dot(dlt.astype(BF16), poolw_ref[gi], preferred_element_type=F32) * pscale_ref[:, cs]
        mix_ref[:, d_lru + gi * LANES:d_lru + (gi + 1) * LANES] = y.astype(mix_ref.dtype)

    sinks = sink_ref[...] * LOG2_E
    o_attn = d_lru + d_pool
    cos = cos_ref[...]
    sin = sin_ref[...]
    q_chunks = [(qc * (ATTN_SCALE * LOG2_E)).astype(BF16)
                for qc in _rope(proj_ref[:, o_q:o_q + d_attn], cos, sin)]
    k_rot = _rope(proj_ref[:, o_k:o_k + d_kv], cos, sin)
    _store_head_variants(kvar, WINDOW, k_rot)
    _store_head_variants(vvar, WINDOW, [proj_ref[:, o_v + c * LANES:o_v + (c + 1) * LANES]
                                        for c in range(n_kc)])
    for blk in range(tile // qblock):
        rs = slice(blk * qblock, (blk + 1) * qblock)
        if is_prompt and blk == 0:
            lim = jnp.where(s == 0, WINDOW, 0)
        else:
            lim = 0
        _attend_block([qc[rs] for qc in q_chunks], kvar, vvar, blk * qblock, sinks, lim,
                      mix_ref, rs, o_attn)

    k_last = jnp.concatenate([kc[tile - qblock:tile] for kc in k_rot], axis=-1)
    if is_prompt:
        lru_ext[0:HALO, :] = lru_ext[tile:tile + HALO, :]
        pool_ext[0:HALO, :] = pool_ext[tile:tile + HALO, :]
        hcar[0:1, :] = h_last
        kvar[:, :, 0:WINDOW, :] = kvar[:, :, tile:tile + WINDOW, :]
        vvar[:, :, 0:WINDOW, :] = vvar[:, :, tile:tile + WINDOW, :]

        @pl.when(s == pl.num_programs(1) - 1)
        def _():
            ko_ref[0] = k_last
            vo_ref[0] = proj_ref[tile - qblock:tile, o_v:o_v + d_kv]
            ho_ref[0] = h_last
            co_ref[0] = lru_ext[HALO + tile - (CONV_WIDTH - 1):HALO + tile, :]
            po_ref[0] = pool_ext[HALO + tile - POOL_PAD:HALO + tile, :]
    else:
        ko_ref[0, 0:WINDOW - tile, :] = ck_ref[0, tile:WINDOW, :]
        ko_ref[0, WINDOW - tile:WINDOW, :] = k_last
        vo_ref[0, 0:WINDOW - tile, :] = cv_ref[0, tile:WINDOW, :]
        vo_ref[0, WINDOW - tile:WINDOW, :] = proj_ref[:, o_v:o_v + d_kv]
        ho_ref[0] = h_last
        co_ref[0] = lru_ext[HALO + tile - (CONV_WIDTH - 1):HALO + tile, :]
        po_ref[0] = pool_ext[HALO + tile - POOL_PAD:HALO + tile, :]


def _layer_consts(p, dims):
    d_lru, d_pool = dims["d_lru"], dims["d_pool"]
    hd = d_lru // LRU_HEADS
    per = LANES // hd
    nchunk = d_lru // LANES

    def blockdiag(w):
        w = w.reshape(nchunk, per, hd, hd)
        eye = jnp.eye(per, dtype=w.dtype)
        return jnp.einsum("cpij,pq->cpiqj", w, eye).reshape(nchunk, LANES, LANES)

    wg = jnp.concatenate([blockdiag(p["lru_wa"]), blockdiag(p["lru_wx"])], axis=-1).astype(BF16)
    return dict(
        convw=p["conv_w"], convb=p["conv_b"].reshape(1, d_lru), wg=wg,
        ba=p["lru_ba"].reshape(1, d_lru), bx=p["lru_bx"].reshape(1, d_lru),
        lam=p["lru_lambda"].reshape(1, d_lru), poolw=p["pool_w"].astype(BF16),
        pscale=p["pool_scale"].reshape(1, d_pool), sinks=p["attn_sinks"].reshape(1, -1))


def _rope_tables(pos):
    half = HEAD_DIM // 2
    inv = ROPE_THETA ** (-jnp.arange(half, dtype=F32) / half)
    ang = pos.astype(F32)[:, None] * inv[None, :]
    cos = jnp.cos(ang)
    sin = jnp.sin(ang)
    cos2 = jnp.concatenate([cos, cos], axis=-1)
    sin2 = jnp.concatenate([-sin, sin], axis=-1)
    reps = LANES // HEAD_DIM
    return jnp.tile(cos2, (1, reps)), jnp.tile(sin2, (1, reps))


CONST_NAMES = ("convw", "convb", "wg", "ba", "bx", "lam", "poolw", "pscale", "sinks")


def _const_specs(consts):
    return [pl.BlockSpec(consts[n].shape, functools.partial(lambda nd, *_: (0,) * nd, consts[n].ndim))
            for n in CONST_NAMES]


def _mixer_prompt(proj, consts, cos, sin, bsz, seq, dims, tile):
    d_lru, d_pool, d_attn, d_kv = dims["d_lru"], dims["d_pool"], dims["d_attn"], dims["d_kv"]
    d_in = proj.shape[1]
    d_mix = d_lru + d_pool + d_attn
    ns = seq // tile
    kern = functools.partial(_mixer_kernel, tile=tile, qblock=WINDOW, is_prompt=True, pos0=0,
                             d_lru=d_lru, d_pool=d_pool, d_attn=d_attn, d_kv=d_kv)
    return pl.pallas_call(
        kern,
        grid=(bsz, ns),
        in_specs=[
            pl.BlockSpec((tile, d_in), lambda b, s: (b * ns + s, 0)),
            pl.BlockSpec((tile, LANES), lambda b, s: (s, 0)),
            pl.BlockSpec((tile, LANES), lambda b, s: (s, 0)),
        ] + _const_specs(consts),
        out_specs=[
            pl.BlockSpec((tile, d_mix), lambda b, s: (b * ns + s, 0)),
            pl.BlockSpec((1, WINDOW, d_kv), lambda b, s: (b, 0, 0)),
            pl.BlockSpec((1, WINDOW, d_kv), lambda b, s: (b, 0, 0)),
            pl.BlockSpec((1, 1, d_lru), lambda b, s: (b, 0, 0)),
            pl.BlockSpec((1, CONV_WIDTH - 1, d_lru), lambda b, s: (b, 0, 0)),
            pl.BlockSpec((1, POOL_PAD, d_pool), lambda b, s: (b, 0, 0)),
        ],
        out_shape=[
            jax.ShapeDtypeStruct((bsz * seq, d_mix), BF16),
            jax.ShapeDtypeStruct((bsz, WINDOW, d_kv), F32),
            jax.ShapeDtypeStruct((bsz, WINDOW, d_kv), F32),
            jax.ShapeDtypeStruct((bsz, 1, d_lru), F32),
            jax.ShapeDtypeStruct((bsz, CONV_WIDTH - 1, d_lru), F32),
            jax.ShapeDtypeStruct((bsz, POOL_PAD, d_pool), F32),
        ],
        scratch_shapes=[
            pltpu.VMEM((tile + HALO, d_lru), F32),
            pltpu.VMEM((tile + HALO, d_pool), F32),
            pltpu.VMEM((N_KV_HEADS, 2, WINDOW + tile, LANES), BF16),
            pltpu.VMEM((N_KV_HEADS, 2, WINDOW + tile, LANES), BF16),
            pltpu.VMEM((8, d_lru), F32),
        ],
        compiler_params=pltpu.CompilerParams(
            dimension_semantics=("arbitrary", "arbitrary"), vmem_limit_bytes=VMEM_LIMIT),
        name="mixer_prompt",
    )(proj, cos, sin, *[consts[n] for n in CONST_NAMES])


def _mixer_sample_kernel(*refs, n_seq, t, d_lru, d_pool, d_attn, d_kv):
    n_in = 3 + len(CONST_NAMES)
    proj_ref, shared = refs[0], refs[1:n_in]
    state_in = refs[n_in:n_in + 5]
    mix_ref = refs[n_in + 5]
    state_out = refs[n_in + 6:n_in + 11]
    scratch = refs[n_in + 11:]
    for q in range(n_seq):
        one = lambda r: r.at[pl.ds(q, 1)]
        _mixer_kernel(proj_ref.at[pl.ds(q * t, t)], *shared, *[one(r) for r in state_in],
                      mix_ref.at[pl.ds(q * t, t)], *[one(r) for r in state_out],
                      *[r.at[q] for r in scratch],
                      tile=t, qblock=t, is_prompt=False, pos0=PAST_LEN,
                      d_lru=d_lru, d_pool=d_pool, d_attn=d_attn, d_kv=d_kv)


def _mixer_sample(proj, row0, consts, cos, sin, cache_k, cache_v, st_h, st_conv, st_pool, layer, dims):
    d_lru, d_pool, d_attn, d_kv = dims["d_lru"], dims["d_pool"], dims["d_attn"], dims["d_kv"]
    d_in = proj.shape[1]
    d_mix = d_lru + d_pool + d_attn
    db, win = cache_k.shape[1], cache_k.shape[2]
    t = cos.shape[0]
    nq = SAMPLE_SEQS_PER_STEP
    assert db % nq == 0 and row0 % (nq * t) == 0
    blk0 = row0 // (nq * t)
    kern = functools.partial(_mixer_sample_kernel, n_seq=nq, t=t,
                             d_lru=d_lru, d_pool=d_pool, d_attn=d_attn, d_kv=d_kv)
    return pl.pallas_call(
        kern,
        grid=(db // nq,),
        in_specs=[
            pl.BlockSpec((nq * t, d_in), lambda b: (blk0 + b, 0)),
            pl.BlockSpec((t, LANES), lambda b: (0, 0)),
            pl.BlockSpec((t, LANES), lambda b: (0, 0)),
        ] + _const_specs(consts) + [
            pl.BlockSpec((None, nq, win, d_kv), lambda b: (layer, b, 0, 0)),
            pl.BlockSpec((None, nq, win, d_kv), lambda b: (layer, b, 0, 0)),
            pl.BlockSpec((None, nq, 1, d_lru), lambda b: (layer, b, 0, 0)),
            pl.BlockSpec((None, nq, CONV_WIDTH - 1, d_lru), lambda b: (layer, b, 0, 0)),
            pl.BlockSpec((None, nq, POOL_PAD, d_pool), lambda b: (layer, b, 0, 0)),
        ],
        out_specs=[
            pl.BlockSpec((nq * t, d_mix), lambda b: (b, 0)),
            pl.BlockSpec((nq, win, d_kv), lambda b: (b, 0, 0)),
            pl.BlockSpec((nq, win, d_kv), lambda b: (b, 0, 0)),
            pl.BlockSpec((nq, 1, d_lru), lambda b: (b, 0, 0)),
            pl.BlockSpec((nq, CONV_WIDTH - 1, d_lru), lambda b: (b, 0, 0)),
            pl.BlockSpec((nq, POOL_PAD, d_pool), lambda b: (b, 0, 0)),
        ],
        out_shape=[
            jax.ShapeDtypeStruct((db * t, d_mix), F32),
            jax.ShapeDtypeStruct((db, win, d_kv), F32),
            jax.ShapeDtypeStruct((db, win, d_kv), F32),
            jax.ShapeDtypeStruct((db, 1, d_lru), F32),
            jax.ShapeDtypeStruct((db, CONV_WIDTH - 1, d_lru), F32),
            jax.ShapeDtypeStruct((db, POOL_PAD, d_pool), F32),
        ],
        scratch_shapes=[
            pltpu.VMEM((nq, t + HALO, d_lru), F32),
            pltpu.VMEM((nq, t + HALO, d_pool), F32),
            pltpu.VMEM((nq, N_KV_HEADS, 2, 2 * WINDOW, LANES), BF16),
            pltpu.VMEM((nq, N_KV_HEADS, 2, 2 * WINDOW, LANES), BF16),
        ],
        compiler_params=pltpu.CompilerParams(
            dimension_semantics=("arbitrary",), vmem_limit_bytes=VMEM_LIMIT),
        name="mixer_sample",
    )(proj, cos, sin, *[consts[n] for n in CONST_NAMES], cache_k, cache_v, st_h, st_conv, st_pool)


def _route(logits, run_cnt):
    t = logits.shape[0]
    lane = lax.broadcasted_iota(I32, (t, LANES), 1)
    lane_f = lane.astype(F32)
    ninf = -jnp.inf
    big = float(LANES)
    is_g = lane < N_EXPERT_GROUPS
    lg = jnp.where(is_g, logits, ninf)
    mg = jnp.max(lg, axis=-1, keepdims=True)
    g_top = jnp.min(jnp.where(lg == mg, lane_f, big), axis=-1, keepdims=True).astype(I32)
    pg_top = 1.0 / jnp.sum(jnp.exp(lg - mg), axis=-1, keepdims=True)
    base = N_EXPERT_GROUPS + EXPERTS_PER_GROUP * g_top
    in_grp = (lane >= base) & (lane < base + EXPERTS_PER_GROUP)
    le = jnp.where(in_grp, logits, ninf)
    m1 = jnp.max(le, axis=-1, keepdims=True)
    i1 = jnp.min(jnp.where(le == m1, lane_f, big), axis=-1, keepdims=True).astype(I32)
    le2 = jnp.where(lane == i1, ninf, le)
    m2 = jnp.max(le2, axis=-1, keepdims=True)
    i2 = jnp.min(jnp.where((le2 == m2) & in_grp & (lane != i1), lane_f, big),
                 axis=-1, keepdims=True).astype(I32)
    se = jnp.sum(jnp.exp(le - m1), axis=-1, keepdims=True)
    p1 = 1.0 / se
    p2 = jnp.exp(m2 - m1) / se
    tot = p1 + p2
    w1 = (p1 / tot) * pg_top
    w2 = (p2 / tot) * pg_top
    a = i1 - base
    b = i2 - base
    lo = jnp.minimum(a, b)
    hi = jnp.maximum(a, b)
    w_lo = jnp.where(a < b, w1, w2)
    w_hi = jnp.where(a < b, w2, w1)
    pid = jnp.where(lo == 0, hi - 1, jnp.where(lo == 1, jnp.where(hi == 3, 3, 4), 5))
    swap = pid == 5
    w_a = jnp.where(swap, w_hi, w_lo)
    w_b = jnp.where(swap, w_lo, w_hi)
    cls = g_top * N_PAIRS + pid
    onehot = lane == cls
    ti = lax.broadcasted_iota(I32, (t, t), 0)
    tj = lax.broadcasted_iota(I32, (t, t), 1)
    lower = jnp.where(tj <= ti, 1.0, 0.0).astype(BF16)
    prefix = jnp.dot(lower, jnp.where(onehot, 1.0, 0.0).astype(BF16), preferred_element_type=F32)
    rank = jnp.sum(jnp.where(onehot, prefix - 1.0 + run_cnt, 0.0), axis=-1, keepdims=True)
    info = jnp.where(lane == 0, w_a, jnp.where(lane == 1, w_b, jnp.where(
        lane == 2, cls.astype(F32), jnp.where(lane == 3, rank, 0.0))))
    return info, run_cnt + prefix[t - 1:t, :]


def _out_proj_kernel(mixp_ref, mixs_ref, xm_ref, xt_ref, w_hbm, g_ref, wr_ref, br_ref,
                     o_ref, info_ref, cnt_ref, w_bf, stage, sem, run_cnt, *, n_main_blocks, d, layer, slab):
    i = pl.program_id(0)

    @pl.when(i == 0)
    def _():
        _load_weight_bf16(w_hbm.at[layer], w_bf, stage, sem)
        run_cnt[...] = jnp.zeros_like(run_cnt)

    mix = jnp.where(i < n_main_blocks, mixp_ref[...], mixs_ref[...].astype(BF16))
    xres = (_pick_rows(i, n_main_blocks, xm_ref, xt_ref, d, slab)
            + jnp.dot(mix, w_bf[...], preferred_element_type=F32))
    h2 = _rms(xres, g_ref[...])
    logits = jnp.dot(h2.astype(BF16), wr_ref[...], preferred_element_type=F32) + br_ref[...]
    info, cnt = _route(logits, run_cnt[0:1, :])
    run_cnt[0:1, :] = cnt
    pitch = _slab_pitch(d)
    _write_slabs(o_ref, (), xres, pitch)
    o_ref[pl.ds(d // LANES, ROW_BLOCK, stride=pitch), :] = info
    info_ref[...] = info
    cnt_ref[...] = jnp.broadcast_to(cnt, cnt_ref.shape)


def _out_proj(mix_p, mix_s, x_main, x_tail, tail_block0, slab, w, layer, g, wr, br, n_rows):
    d = g.shape[1]
    d_mix = w.shape[1]
    npb = mix_p.shape[0] // ROW_BLOCK
    pitch = _slab_pitch(d)
    kern = functools.partial(_out_proj_kernel, n_main_blocks=npb, d=d, layer=layer, slab=slab)
    return pl.pallas_call(
        kern,
        grid=(n_rows // ROW_BLOCK,),
        in_specs=[
            pl.BlockSpec((ROW_BLOCK, d_mix), lambda i: (jnp.minimum(i, npb - 1), 0)),
            pl.BlockSpec((ROW_BLOCK, d_mix), lambda i: (jnp.maximum(i - npb, 0), 0)),
        ] + _row_specs(d, npb, tail_block0, slab) + [
            pl.BlockSpec(memory_space=pl.ANY),
            pl.BlockSpec((1, d), lambda i: (0, 0)),
            pl.BlockSpec((d, LANES), lambda i: (0, 0)),
            pl.BlockSpec((1, LANES), lambda i: (0, 0)),
        ],
        out_specs=[pl.BlockSpec((ROW_BLOCK * pitch, LANES), lambda i: (i, 0)),
                   pl.BlockSpec((ROW_BLOCK, LANES), lambda i: (i, 0)),
                   pl.BlockSpec((8, LANES), lambda i: (0, 0))],
        out_shape=[jax.ShapeDtypeStruct((n_rows * pitch, LANES), F32),
                   jax.ShapeDtypeStruct((n_rows, LANES), F32),
                   jax.ShapeDtypeStruct((8, LANES), F32)],
        scratch_shapes=[
            pltpu.VMEM((d_mix, d), BF16),
            pltpu.VMEM((2, ROW_BLOCK, d), F32),
            pltpu.SemaphoreType.DMA((2,)),
            pltpu.VMEM((8, LANES), F32),
        ],
        compiler_params=pltpu.CompilerParams(
            dimension_semantics=("arbitrary",), vmem_limit_bytes=VMEM_LIMIT),
        name="out_proj",
    )(mix_p, mix_s, x_main, x_tail, w, g, wr, br)


def _moe_kernel(cls_ref, rank_ref, off_ref, cnt_ref, ea_ref, eb_ref, chga_ref, chgb_ref, nch_ref,
                xr_hbm, g_ref, w1_hbm, w3_hbm, w2_hbm,
                xo_hbm,
                perm, xbuf, obuf, wa1, wa3, wa2, wb1, wb3, wb2, sa1, sa3, sa2, sb1, sb3, sb2,
                gsem, ssem, wsem, *, d, n_rows):
    m = MOE_CHUNK

    def weight_copies(e_ref, chunk, stages, sem_row):
        e = e_ref[chunk]
        return [pltpu.make_async_copy(w.at[e], st, wsem.at[sem_row, k])
                for k, (w, st) in enumerate(zip((w1_hbm, w3_hbm, w2_hbm), stages))]
    pitch = _slab_pitch(d)
    n_slabs = d // LANES
    c = pl.program_id(0)
    n = nch_ref[0]
    slot = c % 2
    other = 1 - slot

    def start_gather(chunk, sl, j):
        tok = jnp.maximum(perm[(chunk + 1) * m + j], 0)
        pltpu.make_async_copy(xr_hbm.at[pl.ds(tok * pitch, pitch)],
                              xbuf.at[sl, pl.ds(j * pitch, pitch)], gsem.at[sl]).start()

    def start_scatter(chunk, sl, j):
        tok = perm[(chunk + 1) * m + j]
        r = jnp.where(tok < 0, n_rows + sl * m + j, tok)
        pltpu.make_async_copy(obuf.at[sl, pl.ds(j * pitch, pitch)],
                              xo_hbm.at[pl.ds(r * pitch, pitch)], ssem.at[sl]).start()

    def wait_gather(sl):
        pltpu.make_async_copy(xr_hbm.at[pl.ds(0, m * pitch)], xbuf.at[sl], gsem.at[sl]).wait()

    def wait_scatter(sl):
        pltpu.make_async_copy(obuf.at[sl], xo_hbm.at[pl.ds(0, m * pitch)], ssem.at[sl]).wait()

    @pl.when(c == 0)
    def _():
        for cp in (weight_copies(ea_ref, 0, (sa1, sa3, sa2), 0)
                   + weight_copies(eb_ref, 0, (sb1, sb3, sb2), 1)):
            cp.start()
        obuf[...] = jnp.zeros_like(obuf)
        pad = pltpu.make_async_copy(obuf.at[0], xo_hbm.at[pl.ds(n_rows * pitch, m * pitch)], ssem.at[0])
        pad.start()
        pad.wait()

        def fill(lo, hi):
            def body(i, z):
                perm[i] = -1
                return z
            lax.fori_loop(lo, hi, body, 0)

        def first_gather(j, z):
            start_gather(0, 0, j)
            return z

        fill(0, m)
        fill((n + 1) * m, (n + 3) * m)
        for k in range(N_CLASSES):
            base = (off_ref[k] + 1) * m
            cnt = cnt_ref[k]
            fill(base + cnt, base + ((cnt + m - 1) // m) * m)

        def place(t, z):
            perm[(off_ref[cls_ref[t]] + 1) * m + rank_ref[t]] = t
            return z
        lax.fori_loop(0, n_rows, place, 0, unroll=8)
        lax.fori_loop(0, m, first_gather, 0)

    @pl.when(c <= n)
    def _():
        wait_gather(slot)

        @pl.when(c >= 1)
        def _():
            wait_scatter(slot)

        for e_ref, chg_ref, stages, resident, row in (
                (ea_ref, chga_ref, (sa1, sa3, sa2), (wa1, wa3, wa2), 0),
                (eb_ref, chgb_ref, (sb1, sb3, sb2), (wb1, wb3, wb2), 1)):
            @pl.when(chg_ref[c] == 1)
            def _(e_ref=e_ref, stages=stages, resident=resident, row=row):
                for cp in weight_copies(e_ref, c, stages, row):
                    cp.wait()
                for st, w_bf in zip(stages, resident):
                    w_bf[...] = st[...].astype(BF16)

            @pl.when(chg_ref[c + 1] == 1)
            def _(e_ref=e_ref, stages=stages, row=row):
                for cp in weight_copies(e_ref, c + 1, stages, row):
                    cp.start()

        for j in range(m):
            start_gather(c + 1, other, j)
        for j in range(m):
            start_scatter(c - 1, other, j)
        x = _read_slabs(xbuf, (slot,), m, n_slabs, pitch)
        info = xbuf[slot, pl.ds(n_slabs, m, stride=pitch), :]
        w_a = info[:, 0:1]
        w_b = info[:, 1:2]
        h = _rms(x, g_ref[...]).astype(BF16)
        hid_a = (jax.nn.silu(jnp.dot(h, wa1[...], preferred_element_type=F32))
                 * jnp.dot(h, wa3[...], preferred_element_type=F32) * w_a).astype(BF16)
        hid_b = (jax.nn.silu(jnp.dot(h, wb1[...], preferred_element_type=F32))
                 * jnp.dot(h, wb3[...], preferred_element_type=F32) * w_b).astype(BF16)
        y = (jnp.dot(hid_a, wa2[...], preferred_element_type=F32)
             + jnp.dot(hid_b, wb2[...], preferred_element_type=F32))
        _write_slabs(obuf, (slot,), x + y, pitch)

        @pl.when(c == n)
        def _():
            wait_gather(other)
            wait_scatter(other)


def _moe(xr, g, w1, w3, w2, tables, n_rows, c_max):
    d = g.shape[1]
    f = w1.shape[2]
    m = MOE_CHUNK
    pitch = _slab_pitch(d)
    kern = functools.partial(_moe_kernel, d=d, n_rows=n_rows)

    expert = lambda dt: [pltpu.VMEM((d, f), dt), pltpu.VMEM((d, f), dt), pltpu.VMEM((f, d), dt)]
    grid_spec = pltpu.PrefetchScalarGridSpec(
        num_scalar_prefetch=9,
        grid=(c_max + 1,),
        in_specs=[
            pl.BlockSpec(memory_space=pl.ANY),
            pl.BlockSpec((1, d), lambda c, *pref: (0, 0)),
            pl.BlockSpec(memory_space=pl.ANY),
            pl.BlockSpec(memory_space=pl.ANY),
            pl.BlockSpec(memory_space=pl.ANY),
        ],
        out_specs=pl.BlockSpec(memory_space=pl.ANY),
        scratch_shapes=[
            pltpu.SMEM(((c_max + 3) * m,), I32),
            pltpu.VMEM((2, m * pitch, LANES), F32),
            pltpu.VMEM((2, m * pitch, LANES), F32),
        ] + expert(BF16) + expert(BF16) + expert(F32) + expert(F32) + [
            pltpu.SemaphoreType.DMA((2,)),
            pltpu.SemaphoreType.DMA((2,)),
            pltpu.SemaphoreType.DMA((2, 3)),
        ],
    )
    return pl.pallas_call(
        kern,
        grid_spec=grid_spec,
        out_shape=jax.ShapeDtypeStruct(((n_rows + 2 * m) * pitch, LANES), F32),
        compiler_params=pltpu.CompilerParams(
            dimension_semantics=("arbitrary",), vmem_limit_bytes=VMEM_LIMIT),
        name="moe",
    )(*tables, xr, g, w1, w3, w2)


def _chunk_tables(counts, c_max, expert0):
    m = MOE_CHUNK
    nch_c = (counts + m - 1) // m
    ch_end = jnp.cumsum(nch_c)
    ch_off = ch_end - nch_c
    n_chunks = ch_end[-1]
    chunk = jnp.minimum(jnp.arange(c_max + 2, dtype=I32), jnp.maximum(n_chunks - 1, 0))
    ccls = jnp.sum((chunk[:, None] >= ch_end[None, :]).astype(I32), axis=1)
    ccls = jnp.minimum(ccls, N_CLASSES - 1)
    grp = ccls // N_PAIRS
    pid = ccls % N_PAIRS
    ea = expert0 + grp * EXPERTS_PER_GROUP + jnp.asarray(PAIR_SLOT_A, I32)[pid]
    eb = expert0 + grp * EXPERTS_PER_GROUP + jnp.asarray(PAIR_SLOT_B, I32)[pid]
    first = jnp.ones((1,), I32)
    chga = jnp.concatenate([first, (ea[1:] != ea[:-1]).astype(I32)])
    chgb = jnp.concatenate([first, (eb[1:] != eb[:-1]).astype(I32)])
    return ch_off.astype(I32), counts, ea, eb, chga, chgb, n_chunks.reshape(1).astype(I32)


def _final_norm_kernel(x_ref, g_ref, op_ref, os_ref, *, n_prompt_blocks):
    i = pl.program_id(0)
    d = g_ref.shape[1]
    y = _rms(_read_slabs(x_ref, (), ROW_BLOCK, d // LANES, _slab_pitch(d)), g_ref[...])

    @pl.when(i < n_prompt_blocks)
    def _():
        op_ref[...] = y

    @pl.when(i >= n_prompt_blocks)
    def _():
        os_ref[...] = y


def _final_norm(x, g, n_prompt, n_sample):
    d = g.shape[1]
    npb = n_prompt // ROW_BLOCK
    nsb = n_sample // ROW_BLOCK
    kern = functools.partial(_final_norm_kernel, n_prompt_blocks=npb)
    return pl.pallas_call(
        kern,
        grid=(npb + nsb,),
        in_specs=[pl.BlockSpec((ROW_BLOCK * _slab_pitch(d), LANES), lambda i: (i, 0)),
                  pl.BlockSpec((1, d), lambda i: (0, 0))],
        out_specs=[pl.BlockSpec((ROW_BLOCK, d), lambda i: (jnp.minimum(i, npb - 1), 0)),
                   pl.BlockSpec((ROW_BLOCK, d), lambda i: (jnp.maximum(i - npb, 0), 0))],
        out_shape=[jax.ShapeDtypeStruct((n_prompt, d), F32),
                   jax.ShapeDtypeStruct((n_sample, d), F32)],
        compiler_params=pltpu.CompilerParams(dimension_semantics=("arbitrary",)),
        name="final_norm",
    )(x, g)


def kernel(x_prompt, x_sample, cache_k, cache_v, state_lru_h, state_conv, state_pool, norm1_g, w_in, conv_w, conv_b, lru_wa, lru_ba, lru_wx, lru_bx, lru_lambda, pool_w, pool_scale, attn_sinks, w_out, norm2_g, router_group_w, router_group_b, router_expert_w, router_expert_b, expert_w1, expert_w3, expert_w2, final_norm_g):
    bsz, seq, d = x_prompt.shape
    db, ds, _ = x_sample.shape
    depth = w_in.shape[0]
    win = cache_k.shape[2]
    nkv = cache_k.shape[3]
    d_lru = lru_lambda.shape[1]
    d_pool = pool_scale.shape[1]
    d_kv = nkv * cache_k.shape[4]
    d_attn = attn_sinks.shape[1] * HEAD_DIM
    dims = dict(d_lru=d_lru, d_pool=d_pool, d_attn=d_attn, d_kv=d_kv)
    n_prompt = bsz * seq
    n_sample = db * ds
    n_rows = n_prompt + n_sample
    assert n_prompt % ROW_BLOCK == 0 and n_sample % ROW_BLOCK == 0
    assert win == WINDOW and nkv == N_KV_HEADS and ds <= 8 and WINDOW % ds == 0
    tile = 512 if seq % 512 == 0 else WINDOW
    c_max = -(-n_rows // MOE_CHUNK) + N_CLASSES
    npb = n_prompt // ROW_BLOCK

    cos_p, sin_p = _rope_tables(jnp.arange(seq))
    cos_s, sin_s = _rope_tables(PAST_LEN + jnp.arange(ds))
    wr_all = jnp.concatenate([router_group_w, router_expert_w], axis=2)
    wr_all = jnp.pad(wr_all, ((0, 0), (0, 0), (0, LANES - wr_all.shape[2]))).astype(BF16)
    br_all = jnp.concatenate([router_group_b, router_expert_b], axis=1)
    br_all = jnp.pad(br_all, ((0, 0), (0, LANES - br_all.shape[1])))

    x_main, x_tail, tail0, slab = x_prompt.reshape(n_prompt, d), x_sample.reshape(n_sample, d), 0, False
    ck = cache_k.reshape(depth, db, win, d_kv)
    cv = cache_v.reshape(depth, db, win, d_kv)
    st_h = state_lru_h.reshape(depth, db, 1, d_lru)
    n_exp, _, d_exp = expert_w1.shape[1:]
    w1 = expert_w1.reshape(depth * n_exp, d, d_exp)
    w3 = expert_w3.reshape(depth * n_exp, d, d_exp)
    w2 = expert_w2.reshape(depth * n_exp, d_exp, d)
    outs = [[] for _ in range(10)]
    for l in range(depth):
        p = dict(conv_w=conv_w[l], conv_b=conv_b[l], lru_wa=lru_wa[l], lru_ba=lru_ba[l],
                 lru_wx=lru_wx[l], lru_bx=lru_bx[l], lru_lambda=lru_lambda[l], pool_w=pool_w[l],
                 pool_scale=pool_scale[l], attn_sinks=attn_sinks[l])
        consts = _layer_consts(p, dims)
        proj = _in_proj(x_main, x_tail, tail0, slab, norm1_g[l].reshape(1, d), w_in, l, n_rows, n_prompt)
        mix_p, pk, pv, ph, pc, pp = _mixer_prompt(proj, consts, cos_p, sin_p, bsz, seq, dims, tile)
        mix_s, sk, sv, sh, sc, sp = _mixer_sample(
            proj, n_prompt, consts, cos_s, sin_s, ck, cv, st_h, state_conv, state_pool, l, dims)
        xr, info, cnt = _out_proj(mix_p, mix_s, x_main, x_tail, tail0, slab, w_out, l,
                                  norm2_g[l].reshape(1, d), wr_all[l], br_all[l:l + 1], n_rows)
        route = info[:, 2:4].astype(I32)
        tables = (route[:, 0], route[:, 1]) + _chunk_tables(cnt[0, :N_CLASSES].astype(I32), c_max, l * n_exp)
        x = _moe(xr, norm2_g[l].reshape(1, d), w1, w3, w2, tables, n_rows, c_max)
        x_main, x_tail, tail0, slab = x, x, npb, True
        for lst, val in zip(outs, (pk.reshape(bsz, WINDOW, nkv, HEAD_DIM), pv.reshape(bsz, WINDOW, nkv, HEAD_DIM),
                                   ph.reshape(bsz, d_lru), pc, pp,
                                   sk.reshape(db, win, nkv, HEAD_DIM), sv.reshape(db, win, nkv, HEAD_DIM),
                                   sh.reshape(db, d_lru), sc, sp)):
            lst.append(val)
    y_p, y_s = _final_norm(x_main, final_norm_g.reshape(1, d), n_prompt, n_sample)
    return (y_p.reshape(bsz, seq, d), y_s.reshape(db, ds, d)) + tuple(jnp.stack(o) for o in outs)
```
